```python
import jax, jax.numpy as jnp
from jax import lax
import numpy as np

D_MODEL = 1024
BATCH = 16
SEQ = 256
DEPTH = 2
DEC_BATCH = 2
DEC_SEQ = 4096
PAST_LEN = 256

GRID_W = 64
N_EVEN = (DEPTH + 1) // 2
N_ODD = DEPTH // 2
EPS = 1e-6
H_A = 4
DH_A = D_MODEL // 8
W_A = H_A * DH_A
CHUNK_A = 128
FORGET_BIAS = 3.0
HQ_B = 8
HKV_B = 2
G_B = HQ_B // HKV_B
DH_B = D_MODEL // 16
W_B = HQ_B * DH_B
QBLOCK = 128
ROPE_THETA = 10000.0
H_C = 16
DH_C = D_MODEL // 16
W_C = H_C * DH_C
WIN_R_MAX = 8
WIN_C = 16
EVEN_SIZES = (W_A, W_A, W_A, W_A, 4 * H_A, W_B, HKV_B * DH_B, HKV_B * DH_B, W_A + W_B)
IN_EVEN = 4 * W_A + 4 * H_A + W_B + 2 * HKV_B * DH_B + W_A + W_B
IN_ODD = 4 * W_C

kernel_name = 'hybrid_mlstm_gqa_natten_diffusion_step'


def offsets(sizes):
    out, acc = [], 0
    for s in sizes[:-1]:
        acc += s
        out.append(acc)
    return out


def rmsnorm(x, g):
    xf = x.astype(jnp.float32)
    y = xf * lax.rsqrt(jnp.mean(xf * xf, axis=-1, keepdims=True) + EPS)
    return (y * g.astype(jnp.float32)).astype(x.dtype)


def modulation(cvec, w, b):
    m = jnp.dot(jax.nn.silu(cvec), w) + b
    return jnp.split(m, 3, axis=-1)


def to_heads(a, n):
    B, T, _ = a.shape
    return a.reshape(B, T, n, -1).transpose(0, 2, 1, 3)


def from_heads(a):
    B, H, T, D = a.shape
    return a.transpose(0, 2, 1, 3).reshape(B, T, H * D)


def axial_rope_tables(n_tok):
    t = jnp.arange(n_tok)
    row = (t // GRID_W).astype(jnp.float32)
    col = (t % GRID_W).astype(jnp.float32)
    quarter = DH_B // 4
    freqs = ROPE_THETA ** (-jnp.arange(quarter, dtype=jnp.float32) / quarter)
    ang_r = row[:, None] * freqs
    ang_c = col[:, None] * freqs
    return (jnp.cos(ang_r), jnp.sin(ang_r), jnp.cos(ang_c), jnp.sin(ang_c))


def rope_half(x, cos, sin):
    x1, x2 = jnp.split(x, 2, axis=-1)
    return jnp.concatenate([x1 * cos - x2 * sin, x1 * sin + x2 * cos], axis=-1)


def axial_rope(x, tables):
    cos_r, sin_r, cos_c, sin_c = (t.astype(x.dtype) for t in tables)
    xr, xc = jnp.split(x, 2, axis=-1)
    return jnp.concatenate([rope_half(xr, cos_r, sin_r), rope_half(xc, cos_c, sin_c)], axis=-1)


def mlstm_scan(q, k, v, ig, fg, C0, n0, m0):
    f32 = jnp.float32
    B, H, T, Dh = q.shape
    L = CHUNK_A
    nc = T // L

    def chunks(a):
        a = a.reshape(B, H, nc, L, *a.shape[3:])
        return jnp.moveaxis(a, 2, 0)

    qc = chunks(q.astype(f32))
    kc = chunks(k.astype(f32) * Dh ** -0.5)
    vc = chunks(v.astype(f32))
    logf = chunks(jax.nn.log_sigmoid(fg.astype(f32)))
    logi = chunks(ig.astype(f32))
    causal = jnp.tril(jnp.ones((L, L), dtype=bool))

    def step(carry, xs):
        C, n, m = carry
        qj, kj, vj, lf, li = xs
        b = jnp.cumsum(lf, axis=-1)
        Dm = jnp.where(causal, b[..., :, None] - b[..., None, :] + li[..., None, :], -jnp.inf)
        m_inter = b + m[..., None]
        m_t = jnp.maximum(m_inter, jnp.max(Dm, axis=-1))
        w_inter = jnp.exp(m_inter - m_t)
        P = jnp.exp(Dm - m_t[..., None]) * jnp.einsum('bhld,bhsd->bhls', qj, kj)
        num = w_inter[..., None] * jnp.einsum('bhvk,bhlk->bhlv', C, qj) + jnp.einsum('bhls,bhsv->bhlv', P, vj)
        den = w_inter * jnp.einsum('bhk,bhlk->bhl', n, qj) + jnp.sum(P, axis=-1)
        h = num / jnp.maximum(jnp.abs(den), jnp.exp(-m_t))[..., None]
        bL = b[..., -1]
        g = bL[..., None] - b + li
        m_new = jnp.maximum(bL + m, jnp.max(g, axis=-1))
        a_st = jnp.exp(bL + m - m_new)
        wk = jnp.exp(g - m_new[..., None])
        C_new = a_st[..., None, None] * C + jnp.einsum('bhs,bhsv,bhsk->bhvk', wk, vj, kj)
        n_new = a_st[..., None] * n + jnp.einsum('bhs,bhsk->bhk', wk, kj)
        return (C_new, n_new, m_new), h

    (C, n, m), hs = lax.scan(step, (C0.astype(f32), n0.astype(f32), m0.astype(f32)), (qc, kc, vc, logf, logi))
    h = jnp.moveaxis(hs, 0, 2).reshape(B, H, T, Dh)
    return h, C, n, m


def block_attend(q, k, v):
    B, KV, G, T, D = q.shape
    nb = T // QBLOCK
    scale = D ** -0.5
    qb = jnp.moveaxis(q.reshape(B, KV, G, nb, QBLOCK, D), 3, 0)

    def blk(qi):
        s = jnp.einsum('bkgqd,bksd->bkgqs', qi, k).astype(jnp.float32) * scale
        p = jax.nn.softmax(s, axis=-1).astype(v.dtype)
        return jnp.einsum('bkgqs,bksd->bkgqd', p, v)

    o = lax.map(blk, qb)
    return jnp.moveaxis(o, 0, 3).reshape(B, KV, G, T, D)


def na_attend(q, k, v, kc, vc, rpb):
    B, H, T, D = q.shape
    rows = T // GRID_W
    wr = min(WIN_R_MAX, rows)
    scale = D ** -0.5
    qg = q.reshape(B, H, rows, GRID_W, D)
    kg = k.reshape(B, H, rows, GRID_W, D)
    vg = v.reshape(B, H, rows, GRID_W, D)
    col = jnp.arange(GRID_W)
    cs = jnp.clip(col - WIN_C // 2, 0, GRID_W - WIN_C)
    col_idx = cs[:, None] + jnp.arange(WIN_C)[None, :]
    rpb_cols = rpb[:, :, col_idx - col[:, None] + WIN_C - 1]

    def row_fn(args):
        r, q_r = args
        rs = jnp.clip(r - wr // 2, 0, rows - wr)
        k_win = lax.dynamic_slice_in_dim(kg, rs, wr, axis=2)[:, :, :, col_idx, :]
        v_win = lax.dynamic_slice_in_dim(vg, rs, wr, axis=2)[:, :, :, col_idx, :]
        bias = rpb_cols[:, rs + jnp.arange(wr) - r + WIN_R_MAX - 1]
        s_loc = (jnp.einsum('bhcd,bhrcwd->bhcrw', q_r, k_win).astype(jnp.float32) * scale
                 + jnp.transpose(bias, (0, 2, 1, 3))[None].astype(jnp.float32))
        s_ctx = jnp.einsum('bhcd,bhpd->bhcp', q_r, kc).astype(jnp.float32) * scale
        p = jax.nn.softmax(jnp.concatenate([s_loc.reshape(B, H, GRID_W, wr * WIN_C), s_ctx], axis=-1), axis=-1)
        p_loc = p[..., :wr * WIN_C].reshape(B, H, GRID_W, wr, WIN_C).astype(v.dtype)
        p_ctx = p[..., wr * WIN_C:].astype(v.dtype)
        return jnp.einsum('bhcrw,bhrcwd->bhcd', p_loc, v_win) + jnp.einsum('bhcp,bhpd->bhcd', p_ctx, vc)

    o = lax.map(row_fn, (jnp.arange(rows), jnp.moveaxis(qg, 2, 0)))
    return jnp.moveaxis(o, 0, 2).reshape(B, H, T, D)


def even_mixer(h, w_in, b_gates, g_hn, g_q, g_k, w_out, st_fwd, st_bwd, kv_ctx, rope):
    B, T, _ = h.shape
    p = jnp.einsum('btd,dn->btn', h, w_in)
    qa, ka, va, oa, gates, qb, kb, vb, z = jnp.split(p, offsets(EVEN_SIZES), axis=-1)
    qa, ka, va, oa = (to_heads(a, H_A) for a in (qa, ka, va, oa))
    g4 = (gates + b_gates).reshape(B, T, 4, H_A).transpose(2, 0, 3, 1)
    ig_f, fg_f, ig_b, fg_b = g4[0], g4[1], g4[2], g4[3]
    rev = lambda a: jnp.flip(a, axis=2)
    h_f, C_f, n_f, m_f = mlstm_scan(qa, ka, va, ig_f, fg_f, *st_fwd)
    h_b, C_b, n_b, m_b = mlstm_scan(rev(qa), rev(ka), rev(va), rev(ig_b), rev(fg_b), *st_bwd)
    ha = rmsnorm(h_f + rev(h_b), g_hn.reshape(H_A, 1, DH_A)) * jax.nn.sigmoid(oa.astype(jnp.float32))
    ha = from_heads(ha.astype(h.dtype))
    qb = rmsnorm(to_heads(qb, HQ_B), g_q)
    kb = rmsnorm(to_heads(kb, HKV_B), g_k)
    vb = to_heads(vb, HKV_B)
    if rope is not None:
        qb = axial_rope(qb, rope)
        kb_pos = axial_rope(kb, rope)
    else:
        kb_pos = kb
    if kv_ctx is None:
        k_all, v_all = kb_pos, vb
    else:
        k_all = jnp.concatenate([kb_pos, kv_ctx[0].astype(kb.dtype)], axis=2)
        v_all = jnp.concatenate([vb, kv_ctx[1].astype(vb.dtype)], axis=2)
    hb = block_attend(qb.reshape(B, HKV_B, G_B, T, DH_B), k_all, v_all).reshape(B, HQ_B, T, DH_B)
    y = jnp.concatenate([ha, from_heads(hb)], axis=-1) * jax.nn.silu(z)
    y = jnp.einsum('btn,nd->btd', y, w_out)
    ctx_tensors = (jnp.stack([C_f, C_b], axis=1), jnp.stack([n_f, n_b], axis=1), jnp.stack([m_f, m_b], axis=1), kb, vb)
    return y, ctx_tensors


def odd_mixer(h, w_in, rpb, w_out, kv_ctx):
    p = jnp.einsum('btd,dn->btn', h, w_in)
    q, k, v, z = jnp.split(p, 4, axis=-1)
    q, k, v = (to_heads(a, H_C) for a in (q, k, v))
    if kv_ctx is None:
        o = block_attend(q[:, :, None], k, v)[:, :, 0]
    else:
        o = na_attend(q, k, v, kv_ctx[0].astype(k.dtype), kv_ctx[1].astype(v.dtype), rpb)
    y = jnp.einsum('btn,nd->btd', from_heads(o) * jax.nn.silu(z), w_out)
    return y, (k, v)


def setup_inputs(seed: int = 0) -> dict:
    key = jax.random.key(seed)
    ks = jax.random.split(key, 32)
    nrm = lambda k, shape, s=1.0: s * jax.random.normal(k, shape, jnp.float32)
    gate_offset = jnp.tile(jnp.repeat(jnp.array([0.0, FORGET_BIAS], jnp.float32), H_A), 2)
    return {
        'x_prompt': nrm(ks[0], (BATCH, SEQ, D_MODEL)),
        'x_sample': nrm(ks[1], (DEC_BATCH, DEC_SEQ, D_MODEL)),
        'state_mlstm_C': nrm(ks[2], (DEC_BATCH, N_EVEN, 2, H_A, DH_A, DH_A), 0.1),
        'state_mlstm_n': nrm(ks[3], (DEC_BATCH, N_EVEN, 2, H_A, DH_A), 0.1),
        'state_mlstm_m': nrm(ks[4], (DEC_BATCH, N_EVEN, 2, H_A)),
        'cache_gqa_k': nrm(ks[5], (DEC_BATCH, N_EVEN, HKV_B, PAST_LEN, DH_B)),
        'cache_gqa_v': nrm(ks[6], (DEC_BATCH, N_EVEN, HKV_B, PAST_LEN, DH_B)),
        'cache_na_k': nrm(ks[7], (DEC_BATCH, N_ODD, H_C, PAST_LEN, DH_C)),
        'cache_na_v': nrm(ks[8], (DEC_BATCH, N_ODD, H_C, PAST_LEN, DH_C)),
        'c': nrm(ks[9], (DEC_BATCH, D_MODEL)),
        'c_ctx': nrm(ks[10], (D_MODEL,)),
        'w_mod': nrm(ks[11], (DEPTH, D_MODEL, 3 * D_MODEL), 0.5 * D_MODEL ** -0.5),
        'b_mod': nrm(ks[12], (DEPTH, 3 * D_MODEL), 0.01),
        'g_pre': 1.0 + nrm(ks[13], (DEPTH, D_MODEL), 0.05),
        'g_post': 1.0 + nrm(ks[14], (DEPTH, D_MODEL), 0.05),
        'w_in_ab': nrm(ks[15], (N_EVEN, D_MODEL, IN_EVEN), D_MODEL ** -0.5),
        'b_gates_ab': gate_offset + nrm(ks[16], (N_EVEN, 4 * H_A), 0.1),
        'g_hnorm_a': 1.0 + nrm(ks[17], (N_EVEN, W_A), 0.05),
        'g_qnorm_b': 1.0 + nrm(ks[18], (N_EVEN, DH_B), 0.05),
        'g_knorm_b': 1.0 + nrm(ks[19], (N_EVEN, DH_B), 0.05),
        'w_out_ab': nrm(ks[20], (N_EVEN, W_A + W_B, D_MODEL), (W_A + W_B) ** -0.5),
        'w_in_c': nrm(ks[21], (N_ODD, D_MODEL, IN_ODD), D_MODEL ** -0.5),
        'rpb_c': nrm(ks[22], (N_ODD, H_C, 2 * WIN_R_MAX - 1, 2 * WIN_C - 1), 0.1),
        'w_out_c': nrm(ks[23], (N_ODD, W_C, D_MODEL), W_C ** -0.5),
    }


def reference(x_prompt, x_sample, state_mlstm_C, state_mlstm_n, state_mlstm_m, cache_gqa_k, cache_gqa_v,
              cache_na_k, cache_na_v, c, c_ctx, w_mod, b_mod, g_pre, g_post, w_in_ab, b_gates_ab, g_hnorm_a,
              g_qnorm_b, g_knorm_b, w_out_ab, w_in_c, rpb_c, w_out_c):
    f32 = jnp.float32
    rope = axial_rope_tables(x_sample.shape[1])
    bp = x_prompt.shape[0]
    xp, xs = x_prompt, x_sample
    new_C, new_n, new_m, new_gk, new_gv, new_nk, new_nv = [], [], [], [], [], [], []
    for layer in range(DEPTH):
        sh_p, sc_p, gt_p = modulation(c_ctx, w_mod[layer], b_mod[layer])
        sh_s, sc_s, gt_s = (a[:, None, :] for a in modulation(c, w_mod[layer], b_mod[layer]))
        hp = rmsnorm(xp, g_pre[layer]) * (1.0 + sc_p) + sh_p
        hs = rmsnorm(xs, g_pre[layer]) * (1.0 + sc_s) + sh_s
        j = layer // 2
        if layer % 2 == 0:
            w = (w_in_ab[j], b_gates_ab[j], g_hnorm_a[j], g_qnorm_b[j], g_knorm_b[j], w_out_ab[j])
            zero = (jnp.zeros((bp, H_A, DH_A, DH_A), f32), jnp.zeros((bp, H_A, DH_A), f32), jnp.zeros((bp, H_A), f32))
            yp, (sC, sn, sm, kb, vb) = even_mixer(hp, *w, zero, zero, None, None)
            st_f = (state_mlstm_C[:, j, 0], state_mlstm_n[:, j, 0], state_mlstm_m[:, j, 0])
            st_b = (state_mlstm_C[:, j, 1], state_mlstm_n[:, j, 1], state_mlstm_m[:, j, 1])
            ys, _ = even_mixer(hs, *w, st_f, st_b, (cache_gqa_k[:, j], cache_gqa_v[:, j]), rope)
            new_C.append(sC)
            new_n.append(sn)
            new_m.append(sm)
            new_gk.append(kb)
            new_gv.append(vb)
        else:
            yp, (kc, vc) = odd_mixer(hp, w_in_c[j], rpb_c[j], w_out_c[j], None)
            ys, _ = odd_mixer(hs, w_in_c[j], rpb_c[j], w_out_c[j], (cache_na_k[:, j], cache_na_v[:, j]))
            new_nk.append(kc)
            new_nv.append(vc)
        xp = xp + gt_p * rmsnorm(yp, g_post[layer])
        xs = xs + gt_s * rmsnorm(ys, g_post[layer])
    dt = x_prompt.dtype
    out_C = jnp.stack(new_C, axis=1).astype(dt)
    out_n = jnp.stack(new_n, axis=1).astype(dt)
    out_m = jnp.stack(new_m, axis=1).astype(dt)
    out_gk = jnp.stack(new_gk, axis=1).astype(dt)
    out_gv = jnp.stack(new_gv, axis=1).astype(dt)
    out_nk = jnp.stack(new_nk, axis=1).astype(dt)
    out_nv = jnp.stack(new_nv, axis=1).astype(dt)
    return (xp, xs, out_C, out_n, out_m, out_gk, out_gv, out_nk, out_nv)
```

```python
import functools

import jax
import jax.numpy as jnp
import numpy as np
from jax import lax
from jax.experimental import pallas as pl
from jax.experimental.pallas import tpu as pltpu

F32 = jnp.float32
BF16 = jnp.bfloat16

D_MODEL = 1024
GRID_W = 64
EPS = 1e-6
H_A = 4
DH_A = 128
W_A = H_A * DH_A
CHUNK_A = 128
HQ_B = 8
HKV_B = 2
DH_B = 64
W_B = HQ_B * DH_B
ROPE_THETA = 10000.0
H_C = 16
DH_C = 64
W_C = H_C * DH_C
WIN_R = 8
WIN_C = 16

LANES = 128
NA_ROWS = 4
NA_WIN = NA_ROWS + WIN_R - 1
NEG = -1e30
VMEM_LIMIT = 56 * 1024 * 1024


def _cparams(sem):
    return pltpu.CompilerParams(dimension_semantics=sem, vmem_limit_bytes=VMEM_LIMIT)


def _silu(x):
    return x / (1.0 + jnp.exp(-x))


def _sigmoid(x):
    return 1.0 / (1.0 + jnp.exp(-x))


def _log_sigmoid(x):
    return jnp.minimum(x, 0.0) - jnp.log1p(jnp.exp(-jnp.abs(x)))


def _dot_nt(a, b):
    return lax.dot_general(a, b, (((1,), (1,)), ((), ())), preferred_element_type=F32)


def _mod_kernel(c_ref, w_ref, b_ref, o_ref):
    s = _silu(c_ref[...])
    o_ref[0] = jnp.dot(s, w_ref[0], preferred_element_type=F32,
                       precision=lax.Precision.HIGHEST) + b_ref[0]


def _modulation(cvec, w_mod, b_mod):
    depth, d, n = w_mod.shape
    tn = n // 4
    return pl.pallas_call(
        _mod_kernel,
        grid=(depth, n // tn),
        in_specs=[pl.BlockSpec((8, d), lambda l, j: (0, 0)),
                  pl.BlockSpec((1, d, tn), lambda l, j: (l, 0, j)),
                  pl.BlockSpec((1, 1, tn), lambda l, j: (l, 0, j))],
        out_specs=pl.BlockSpec((1, 8, tn), lambda l, j: (l, 0, j)),
        out_shape=jax.ShapeDtypeStruct((depth, 8, n), F32),
        compiler_params=_cparams(("arbitrary", "arbitrary")),
        name="modulation",
    )(cvec, w_mod, b_mod.reshape(depth, 1, n))


def _proj_kernel(*refs, n_seg, with_gates, row0, per_batch):
    x_ref, g_ref, mod_ref = refs[:3]
    w_refs = refs[3:3 + n_seg]
    pos = 3 + n_seg
    if with_gates:
        wg_ref, wgt_ref, bg_ref, bgt_ref = refs[pos:pos + 4]
        pos += 4
    o_refs = refs[pos:pos + n_seg]
    pos += n_seg
    d = x_ref.shape[-1]
    row = row0 + (pl.program_id(0) if per_batch else 0)
    shift = mod_ref[pl.ds(row, 1), 0:d]
    scale = mod_ref[pl.ds(row, 1), d:2 * d]
    x = x_ref[0]
    r = lax.rsqrt(jnp.mean(x * x, axis=-1, keepdims=True) + EPS)
    h = (x * r * g_ref[...]) * (1.0 + scale) + shift
    hb = h.astype(BF16)
    for w_ref, o_ref in zip(w_refs, o_refs):
        o_ref[0] = jnp.dot(hb, w_ref[...], preferred_element_type=F32).astype(o_ref.dtype)
    if with_gates:
        go_ref, gto_ref = refs[pos:pos + 2]
        go_ref[0] = jnp.dot(hb, wg_ref[...], preferred_element_type=F32) + bg_ref[...]
        gto_ref[0] = _dot_nt(wgt_ref[...], hb) + bgt_ref[...]


def _proj(x, g_pre, mod, weights, out_dtypes, gates, *, row0, per_batch, tm):
    bsz, t, d = x.shape
    n_seg = len(weights)
    grid = (bsz, t // tm)
    in_specs = [pl.BlockSpec((1, tm, d), lambda b, i: (b, i, 0)),
                pl.BlockSpec((1, d), lambda b, i: (0, 0)),
                pl.BlockSpec(mod.shape, lambda b, i: (0, 0))]
    args = [x, g_pre.reshape(1, d), mod]
    for w in weights:
        in_specs.append(pl.BlockSpec(w.shape, lambda b, i: (0, 0)))
        args.append(w)
    out_specs, out_shape = [], []
    for w, dt in zip(weights, out_dtypes):
        n = w.shape[1]
        out_specs.append(pl.BlockSpec((1, tm, n), lambda b, i: (b, i, 0)))
        out_shape.append(jax.ShapeDtypeStruct((bsz, t, n), dt))
    if gates is not None:
        w_g, b_g = gates
        ng = w_g.shape[1]
        in_specs += [pl.BlockSpec((d, ng), lambda b, i: (0, 0)),
                     pl.BlockSpec((ng, d), lambda b, i: (0, 0)),
                     pl.BlockSpec((1, ng), lambda b, i: (0, 0)),
                     pl.BlockSpec((ng, 1), lambda b, i: (0, 0))]
        args += [w_g, w_g.T, b_g.reshape(1, ng), b_g.reshape(ng, 1)]
        out_specs += [pl.BlockSpec((1, tm, ng), lambda b, i: (b, i, 0)),
                      pl.BlockSpec((1, ng, tm), lambda b, i: (b, 0, i))]
        out_shape += [jax.ShapeDtypeStruct((bsz, t, ng), F32),
                      jax.ShapeDtypeStruct((bsz, ng, t), F32)]
    kern = functools.partial(_proj_kernel, n_seg=n_seg, with_gates=gates is not None,
                             row0=row0, per_batch=per_batch)
    return pl.pallas_call(
        kern, grid=grid, in_specs=in_specs, out_specs=out_specs, out_shape=out_shape,
        compiler_params=_cparams(("arbitrary", "arbitrary")), name="in_proj",
    )(*args)


def _outproj_kernel(*refs, n_in, row0, per_batch):
    a_refs = refs[:n_in]
    z_ref, w_ref, x_ref, gp_ref, mod_ref, o_ref = refs[n_in:n_in + 6]
    d = x_ref.shape[-1]
    row = row0 + (pl.program_id(0) if per_batch else 0)
    gate = mod_ref[pl.ds(row, 1), 2 * d:3 * d]
    z = z_ref[0].astype(F32)
    sz = _silu(z)
    acc = None
    off = 0
    for a_ref in a_refs:
        kk = a_ref.shape[-1]
        y = (a_ref[0].astype(F32) * sz[:, off:off + kk]).astype(BF16)
        part = jnp.dot(y, w_ref[off:off + kk, :], preferred_element_type=F32)
        acc = part if acc is None else acc + part
        off += kk
    r = lax.rsqrt(jnp.mean(acc * acc, axis=-1, keepdims=True) + EPS)
    o_ref[0] = x_ref[0] + gate * (acc * r * gp_ref[...])


def _outproj(parts, z, w_out, x, g_post, mod, *, row0, per_batch, tm):
    bsz, t, d = x.shape
    grid = (bsz, t // tm)
    in_specs, args = [], []
    for a in parts:
        in_specs.append(pl.BlockSpec((1, tm, a.shape[-1]), lambda b, i: (b, i, 0)))
        args.append(a)
    in_specs += [pl.BlockSpec((1, tm, z.shape[-1]), lambda b, i: (b, i, 0)),
                 pl.BlockSpec(w_out.shape, lambda b, i: (0, 0)),
                 pl.BlockSpec((1, tm, d), lambda b, i: (b, i, 0)),
                 pl.BlockSpec((1, d), lambda b, i: (0, 0)),
                 pl.BlockSpec(mod.shape, lambda b, i: (0, 0))]
    args += [z, w_out, x, g_post.reshape(1, d), mod]
    kern = functools.partial(_outproj_kernel, n_in=len(parts), row0=row0, per_batch=per_batch)
    return pl.pallas_call(
        kern, grid=grid, in_specs=in_specs,
        out_specs=pl.BlockSpec((1, tm, d), lambda b, i: (b, i, 0)),
        out_shape=jax.ShapeDtypeStruct((bsz, t, d), F32),
        compiler_params=_cparams(("arbitrary", "arbitrary")), name="out_proj",
    )(*args)


def _mlstm_chunk(q, k, v, li_c, lf_c, li_r, lf_r, c_st, n_st, m_st, reverse):
    ll, dh = q.shape
    kscale = dh ** -0.5
    ti = lax.broadcasted_iota(jnp.int32, (ll, ll), 0)
    si = lax.broadcasted_iota(jnp.int32, (ll, ll), 1)
    allowed = (si >= ti) if reverse else (si <= ti)
    allowed_t = (ti >= si) if reverse else (ti <= si)
    b_c = jnp.sum(jnp.where(allowed, lf_r, 0.0), axis=-1, keepdims=True)
    b_r = jnp.sum(jnp.where(allowed_t, lf_c, 0.0), axis=0, keepdims=True)
    tot = jnp.sum(lf_r, axis=-1, keepdims=True)
    dm = jnp.where(allowed, b_c - b_r + li_r, -jnp.inf)
    m_inter = b_c + m_st
    m_t = jnp.maximum(m_inter, jnp.max(dm, axis=-1, keepdims=True))
    w_inter = jnp.exp(m_inter - m_t)
    p = jnp.exp(dm - m_t) * (_dot_nt(q, k) * kscale)
    cq = _dot_nt(q, c_st.astype(BF16))
    num = w_inter * cq + jnp.dot(p.astype(BF16), v, preferred_element_type=F32)
    qn = jnp.sum(q.astype(F32) * n_st, axis=-1, keepdims=True)
    den = w_inter * qn + jnp.sum(p, axis=-1, keepdims=True)
    h = num / jnp.maximum(jnp.abs(den), jnp.exp(-m_t))
    g_r = tot - b_r + li_r
    g_c = tot - b_c + li_c
    m_new = jnp.maximum(tot + m_st, jnp.max(g_r, axis=-1, keepdims=True))
    a_st = jnp.exp(tot + m_st - m_new)
    wk_c = jnp.exp(g_c - m_new) * kscale
    vw_t = (v.astype(F32) * wk_c).T.astype(BF16)
    c_new = a_st * c_st + jnp.dot(vw_t, k, preferred_element_type=F32)
    n_new = a_st * n_st + jnp.sum(k.astype(F32) * wk_c, axis=0, keepdims=True)
    return h, c_new, n_new, m_new


def _mlstm_kernel(*refs, nc, has_init, emit_state):
    q_ref, k_ref, v_ref, o_ref, gt_ref, gtt_ref, ghn_ref = refs[:7]
    pos = 7
    if has_init:
        c0_ref, n0_ref, m0_ref = refs[pos:pos + 3]
        pos += 3
    ha_ref = refs[pos]
    pos += 1
    if emit_state:
        cout_ref, nout_ref, mout_ref = refs[pos:pos + 3]
        pos += 3
    hs_sc, c_sc, n_sc, m_sc = refs[pos:pos + 4]
    ll = CHUNK_A
    hd = pl.program_id(1)

    for d in range(2):
        if has_init:
            c_sc[d] = c0_ref[0, d, 0]
            n_sc[d] = n0_ref[0, 0, d:d + 1, :]
            m_sc[d] = m0_ref[0, 0, d:d + 1, :]
        else:
            c_sc[d] = jnp.zeros((DH_A, DH_A), F32)
            n_sc[d] = jnp.zeros((1, DH_A), F32)
            m_sc[d] = jnp.zeros((1, DH_A), F32)

    lane16 = lax.broadcasted_iota(jnp.int32, (ll, 4 * H_A), 1)

    def run_dir(d, c, first_touch):
        start = pl.multiple_of(c * ll, ll)
        rows = pl.ds(start, ll)
        q = q_ref[0, rows, :]
        k = k_ref[0, rows, :]
        v = v_ref[0, rows, :]
        gts = gt_ref[0, rows, :]
        col_i = 2 * H_A * d + hd
        col_f = col_i + H_A
        ig_c = jnp.sum(jnp.where(lane16 == col_i, gts, 0.0), axis=-1, keepdims=True)
        fg_c = jnp.sum(jnp.where(lane16 == col_f, gts, 0.0), axis=-1, keepdims=True)
        ig_r = gtt_ref[0, pl.ds(col_i, 1), pl.ds(c, 1), :].reshape(1, ll)
        fg_r = gtt_ref[0, pl.ds(col_f, 1), pl.ds(c, 1), :].reshape(1, ll)
        h, c_new, n_new, m_new = _mlstm_chunk(
            q, k, v, ig_c, _log_sigmoid(fg_c), ig_r, _log_sigmoid(fg_r),
            c_sc[d], n_sc[d], m_sc[d][:, 0:1], reverse=(d == 1))
        c_sc[d] = c_new
        n_sc[d] = n_new
        m_sc[d] = jnp.broadcast_to(m_new, (1, DH_A))
        if first_touch:
            hs_sc[rows, :] = h
        else:
            hsum = h + hs_sc[rows, :]
            r = lax.rsqrt(jnp.mean(hsum * hsum, axis=-1, keepdims=True) + EPS)
            out = (hsum * r * ghn_ref[...]) * _sigmoid(o_ref[0, rows, :].astype(F32))
            ha_ref[0, rows, :] = out.astype(ha_ref.dtype)

    def make_body(first_touch):
        def body(j, carry):
            run_dir(0, j, first_touch)
            run_dir(1, nc - 1 - j, first_touch)
            return carry
        return body

    half = nc // 2
    lax.fori_loop(0, half, make_body(True), 0)
    lax.fori_loop(half, nc, make_body(False), 0)

    if emit_state:
        for d in range(2):
            cout_ref[0, 0, d, 0] = c_sc[d]
            nout_ref[0, 0, d:d + 1, :] = n_sc[d]
            mout_ref[0, 0, d:d + 1, :] = m_sc[d]


def _mlstm(pa, gates, gates_t, g_hn, init, *, emit_state):
    bsz, t, _ = pa.shape
    nc = t // CHUNK_A
    assert nc % 2 == 0
    gtt = gates_t.reshape(bsz, 4 * H_A, nc, CHUNK_A)

    def col(off):
        return pl.BlockSpec((1, t, DH_A), lambda b, h: (b, 0, off + h))

    in_specs = [col(0), col(H_A), col(2 * H_A), col(3 * H_A),
                pl.BlockSpec((1, t, 4 * H_A), lambda b, h: (b, 0, 0)),
                pl.BlockSpec((1, 4 * H_A, nc, CHUNK_A), lambda b, h: (b, 0, 0, 0)),
                pl.BlockSpec((1, DH_A), lambda b, h: (0, h))]
    args = [pa, pa, pa, pa, gates, gtt, g_hn.reshape(1, W_A)]
    if init is not None:
        c0, n0, m0 = init
        in_specs += [pl.BlockSpec((1, 2, 1, DH_A, DH_A), lambda b, h: (b, 0, h, 0, 0)),
                     pl.BlockSpec((1, 1, 2, DH_A), lambda b, h: (b, h, 0, 0)),
                     pl.BlockSpec((1, 1, 2, DH_A), lambda b, h: (b, h, 0, 0))]
        args += [c0, n0, m0]
    out_specs = [pl.BlockSpec((1, t, DH_A), lambda b, h: (b, 0, h))]
    out_shape = [jax.ShapeDtypeStruct((bsz, t, W_A), BF16)]
    if emit_state:
        out_specs += [pl.BlockSpec((1, 1, 2, 1, DH_A, DH_A), lambda b, h: (b, 0, 0, h, 0, 0)),
                      pl.BlockSpec((1, 1, 2, DH_A), lambda b, h: (b, h, 0, 0)),
                      pl.BlockSpec((1, 1, 2, DH_A), lambda b, h: (b, h, 0, 0))]
        out_shape += [jax.ShapeDtypeStruct((bsz, 1, 2, H_A, DH_A, DH_A), F32),
                      jax.ShapeDtypeStruct((bsz, H_A, 2, DH_A), F32),
                      jax.ShapeDtypeStruct((bsz, H_A, 2, DH_A), F32)]
    kern = functools.partial(_mlstm_kernel, nc=nc, has_init=init is not None, emit_state=emit_state)
    return pl.pallas_call(
        kern, grid=(bsz, H_A), in_specs=in_specs, out_specs=out_specs, out_shape=out_shape,
        scratch_shapes=[pltpu.VMEM((t, DH_A), F32), pltpu.VMEM((2, DH_A, DH_A), F32),
                        pltpu.VMEM((2, 1, DH_A), F32), pltpu.VMEM((2, 1, DH_A), F32)],
        compiler_params=_cparams(("arbitrary", "arbitrary")), name="mlstm",
    )(*args)


def _head_norm(x, g):
    lo = lax.broadcasted_iota(jnp.int32, x.shape, 1) < DH_B
    x2 = x * x
    s_lo = jnp.sum(jnp.where(lo, x2, 0.0), axis=-1, keepdims=True)
    s_hi = jnp.sum(jnp.where(lo, 0.0, x2), axis=-1, keepdims=True)
    ms = jnp.where(lo, s_lo, s_hi) * (1.0 / DH_B)
    return x * lax.rsqrt(ms + EPS) * g


def _rope(x, cos, sin):
    quarter = DH_B // 4
    first = (lax.broadcasted_iota(jnp.int32, x.shape, 1) % (2 * quarter)) < quarter
    partner = jnp.where(first, pltpu.roll(x, LANES - quarter, 1), pltpu.roll(x, quarter, 1))
    return x * cos + partner * sin


def _qkprep_kernel(*refs, rope, emit_kn):
    qkv_ref, gq_ref, gk_ref = refs[:3]
    pos = 3
    if rope:
        cos_ref, sin_ref = refs[pos:pos + 2]
        pos += 2
    q_out, k_out = refs[pos:pos + 2]
    pos += 2
    nq = W_B // LANES
    for j in range(nq + 1):
        x = qkv_ref[0, :, j * LANES:(j + 1) * LANES]
        xn = _head_norm(x, gq_ref[...] if j < nq else gk_ref[...])
        if j == nq and emit_kn:
            refs[pos][0] = xn
        if rope:
            xn = _rope(xn, cos_ref[...], sin_ref[...])
        if j < nq:
            q_out[0, :, j * LANES:(j + 1) * LANES] = xn.astype(q_out.dtype)
        else:
            k_out[0] = xn.astype(k_out.dtype)


def _qkprep(qkv, g_q, g_k, rope_tabs, *, emit_kn, tm):
    bsz, t, w = qkv.shape
    in_specs = [pl.BlockSpec((1, tm, w), lambda b, i: (b, i, 0)),
                pl.BlockSpec((1, LANES), lambda b, i: (0, 0)),
                pl.BlockSpec((1, LANES), lambda b, i: (0, 0))]
    args = [qkv, jnp.tile(g_q, 2).reshape(1, LANES), jnp.tile(g_k, 2).reshape(1, LANES)]
    if rope_tabs is not None:
        in_specs += [pl.BlockSpec((tm, LANES), lambda b, i: (i, 0))] * 2
        args += list(rope_tabs)
    out_specs = [pl.BlockSpec((1, tm, W_B), lambda b, i: (b, i, 0)),
                 pl.BlockSpec((1, tm, LANES), lambda b, i: (b, i, 0))]
    out_shape = [jax.ShapeDtypeStruct((bsz, t, W_B), BF16),
                 jax.ShapeDtypeStruct((bsz, t, LANES), BF16)]
    if emit_kn:
        out_specs.append(pl.BlockSpec((1, tm, LANES), lambda b, i: (b, i, 0)))
        out_shape.append(jax.ShapeDtypeStruct((bsz, t, LANES), F32))
    kern = functools.partial(_qkprep_kernel, rope=rope_tabs is not None, emit_kn=emit_kn)
    return pl.pallas_call(
        kern, grid=(bsz, t // tm), in_specs=in_specs, out_specs=out_specs, out_shape=out_shape,
        compiler_params=_cparams(("arbitrary", "arbitrary")), name="qk_prep",
    )(*args)


def _rope_tables(n_tok):
    t = jnp.arange(n_tok)
    row = (t // GRID_W).astype(F32)
    colp = (t % GRID_W).astype(F32)
    quarter = DH_B // 4
    freqs = ROPE_THETA ** (-jnp.arange(quarter, dtype=F32) / quarter)
    ar = row[:, None] * freqs
    ac = colp[:, None] * freqs
    cos = jnp.concatenate([jnp.cos(ar), jnp.cos(ar), jnp.cos(ac), jnp.cos(ac)], axis=-1)
    sin = jnp.concatenate([-jnp.sin(ar), jnp.sin(ar), -jnp.sin(ac), jnp.sin(ac)], axis=-1)
    return jnp.tile(cos, (1, 2)), jnp.tile(sin, (1, 2))


def _attn_kernel(q_ref, k_ref, v_ref, o_ref, *, scale):
    q = q_ref[0] * scale
    k = k_ref[0].astype(BF16)
    v = v_ref[0].astype(BF16)
    lo = lax.broadcasted_iota(jnp.int32, q.shape, 1) < (LANES // 2)
    outs = []
    for part in (lo, jnp.logical_not(lo)):
        qm = jnp.where(part, q, jnp.zeros_like(q))
        s = _dot_nt(qm, k)
        m = jnp.max(s, axis=-1, keepdims=True)
        p = jnp.exp(s - m)
        l = jnp.sum(p, axis=-1, keepdims=True)
        outs.append(jnp.dot(p.astype(BF16), v, preferred_element_type=F32) / l)
    o_ref[0] = jnp.where(lo, outs[0], outs[1]).astype(o_ref.dtype)


def _attention(q, k, v, *, kv_shared, tq):
    bsz, t, w = q.shape
    s = k.shape[1]
    scale = DH_B ** -0.5
    assert scale == 0.125
    kv_idx = (lambda b, i, j: (b, 0, 0)) if kv_shared else (lambda b, i, j: (b, 0, j))
    return pl.pallas_call(
        functools.partial(_attn_kernel, scale=scale),
        grid=(bsz, t // tq, w // LANES),
        in_specs=[pl.BlockSpec((1, tq, LANES), lambda b, i, j: (b, i, j)),
                  pl.BlockSpec((1, s, LANES), kv_idx),
                  pl.BlockSpec((1, s, LANES), kv_idx)],
        out_specs=pl.BlockSpec((1, tq, LANES), lambda b, i, j: (b, i, j)),
        out_shape=jax.ShapeDtypeStruct((bsz, t, w), BF16),
        compiler_params=_cparams(("arbitrary", "arbitrary", "arbitrary")), name="attention",
    )(q, k, v)


def _na_bias_blocks(variant):
    out = {}
    for qr in range(NA_ROWS):
        for kr in range(NA_WIN):
            if variant == 0:
                dr = kr - qr if kr < WIN_R else None
            elif variant == 1:
                dr = kr - qr - WIN_R // 2 if qr <= kr < qr + WIN_R else None
            else:
                dr = kr - qr - (WIN_R - 1) if kr >= NA_WIN - WIN_R else None
            out[(qr, kr)] = None if dr is None else dr + WIN_R - 1
    return out


def _na_kernel(q_ref, k_ref, v_ref, kc_ref, vc_ref, bc_ref, o_ref, bias_sc, *, scale, n_rb):
    rb = pl.program_id(2)
    w = GRID_W

    def build(variant):
        blocks = _na_bias_blocks(variant)
        for hh in range(2):
            for (qr, kr), di in blocks.items():
                if di is None:
                    val = jnp.full((w, w), NEG, F32)
                else:
                    val = bc_ref[hh, di]
                bias_sc[hh, qr * w:(qr + 1) * w, kr * w:(kr + 1) * w] = val

    for variant, step in ((0, 0), (1, 1), (2, n_rb - 1)):
        @pl.when(rb == step)
        def _(variant=variant):
            build(variant)

    n_rows = n_rb * NA_ROWS
    ws = jnp.clip(rb * NA_ROWS - WIN_R // 2, 0, n_rows - NA_WIN)
    start = pl.multiple_of(ws * w, w)
    kw = k_ref[0, pl.ds(start, NA_WIN * w), :]
    vw = v_ref[0, pl.ds(start, NA_WIN * w), :]
    kc = kc_ref[0]
    vc = vc_ref[0]
    q = q_ref[0] * scale
    lo = lax.broadcasted_iota(jnp.int32, q.shape, 1) < (LANES // 2)
    outs = []
    for hh, part in enumerate((lo, jnp.logical_not(lo))):
        qm = jnp.where(part, q, jnp.zeros_like(q))
        s_loc = _dot_nt(qm, kw) + bias_sc[hh]
        s_ctx = _dot_nt(qm, kc)
        m = jnp.maximum(jnp.max(s_loc, axis=-1, keepdims=True), jnp.max(s_ctx, axis=-1, keepdims=True))
        p_loc = jnp.exp(s_loc - m)
        p_ctx = jnp.exp(s_ctx - m)
        l = jnp.sum(p_loc, axis=-1, keepdims=True) + jnp.sum(p_ctx, axis=-1, keepdims=True)
        o = (jnp.dot(p_loc.astype(BF16), vw, preferred_element_type=F32)
             + jnp.dot(p_ctx.astype(BF16), vc, preferred_element_type=F32))
        outs.append(o / l)
    o_ref[0] = jnp.where(lo, outs[0], outs[1]).astype(o_ref.dtype)


def _na_bias_table(rpb):
    c = np.arange(GRID_W)
    cs = np.clip(c - WIN_C // 2, 0, GRID_W - WIN_C)
    ck = np.arange(GRID_W)
    valid = (ck[None, :] >= cs[:, None]) & (ck[None, :] < cs[:, None] + WIN_C)
    idx = np.clip(ck[None, :] - c[:, None] + WIN_C - 1, 0, 2 * WIN_C - 2)
    onehot = (idx[..., None] == np.arange(2 * WIN_C - 1)).astype(np.float32)
    tab = jnp.einsum('hrx,ckx->hrck', rpb, jnp.asarray(onehot), precision=lax.Precision.HIGHEST)
    return jnp.where(jnp.asarray(valid), tab, NEG)


def _na_attention(q, k, v, kc, vc, rpb):
    bsz, t, w = q.shape
    p = kc.shape[1]
    n_rb = t // (NA_ROWS * GRID_W)
    tq = NA_ROWS * GRID_W
    scale = DH_C ** -0.5
    assert scale == 0.125
    bc = _na_bias_table(rpb)
    return pl.pallas_call(
        functools.partial(_na_kernel, scale=scale, n_rb=n_rb),
        grid=(bsz, w // LANES, n_rb),
        in_specs=[pl.BlockSpec((1, tq, LANES), lambda b, j, r: (b, r, j)),
                  pl.BlockSpec((1, t, LANES), lambda b, j, r: (b, 0, j)),
                  pl.BlockSpec((1, t, LANES), lambda b, j, r: (b, 0, j)),
                  pl.BlockSpec((1, p, LANES), lambda b, j, r: (b, 0, j)),
                  pl.BlockSpec((1, p, LANES), lambda b, j, r: (b, 0, j)),
                  pl.BlockSpec((2, 2 * WIN_R - 1, GRID_W, GRID_W), lambda b, j, r: (j, 0, 0, 0))],
        out_specs=pl.BlockSpec((1, tq, LANES), lambda b, j, r: (b, r, j)),
        out_shape=jax.ShapeDtypeStruct((bsz, t, w), BF16),
        scratch_shapes=[pltpu.VMEM((2, tq, NA_WIN * GRID_W), F32)],
        compiler_params=_cparams(("arbitrary", "arbitrary", "arbitrary")), name="na_attention",
    )(q, k, v, kc, vc, bc)


_GQA_PERM = np.array([0, 4, 1, 5, 2, 6, 3, 7])


def _perm_heads(w, axis):
    shp = w.shape
    n = shp[axis] // DH_B
    w = w.reshape(shp[:axis] + (n, DH_B) + shp[axis + 1:])
    w = jnp.take(w, jnp.asarray(_GQA_PERM), axis=axis)
    return w.reshape(shp)


def _tok_major(cache):
    b, h, p, dh = cache.shape
    return cache.transpose(0, 2, 1, 3).reshape(b, p, h * dh)


def _head_major(x, n_heads):
    b, t, w = x.shape
    return x.reshape(b, t, n_heads, w // n_heads).transpose(0, 2, 1, 3)


def _even_layer(xp, xs, mod, g_pre, g_post, w_in, b_gates, g_hn, g_q, g_k, w_out, st_c, st_n, st_m,
                ck, cv, rope_tabs):
    o_q, o_k, o_v, o_o = 0, W_A, 2 * W_A, 3 * W_A
    o_g = 4 * W_A
    o_qb = o_g + 4 * H_A
    o_kb = o_qb + W_B
    o_vb = o_kb + HKV_B * DH_B
    o_z = o_vb + HKV_B * DH_B
    w_pa = w_in[:, :o_g].astype(BF16)
    w_g = w_in[:, o_g:o_qb].astype(BF16)
    w_qkv = jnp.concatenate([_perm_heads(w_in[:, o_qb:o_kb], 1), w_in[:, o_kb:o_z]], axis=1).astype(BF16)
    w_z = jnp.concatenate([w_in[:, o_z:o_z + W_A], _perm_heads(w_in[:, o_z + W_A:], 1)], axis=1).astype(BF16)
    w_o = jnp.concatenate([w_out[:W_A], _perm_heads(w_out[W_A:], 0)], axis=0).astype(BF16)

    def stream(x, row0, per_batch, init, cache, rope, emit):
        bsz, t, _ = x.shape
        tm = 512
        xf = x if per_batch else x.reshape(1, bsz * t, D_MODEL)
        pa, qkv, z, gts, gtt = _proj(xf, g_pre, mod, [w_pa, w_qkv, w_z], [BF16, F32, BF16], (w_g, b_gates),
                                     row0=row0, per_batch=per_batch, tm=tm)
        if not per_batch:
            pa, qkv, z, gts = (a.reshape(bsz, t, a.shape[-1]) for a in (pa, qkv, z, gts))
            gtt = gtt.reshape(4 * H_A, bsz, t).transpose(1, 0, 2)
        res = _mlstm(pa, gts, gtt, g_hn, init, emit_state=emit)
        ha = res[0]
        prep = _qkprep(qkv, g_q, g_k, rope, emit_kn=emit, tm=min(t, 512))
        qn, kn = prep[0], prep[1]
        vb = qkv[..., W_B + HKV_B * DH_B:]
        if cache is not None:
            k_all = jnp.concatenate([kn, cache[0]], axis=1)
            v_all = jnp.concatenate([vb.astype(BF16), cache[1]], axis=1)
        else:
            k_all, v_all = kn, vb
        hb = _attention(qn, k_all, v_all, kv_shared=True, tq=256)
        y = _outproj([ha.reshape(xf.shape[0], -1, W_A), hb.reshape(xf.shape[0], -1, W_B)],
                     z.reshape(xf.shape[0], -1, W_A + W_B), w_o, xf, g_post, mod,
                     row0=row0, per_batch=per_batch, tm=tm)
        return y.reshape(bsz, t, D_MODEL), res[1:], (prep[2] if emit else None), vb

    yp, st, kn_p, vb_p = stream(xp, 0, False, None, None, None, True)
    n0 = st_n.transpose(0, 2, 1, 3)
    m0 = jnp.broadcast_to(st_m.transpose(0, 2, 1)[..., None], n0.shape)
    cache = (_tok_major(ck).astype(BF16), _tok_major(cv).astype(BF16))
    ys, _, _, _ = stream(xs, 1, True, (st_c, n0, m0), cache, rope_tabs, False)
    c_out, n_out, m_out = st
    new_n = n_out.transpose(0, 2, 1, 3)[:, None]
    new_m = m_out[..., 0].transpose(0, 2, 1)[:, None]
    new_gk = _head_major(kn_p, HKV_B)[:, None]
    new_gv = _head_major(vb_p, HKV_B)[:, None]
    return yp, ys, c_out, new_n, new_m, new_gk, new_gv


def _odd_layer(xp, xs, mod, g_pre, g_post, w_in, rpb, w_out, ck, cv):
    ws = [w_in[:, i * W_C:(i + 1) * W_C].astype(BF16) for i in range(4)]
    w_o = w_out.astype(BF16)
    tm = 512
    bsz, t, _ = xp.shape
    xf = xp.reshape(1, bsz * t, D_MODEL)
    q, k, v, z = _proj(xf, g_pre, mod, ws, [BF16, F32, F32, BF16], None, row0=0, per_batch=False, tm=tm)
    k = k.reshape(bsz, t, W_C)
    v = v.reshape(bsz, t, W_C)
    o = _attention(q.reshape(bsz, t, W_C), k, v, kv_shared=False, tq=t)
    yp = _outproj([o.reshape(1, bsz * t, W_C)], z, w_o, xf, g_post, mod, row0=0, per_batch=False, tm=tm)
    yp = yp.reshape(bsz, t, D_MODEL)
    new_nk = _head_major(k, H_C)[:, None]
    new_nv = _head_major(v, H_C)[:, None]
    q, k, v, z = _proj(xs, g_pre, mod, ws, [BF16, BF16, BF16, BF16], None, row0=1, per_batch=True, tm=tm)
    o = _na_attention(q, k, v, _tok_major(ck).astype(BF16), _tok_major(cv).astype(BF16), rpb)
    ys = _outproj([o], z, w_o, xs, g_post, mod, row0=1, per_batch=True, tm=tm)
    return yp, ys, new_nk, new_nv


def kernel(x_prompt, x_sample, state_mlstm_C, state_mlstm_n, state_mlstm_m, cache_gqa_k, cache_gqa_v,
           cache_na_k, cache_na_v, c, c_ctx, w_mod, b_mod, g_pre, g_post, w_in_ab, b_gates_ab, g_hnorm_a,
           g_qnorm_b, g_knorm_b, w_out_ab, w_in_c, rpb_c, w_out_c):
    depth = w_mod.shape[0]
    assert depth == 2 and c.shape[0] == 2
    cvec = jnp.zeros((8, D_MODEL), F32).at[0].set(c_ctx).at[1:1 + c.shape[0]].set(c)
    mod = _modulation(cvec, w_mod, b_mod)
    rope_tabs = _rope_tables(x_sample.shape[1])
    xp, xs, c_out, n_out, m_out, gk, gv = _even_layer(
        x_prompt, x_sample, mod[0], g_pre[0], g_post[0], w_in_ab[0], b_gates_ab[0], g_hnorm_a[0],
        g_qnorm_b[0], g_knorm_b[0], w_out_ab[0], state_mlstm_C[:, 0], state_mlstm_n[:, 0],
        state_mlstm_m[:, 0], cache_gqa_k[:, 0], cache_gqa_v[:, 0], rope_tabs)
    xp, xs, nk, nv = _odd_layer(xp, xs, mod[1], g_pre[1], g_post[1], w_in_c[0], rpb_c[0], w_out_c[0],
                                cache_na_k[:, 0], cache_na_v[:, 0])
    return (xp, xs, c_out, n_out, m_out, gk, gv, nk, nv)
```

```python
import functools

import jax
import jax.numpy as jnp
import numpy as np
from jax import lax
from jax.experimental import pallas as pl
from jax.experimental.pallas import tpu as pltpu

F32 = jnp.float32
BF16 = jnp.bfloat16

D_MODEL = 1024
GRID_W = 64
EPS = 1e-6
H_A = 4
DH_A = 128
W_A = H_A * DH_A
CHUNK_A = 128
HQ_B = 8
HKV_B = 2
DH_B = 64
W_B = HQ_B * DH_B
ROPE_THETA = 10000.0
H_C = 16
DH_C = 64
W_C = H_C * DH_C
WIN_R = 8
WIN_C = 16

LANES = 128
NA_ROWS = 4
NA_TILES = 2
NA_RBS = 2
NA_BLKS = -(-(NA_ROWS + WIN_R - 1) // NA_ROWS)
NA_WIN = NA_BLKS * NA_ROWS
assert WIN_R // 2 == NA_ROWS
BF16_ROWS = 16
KEY_CHUNK = 256
QK_AHEAD = 3
LOG2E = 1.4426950408889634
QSCALE = DH_B ** -0.5 * LOG2E
assert DH_B == DH_C
MASKED = 1e30
VMEM_LIMIT = 56 * 1024 * 1024


def _cparams(sem):
    return pltpu.CompilerParams(dimension_semantics=sem, vmem_limit_bytes=VMEM_LIMIT)


def _silu(x):
    return x / (1.0 + jnp.exp(-x))


def _sigmoid(x):
    return 1.0 / (1.0 + jnp.exp(-x))


def _log_sigmoid(x):
    return jnp.minimum(x, 0.0) - jnp.log1p(jnp.exp(-jnp.abs(x)))


def _dot_nt(a, b):
    return lax.dot_general(a, b, (((1,), (1,)), ((), ())), preferred_element_type=F32)


def _mod_kernel(c_ref, w_ref, b_ref, o_ref):
    s = _silu(c_ref[...])
    o_ref[0] = jnp.dot(s, w_ref[0], preferred_element_type=F32,
                       precision=lax.Precision.HIGHEST) + b_ref[0]


def _modulation(cvec, w_mod, b_mod):
    depth, d, n = w_mod.shape
    tn = n // 4
    return pl.pallas_call(
        _mod_kernel,
        grid=(depth, n // tn),
        in_specs=[pl.BlockSpec((8, d), lambda l, j: (0, 0)),
                  pl.BlockSpec((1, d, tn), lambda l, j: (l, 0, j)),
                  pl.BlockSpec((1, 1, tn), lambda l, j: (l, 0, j))],
        out_specs=pl.BlockSpec((1, 8, tn), lambda l, j: (l, 0, j)),
        out_shape=jax.ShapeDtypeStruct((depth, 8, n), F32),
        compiler_params=_cparams(("arbitrary", "arbitrary")),
        name="modulation",
    )(cvec, w_mod, b_mod.reshape(depth, 1, n))


def _proj_kernel(*refs, n_seg, t_blocks, with_gates, row0, per_batch):
    n_tseg = len(t_blocks)
    x_ref, g_ref, mod_ref = refs[:3]
    w_refs = refs[3:3 + n_seg]
    wt_refs = refs[3 + n_seg:3 + n_seg + n_tseg]
    pos = 3 + n_seg + n_tseg
    if with_gates:
        wg_ref, wgt_ref, bg_ref, bgt_ref = refs[pos:pos + 4]
        pos += 4
    o_refs = refs[pos:pos + n_seg]
    pos += n_seg
    ot_refs = refs[pos:pos + n_tseg]
    pos += n_tseg
    d = x_ref.shape[-1]
    tm = x_ref.shape[1]
    row = row0 + (pl.program_id(0) if per_batch else 0)
    shift = mod_ref[pl.ds(row, 1), 0:d]
    scale = mod_ref[pl.ds(row, 1), d:2 * d]
    x = x_ref[0]
    r = lax.rsqrt(jnp.mean(x * x, axis=-1, keepdims=True) + EPS)
    h = (x * r * g_ref[...]) * (1.0 + scale) + shift
    hb = h.astype(BF16)
    for w_ref, o_ref in zip(w_refs, o_refs):
        o_ref[0] = jnp.dot(hb, w_ref[...], preferred_element_type=F32).astype(o_ref.dtype)
    for wt_ref, ot_ref, tb in zip(wt_refs, ot_refs, t_blocks):
        res = _dot_nt(wt_ref[...], hb).astype(ot_ref.dtype)
        if tb is None:
            ot_ref[0] = res
        else:
            for i in range(tm // tb):
                ot_ref[0, i] = res[:, i * tb:(i + 1) * tb]
    if with_gates:
        go_ref, gto_ref = refs[pos:pos + 2]
        go_ref[0] = jnp.dot(hb, wg_ref[...], preferred_element_type=F32) + bg_ref[...]
        gto_ref[0] = _dot_nt(wgt_ref[...], hb) + bgt_ref[...]


def _proj(x, g_pre, mod, weights, out_dtypes, t_weights, t_blocks, gates, *, row0, per_batch, tm):
    bsz, t, d = x.shape
    n_seg = len(weights)
    grid = (bsz, t // tm)
    const = lambda b, i: (0, 0)
    in_specs = [pl.BlockSpec((1, tm, d), lambda b, i: (b, i, 0)),
                pl.BlockSpec((1, d), const),
                pl.BlockSpec(mod.shape, const)]
    args = [x, g_pre.reshape(1, d), mod]
    for w in list(weights) + list(t_weights):
        in_specs.append(pl.BlockSpec(w.shape, const))
        args.append(w)
    out_specs, out_shape = [], []
    for w, dt in zip(weights, out_dtypes):
        n = w.shape[1]
        out_specs.append(pl.BlockSpec((1, tm, n), lambda b, i: (b, i, 0)))
        out_shape.append(jax.ShapeDtypeStruct((bsz, t, n), dt))
    for w, tb in zip(t_weights, t_blocks):
        n = w.shape[0]
        if tb is None:
            out_specs.append(pl.BlockSpec((1, n, tm), lambda b, i: (b, 0, i)))
            out_shape.append(jax.ShapeDtypeStruct((bsz, n, t), BF16))
        else:
            out_specs.append(pl.BlockSpec((1, tm // tb, n, tb), lambda b, i: (b, i, 0, 0)))
            out_shape.append(jax.ShapeDtypeStruct((bsz, t // tb, n, tb), BF16))
    if gates is not None:
        w_g, b_g = gates
        ng = w_g.shape[1]
        in_specs += [pl.BlockSpec((d, ng), const), pl.BlockSpec((ng, d), const),
                     pl.BlockSpec((1, ng), const), pl.BlockSpec((ng, 1), const)]
        args += [w_g, w_g.T, b_g.reshape(1, ng), b_g.reshape(ng, 1)]
        out_specs += [pl.BlockSpec((1, tm, ng), lambda b, i: (b, i, 0)),
                      pl.BlockSpec((1, ng, tm), lambda b, i: (b, 0, i))]
        out_shape += [jax.ShapeDtypeStruct((bsz, t, ng), F32),
                      jax.ShapeDtypeStruct((bsz, ng, t), F32)]
    kern = functools.partial(_proj_kernel, n_seg=n_seg, t_blocks=tuple(t_blocks),
                             with_gates=gates is not None, row0=row0, per_batch=per_batch)
    return pl.pallas_call(
        kern, grid=grid, in_specs=in_specs, out_specs=out_specs, out_shape=out_shape,
        compiler_params=_cparams(("arbitrary", "arbitrary")), name="in_proj",
    )(*args)


def _outproj_kernel(*refs, n_in, row0, per_batch):
    a_refs = refs[:n_in]
    z_ref, w_ref, x_ref, gp_ref, mod_ref, o_ref = refs[n_in:n_in + 6]
    d = x_ref.shape[-1]
    row = row0 + (pl.program_id(0) if per_batch else 0)
    gate = mod_ref[pl.ds(row, 1), 2 * d:3 * d]
    z = z_ref[0].astype(F32)
    sz = _silu(z)
    acc = None
    off = 0
    for a_ref in a_refs:
        kk = a_ref.shape[-1]
        y = (a_ref[0].astype(F32) * sz[:, off:off + kk]).astype(BF16)
        part = jnp.dot(y, w_ref[off:off + kk, :], preferred_element_type=F32)
        acc = part if acc is None else acc + part
        off += kk
    r = lax.rsqrt(jnp.mean(acc * acc, axis=-1, keepdims=True) + EPS)
    o_ref[0] = x_ref[0] + gate * (acc * r * gp_ref[...])


def _outproj(parts, z, w_out, x, g_post, mod, *, row0, per_batch, tm):
    bsz, t, d = x.shape
    grid = (bsz, t // tm)
    in_specs, args = [], []
    for a in parts:
        in_specs.append(pl.BlockSpec((1, tm, a.shape[-1]), lambda b, i: (b, i, 0)))
        args.append(a)
    in_specs += [pl.BlockSpec((1, tm, z.shape[-1]), lambda b, i: (b, i, 0)),
                 pl.BlockSpec(w_out.shape, lambda b, i: (0, 0)),
                 pl.BlockSpec((1, tm, d), lambda b, i: (b, i, 0)),
                 pl.BlockSpec((1, d), lambda b, i: (0, 0)),
                 pl.BlockSpec(mod.shape, lambda b, i: (0, 0))]
    args += [z, w_out, x, g_post.reshape(1, d), mod]
    kern = functools.partial(_outproj_kernel, n_in=len(parts), row0=row0, per_batch=per_batch)
    return pl.pallas_call(
        kern, grid=grid, in_specs=in_specs,
        out_specs=pl.BlockSpec((1, tm, d), lambda b, i: (b, i, 0)),
        out_shape=jax.ShapeDtypeStruct((bsz, t, d), F32),
        compiler_params=_cparams(("arbitrary", "arbitrary")), name="out_proj",
    )(*args)


def _mlstm_chunk(q, k, v, li_c, lf_c, li_r, lf_r, c_st, n_st, m_st, reverse):
    ll, dh = q.shape
    kscale = dh ** -0.5
    ti = lax.broadcasted_iota(jnp.int32, (ll, ll), 0)
    si = lax.broadcasted_iota(jnp.int32, (ll, ll), 1)
    allowed = (si >= ti) if reverse else (si <= ti)
    allowed_t = (ti >= si) if reverse else (ti <= si)
    b_c = jnp.sum(jnp.where(allowed, lf_r, 0.0), axis=-1, keepdims=True)
    b_r = jnp.sum(jnp.where(allowed_t, lf_c, 0.0), axis=0, keepdims=True)
    tot = jnp.sum(lf_r, axis=-1, keepdims=True)
    dm = jnp.where(allowed, b_c - b_r + li_r, -jnp.inf)
    m_inter = b_c + m_st
    m_t = jnp.maximum(m_inter, jnp.max(dm, axis=-1, keepdims=True))
    w_inter = jnp.exp(m_inter - m_t)
    p = jnp.exp(dm - m_t) * (_dot_nt(q, k) * kscale)
    cq = _dot_nt(q, c_st.astype(BF16))
    num = w_inter * cq + jnp.dot(p.astype(BF16), v, preferred_element_type=F32)
    qn = jnp.sum(q.astype(F32) * n_st, axis=-1, keepdims=True)
    den = w_inter * qn + jnp.sum(p, axis=-1, keepdims=True)
    h = num / jnp.maximum(jnp.abs(den), jnp.exp(-m_t))
    g_r = tot - b_r + li_r
    g_c = tot - b_c + li_c
    m_new = jnp.maximum(tot + m_st, jnp.max(g_r, axis=-1, keepdims=True))
    a_st = jnp.exp(tot + m_st - m_new)
    wk_c = jnp.exp(g_c - m_new) * kscale
    vw_t = (v.astype(F32) * wk_c).T.astype(BF16)
    c_new = a_st * c_st + jnp.dot(vw_t, k, preferred_element_type=F32)
    n_new = a_st * n_st + jnp.sum(k.astype(F32) * wk_c, axis=0, keepdims=True)
    return h, c_new, n_new, m_new


def _mlstm_kernel(*refs, nc, has_init, emit_state):
    q_ref, k_ref, v_ref, o_ref, gt_ref, gtt_ref, ghn_ref = refs[:7]
    pos = 7
    if has_init:
        c0_ref, n0_ref, m0_ref = refs[pos:pos + 3]
        pos += 3
    ha_ref = refs[pos]
    pos += 1
    if emit_state:
        cout_ref, nout_ref, mout_ref = refs[pos:pos + 3]
        pos += 3
    hs_sc, c_sc, n_sc, m_sc = refs[pos:pos + 4]
    ll = CHUNK_A
    hd = pl.program_id(1)

    for d in range(2):
        if has_init:
            c_sc[d] = c0_ref[0, d, 0]
            n_sc[d] = n0_ref[0, 0, d:d + 1, :]
            m_sc[d] = m0_ref[0, 0, d:d + 1, :]
        else:
            c_sc[d] = jnp.zeros((DH_A, DH_A), F32)
            n_sc[d] = jnp.zeros((1, DH_A), F32)
            m_sc[d] = jnp.zeros((1, DH_A), F32)

    lane16 = lax.broadcasted_iota(jnp.int32, (ll, 4 * H_A), 1)

    def run_dir(d, c, first_touch):
        start = pl.multiple_of(c * ll, ll)
        rows = pl.ds(start, ll)
        q = q_ref[0, rows, :]
        k = k_ref[0, rows, :]
        v = v_ref[0, rows, :]
        gts = gt_ref[0, rows, :]
        col_i = 2 * H_A * d + hd
        col_f = col_i + H_A
        ig_c = jnp.sum(jnp.where(lane16 == col_i, gts, 0.0), axis=-1, keepdims=True)
        fg_c = jnp.sum(jnp.where(lane16 == col_f, gts, 0.0), axis=-1, keepdims=True)
        ig_r = gtt_ref[0, pl.ds(col_i, 1), pl.ds(c, 1), :].reshape(1, ll)
        fg_r = gtt_ref[0, pl.ds(col_f, 1), pl.ds(c, 1), :].reshape(1, ll)
        h, c_new, n_new, m_new = _mlstm_chunk(
            q, k, v, ig_c, _log_sigmoid(fg_c), ig_r, _log_sigmoid(fg_r),
            c_sc[d], n_sc[d], m_sc[d][:, 0:1], reverse=(d == 1))
        c_sc[d] = c_new
        n_sc[d] = n_new
        m_sc[d] = jnp.broadcast_to(m_new, (1, DH_A))
        if first_touch:
            hs_sc[rows, :] = h
        else:
            hsum = h + hs_sc[rows, :]
            r = lax.rsqrt(jnp.mean(hsum * hsum, axis=-1, keepdims=True) + EPS)
            out = (hsum * r * ghn_ref[...]) * _sigmoid(o_ref[0, rows, :].astype(F32))
            ha_ref[0, rows, :] = out.astype(ha_ref.dtype)

    def make_body(first_touch):
        def body(j, carry):
            run_dir(0, j, first_touch)
            run_dir(1, nc - 1 - j, first_touch)
            return carry
        return body

    half = nc // 2
    lax.fori_loop(0, half, make_body(True), 0)
    lax.fori_loop(half, nc, make_body(False), 0)

    if emit_state:
        for d in range(2):
            cout_ref[0, 0, d, 0] = c_sc[d]
            nout_ref[0, 0, d:d + 1, :] = n_sc[d]
            mout_ref[0, 0, d:d + 1, :] = m_sc[d]


def _mlstm(pa, gates, gates_t, g_hn, init, *, emit_state):
    bsz, t, _ = pa.shape
    nc = t // CHUNK_A
    assert nc % 2 == 0
    gtt = gates_t.reshape(bsz, 4 * H_A, nc, CHUNK_A)

    def col(off):
        return pl.BlockSpec((1, t, DH_A), lambda b, h: (b, 0, off + h))

    in_specs = [col(0), col(H_A), col(2 * H_A), col(3 * H_A),
                pl.BlockSpec((1, t, 4 * H_A), lambda b, h: (b, 0, 0)),
                pl.BlockSpec((1, 4 * H_A, nc, CHUNK_A), lambda b, h: (b, 0, 0, 0)),
                pl.BlockSpec((1, DH_A), lambda b, h: (0, h))]
    args = [pa, pa, pa, pa, gates, gtt, g_hn.reshape(1, W_A)]
    if init is not None:
        c0, n0, m0 = init
        in_specs += [pl.BlockSpec((1, 2, 1, DH_A, DH_A), lambda b, h: (b, 0, h, 0, 0)),
                     pl.BlockSpec((1, 1, 2, DH_A), lambda b, h: (b, h, 0, 0)),
                     pl.BlockSpec((1, 1, 2, DH_A), lambda b, h: (b, h, 0, 0))]
        args += [c0, n0, m0]
    out_specs = [pl.BlockSpec((1, t, DH_A), lambda b, h: (b, 0, h))]
    out_shape = [jax.ShapeDtypeStruct((bsz, t, W_A), BF16)]
    if emit_state:
        out_specs += [pl.BlockSpec((1, 1, 2, 1, DH_A, DH_A), lambda b, h: (b, 0, 0, h, 0, 0)),
                      pl.BlockSpec((1, 1, 2, DH_A), lambda b, h: (b, h, 0, 0)),
                      pl.BlockSpec((1, 1, 2, DH_A), lambda b, h: (b, h, 0, 0))]
        out_shape += [jax.ShapeDtypeStruct((bsz, 1, 2, H_A, DH_A, DH_A), F32),
                      jax.ShapeDtypeStruct((bsz, H_A, 2, DH_A), F32),
                      jax.ShapeDtypeStruct((bsz, H_A, 2, DH_A), F32)]
    kern = functools.partial(_mlstm_kernel, nc=nc, has_init=init is not None, emit_state=emit_state)
    return pl.pallas_call(
        kern, grid=(bsz, H_A), in_specs=in_specs, out_specs=out_specs, out_shape=out_shape,
        scratch_shapes=[pltpu.VMEM((t, DH_A), F32), pltpu.VMEM((2, DH_A, DH_A), F32),
                        pltpu.VMEM((2, 1, DH_A), F32), pltpu.VMEM((2, 1, DH_A), F32)],
        compiler_params=_cparams(("arbitrary", "arbitrary")), name="mlstm",
    )(*args)


def _head_norm(x, g):
    lo = lax.broadcasted_iota(jnp.int32, x.shape, 1) < DH_B
    x2 = x * x
    s_lo = jnp.sum(jnp.where(lo, x2, 0.0), axis=-1, keepdims=True)
    s_hi = jnp.sum(jnp.where(lo, 0.0, x2), axis=-1, keepdims=True)
    ms = jnp.where(lo, s_lo, s_hi) * (1.0 / DH_B)
    return x * lax.rsqrt(ms + EPS) * g


def _rope(x, cos, sin):
    quarter = DH_B // 4
    first = (lax.broadcasted_iota(jnp.int32, x.shape, 1) % (2 * quarter)) < quarter
    partner = jnp.where(first, pltpu.roll(x, LANES - quarter, 1), pltpu.roll(x, quarter, 1))
    return x * cos + partner * sin


def _qkprep_kernel(*refs, rope, emit_kn):
    qkv_ref, gq_ref, gk_ref = refs[:3]
    pos = 3
    if rope:
        cos_ref, sin_ref = refs[pos:pos + 2]
        pos += 2
    q_out, k_out = refs[pos:pos + 2]
    pos += 2
    nq = W_B // LANES
    for j in range(nq + 1):
        x = qkv_ref[0, :, j * LANES:(j + 1) * LANES]
        xn = _head_norm(x, gq_ref[...] if j < nq else gk_ref[...])
        if j == nq and emit_kn:
            refs[pos][0] = xn
        if rope:
            xn = _rope(xn, cos_ref[...], sin_ref[...])
        if j < nq:
            q_out[0, :, j * LANES:(j + 1) * LANES] = (xn * QSCALE).astype(q_out.dtype)
        else:
            k_out[0] = xn.astype(k_out.dtype)


def _qkprep(qkv, g_q, g_k, rope_tabs, *, emit_kn, tm):
    bsz, t, w = qkv.shape
    in_specs = [pl.BlockSpec((1, tm, w), lambda b, i: (b, i, 0)),
                pl.BlockSpec((1, LANES), lambda b, i: (0, 0)),
                pl.BlockSpec((1, LANES), lambda b, i: (0, 0))]
    args = [qkv, jnp.tile(g_q, 2).reshape(1, LANES), jnp.tile(g_k, 2).reshape(1, LANES)]
    if rope_tabs is not None:
        in_specs += [pl.BlockSpec((tm, LANES), lambda b, i: (i, 0))] * 2
        args += list(rope_tabs)
    out_specs = [pl.BlockSpec((1, tm, W_B), lambda b, i: (b, i, 0)),
                 pl.BlockSpec((1, tm, LANES), lambda b, i: (b, i, 0))]
    out_shape = [jax.ShapeDtypeStruct((bsz, t, W_B), BF16),
                 jax.ShapeDtypeStruct((bsz, t, LANES), BF16)]
    if emit_kn:
        out_specs.append(pl.BlockSpec((1, tm, LANES), lambda b, i: (b, i, 0)))
        out_shape.append(jax.ShapeDtypeStruct((bsz, t, LANES), F32))
    kern = functools.partial(_qkprep_kernel, rope=rope_tabs is not None, emit_kn=emit_kn)
    return pl.pallas_call(
        kern, grid=(bsz, t // tm), in_specs=in_specs, out_specs=out_specs, out_shape=out_shape,
        compiler_params=_cparams(("arbitrary", "arbitrary")), name="qk_prep",
    )(*args)


def _rope_tables(n_tok):
    t = jnp.arange(n_tok)
    row = (t // GRID_W).astype(F32)
    colp = (t % GRID_W).astype(F32)
    quarter = DH_B // 4
    freqs = ROPE_THETA ** (-jnp.arange(quarter, dtype=F32) / quarter)
    ar = row[:, None] * freqs
    ac = colp[:, None] * freqs
    cos = jnp.concatenate([jnp.cos(ar), jnp.cos(ar), jnp.cos(ac), jnp.cos(ac)], axis=-1)
    sin = jnp.concatenate([-jnp.sin(ar), jnp.sin(ar), -jnp.sin(ac), jnp.sin(ac)], axis=-1)
    return jnp.tile(cos, (1, 2)), jnp.tile(sin, (1, 2))


def _split_heads_q(q):
    lo = lax.broadcasted_iota(jnp.int32, q.shape, 1) < (LANES // 2)
    zero = jnp.zeros_like(q)
    return jnp.concatenate([jnp.where(lo, q, zero), jnp.where(lo, zero, q)], axis=0)


def _merge_heads_o(o_t, tq):
    half = LANES // 2
    return jnp.concatenate([o_t[:half, :tq], o_t[half:, tq:]], axis=0).T


def _with_ones_rows(vt):
    return jnp.concatenate([vt, jnp.ones((BF16_ROWS, vt.shape[1]), BF16)], axis=0)


def _online_softmax_pv_t(problems):
    seq = [(pi, ci) for pi, (_, chunks, _) in enumerate(problems) for ci in range(len(chunks))]
    scores = {}

    def issue(t):
        pi, ci = seq[t]
        qm, chunks, _ = problems[pi]
        scores[(pi, ci)] = _dot_nt(chunks[ci][0], qm)

    for t in range(min(QK_AHEAD, len(seq))):
        issue(t)
    m = acc = None
    for t, (pi, ci) in enumerate(seq):
        if t + QK_AHEAD < len(seq):
            issue(t + QK_AHEAD)
        _, chunks, emit = problems[pi]
        _, vt, penalty = chunks[ci]
        s = scores.pop((pi, ci))
        if penalty is not None:
            s = s - penalty
        mc = jnp.max(s, axis=0, keepdims=True)
        m_new = mc if ci == 0 else jnp.maximum(m, mc)
        p = jnp.exp2(s - m_new).astype(BF16)
        part = jnp.dot(_with_ones_rows(vt), p, preferred_element_type=F32)
        acc = part if ci == 0 else jnp.exp2(m - m_new) * acc + part
        m = m_new
        if ci == len(chunks) - 1:
            dv = vt.shape[0]
            emit(acc[:dv] / acc[dv:dv + 1])


def _attn_kernel(*refs, n_tiles, kv_shared, has_cache):
    q_ref, k_ref, vt_ref = refs[:3]
    pos = 3
    if has_cache:
        kc_ref, vtc_ref = refs[pos:pos + 2]
        pos += 2
    o_ref = refs[pos]
    tq = q_ref.shape[1]
    s_len = k_ref.shape[1]
    sc = min(s_len, KEY_CHUNK)
    problems = []
    for j in range(n_tiles):
        kj = 0 if kv_shared else j
        kcols = slice(kj * LANES, (kj + 1) * LANES)
        qm = _split_heads_q(q_ref[0, :, j * LANES:(j + 1) * LANES])
        chunks = [(k_ref[0, c * sc:(c + 1) * sc, kcols].astype(BF16),
                   vt_ref[0, kcols, c * sc:(c + 1) * sc], None) for c in range(s_len // sc)]
        if has_cache:
            chunks.append((kc_ref[0], vtc_ref[0], None))

        def emit(o_t, j=j):
            o_ref[0, :, j * LANES:(j + 1) * LANES] = _merge_heads_o(o_t, tq).astype(o_ref.dtype)

        problems.append((qm, chunks, emit))
    _online_softmax_pv_t(problems)


def _attention(q, k, vt, cache, *, kv_shared, tq, n_tiles):
    bsz, t, w = q.shape
    s = k.shape[1]
    wt = n_tiles * LANES
    if kv_shared:
        k_spec = pl.BlockSpec((1, s, LANES), lambda b, i, g: (b, 0, 0))
        vt_spec = pl.BlockSpec((1, LANES, s), lambda b, i, g: (b, 0, 0))
    else:
        k_spec = pl.BlockSpec((1, s, wt), lambda b, i, g: (b, 0, g))
        vt_spec = pl.BlockSpec((1, wt, s), lambda b, i, g: (b, g, 0))
    in_specs = [pl.BlockSpec((1, tq, wt), lambda b, i, g: (b, i, g)), k_spec, vt_spec]
    args = [q, k, vt]
    if cache is not None:
        p = cache[0].shape[1]
        in_specs += [pl.BlockSpec((1, p, LANES), lambda b, i, g: (b, 0, 0)),
                     pl.BlockSpec((1, LANES, p), lambda b, i, g: (b, 0, 0))]
        args += list(cache)
    kern = functools.partial(_attn_kernel, n_tiles=n_tiles, kv_shared=kv_shared,
                             has_cache=cache is not None)
    return pl.pallas_call(
        kern, grid=(bsz, t // tq, w // wt), in_specs=in_specs,
        out_specs=pl.BlockSpec((1, tq, wt), lambda b, i, g: (b, i, g)),
        out_shape=jax.ShapeDtypeStruct((bsz, t, w), BF16),
        compiler_params=_cparams(("arbitrary", "arbitrary", "arbitrary")), name="attention",
    )(*args)


def _na_bias_blocks(variant):
    out = {}
    for qr in range(NA_ROWS):
        for kr in range(NA_WIN):
            if variant == 0:
                dr = kr - qr if kr < WIN_R else None
            elif variant == 1:
                dr = kr - qr - WIN_R // 2 if qr <= kr < qr + WIN_R else None
            else:
                dr = kr - qr - (NA_WIN - NA_ROWS) if kr >= NA_WIN - WIN_R else None
            out[(qr, kr)] = None if dr is None else dr + WIN_R - 1
    return out


def _na_kernel(q_ref, k_ref, vt_ref, kc_ref, vtc_ref, bc_ref, o_ref, bias_sc, *, n_rb):
    w = GRID_W
    tq = NA_ROWS * w

    @pl.when(jnp.logical_and(pl.program_id(1) == 0, pl.program_id(2) == 0))
    def _():
        for variant in range(3):
            for (qr, kr), di in _na_bias_blocks(variant).items():
                for hh in range(2 * NA_TILES):
                    val = jnp.full((w, w), MASKED, F32) if di is None else bc_ref[hh, di]
                    bias_sc[variant, kr * w:(kr + 1) * w, hh * tq + qr * w:hh * tq + (qr + 1) * w] = val

    n_rows = n_rb * NA_ROWS
    problems = []
    for rr in range(NA_RBS):
        rb = pl.program_id(2) * NA_RBS + rr
        variant = jnp.where(rb == 0, 0, jnp.where(rb == n_rb - 1, 2, 1))
        ws = jnp.clip(rb * NA_ROWS - WIN_R // 2, 0, n_rows - NA_WIN)
        blk0 = ws // NA_ROWS
        start = ws * w
        qrows = slice(rr * tq, (rr + 1) * tq)
        for j in range(NA_TILES):
            cols = slice(j * LANES, (j + 1) * LANES)
            qcols = slice(j * 2 * tq, (j + 1) * 2 * tq)
            qm = _split_heads_q(q_ref[0, qrows, cols])
            chunks = [(k_ref[0, pl.ds(pl.multiple_of(start + i * tq, tq), tq), cols],
                       vt_ref[0, blk0 + i, cols, :],
                       bias_sc[variant, i * tq:(i + 1) * tq, qcols]) for i in range(NA_BLKS)]
            chunks.append((kc_ref[0, :, cols], vtc_ref[0, cols, :], None))

            def emit(o_t, qrows=qrows, cols=cols):
                o_ref[0, qrows, cols] = _merge_heads_o(o_t, tq).astype(o_ref.dtype)

            problems.append((qm, chunks, emit))
    _online_softmax_pv_t(problems)


def _na_bias_table(rpb):
    c = np.arange(GRID_W)
    cs = np.clip(c - WIN_C // 2, 0, GRID_W - WIN_C)
    ck = np.arange(GRID_W)
    valid = (ck[:, None] >= cs[None, :]) & (ck[:, None] < cs[None, :] + WIN_C)
    idx = np.clip(ck[:, None] - c[None, :] + WIN_C - 1, 0, 2 * WIN_C - 2)
    onehot = (idx[..., None] == np.arange(2 * WIN_C - 1)).astype(np.float32)
    tab = jnp.einsum('hrx,kcx->hrkc', rpb, jnp.asarray(onehot), precision=lax.Precision.HIGHEST)
    return jnp.where(jnp.asarray(valid), tab * -LOG2E, MASKED)


def _na_attention(q, k, vt, kc, vtc, rpb):
    bsz, t, w = q.shape
    p = kc.shape[1]
    tq = NA_ROWS * GRID_W
    n_rb = t // tq
    bc = _na_bias_table(rpb)
    wt = NA_TILES * LANES
    tqs = NA_RBS * tq
    return pl.pallas_call(
        functools.partial(_na_kernel, n_rb=n_rb),
        grid=(w // wt, bsz, n_rb // NA_RBS),
        in_specs=[pl.BlockSpec((1, tqs, wt), lambda j, b, r: (b, r, j)),
                  pl.BlockSpec((1, t, wt), lambda j, b, r: (b, 0, j)),
                  pl.BlockSpec((1, n_rb, wt, tq), lambda j, b, r: (b, 0, j, 0)),
                  pl.BlockSpec((1, p, wt), lambda j, b, r: (b, 0, j)),
                  pl.BlockSpec((1, wt, p), lambda j, b, r: (b, j, 0)),
                  pl.BlockSpec((2 * NA_TILES, 2 * WIN_R - 1, GRID_W, GRID_W), lambda j, b, r: (j, 0, 0, 0))],
        out_specs=pl.BlockSpec((1, tqs, wt), lambda j, b, r: (b, r, j)),
        out_shape=jax.ShapeDtypeStruct((bsz, t, w), BF16),
        scratch_shapes=[pltpu.VMEM((3, NA_WIN * GRID_W, NA_TILES * 2 * tq), F32)],
        compiler_params=_cparams(("arbitrary", "arbitrary", "arbitrary")), name="na_attention",
    )(q, k, vt, kc, vtc, bc)


_GQA_PERM = np.array([0, 4, 1, 5, 2, 6, 3, 7])


def _perm_heads(w, axis):
    shp = w.shape
    n = shp[axis] // DH_B
    w = w.reshape(shp[:axis] + (n, DH_B) + shp[axis + 1:])
    w = jnp.take(w, jnp.asarray(_GQA_PERM), axis=axis)
    return w.reshape(shp)


def _tok_major(cache):
    b, h, p, dh = cache.shape
    return cache.transpose(0, 2, 1, 3).reshape(b, p, h * dh)


def _feat_major(cache):
    b, h, p, dh = cache.shape
    return cache.transpose(0, 1, 3, 2).reshape(b, h * dh, p)


def _head_major(x, n_heads):
    b, t, w = x.shape
    return x.reshape(b, t, n_heads, w // n_heads).transpose(0, 2, 1, 3)


def _even_layer(xp, xs, mod, g_pre, g_post, w_in, b_gates, g_hn, g_q, g_k, w_out, st_c, st_n, st_m,
                ck, cv, rope_tabs):
    o_q, o_k, o_v, o_o = 0, W_A, 2 * W_A, 3 * W_A
    o_g = 4 * W_A
    o_qb = o_g + 4 * H_A
    o_kb = o_qb + W_B
    o_vb = o_kb + HKV_B * DH_B
    o_z = o_vb + HKV_B * DH_B
    w_pa = w_in[:, :o_g].astype(BF16)
    w_g = w_in[:, o_g:o_qb].astype(BF16)
    w_qkv = jnp.concatenate([_perm_heads(w_in[:, o_qb:o_kb], 1), w_in[:, o_kb:o_z]], axis=1).astype(BF16)
    w_z = jnp.concatenate([w_in[:, o_z:o_z + W_A], _perm_heads(w_in[:, o_z + W_A:], 1)], axis=1).astype(BF16)
    w_o = jnp.concatenate([w_out[:W_A], _perm_heads(w_out[W_A:], 0)], axis=0).astype(BF16)
    w_vt = w_in[:, o_vb:o_z].T.astype(BF16)

    def stream(x, row0, per_batch, init, cache, rope, emit):
        bsz, t, _ = x.shape
        tm = 512
        xf = x if per_batch else x.reshape(1, bsz * t, D_MODEL)
        pa, qkv, z, vbt, gts, gtt = _proj(
            xf, g_pre, mod, [w_pa, w_qkv, w_z], [BF16, F32, BF16], [w_vt], [None if per_batch else t],
            (w_g, b_gates), row0=row0, per_batch=per_batch, tm=tm)
        if not per_batch:
            pa, qkv, z, gts = (a.reshape(bsz, t, a.shape[-1]) for a in (pa, qkv, z, gts))
            gtt = gtt.reshape(4 * H_A, bsz, t).transpose(1, 0, 2)
            vbt = vbt.reshape(bsz, HKV_B * DH_B, t)
        res = _mlstm(pa, gts, gtt, g_hn, init, emit_state=emit)
        ha = res[0]
        prep = _qkprep(qkv, g_q, g_k, rope, emit_kn=emit, tm=min(t, 512))
        qn, kn = prep[0], prep[1]
        vb = qkv[..., W_B + HKV_B * DH_B:]
        if per_batch:
            hb = _attention(qn, kn, vbt, cache, kv_shared=True, tq=512, n_tiles=1)
        else:
            hb = _attention(qn, kn, vbt, cache, kv_shared=True, tq=t, n_tiles=W_B // LANES)
        y = _outproj([ha.reshape(xf.shape[0], -1, W_A), hb.reshape(xf.shape[0], -1, W_B)],
                     z.reshape(xf.shape[0], -1, W_A + W_B), w_o, xf, g_post, mod,
                     row0=row0, per_batch=per_batch, tm=tm)
        return y.reshape(bsz, t, D_MODEL), res[1:], (prep[2] if emit else None), vb

    yp, st, kn_p, vb_p = stream(xp, 0, False, None, None, None, True)
    n0 = st_n.transpose(0, 2, 1, 3)
    m0 = jnp.broadcast_to(st_m.transpose(0, 2, 1)[..., None], n0.shape)
    cache = (_tok_major(ck).astype(BF16), _feat_major(cv).astype(BF16))
    ys, _, _, _ = stream(xs, 1, True, (st_c, n0, m0), cache, rope_tabs, False)
    c_out, n_out, m_out = st
    new_n = n_out.transpose(0, 2, 1, 3)[:, None]
    new_m = m_out[..., 0].transpose(0, 2, 1)[:, None]
    new_gk = _head_major(kn_p, HKV_B)[:, None]
    new_gv = _head_major(vb_p, HKV_B)[:, None]
    return yp, ys, c_out, new_n, new_m, new_gk, new_gv


def _odd_layer(xp, xs, mod, g_pre, g_post, w_in, rpb, w_out, ck, cv):
    ws = [(w_in[:, i * W_C:(i + 1) * W_C] * (QSCALE if i == 0 else 1.0)).astype(BF16) for i in range(4)]
    w_o = w_out.astype(BF16)
    tm = 512
    bsz, t, _ = xp.shape
    xf = xp.reshape(1, bsz * t, D_MODEL)
    w_vt = ws[2].T
    q, k, v, z, vt = _proj(xf, g_pre, mod, ws, [BF16, F32, F32, BF16], [w_vt], [t], None,
                           row0=0, per_batch=False, tm=tm)
    k = k.reshape(bsz, t, W_C)
    v = v.reshape(bsz, t, W_C)
    o = _attention(q.reshape(bsz, t, W_C), k, vt.reshape(bsz, W_C, t), None,
                   kv_shared=False, tq=t, n_tiles=W_C // LANES)
    yp = _outproj([o.reshape(1, bsz * t, W_C)], z, w_o, xf, g_post, mod, row0=0, per_batch=False, tm=tm)
    yp = yp.reshape(bsz, t, D_MODEL)
    new_nk = _head_major(k, H_C)[:, None]
    new_nv = _head_major(v, H_C)[:, None]
    q, k, z, vt = _proj(xs, g_pre, mod, [ws[0], ws[1], ws[3]], [BF16, BF16, BF16], [w_vt],
                        [NA_ROWS * GRID_W], None, row0=1, per_batch=True, tm=tm)
    o = _na_attention(q, k, vt, _tok_major(ck).astype(BF16), _feat_major(cv).astype(BF16), rpb)
    ys = _outproj([o], z, w_o, xs, g_post, mod, row0=1, per_batch=True, tm=tm)
    return yp, ys, new_nk, new_nv


def kernel(x_prompt, x_sample, state_mlstm_C, state_mlstm_n, state_mlstm_m, cache_gqa_k, cache_gqa_v,
           cache_na_k, cache_na_v, c, c_ctx, w_mod, b_mod, g_pre, g_post, w_in_ab, b_gates_ab, g_hnorm_a,
           g_qnorm_b, g_knorm_b, w_out_ab, w_in_c, rpb_c, w_out_c):
    depth = w_mod.shape[0]
    assert depth == 2 and c.shape[0] == 2
    cvec = jnp.zeros((8, D_MODEL), F32).at[0].set(c_ctx).at[1:1 + c.shape[0]].set(c)
    mod = _modulation(cvec, w_mod, b_mod)
    rope_tabs = _rope_tables(x_sample.shape[1])
    xp, xs, c_out, n_out, m_out, gk, gv = _even_layer(
        x_prompt, x_sample, mod[0], g_pre[0], g_post[0], w_in_ab[0], b_gates_ab[0], g_hnorm_a[0],
        g_qnorm_b[0], g_knorm_b[0], w_out_ab[0], state_mlstm_C[:, 0], state_mlstm_n[:, 0],
        state_mlstm_m[:, 0], cache_gqa_k[:, 0], cache_gqa_v[:, 0], rope_tabs)
    xp, xs, nk, nv = _odd_layer(xp, xs, mod[1], g_pre[1], g_post[1], w_in_c[0], rpb_c[0], w_out_c[0],
                                cache_na_k[:, 0], cache_na_v[:, 0])
    return (xp, xs, c_out, n_out, m_out, gk, gv, nk, nv)
```

```python
import functools

import jax
import jax.numpy as jnp
import numpy as np
from jax import lax
from jax.experimental import pallas as pl
from jax.experimental.pallas import tpu as pltpu

F32 = jnp.float32
BF16 = jnp.bfloat16

D_MODEL = 1024
GRID_W = 64
EPS = 1e-6
H_A = 4
DH_A = 128
W_A = H_A * DH_A
CHUNK_A = 128
HQ_B = 8
HKV_B = 2
DH_B = 64
W_B = HQ_B * DH_B
ROPE_THETA = 10000.0
H_C = 16
DH_C = 64
W_C = H_C * DH_C
WIN_R = 8
WIN_C = 16

LANES = 128
NA_ROWS = 4
NA_TILES = 2
NA_RBS = 2
NA_BLKS = -(-(NA_ROWS + WIN_R - 1) // NA_ROWS)
NA_WIN = NA_BLKS * NA_ROWS
assert WIN_R // 2 == NA_ROWS
BF16_ROWS = 16
KEY_CHUNK = 256
QK_AHEAD = 3
LOG2E = 1.4426950408889634
QSCALE = DH_B ** -0.5 * LOG2E
assert DH_B == DH_C
MASKED = 1e30
VMEM_LIMIT = 56 * 1024 * 1024


def _cparams(sem):
    return pltpu.CompilerParams(dimension_semantics=sem, vmem_limit_bytes=VMEM_LIMIT)


def _silu(x):
    return x / (1.0 + jnp.exp(-x))


def _sigmoid(x):
    return 1.0 / (1.0 + jnp.exp(-x))


def _log_sigmoid(x):
    return jnp.minimum(x, 0.0) - jnp.log1p(jnp.exp(-jnp.abs(x)))


def _dot_nt(a, b):
    return lax.dot_general(a, b, (((1,), (1,)), ((), ())), preferred_element_type=F32)


def _mod_kernel(c_ref, w_ref, b_ref, o_ref):
    s = _silu(c_ref[...])
    o_ref[0] = jnp.dot(s, w_ref[0], preferred_element_type=F32,
                       precision=lax.Precision.HIGHEST) + b_ref[0]


def _modulation(cvec, w_mod, b_mod):
    depth, d, n = w_mod.shape
    tn = n // 4
    return pl.pallas_call(
        _mod_kernel,
        grid=(depth, n // tn),
        in_specs=[pl.BlockSpec((8, d), lambda l, j: (0, 0)),
                  pl.BlockSpec((1, d, tn), lambda l, j: (l, 0, j)),
                  pl.BlockSpec((1, 1, tn), lambda l, j: (l, 0, j))],
        out_specs=pl.BlockSpec((1, 8, tn), lambda l, j: (l, 0, j)),
        out_shape=jax.ShapeDtypeStruct((depth, 8, n), F32),
        compiler_params=_cparams(("arbitrary", "arbitrary")),
        name="modulation",
    )(cvec, w_mod, b_mod.reshape(depth, 1, n))


def _proj_kernel(*refs, n_seg, t_blocks, with_gates, row0, per_batch):
    n_tseg = len(t_blocks)
    x_ref, g_ref, mod_ref = refs[:3]
    w_refs = refs[3:3 + n_seg]
    wt_refs = refs[3 + n_seg:3 + n_seg + n_tseg]
    pos = 3 + n_seg + n_tseg
    if with_gates:
        wgt_ref, bgt_ref = refs[pos:pos + 2]
        pos += 2
    o_refs = refs[pos:pos + n_seg]
    pos += n_seg
    ot_refs = refs[pos:pos + n_tseg]
    pos += n_tseg
    d = x_ref.shape[-1]
    tm = x_ref.shape[1]
    row = row0 + (pl.program_id(0) if per_batch else 0)
    shift = mod_ref[pl.ds(row, 1), 0:d]
    scale = mod_ref[pl.ds(row, 1), d:2 * d]
    x = x_ref[0]
    r = lax.rsqrt(jnp.mean(x * x, axis=-1, keepdims=True) + EPS)
    h = (x * r * g_ref[...]) * (1.0 + scale) + shift
    hb = h.astype(BF16)
    for w_ref, o_ref in zip(w_refs, o_refs):
        o_ref[0] = jnp.dot(hb, w_ref[...], preferred_element_type=F32).astype(o_ref.dtype)
    for wt_ref, ot_ref, tb in zip(wt_refs, ot_refs, t_blocks):
        res = _dot_nt(wt_ref[...], hb).astype(ot_ref.dtype)
        if tb is None:
            ot_ref[0] = res
        else:
            for i in range(tm // tb):
                ot_ref[0, i] = res[:, i * tb:(i + 1) * tb]
    if with_gates:
        refs[pos][0] = _dot_nt(wgt_ref[...], hb) + bgt_ref[...]


def _proj(x, g_pre, mod, weights, out_dtypes, t_weights, t_blocks, gates, *, row0, per_batch, tm):
    bsz, t, d = x.shape
    n_seg = len(weights)
    grid = (bsz, t // tm)
    const = lambda b, i: (0, 0)
    in_specs = [pl.BlockSpec((1, tm, d), lambda b, i: (b, i, 0)),
                pl.BlockSpec((1, d), const),
                pl.BlockSpec(mod.shape, const)]
    args = [x, g_pre.reshape(1, d), mod]
    for w in list(weights) + list(t_weights):
        in_specs.append(pl.BlockSpec(w.shape, const))
        args.append(w)
    out_specs, out_shape = [], []
    for w, dt in zip(weights, out_dtypes):
        n = w.shape[1]
        out_specs.append(pl.BlockSpec((1, tm, n), lambda b, i: (b, i, 0)))
        out_shape.append(jax.ShapeDtypeStruct((bsz, t, n), dt))
    for w, tb in zip(t_weights, t_blocks):
        n = w.shape[0]
        if tb is None:
            out_specs.append(pl.BlockSpec((1, n, tm), lambda b, i: (b, 0, i)))
            out_shape.append(jax.ShapeDtypeStruct((bsz, n, t), BF16))
        else:
            out_specs.append(pl.BlockSpec((1, tm // tb, n, tb), lambda b, i: (b, i, 0, 0)))
            out_shape.append(jax.ShapeDtypeStruct((bsz, t // tb, n, tb), BF16))
    if gates is not None:
        w_g, b_g = gates
        ng = w_g.shape[1]
        in_specs += [pl.BlockSpec((ng, d), const), pl.BlockSpec((ng, 1), const)]
        args += [w_g.T, b_g.reshape(ng, 1)]
        out_specs.append(pl.BlockSpec((1, ng, tm), lambda b, i: (b, 0, i)))
        out_shape.append(jax.ShapeDtypeStruct((bsz, ng, t), F32))
    kern = functools.partial(_proj_kernel, n_seg=n_seg, t_blocks=tuple(t_blocks),
                             with_gates=gates is not None, row0=row0, per_batch=per_batch)
    return pl.pallas_call(
        kern, grid=grid, in_specs=in_specs, out_specs=out_specs, out_shape=out_shape,
        compiler_params=_cparams(("arbitrary", "arbitrary")), name="in_proj",
    )(*args)


def _outproj_kernel(*refs, n_in, row0, per_batch):
    a_refs = refs[:n_in]
    z_ref, w_ref, x_ref, gp_ref, mod_ref, o_ref = refs[n_in:n_in + 6]
    d = x_ref.shape[-1]
    row = row0 + (pl.program_id(0) if per_batch else 0)
    gate = mod_ref[pl.ds(row, 1), 2 * d:3 * d]
    z = z_ref[0].astype(F32)
    sz = _silu(z)
    acc = None
    off = 0
    for a_ref in a_refs:
        kk = a_ref.shape[-1]
        y = (a_ref[0].astype(F32) * sz[:, off:off + kk]).astype(BF16)
        part = jnp.dot(y, w_ref[off:off + kk, :], preferred_element_type=F32)
        acc = part if acc is None else acc + part
        off += kk
    r = lax.rsqrt(jnp.mean(acc * acc, axis=-1, keepdims=True) + EPS)
    o_ref[0] = x_ref[0] + gate * (acc * r * gp_ref[...])


def _outproj(parts, z, w_out, x, g_post, mod, *, row0, per_batch, tm):
    bsz, t, d = x.shape
    grid = (bsz, t // tm)
    in_specs, args = [], []
    for a in parts:
        in_specs.append(pl.BlockSpec((1, tm, a.shape[-1]), lambda b, i: (b, i, 0)))
        args.append(a)
    in_specs += [pl.BlockSpec((1, tm, z.shape[-1]), lambda b, i: (b, i, 0)),
                 pl.BlockSpec(w_out.shape, lambda b, i: (0, 0)),
                 pl.BlockSpec((1, tm, d), lambda b, i: (b, i, 0)),
                 pl.BlockSpec((1, d), lambda b, i: (0, 0)),
                 pl.BlockSpec(mod.shape, lambda b, i: (0, 0))]
    args += [z, w_out, x, g_post.reshape(1, d), mod]
    kern = functools.partial(_outproj_kernel, n_in=len(parts), row0=row0, per_batch=per_batch)
    return pl.pallas_call(
        kern, grid=grid, in_specs=in_specs,
        out_specs=pl.BlockSpec((1, tm, d), lambda b, i: (b, i, 0)),
        out_shape=jax.ShapeDtypeStruct((bsz, t, d), F32),
        compiler_params=_cparams(("arbitrary", "arbitrary")), name="out_proj",
    )(*args)


def _lane_scan(x, op, reverse):
    ll = x.shape[-1]
    lane = lax.broadcasted_iota(jnp.int32, x.shape, 1)
    d = 1
    while d < ll:
        if reverse:
            x = jnp.where(lane < ll - d, op(x, pltpu.roll(x, ll - d, 1)), x)
        else:
            x = jnp.where(lane >= d, op(x, pltpu.roll(x, d, 1)), x)
        d *= 2
    return x


def _mlstm_gate_rows(ig, fg, reverse):
    lf = _log_sigmoid(fg)
    b = _lane_scan(lf, jnp.add, reverse)
    c = ig - b
    pm = _lane_scan(c, jnp.maximum, reverse)
    cmax = jnp.broadcast_to(jnp.max(c, axis=-1, keepdims=True), c.shape)
    tot = jnp.broadcast_to(jnp.sum(lf, axis=-1, keepdims=True), c.shape)
    return c, pm, b, cmax, tot


def _mlstm_chunk(q, k, vt, c_r, pm_r, b_r, cmax_r, tot_r, c_st, n_st, m_st, reverse):
    ll, dh = q.shape
    kscale = dh ** -0.5
    si = lax.broadcasted_iota(jnp.int32, (ll, ll), 0)
    ti = lax.broadcasted_iota(jnp.int32, (ll, ll), 1)
    allowed = (si >= ti) if reverse else (si <= ti)
    big_m = jnp.maximum(m_st, pm_r)
    cb = jnp.broadcast_to(c_r, (ll, ll)).T
    e = jnp.where(allowed, jnp.exp(cb - (big_m - float(np.log(kscale)))), 0.0)
    n16 = jnp.broadcast_to(n_st.astype(BF16), (BF16_ROWS, dh))
    r = _dot_nt(jnp.concatenate([k, c_st.astype(BF16), n16], axis=0), q)
    p = e * r[:ll]
    w_inter = jnp.exp(m_st - big_m)
    num = jnp.dot(vt, p.astype(BF16), preferred_element_type=F32) - r[ll:ll + dh] * (-w_inter)
    den = w_inter * r[ll + dh:ll + dh + 1] + jnp.sum(p, axis=0, keepdims=True)
    h_t = num / jnp.maximum(jnp.abs(den), jnp.exp(-(b_r + big_m)))
    m_c = jnp.maximum(m_st, cmax_r)
    a_st = jnp.exp(m_st - m_c)
    wk = jnp.exp(c_r - m_c) * kscale
    vw = jnp.concatenate([vt.astype(F32) * wk, jnp.broadcast_to(wk, (BF16_ROWS, ll))], axis=0)
    upd = jnp.dot(vw.astype(BF16), k, preferred_element_type=F32)
    c_new = a_st * c_st + upd[:dh]
    n_new = a_st * n_st + upd[dh:dh + 1]
    return h_t, c_new, n_new, tot_r + m_c


def _mlstm_kernel(*refs, nc, has_init, emit_state):
    q_ref, k_ref, vt_ref, ot_ref, gtt_ref, ghn_ref = refs[:6]
    pos = 6
    if has_init:
        c0_ref, n0_ref, m0_ref = refs[pos:pos + 3]
        pos += 3
    ha_ref = refs[pos]
    pos += 1
    if emit_state:
        cout_ref, nout_ref, mout_ref = refs[pos:pos + 3]
        pos += 3
    hs_sc, gate_sc, c_sc, n_sc, m_sc = refs[pos:pos + 5]
    ll = CHUNK_A
    hd = pl.program_id(1)

    for d in range(2):
        if has_init:
            c_sc[d] = c0_ref[0, d, 0]
            n_sc[d] = n0_ref[0, 0, d:d + 1, :]
            m_sc[d] = m0_ref[0, 0, d:d + 1, :]
        else:
            c_sc[d] = jnp.zeros((DH_A, DH_A), F32)
            n_sc[d] = jnp.zeros((1, DH_A), F32)
            m_sc[d] = jnp.zeros((1, DH_A), F32)
        col_i = 2 * H_A * d + hd
        ig = gtt_ref[0, pl.ds(col_i, 1)].reshape(nc, ll)
        fg = gtt_ref[0, pl.ds(col_i + H_A, 1)].reshape(nc, ll)
        for i, rows in enumerate(_mlstm_gate_rows(ig, fg, reverse=(d == 1))):
            gate_sc[d, i] = rows

    ghn_t = jnp.broadcast_to(ghn_ref[...], (ll, DH_A)).T

    def run_dir(d, c, first_touch):
        rows = pl.ds(pl.multiple_of(c * ll, ll), ll)
        c_r, pm_r, b_r, cmax_r, tot_r = (gate_sc[d, i, pl.ds(c, 1), :] for i in range(5))
        h_t, c_new, n_new, m_new = _mlstm_chunk(
            q_ref[0, rows, :], k_ref[0, rows, :], vt_ref[0, c], c_r, pm_r, b_r, cmax_r, tot_r,
            c_sc[d], n_sc[d], m_sc[d], reverse=(d == 1))
        c_sc[d] = c_new
        n_sc[d] = n_new
        m_sc[d] = m_new
        if first_touch:
            hs_sc[c] = h_t
        else:
            hsum = h_t + hs_sc[c]
            r = lax.rsqrt(jnp.mean(hsum * hsum, axis=0, keepdims=True) + EPS)
            out_t = (hsum * r * ghn_t) * _sigmoid(ot_ref[0, c].astype(F32))
            ha_ref[0, rows, :] = out_t.T.astype(ha_ref.dtype)

    def make_body(first_touch):
        def body(j, carry):
            run_dir(0, j, first_touch)
            run_dir(1, nc - 1 - j, first_touch)
            return carry
        return body

    half = nc // 2
    lax.fori_loop(0, half, make_body(True), 0)
    lax.fori_loop(half, nc, make_body(False), 0)

    if emit_state:
        for d in range(2):
            cout_ref[0, 0, d, 0] = c_sc[d]
            nout_ref[0, 0, d:d + 1, :] = n_sc[d]
            mout_ref[0, 0, d:d + 1, :] = m_sc[d]


def _mlstm(qk, vt, ot, gates_t, g_hn, init, *, emit_state):
    bsz, t, _ = qk.shape
    nc = t // CHUNK_A
    assert nc % 2 == 0
    gtt = gates_t.reshape(bsz, 4 * H_A, nc, CHUNK_A)

    def col(off):
        return pl.BlockSpec((1, t, DH_A), lambda b, h: (b, 0, off + h))

    def tblk():
        return pl.BlockSpec((1, nc, DH_A, CHUNK_A), lambda b, h: (b, 0, h, 0))

    in_specs = [col(0), col(H_A), tblk(), tblk(),
                pl.BlockSpec((1, 4 * H_A, nc, CHUNK_A), lambda b, h: (b, 0, 0, 0)),
                pl.BlockSpec((1, DH_A), lambda b, h: (0, h))]
    args = [qk, qk, vt, ot, gtt, g_hn.reshape(1, W_A)]
    if init is not None:
        c0, n0, m0 = init
        in_specs += [pl.BlockSpec((1, 2, 1, DH_A, DH_A), lambda b, h: (b, 0, h, 0, 0)),
                     pl.BlockSpec((1, 1, 2, DH_A), lambda b, h: (b, h, 0, 0)),
                     pl.BlockSpec((1, 1, 2, DH_A), lambda b, h: (b, h, 0, 0))]
        args += [c0, n0, m0]
    out_specs = [pl.BlockSpec((1, t, DH_A), lambda b, h: (b, 0, h))]
    out_shape = [jax.ShapeDtypeStruct((bsz, t, W_A), BF16)]
    if emit_state:
        out_specs += [pl.BlockSpec((1, 1, 2, 1, DH_A, DH_A), lambda b, h: (b, 0, 0, h, 0, 0)),
                      pl.BlockSpec((1, 1, 2, DH_A), lambda b, h: (b, h, 0, 0)),
                      pl.BlockSpec((1, 1, 2, DH_A), lambda b, h: (b, h, 0, 0))]
        out_shape += [jax.ShapeDtypeStruct((bsz, 1, 2, H_A, DH_A, DH_A), F32),
                      jax.ShapeDtypeStruct((bsz, H_A, 2, DH_A), F32),
                      jax.ShapeDtypeStruct((bsz, H_A, 2, DH_A), F32)]
    kern = functools.partial(_mlstm_kernel, nc=nc, has_init=init is not None, emit_state=emit_state)
    return pl.pallas_call(
        kern, grid=(bsz, H_A), in_specs=in_specs, out_specs=out_specs, out_shape=out_shape,
        scratch_shapes=[pltpu.VMEM((nc, DH_A, CHUNK_A), F32), pltpu.VMEM((2, 5, nc, CHUNK_A), F32),
                        pltpu.VMEM((2, DH_A, DH_A), F32), pltpu.VMEM((2, 1, DH_A), F32),
                        pltpu.VMEM((2, 1, DH_A), F32)],
        compiler_params=_cparams(("arbitrary", "arbitrary")), name="mlstm",
    )(*args)


def _head_norm(x, g):
    lo = lax.broadcasted_iota(jnp.int32, x.shape, 1) < DH_B
    x2 = x * x
    s_lo = jnp.sum(jnp.where(lo, x2, 0.0), axis=-1, keepdims=True)
    s_hi = jnp.sum(jnp.where(lo, 0.0, x2), axis=-1, keepdims=True)
    ms = jnp.where(lo, s_lo, s_hi) * (1.0 / DH_B)
    return x * lax.rsqrt(ms + EPS) * g


def _rope(x, cos, sin):
    quarter = DH_B // 4
    first = (lax.broadcasted_iota(jnp.int32, x.shape, 1) % (2 * quarter)) < quarter
    partner = jnp.where(first, pltpu.roll(x, LANES - quarter, 1), pltpu.roll(x, quarter, 1))
    return x * cos + partner * sin


def _qkprep_kernel(*refs, rope, emit_kn):
    qkv_ref, gq_ref, gk_ref = refs[:3]
    pos = 3
    if rope:
        cos_ref, sin_ref = refs[pos:pos + 2]
        pos += 2
    q_out, k_out = refs[pos:pos + 2]
    pos += 2
    nq = W_B // LANES
    for j in range(nq + 1):
        x = qkv_ref[0, :, j * LANES:(j + 1) * LANES]
        xn = _head_norm(x, gq_ref[...] if j < nq else gk_ref[...])
        if j == nq and emit_kn:
            refs[pos][0] = xn
        if rope:
            xn = _rope(xn, cos_ref[...], sin_ref[...])
        if j < nq:
            q_out[0, :, j * LANES:(j + 1) * LANES] = (xn * QSCALE).astype(q_out.dtype)
        else:
            k_out[0] = xn.astype(k_out.dtype)


def _qkprep(qkv, g_q, g_k, rope_tabs, *, emit_kn, tm):
    bsz, t, w = qkv.shape
    in_specs = [pl.BlockSpec((1, tm, w), lambda b, i: (b, i, 0)),
                pl.BlockSpec((1, LANES), lambda b, i: (0, 0)),
                pl.BlockSpec((1, LANES), lambda b, i: (0, 0))]
    args = [qkv, jnp.tile(g_q, 2).reshape(1, LANES), jnp.tile(g_k, 2).reshape(1, LANES)]
    if rope_tabs is not None:
        in_specs += [pl.BlockSpec((tm, LANES), lambda b, i: (i, 0))] * 2
        args += list(rope_tabs)
    out_specs = [pl.BlockSpec((1, tm, W_B), lambda b, i: (b, i, 0)),
                 pl.BlockSpec((1, tm, LANES), lambda b, i: (b, i, 0))]
    out_shape = [jax.ShapeDtypeStruct((bsz, t, W_B), BF16),
                 jax.ShapeDtypeStruct((bsz, t, LANES), BF16)]
    if emit_kn:
        out_specs.append(pl.BlockSpec((1, tm, LANES), lambda b, i: (b, i, 0)))
        out_shape.append(jax.ShapeDtypeStruct((bsz, t, LANES), F32))
    kern = functools.partial(_qkprep_kernel, rope=rope_tabs is not None, emit_kn=emit_kn)
    return pl.pallas_call(
        kern, grid=(bsz, t // tm), in_specs=in_specs, out_specs=out_specs, out_shape=out_shape,
        compiler_params=_cparams(("arbitrary", "arbitrary")), name="qk_prep",
    )(*args)


def _rope_tables(n_tok):
    t = jnp.arange(n_tok)
    row = (t // GRID_W).astype(F32)
    colp = (t % GRID_W).astype(F32)
    quarter = DH_B // 4
    freqs = ROPE_THETA ** (-jnp.arange(quarter, dtype=F32) / quarter)
    ar = row[:, None] * freqs
    ac = colp[:, None] * freqs
    cos = jnp.concatenate([jnp.cos(ar), jnp.cos(ar), jnp.cos(ac), jnp.cos(ac)], axis=-1)
    sin = jnp.concatenate([-jnp.sin(ar), jnp.sin(ar), -jnp.sin(ac), jnp.sin(ac)], axis=-1)
    return jnp.tile(cos, (1, 2)), jnp.tile(sin, (1, 2))


def _split_heads_q(q):
    lo = lax.broadcasted_iota(jnp.int32, q.shape, 1) < (LANES // 2)
    zero = jnp.zeros_like(q)
    return jnp.concatenate([jnp.where(lo, q, zero), jnp.where(lo, zero, q)], axis=0)


def _merge_heads_o(o_t, tq):
    half = LANES // 2
    return jnp.concatenate([o_t[:half, :tq], o_t[half:, tq:]], axis=0).T


def _with_ones_rows(vt):
    return jnp.concatenate([vt, jnp.ones((BF16_ROWS, vt.shape[1]), BF16)], axis=0)


def _online_softmax_pv_t(problems):
    seq = [(pi, ci) for pi, (_, chunks, _) in enumerate(problems) for ci in range(len(chunks))]
    scores = {}

    def issue(t):
        pi, ci = seq[t]
        qm, chunks, _ = problems[pi]
        scores[(pi, ci)] = _dot_nt(chunks[ci][0], qm)

    for t in range(min(QK_AHEAD, len(seq))):
        issue(t)
    m = acc = None
    for t, (pi, ci) in enumerate(seq):
        if t + QK_AHEAD < len(seq):
            issue(t + QK_AHEAD)
        _, chunks, emit = problems[pi]
        _, vt, penalty = chunks[ci]
        s = scores.pop((pi, ci))
        if penalty is not None:
            s = s - penalty
        mc = jnp.max(s, axis=0, keepdims=True)
        m_new = mc if ci == 0 else jnp.maximum(m, mc)
        p = jnp.exp2(s - m_new).astype(BF16)
        part = jnp.dot(_with_ones_rows(vt), p, preferred_element_type=F32)
        acc = part if ci == 0 else jnp.exp2(m - m_new) * acc + part
        m = m_new
        if ci == len(chunks) - 1:
            dv = vt.shape[0]
            emit(acc[:dv] / acc[dv:dv + 1])


def _attn_kernel(*refs, n_tiles, kv_shared, has_cache):
    q_ref, k_ref, vt_ref = refs[:3]
    pos = 3
    if has_cache:
        kc_ref, vtc_ref = refs[pos:pos + 2]
        pos += 2
    o_ref = refs[pos]
    tq = q_ref.shape[1]
    s_len = k_ref.shape[1]
    sc = min(s_len, KEY_CHUNK)
    problems = []
    for j in range(n_tiles):
        kj = 0 if kv_shared else j
        kcols = slice(kj * LANES, (kj + 1) * LANES)
        qm = _split_heads_q(q_ref[0, :, j * LANES:(j + 1) * LANES])
        chunks = [(k_ref[0, c * sc:(c + 1) * sc, kcols].astype(BF16),
                   vt_ref[0, kcols, c * sc:(c + 1) * sc], None) for c in range(s_len // sc)]
        if has_cache:
            chunks.append((kc_ref[0], vtc_ref[0], None))

        def emit(o_t, j=j):
            o_ref[0, :, j * LANES:(j + 1) * LANES] = _merge_heads_o(o_t, tq).astype(o_ref.dtype)

        problems.append((qm, chunks, emit))
    _online_softmax_pv_t(problems)


def _attention(q, k, vt, cache, *, kv_shared, tq, n_tiles):
    bsz, t, w = q.shape
    s = k.shape[1]
    wt = n_tiles * LANES
    if kv_shared:
        k_spec = pl.BlockSpec((1, s, LANES), lambda b, i, g: (b, 0, 0))
        vt_spec = pl.BlockSpec((1, LANES, s), lambda b, i, g: (b, 0, 0))
    else:
        k_spec = pl.BlockSpec((1, s, wt), lambda b, i, g: (b, 0, g))
        vt_spec = pl.BlockSpec((1, wt, s), lambda b, i, g: (b, g, 0))
    in_specs = [pl.BlockSpec((1, tq, wt), lambda b, i, g: (b, i, g)), k_spec, vt_spec]
    args = [q, k, vt]
    if cache is not None:
        p = cache[0].shape[1]
        in_specs += [pl.BlockSpec((1, p, LANES), lambda b, i, g: (b, 0, 0)),
                     pl.BlockSpec((1, LANES, p), lambda b, i, g: (b, 0, 0))]
        args += list(cache)
    kern = functools.partial(_attn_kernel, n_tiles=n_tiles, kv_shared=kv_shared,
                             has_cache=cache is not None)
    return pl.pallas_call(
        kern, grid=(bsz, t // tq, w // wt), in_specs=in_specs,
        out_specs=pl.BlockSpec((1, tq, wt), lambda b, i, g: (b, i, g)),
        out_shape=jax.ShapeDtypeStruct((bsz, t, w), BF16),
        compiler_params=_cparams(("arbitrary", "arbitrary", "arbitrary")), name="attention",
    )(*args)


def _na_bias_blocks(variant):
    out = {}
    for qr in range(NA_ROWS):
        for kr in range(NA_WIN):
            if variant == 0:
                dr = kr - qr if kr < WIN_R else None
            elif variant == 1:
                dr = kr - qr - WIN_R // 2 if qr <= kr < qr + WIN_R else None
            else:
                dr = kr - qr - (NA_WIN - NA_ROWS) if kr >= NA_WIN - WIN_R else None
            out[(qr, kr)] = None if dr is None else dr + WIN_R - 1
    return out


def _na_kernel(q_ref, k_ref, vt_ref, kc_ref, vtc_ref, bc_ref, o_ref, bias_sc, *, n_rb):
    w = GRID_W
    tq = NA_ROWS * w

    @pl.when(jnp.logical_and(pl.program_id(1) == 0, pl.program_id(2) == 0))
    def _():
        for variant in range(3):
            for (qr, kr), di in _na_bias_blocks(variant).items():
                for hh in range(2 * NA_TILES):
                    val = jnp.full((w, w), MASKED, F32) if di is None else bc_ref[hh, di]
                    bias_sc[variant, kr * w:(kr + 1) * w, hh * tq + qr * w:hh * tq + (qr + 1) * w] = val

    n_rows = n_rb * NA_ROWS
    problems = []
    for rr in range(NA_RBS):
        rb = pl.program_id(2) * NA_RBS + rr
        variant = jnp.where(rb == 0, 0, jnp.where(rb == n_rb - 1, 2, 1))
        ws = jnp.clip(rb * NA_ROWS - WIN_R // 2, 0, n_rows - NA_WIN)
        blk0 = ws // NA_ROWS
        start = ws * w
        qrows = slice(rr * tq, (rr + 1) * tq)
        for j in range(NA_TILES):
            cols = slice(j * LANES, (j + 1) * LANES)
            qcols = slice(j * 2 * tq, (j + 1) * 2 * tq)
            qm = _split_heads_q(q_ref[0, qrows, cols])
            chunks = [(k_ref[0, pl.ds(pl.multiple_of(start + i * tq, tq), tq), cols],
                       vt_ref[0, blk0 + i, cols, :],
                       bias_sc[variant, i * tq:(i + 1) * tq, qcols]) for i in range(NA_BLKS)]
            chunks.append((kc_ref[0, :, cols], vtc_ref[0, cols, :], None))

            def emit(o_t, qrows=qrows, cols=cols):
                o_ref[0, qrows, cols] = _merge_heads_o(o_t, tq).astype(o_ref.dtype)

            problems.append((qm, chunks, emit))
    _online_softmax_pv_t(problems)


def _na_bias_table(rpb):
    c = np.arange(GRID_W)
    cs = np.clip(c - WIN_C // 2, 0, GRID_W - WIN_C)
    ck = np.arange(GRID_W)
    valid = (ck[:, None] >= cs[None, :]) & (ck[:, None] < cs[None, :] + WIN_C)
    idx = np.clip(ck[:, None] - c[None, :] + WIN_C - 1, 0, 2 * WIN_C - 2)
    onehot = (idx[..., None] == np.arange(2 * WIN_C - 1)).astype(np.float32)
    tab = jnp.einsum('hrx,kcx->hrkc', rpb, jnp.asarray(onehot), precision=lax.Precision.HIGHEST)
    return jnp.where(jnp.asarray(valid), tab * -LOG2E, MASKED)


def _na_attention(q, k, vt, kc, vtc, rpb):
    bsz, t, w = q.shape
    p = kc.shape[1]
    tq = NA_ROWS * GRID_W
    n_rb = t // tq
    bc = _na_bias_table(rpb)
    wt = NA_TILES * LANES
    tqs = NA_RBS * tq
    return pl.pallas_call(
        functools.partial(_na_kernel, n_rb=n_rb),
        grid=(w // wt, bsz, n_rb // NA_RBS),
        in_specs=[pl.BlockSpec((1, tqs, wt), lambda j, b, r: (b, r, j)),
                  pl.BlockSpec((1, t, wt), lambda j, b, r: (b, 0, j)),
                  pl.BlockSpec((1, n_rb, wt, tq), lambda j, b, r: (b, 0, j, 0)),
                  pl.BlockSpec((1, p, wt), lambda j, b, r: (b, 0, j)),
                  pl.BlockSpec((1, wt, p), lambda j, b, r: (b, j, 0)),
                  pl.BlockSpec((2 * NA_TILES, 2 * WIN_R - 1, GRID_W, GRID_W), lambda j, b, r: (j, 0, 0, 0))],
        out_specs=pl.BlockSpec((1, tqs, wt), lambda j, b, r: (b, r, j)),
        out_shape=jax.ShapeDtypeStruct((bsz, t, w), BF16),
        scratch_shapes=[pltpu.VMEM((3, NA_WIN * GRID_W, NA_TILES * 2 * tq), F32)],
        compiler_params=_cparams(("arbitrary", "arbitrary", "arbitrary")), name="na_attention",
    )(q, k, vt, kc, vtc, bc)


_GQA_PERM = np.array([0, 4, 1, 5, 2, 6, 3, 7])


def _perm_heads(w, axis):
    shp = w.shape
    n = shp[axis] // DH_B
    w = w.reshape(shp[:axis] + (n, DH_B) + shp[axis + 1:])
    w = jnp.take(w, jnp.asarray(_GQA_PERM), axis=axis)
    return w.reshape(shp)


def _tok_major(cache):
    b, h, p, dh = cache.shape
    return cache.transpose(0, 2, 1, 3).reshape(b, p, h * dh)


def _feat_major(cache):
    b, h, p, dh = cache.shape
    return cache.transpose(0, 1, 3, 2).reshape(b, h * dh, p)


def _head_major(x, n_heads):
    b, t, w = x.shape
    return x.reshape(b, t, n_heads, w // n_heads).transpose(0, 2, 1, 3)


def _even_layer(xp, xs, mod, g_pre, g_post, w_in, b_gates, g_hn, g_q, g_k, w_out, st_c, st_n, st_m,
                ck, cv, rope_tabs):
    o_v, o_o = 2 * W_A, 3 * W_A
    o_g = 4 * W_A
    o_qb = o_g + 4 * H_A
    o_kb = o_qb + W_B
    o_vb = o_kb + HKV_B * DH_B
    o_z = o_vb + HKV_B * DH_B
    w_qk = w_in[:, :o_v].astype(BF16)
    w_vta = w_in[:, o_v:o_o].T.astype(BF16)
    w_ota = w_in[:, o_o:o_g].T.astype(BF16)
    w_g = w_in[:, o_g:o_qb].astype(BF16)
    w_qkv = jnp.concatenate([_perm_heads(w_in[:, o_qb:o_kb], 1), w_in[:, o_kb:o_z]], axis=1).astype(BF16)
    w_z = jnp.concatenate([w_in[:, o_z:o_z + W_A], _perm_heads(w_in[:, o_z + W_A:], 1)], axis=1).astype(BF16)
    w_o = jnp.concatenate([w_out[:W_A], _perm_heads(w_out[W_A:], 0)], axis=0).astype(BF16)
    w_vt = w_in[:, o_vb:o_z].T.astype(BF16)

    def stream(x, row0, per_batch, init, cache, rope, emit):
        bsz, t, _ = x.shape
        tm = 512
        xf = x if per_batch else x.reshape(1, bsz * t, D_MODEL)
        qk, qkv, z, vta, ota, vbt, gtt = _proj(
            xf, g_pre, mod, [w_qk, w_qkv, w_z], [BF16, F32, BF16], [w_vta, w_ota, w_vt],
            [CHUNK_A, CHUNK_A, None if per_batch else t], (w_g, b_gates),
            row0=row0, per_batch=per_batch, tm=tm)
        if not per_batch:
            qk, qkv, z = (a.reshape(bsz, t, a.shape[-1]) for a in (qk, qkv, z))
            gtt = gtt.reshape(4 * H_A, bsz, t).transpose(1, 0, 2)
            vbt = vbt.reshape(bsz, HKV_B * DH_B, t)
            vta, ota = (a.reshape(bsz, t // CHUNK_A, W_A, CHUNK_A) for a in (vta, ota))
        res = _mlstm(qk, vta, ota, gtt, g_hn, init, emit_state=emit)
        ha = res[0]
        prep = _qkprep(qkv, g_q, g_k, rope, emit_kn=emit, tm=min(t, 512))
        qn, kn = prep[0], prep[1]
        vb = qkv[..., W_B + HKV_B * DH_B:]
        if per_batch:
            hb = _attention(qn, kn, vbt, cache, kv_shared=True, tq=512, n_tiles=1)
        else:
            hb = _attention(qn, kn, vbt, cache, kv_shared=True, tq=t, n_tiles=W_B // LANES)
        y = _outproj([ha.reshape(xf.shape[0], -1, W_A), hb.reshape(xf.shape[0], -1, W_B)],
                     z.reshape(xf.shape[0], -1, W_A + W_B), w_o, xf, g_post, mod,
                     row0=row0, per_batch=per_batch, tm=tm)
        return y.reshape(bsz, t, D_MODEL), res[1:], (prep[2] if emit else None), vb

    yp, st, kn_p, vb_p = stream(xp, 0, False, None, None, None, True)
    n0 = st_n.transpose(0, 2, 1, 3)
    m0 = jnp.broadcast_to(st_m.transpose(0, 2, 1)[..., None], n0.shape)
    cache = (_tok_major(ck).astype(BF16), _feat_major(cv).astype(BF16))
    ys, _, _, _ = stream(xs, 1, True, (st_c, n0, m0), cache, rope_tabs, False)
    c_out, n_out, m_out = st
    new_n = n_out.transpose(0, 2, 1, 3)[:, None]
    new_m = m_out[..., 0].transpose(0, 2, 1)[:, None]
    new_gk = _head_major(kn_p, HKV_B)[:, None]
    new_gv = _head_major(vb_p, HKV_B)[:, None]
    return yp, ys, c_out, new_n, new_m, new_gk, new_gv


def _odd_layer(xp, xs, mod, g_pre, g_post, w_in, rpb, w_out, ck, cv):
    ws = [(w_in[:, i * W_C:(i + 1) * W_C] * (QSCALE if i == 0 else 1.0)).astype(BF16) for i in range(4)]
    w_o = w_out.astype(BF16)
    tm = 512
    bsz, t, _ = xp.shape
    xf = xp.reshape(1, bsz * t, D_MODEL)
    w_vt = ws[2].T
    q, k, v, z, vt = _proj(xf, g_pre, mod, ws, [BF16, F32, F32, BF16], [w_vt], [t], None,
                           row0=0, per_batch=False, tm=tm)
    k = k.reshape(bsz, t, W_C)
    v = v.reshape(bsz, t, W_C)
    o = _attention(q.reshape(bsz, t, W_C), k, vt.reshape(bsz, W_C, t), None,
                   kv_shared=False, tq=t, n_tiles=W_C // LANES)
    yp = _outproj([o.reshape(1, bsz * t, W_C)], z, w_o, xf, g_post, mod, row0=0, per_batch=False, tm=tm)
    yp = yp.reshape(bsz, t, D_MODEL)
    new_nk = _head_major(k, H_C)[:, None]
    new_nv = _head_major(v, H_C)[:, None]
    q, k, z, vt = _proj(xs, g_pre, mod, [ws[0], ws[1], ws[3]], [BF16, BF16, BF16], [w_vt],
                        [NA_ROWS * GRID_W], None, row0=1, per_batch=True, tm=tm)
    o = _na_attention(q, k, vt, _tok_major(ck).astype(BF16), _feat_major(cv).astype(BF16), rpb)
    ys = _outproj([o], z, w_o, xs, g_post, mod, row0=1, per_batch=True, tm=tm)
    return yp, ys, new_nk, new_nv


def kernel(x_prompt, x_sample, state_mlstm_C, state_mlstm_n, state_mlstm_m, cache_gqa_k, cache_gqa_v,
           cache_na_k, cache_na_v, c, c_ctx, w_mod, b_mod, g_pre, g_post, w_in_ab, b_gates_ab, g_hnorm_a,
           g_qnorm_b, g_knorm_b, w_out_ab, w_in_c, rpb_c, w_out_c):
    depth = w_mod.shape[0]
    assert depth == 2 and c.shape[0] == 2
    cvec = jnp.zeros((8, D_MODEL), F32).at[0].set(c_ctx).at[1:1 + c.shape[0]].set(c)
    mod = _modulation(cvec, w_mod, b_mod)
    rope_tabs = _rope_tables(x_sample.shape[1])
    xp, xs, c_out, n_out, m_out, gk, gv = _even_layer(
        x_prompt, x_sample, mod[0], g_pre[0], g_post[0], w_in_ab[0], b_gates_ab[0], g_hnorm_a[0],
        g_qnorm_b[0], g_knorm_b[0], w_out_ab[0], state_mlstm_C[:, 0], state_mlstm_n[:, 0],
        state_mlstm_m[:, 0], cache_gqa_k[:, 0], cache_gqa_v[:, 0], rope_tabs)
    xp, xs, nk, nv = _odd_layer(xp, xs, mod[1], g_pre[1], g_post[1], w_in_c[0], rpb_c[0], w_out_c[0],
                                cache_na_k[:, 0], cache_na_v[:, 0])
    return (xp, xs, c_out, n_out, m_out, gk, gv, nk, nv)
```

```python
import functools

import jax
import jax.numpy as jnp
import numpy as np
from jax import lax
from jax.experimental import pallas as pl
from jax.experimental.pallas import tpu as pltpu

F32 = jnp.float32
BF16 = jnp.bfloat16

D_MODEL = 1024
GRID_W = 64
EPS = 1e-6
H_A = 4
DH_A = 128
W_A = H_A * DH_A
CHUNK_A = 128
HQ_B = 8
HKV_B = 2
DH_B = 64
W_B = HQ_B * DH_B
ROPE_THETA = 10000.0
H_C = 16
DH_C = 64
W_C = H_C * DH_C
WIN_R = 8
WIN_C = 16

LANES = 128
NA_ROWS = 4
NA_TILES = 2
NA_RBS = 2
NA_BLKS = -(-(NA_ROWS + WIN_R - 1) // NA_ROWS)
NA_WIN = NA_BLKS * NA_ROWS
assert WIN_R // 2 == NA_ROWS
BF16_ROWS = 16
KEY_CHUNK = 256
QK_AHEAD = 3
LOG2E = 1.4426950408889634
QSCALE = DH_B ** -0.5 * LOG2E
assert DH_B == DH_C
MASKED = 1e30
VMEM_LIMIT = 56 * 1024 * 1024


def _cparams(sem):
    return pltpu.CompilerParams(dimension_semantics=sem, vmem_limit_bytes=VMEM_LIMIT)


def _silu(x):
    return x / (1.0 + jnp.exp(-x))


def _sigmoid(x):
    return 1.0 / (1.0 + jnp.exp(-x))


def _log_sigmoid(x):
    return jnp.minimum(x, 0.0) - jnp.log1p(jnp.exp(-jnp.abs(x)))


def _dot_nt(a, b):
    return lax.dot_general(a, b, (((1,), (1,)), ((), ())), preferred_element_type=F32)


def _mod_kernel(c_ref, w_ref, b_ref, o_ref):
    s = _silu(c_ref[...])
    o_ref[0] = jnp.dot(s, w_ref[0], preferred_element_type=F32,
                       precision=lax.Precision.HIGHEST) + b_ref[0]


def _modulation(cvec, w_mod, b_mod):
    depth, d, n = w_mod.shape
    tn = n // 4
    return pl.pallas_call(
        _mod_kernel,
        grid=(depth, n // tn),
        in_specs=[pl.BlockSpec((8, d), lambda l, j: (0, 0)),
                  pl.BlockSpec((1, d, tn), lambda l, j: (l, 0, j)),
                  pl.BlockSpec((1, 1, tn), lambda l, j: (l, 0, j))],
        out_specs=pl.BlockSpec((1, 8, tn), lambda l, j: (l, 0, j)),
        out_shape=jax.ShapeDtypeStruct((depth, 8, n), F32),
        compiler_params=_cparams(("arbitrary", "arbitrary")),
        name="modulation",
    )(cvec, w_mod, b_mod.reshape(depth, 1, n))


def _proj_kernel(*refs, n_seg, t_blocks, with_gates, row0, per_batch):
    n_tseg = len(t_blocks)
    x_ref, g_ref, mod_ref = refs[:3]
    w_refs = refs[3:3 + n_seg]
    wt_refs = refs[3 + n_seg:3 + n_seg + n_tseg]
    pos = 3 + n_seg + n_tseg
    if with_gates:
        wgt_ref, bgt_ref = refs[pos:pos + 2]
        pos += 2
    o_refs = refs[pos:pos + n_seg]
    pos += n_seg
    ot_refs = refs[pos:pos + n_tseg]
    pos += n_tseg
    d = x_ref.shape[-1]
    tm = x_ref.shape[1]
    row = row0 + (pl.program_id(0) if per_batch else 0)
    shift = mod_ref[pl.ds(row, 1), 0:d]
    scale = mod_ref[pl.ds(row, 1), d:2 * d]
    x = x_ref[0]
    r = lax.rsqrt(jnp.mean(x * x, axis=-1, keepdims=True) + EPS)
    h = (x * r * g_ref[...]) * (1.0 + scale) + shift
    hb = h.astype(BF16)
    for w_ref, o_ref in zip(w_refs, o_refs):
        o_ref[0] = jnp.dot(hb, w_ref[...], preferred_element_type=F32).astype(o_ref.dtype)
    for wt_ref, ot_ref, tb in zip(wt_refs, ot_refs, t_blocks):
        res = _dot_nt(wt_ref[...], hb).astype(ot_ref.dtype)
        if tb is None:
            ot_ref[0] = res
        else:
            for i in range(tm // tb):
                ot_ref[0, i] = res[:, i * tb:(i + 1) * tb]
    if with_gates:
        refs[pos][0] = _dot_nt(wgt_ref[...], hb) + bgt_ref[...]


def _proj(x, g_pre, mod, weights, out_dtypes, t_weights, t_blocks, gates, *, row0, per_batch, tm):
    bsz, t, d = x.shape
    n_seg = len(weights)
    grid = (bsz, t // tm)
    const = lambda b, i: (0, 0)
    in_specs = [pl.BlockSpec((1, tm, d), lambda b, i: (b, i, 0)),
                pl.BlockSpec((1, d), const),
                pl.BlockSpec(mod.shape, const)]
    args = [x, g_pre.reshape(1, d), mod]
    for w in list(weights) + list(t_weights):
        in_specs.append(pl.BlockSpec(w.shape, const))
        args.append(w)
    out_specs, out_shape = [], []
    for w, dt in zip(weights, out_dtypes):
        n = w.shape[1]
        out_specs.append(pl.BlockSpec((1, tm, n), lambda b, i: (b, i, 0)))
        out_shape.append(jax.ShapeDtypeStruct((bsz, t, n), dt))
    for w, tb in zip(t_weights, t_blocks):
        n = w.shape[0]
        if tb is None:
            out_specs.append(pl.BlockSpec((1, n, tm), lambda b, i: (b, 0, i)))
            out_shape.append(jax.ShapeDtypeStruct((bsz, n, t), BF16))
        else:
            out_specs.append(pl.BlockSpec((1, tm // tb, n, tb), lambda b, i: (b, i, 0, 0)))
            out_shape.append(jax.ShapeDtypeStruct((bsz, t // tb, n, tb), BF16))
    if gates is not None:
        w_g, b_g = gates
        ng = w_g.shape[1]
        in_specs += [pl.BlockSpec((ng, d), const), pl.BlockSpec((ng, 1), const)]
        args += [w_g.T, b_g.reshape(ng, 1)]
        out_specs.append(pl.BlockSpec((1, ng, tm), lambda b, i: (b, 0, i)))
        out_shape.append(jax.ShapeDtypeStruct((bsz, ng, t), F32))
    kern = functools.partial(_proj_kernel, n_seg=n_seg, t_blocks=tuple(t_blocks),
                             with_gates=gates is not None, row0=row0, per_batch=per_batch)
    return pl.pallas_call(
        kern, grid=grid, in_specs=in_specs, out_specs=out_specs, out_shape=out_shape,
        compiler_params=_cparams(("arbitrary", "arbitrary")), name="in_proj",
    )(*args)


def _outproj_kernel(*refs, n_in, row0, per_batch):
    a_refs = refs[:n_in]
    z_ref, w_ref, x_ref, gp_ref, mod_ref, o_ref = refs[n_in:n_in + 6]
    d = x_ref.shape[-1]
    row = row0 + (pl.program_id(0) if per_batch else 0)
    gate = mod_ref[pl.ds(row, 1), 2 * d:3 * d]
    z = z_ref[0].astype(F32)
    sz = _silu(z)
    acc = None
    off = 0
    for a_ref in a_refs:
        kk = a_ref.shape[-1]
        y = (a_ref[0].astype(F32) * sz[:, off:off + kk]).astype(BF16)
        part = jnp.dot(y, w_ref[off:off + kk, :], preferred_element_type=F32)
        acc = part if acc is None else acc + part
        off += kk
    r = lax.rsqrt(jnp.mean(acc * acc, axis=-1, keepdims=True) + EPS)
    o_ref[0] = x_ref[0] + gate * (acc * r * gp_ref[...])


def _outproj(parts, z, w_out, x, g_post, mod, *, row0, per_batch, tm):
    bsz, t, d = x.shape
    grid = (bsz, t // tm)
    in_specs, args = [], []
    for a in parts:
        in_specs.append(pl.BlockSpec((1, tm, a.shape[-1]), lambda b, i: (b, i, 0)))
        args.append(a)
    in_specs += [pl.BlockSpec((1, tm, z.shape[-1]), lambda b, i: (b, i, 0)),
                 pl.BlockSpec(w_out.shape, lambda b, i: (0, 0)),
                 pl.BlockSpec((1, tm, d), lambda b, i: (b, i, 0)),
                 pl.BlockSpec((1, d), lambda b, i: (0, 0)),
                 pl.BlockSpec(mod.shape, lambda b, i: (0, 0))]
    args += [z, w_out, x, g_post.reshape(1, d), mod]
    kern = functools.partial(_outproj_kernel, n_in=len(parts), row0=row0, per_batch=per_batch)
    return pl.pallas_call(
        kern, grid=grid, in_specs=in_specs,
        out_specs=pl.BlockSpec((1, tm, d), lambda b, i: (b, i, 0)),
        out_shape=jax.ShapeDtypeStruct((bsz, t, d), F32),
        compiler_params=_cparams(("arbitrary", "arbitrary")), name="out_proj",
    )(*args)


def _split3_bf16(x):
    hi = x.astype(BF16)
    r1 = x - hi.astype(F32)
    mid = r1.astype(BF16)
    lo = (r1 - mid.astype(F32)).astype(BF16)
    return hi, mid, lo


def _mlstm_gate_rows(ig_f, fg_f, ig_b, fg_b):
    rr, ll = fg_f.shape
    lf = _log_sigmoid(jnp.concatenate([fg_f, fg_b], axis=0))
    pieces = jnp.concatenate(_split3_bf16(lf), axis=0)
    u = lax.broadcasted_iota(jnp.int32, (ll, ll), 0)
    t = lax.broadcasted_iota(jnp.int32, (ll, ll), 1)
    out = []
    for d, ig in enumerate((ig_f, ig_b)):
        tri = jnp.where((u >= t) if d else (u <= t), 1.0, 0.0).astype(BF16)
        y = jnp.dot(pieces, tri, preferred_element_type=F32)
        rows = slice(d * rr, (d + 1) * rr)
        b = y[0:2 * rr][rows] + y[2 * rr:4 * rr][rows] + y[4 * rr:6 * rr][rows]
        c = ig - b
        cmax = jnp.broadcast_to(jnp.max(c, axis=-1, keepdims=True), c.shape)
        tot = jnp.broadcast_to(jnp.sum(lf[rows], axis=-1, keepdims=True), c.shape)
        out.append((c, b, cmax, tot))
    return out


def _mlstm_kernel(*refs, nc, hps, unroll, has_init, emit_state):
    q_ref, k_ref, vt_ref, ot_ref, gtt_ref, ghn_ref = refs[:6]
    pos = 6
    if has_init:
        c0_ref, n0_ref, m0_ref = refs[pos:pos + 3]
        pos += 3
    ha_ref = refs[pos]
    pos += 1
    if emit_state:
        cout_ref, nout_ref, mout_ref = refs[pos:pos + 3]
        pos += 3
    hs_sc, gate_sc, c_sc, n_sc, m_sc = refs[pos:pos + 5]
    ll, dh = CHUNK_A, DH_A
    kscale = dh ** -0.5
    hd0 = pl.program_id(1) * hps
    chains = [(i, d) for i in range(hps) for d in range(2)]

    for i, d in chains:
        if has_init:
            c_sc[i, d] = c0_ref[0, d, i]
            n_sc[i, d] = n0_ref[0, i, d:d + 1, :]
            m_sc[i, d] = m0_ref[0, i, d:d + 1, :]
        else:
            c_sc[i, d] = jnp.zeros((dh, dh), F32)
            n_sc[i, d] = jnp.zeros((1, dh), F32)
            m_sc[i, d] = jnp.zeros((1, dh), F32)

    def gate_rows(col0):
        return gtt_ref[0, pl.ds(pl.multiple_of((col0 + hd0) * nc, 8), hps * nc), :]

    gates = _mlstm_gate_rows(gate_rows(0), gate_rows(H_A), gate_rows(2 * H_A), gate_rows(3 * H_A))
    for d in range(2):
        for kind in range(4):
            gate_sc[d, kind] = gates[d][kind]

    ghn_t = [jnp.broadcast_to(ghn_ref[:, i * dh:(i + 1) * dh], (ll, dh)).T for i in range(hps)]
    si = lax.broadcasted_iota(jnp.int32, (ll, ll), 0)
    ti = lax.broadcasted_iota(jnp.int32, (ll, ll), 1)

    def run_trip(j, first_touch):
        jobs = []
        for u in range(unroll):
            step = j * unroll + u
            for i, d in chains:
                jobs.append((i, d, step if d == 0 else nc - 1 - step))
        state = {ch: [c_sc[ch], n_sc[ch], m_sc[ch]] for ch in chains}
        hcols = [slice(i * dh, (i + 1) * dh) for i in range(hps)]

        def tok_rows(cidx):
            return pl.ds(pl.multiple_of(cidx * ll, ll), ll)

        rows = []
        for i, d, cidx in jobs:
            c_r, b_r, cmax_r, tot_r = (gate_sc[d, kind, pl.ds(i * nc + cidx, 1), :] for kind in range(4))
            m_st = state[(i, d)][2]
            m_c = jnp.maximum(m_st, cmax_r)
            state[(i, d)][2] = tot_r + m_c
            rows.append((c_r, b_r, m_st, jnp.exp(m_st - m_c), jnp.exp(c_r - m_c) * kscale))
        start = []
        for (i, d, cidx), (_, _, _, a_st, wk) in zip(jobs, rows):
            k = k_ref[0, tok_rows(cidx), hcols[i]]
            vt = vt_ref[0, cidx, hcols[i], :]
            vw = jnp.concatenate([vt.astype(F32) * wk, jnp.broadcast_to(wk, (BF16_ROWS, ll))], axis=0)
            upd = jnp.dot(vw.astype(BF16), k, preferred_element_type=F32)
            c_st, n_st, _ = state[(i, d)]
            start.append((c_st, n_st))
            state[(i, d)][0] = a_st * c_st + upd[:dh]
            state[(i, d)][1] = a_st * n_st + upd[dh:dh + 1]
        prods = []
        for (i, d, cidx), (c_st, n_st) in zip(jobs, start):
            q = q_ref[0, tok_rows(cidx), hcols[i]]
            k = k_ref[0, tok_rows(cidx), hcols[i]]
            n16 = jnp.broadcast_to(n_st.astype(BF16), (BF16_ROWS, dh))
            prods.append(_dot_nt(jnp.concatenate([k, c_st.astype(BF16), n16], axis=0), q))
        for (i, d, cidx), (c_r, b_r, m_st, _, _), r in zip(jobs, rows, prods):
            allowed = (si >= ti) if d else (si <= ti)
            cb = jnp.where(allowed, jnp.broadcast_to(c_r, (ll, ll)).T, -jnp.inf)
            big_m = jnp.maximum(m_st, jnp.max(cb, axis=0, keepdims=True))
            p = jnp.exp(cb - (big_m - float(np.log(kscale)))) * r[:ll]
            w_inter = jnp.exp(m_st - big_m)
            vt = vt_ref[0, cidx, hcols[i], :]
            num = jnp.dot(vt, p.astype(BF16), preferred_element_type=F32) - r[ll:ll + dh] * (-w_inter)
            den = w_inter * r[ll + dh:ll + dh + 1] + jnp.sum(p, axis=0, keepdims=True)
            h_t = num / jnp.maximum(jnp.abs(den), jnp.exp(-(b_r + big_m)))
            if first_touch:
                hs_sc[i, cidx] = h_t
            else:
                hsum = h_t + hs_sc[i, cidx]
                rn = lax.rsqrt(jnp.mean(hsum * hsum, axis=0, keepdims=True) + EPS)
                out_t = (hsum * rn * ghn_t[i]) * _sigmoid(ot_ref[0, cidx, hcols[i], :].astype(F32))
                ha_ref[0, tok_rows(cidx), hcols[i]] = out_t.T.astype(ha_ref.dtype)
        for ch in chains:
            c_sc[ch], n_sc[ch], m_sc[ch] = state[ch]

    def make_body(first_touch):
        def body(j, carry):
            run_trip(j, first_touch)
            return carry
        return body

    trips = nc // unroll
    lax.fori_loop(0, trips // 2, make_body(True), 0)
    lax.fori_loop(trips // 2, trips, make_body(False), 0)

    if emit_state:
        for i, d in chains:
            cout_ref[0, 0, d, i] = c_sc[i, d]
            nout_ref[0, i, d:d + 1, :] = n_sc[i, d]
            mout_ref[0, i, d:d + 1, :] = m_sc[i, d]


def _mlstm(qk, vt, ot, gates_t, g_hn, init, *, emit_state):
    bsz, t, _ = qk.shape
    nc = t // CHUNK_A
    hps, unroll = (H_A, 1) if nc < 8 else (1, 2)
    assert (nc // 2) % unroll == 0 and nc % 2 == 0
    gtt = gates_t.reshape(bsz, 4 * H_A * nc, CHUNK_A)
    wh = hps * DH_A

    def tblk():
        return pl.BlockSpec((1, nc, wh, CHUNK_A), lambda b, g: (b, 0, g, 0))

    in_specs = [pl.BlockSpec((1, t, wh), lambda b, g: (b, 0, g)),
                pl.BlockSpec((1, t, wh), lambda b, g: (b, 0, H_A // hps + g)),
                tblk(), tblk(),
                pl.BlockSpec((1, 4 * H_A * nc, CHUNK_A), lambda b, g: (b, 0, 0)),
                pl.BlockSpec((1, wh), lambda b, g: (0, g))]
    args = [qk, qk, vt, ot, gtt, g_hn.reshape(1, W_A)]
    if init is not None:
        c0, n0, m0 = init
        in_specs += [pl.BlockSpec((1, 2, hps, DH_A, DH_A), lambda b, g: (b, 0, g, 0, 0)),
                     pl.BlockSpec((1, hps, 2, DH_A), lambda b, g: (b, g, 0, 0)),
                     pl.BlockSpec((1, hps, 2, DH_A), lambda b, g: (b, g, 0, 0))]
        args += [c0, n0, m0]
    out_specs = [pl.BlockSpec((1, t, wh), lambda b, g: (b, 0, g))]
    out_shape = [jax.ShapeDtypeStruct((bsz, t, W_A), BF16)]
    if emit_state:
        out_specs += [pl.BlockSpec((1, 1, 2, hps, DH_A, DH_A), lambda b, g: (b, 0, 0, g, 0, 0)),
                      pl.BlockSpec((1, hps, 2, DH_A), lambda b, g: (b, g, 0, 0)),
                      pl.BlockSpec((1, hps, 2, DH_A), lambda b, g: (b, g, 0, 0))]
        out_shape += [jax.ShapeDtypeStruct((bsz, 1, 2, H_A, DH_A, DH_A), F32),
                      jax.ShapeDtypeStruct((bsz, H_A, 2, DH_A), F32),
                      jax.ShapeDtypeStruct((bsz, H_A, 2, DH_A), F32)]
    kern = functools.partial(_mlstm_kernel, nc=nc, hps=hps, unroll=unroll, has_init=init is not None,
                             emit_state=emit_state)
    return pl.pallas_call(
        kern, grid=(bsz, H_A // hps), in_specs=in_specs, out_specs=out_specs, out_shape=out_shape,
        scratch_shapes=[pltpu.VMEM((hps, nc, DH_A, CHUNK_A), F32), pltpu.VMEM((2, 4, hps * nc, CHUNK_A), F32),
                        pltpu.VMEM((hps, 2, DH_A, DH_A), F32), pltpu.VMEM((hps, 2, 1, DH_A), F32),
                        pltpu.VMEM((hps, 2, 1, DH_A), F32)],
        compiler_params=_cparams(("arbitrary", "arbitrary")), name="mlstm",
    )(*args)


def _head_norm(x, g):
    lo = lax.broadcasted_iota(jnp.int32, x.shape, 1) < DH_B
    x2 = x * x
    s_lo = jnp.sum(jnp.where(lo, x2, 0.0), axis=-1, keepdims=True)
    s_hi = jnp.sum(jnp.where(lo, 0.0, x2), axis=-1, keepdims=True)
    ms = jnp.where(lo, s_lo, s_hi) * (1.0 / DH_B)
    return x * lax.rsqrt(ms + EPS) * g


def _rope(x, cos, sin):
    quarter = DH_B // 4
    first = (lax.broadcasted_iota(jnp.int32, x.shape, 1) % (2 * quarter)) < quarter
    partner = jnp.where(first, pltpu.roll(x, LANES - quarter, 1), pltpu.roll(x, quarter, 1))
    return x * cos + partner * sin


def _qkprep_kernel(*refs, rope, emit_kn):
    qkv_ref, gq_ref, gk_ref = refs[:3]
    pos = 3
    if rope:
        cos_ref, sin_ref = refs[pos:pos + 2]
        pos += 2
    q_out, k_out = refs[pos:pos + 2]
    pos += 2
    nq = W_B // LANES
    for j in range(nq + 1):
        x = qkv_ref[0, :, j * LANES:(j + 1) * LANES]
        xn = _head_norm(x, gq_ref[...] if j < nq else gk_ref[...])
        if j == nq and emit_kn:
            refs[pos][0] = xn
        if rope:
            xn = _rope(xn, cos_ref[...], sin_ref[...])
        if j < nq:
            q_out[0, :, j * LANES:(j + 1) * LANES] = (xn * QSCALE).astype(q_out.dtype)
        else:
            k_out[0] = xn.astype(k_out.dtype)


def _qkprep(qkv, g_q, g_k, rope_tabs, *, emit_kn, tm):
    bsz, t, w = qkv.shape
    in_specs = [pl.BlockSpec((1, tm, w), lambda b, i: (b, i, 0)),
                pl.BlockSpec((1, LANES), lambda b, i: (0, 0)),
                pl.BlockSpec((1, LANES), lambda b, i: (0, 0))]
    args = [qkv, jnp.tile(g_q, 2).reshape(1, LANES), jnp.tile(g_k, 2).reshape(1, LANES)]
    if rope_tabs is not None:
        in_specs += [pl.BlockSpec((tm, LANES), lambda b, i: (i, 0))] * 2
        args += list(rope_tabs)
    out_specs = [pl.BlockSpec((1, tm, W_B), lambda b, i: (b, i, 0)),
                 pl.BlockSpec((1, tm, LANES), lambda b, i: (b, i, 0))]
    out_shape = [jax.ShapeDtypeStruct((bsz, t, W_B), BF16),
                 jax.ShapeDtypeStruct((bsz, t, LANES), BF16)]
    if emit_kn:
        out_specs.append(pl.BlockSpec((1, tm, LANES), lambda b, i: (b, i, 0)))
        out_shape.append(jax.ShapeDtypeStruct((bsz, t, LANES), F32))
    kern = functools.partial(_qkprep_kernel, rope=rope_tabs is not None, emit_kn=emit_kn)
    return pl.pallas_call(
        kern, grid=(bsz, t // tm), in_specs=in_specs, out_specs=out_specs, out_shape=out_shape,
        compiler_params=_cparams(("arbitrary", "arbitrary")), name="qk_prep",
    )(*args)


def _rope_tables(n_tok):
    t = jnp.arange(n_tok)
    row = (t // GRID_W).astype(F32)
    colp = (t % GRID_W).astype(F32)
    quarter = DH_B // 4
    freqs = ROPE_THETA ** (-jnp.arange(quarter, dtype=F32) / quarter)
    ar = row[:, None] * freqs
    ac = colp[:, None] * freqs
    cos = jnp.concatenate([jnp.cos(ar), jnp.cos(ar), jnp.cos(ac), jnp.cos(ac)], axis=-1)
    sin = jnp.concatenate([-jnp.sin(ar), jnp.sin(ar), -jnp.sin(ac), jnp.sin(ac)], axis=-1)
    return jnp.tile(cos, (1, 2)), jnp.tile(sin, (1, 2))


def _split_heads_q(q):
    lo = lax.broadcasted_iota(jnp.int32, q.shape, 1) < (LANES // 2)
    zero = jnp.zeros_like(q)
    return jnp.concatenate([jnp.where(lo, q, zero), jnp.where(lo, zero, q)], axis=0)


def _merge_heads_o(o_t, tq):
    half = LANES // 2
    return jnp.concatenate([o_t[:half, :tq], o_t[half:, tq:]], axis=0).T


def _with_ones_rows(vt):
    return jnp.concatenate([vt, jnp.ones((BF16_ROWS, vt.shape[1]), BF16)], axis=0)


def _online_softmax_pv_t(problems):
    seq = [(pi, ci) for pi, (_, chunks, _) in enumerate(problems) for ci in range(len(chunks))]
    scores = {}

    def issue(t):
        pi, ci = seq[t]
        qm, chunks, _ = problems[pi]
        scores[(pi, ci)] = _dot_nt(chunks[ci][0], qm)

    for t in range(min(QK_AHEAD, len(seq))):
        issue(t)
    m = acc = None
    for t, (pi, ci) in enumerate(seq):
        if t + QK_AHEAD < len(seq):
            issue(t + QK_AHEAD)
        _, chunks, emit = problems[pi]
        _, vt, penalty = chunks[ci]
        s = scores.pop((pi, ci))
        if penalty is not None:
            s = s - penalty
        mc = jnp.max(s, axis=0, keepdims=True)
        m_new = mc if ci == 0 else jnp.maximum(m, mc)
        p = jnp.exp2(s - m_new).astype(BF16)
        part = jnp.dot(_with_ones_rows(vt), p, preferred_element_type=F32)
        acc = part if ci == 0 else jnp.exp2(m - m_new) * acc + part
        m = m_new
        if ci == len(chunks) - 1:
            dv = vt.shape[0]
            emit(acc[:dv] / acc[dv:dv + 1])


def _attn_kernel(*refs, n_tiles, kv_shared, has_cache):
    q_ref, k_ref, vt_ref = refs[:3]
    pos = 3
    if has_cache:
        kc_ref, vtc_ref = refs[pos:pos + 2]
        pos += 2
    o_ref = refs[pos]
    tq = q_ref.shape[1]
    s_len = k_ref.shape[1]
    sc = min(s_len, KEY_CHUNK)
    problems = []
    for j in range(n_tiles):
        kj = 0 if kv_shared else j
        kcols = slice(kj * LANES, (kj + 1) * LANES)
        qm = _split_heads_q(q_ref[0, :, j * LANES:(j + 1) * LANES])
        chunks = [(k_ref[0, c * sc:(c + 1) * sc, kcols].astype(BF16),
                   vt_ref[0, kcols, c * sc:(c + 1) * sc], None) for c in range(s_len // sc)]
        if has_cache:
            chunks.append((kc_ref[0], vtc_ref[0], None))

        def emit(o_t, j=j):
            o_ref[0, :, j * LANES:(j + 1) * LANES] = _merge_heads_o(o_t, tq).astype(o_ref.dtype)

        problems.append((qm, chunks, emit))
    _online_softmax_pv_t(problems)


def _attention(q, k, vt, cache, *, kv_shared, tq, n_tiles):
    bsz, t, w = q.shape
    s = k.shape[1]
    wt = n_tiles * LANES
    if kv_shared:
        k_spec = pl.BlockSpec((1, s, LANES), lambda b, i, g: (b, 0, 0))
        vt_spec = pl.BlockSpec((1, LANES, s), lambda b, i, g: (b, 0, 0))
    else:
        k_spec = pl.BlockSpec((1, s, wt), lambda b, i, g: (b, 0, g))
        vt_spec = pl.BlockSpec((1, wt, s), lambda b, i, g: (b, g, 0))
    in_specs = [pl.BlockSpec((1, tq, wt), lambda b, i, g: (b, i, g)), k_spec, vt_spec]
    args = [q, k, vt]
    if cache is not None:
        p = cache[0].shape[1]
        in_specs += [pl.BlockSpec((1, p, LANES), lambda b, i, g: (b, 0, 0)),
                     pl.BlockSpec((1, LANES, p), lambda b, i, g: (b, 0, 0))]
        args += list(cache)
    kern = functools.partial(_attn_kernel, n_tiles=n_tiles, kv_shared=kv_shared,
                             has_cache=cache is not None)
    return pl.pallas_call(
        kern, grid=(bsz, t // tq, w // wt), in_specs=in_specs,
        out_specs=pl.BlockSpec((1, tq, wt), lambda b, i, g: (b, i, g)),
        out_shape=jax.ShapeDtypeStruct((bsz, t, w), BF16),
        compiler_params=_cparams(("arbitrary", "arbitrary", "arbitrary")), name="attention",
    )(*args)


def _na_bias_blocks(variant):
    out = {}
    for qr in range(NA_ROWS):
        for kr in range(NA_WIN):
            if variant == 0:
                dr = kr - qr if kr < WIN_R else None
            elif variant == 1:
                dr = kr - qr - WIN_R // 2 if qr <= kr < qr + WIN_R else None
            else:
                dr = kr - qr - (NA_WIN - NA_ROWS) if kr >= NA_WIN - WIN_R else None
            out[(qr, kr)] = None if dr is None else dr + WIN_R - 1
    return out


def _na_kernel(q_ref, k_ref, vt_ref, kc_ref, vtc_ref, bc_ref, o_ref, bias_sc, *, n_rb):
    w = GRID_W
    tq = NA_ROWS * w

    @pl.when(jnp.logical_and(pl.program_id(1) == 0, pl.program_id(2) == 0))
    def _():
        for variant in range(3):
            for (qr, kr), di in _na_bias_blocks(variant).items():
                for hh in range(2 * NA_TILES):
                    val = jnp.full((w, w), MASKED, F32) if di is None else bc_ref[hh, di]
                    bias_sc[variant, kr * w:(kr + 1) * w, hh * tq + qr * w:hh * tq + (qr + 1) * w] = val

    n_rows = n_rb * NA_ROWS
    problems = []
    for rr in range(NA_RBS):
        rb = pl.program_id(2) * NA_RBS + rr
        variant = jnp.where(rb == 0, 0, jnp.where(rb == n_rb - 1, 2, 1))
        ws = jnp.clip(rb * NA_ROWS - WIN_R // 2, 0, n_rows - NA_WIN)
        blk0 = ws // NA_ROWS
        start = ws * w
        qrows = slice(rr * tq, (rr + 1) * tq)
        for j in range(NA_TILES):
            cols = slice(j * LANES, (j + 1) * LANES)
            qcols = slice(j * 2 * tq, (j + 1) * 2 * tq)
            qm = _split_heads_q(q_ref[0, qrows, cols])
            chunks = [(k_ref[0, pl.ds(pl.multiple_of(start + i * tq, tq), tq), cols],
                       vt_ref[0, blk0 + i, cols, :],
                       bias_sc[variant, i * tq:(i + 1) * tq, qcols]) for i in range(NA_BLKS)]
            chunks.append((kc_ref[0, :, cols], vtc_ref[0, cols, :], None))

            def emit(o_t, qrows=qrows, cols=cols):
                o_ref[0, qrows, cols] = _merge_heads_o(o_t, tq).astype(o_ref.dtype)

            problems.append((qm, chunks, emit))
    _online_softmax_pv_t(problems)


def _na_bias_table(rpb):
    c = np.arange(GRID_W)
    cs = np.clip(c - WIN_C // 2, 0, GRID_W - WIN_C)
    ck = np.arange(GRID_W)
    valid = (ck[:, None] >= cs[None, :]) & (ck[:, None] < cs[None, :] + WIN_C)
    idx = np.clip(ck[:, None] - c[None, :] + WIN_C - 1, 0, 2 * WIN_C - 2)
    onehot = (idx[..., None] == np.arange(2 * WIN_C - 1)).astype(np.float32)
    tab = jnp.einsum('hrx,kcx->hrkc', rpb, jnp.asarray(onehot), precision=lax.Precision.HIGHEST)
    return jnp.where(jnp.asarray(valid), tab * -LOG2E, MASKED)


def _na_attention(q, k, vt, kc, vtc, rpb):
    bsz, t, w = q.shape
    p = kc.shape[1]
    tq = NA_ROWS * GRID_W
    n_rb = t // tq
    bc = _na_bias_table(rpb)
    wt = NA_TILES * LANES
    tqs = NA_RBS * tq
    return pl.pallas_call(
        functools.partial(_na_kernel, n_rb=n_rb),
        grid=(w // wt, bsz, n_rb // NA_RBS),
        in_specs=[pl.BlockSpec((1, tqs, wt), lambda j, b, r: (b, r, j)),
                  pl.BlockSpec((1, t, wt), lambda j, b, r: (b, 0, j)),
                  pl.BlockSpec((1, n_rb, wt, tq), lambda j, b, r: (b, 0, j, 0)),
                  pl.BlockSpec((1, p, wt), lambda j, b, r: (b, 0, j)),
                  pl.BlockSpec((1, wt, p), lambda j, b, r: (b, j, 0)),
                  pl.BlockSpec((2 * NA_TILES, 2 * WIN_R - 1, GRID_W, GRID_W), lambda j, b, r: (j, 0, 0, 0))],
        out_specs=pl.BlockSpec((1, tqs, wt), lambda j, b, r: (b, r, j)),
        out_shape=jax.ShapeDtypeStruct((bsz, t, w), BF16),
        scratch_shapes=[pltpu.VMEM((3, NA_WIN * GRID_W, NA_TILES * 2 * tq), F32)],
        compiler_params=_cparams(("arbitrary", "arbitrary", "arbitrary")), name="na_attention",
    )(q, k, vt, kc, vtc, bc)


_GQA_PERM = np.array([0, 4, 1, 5, 2, 6, 3, 7])


def _perm_heads(w, axis):
    shp = w.shape
    n = shp[axis] // DH_B
    w = w.reshape(shp[:axis] + (n, DH_B) + shp[axis + 1:])
    w = jnp.take(w, jnp.asarray(_GQA_PERM), axis=axis)
    return w.reshape(shp)


def _tok_major(cache):
    b, h, p, dh = cache.shape
    return cache.transpose(0, 2, 1, 3).reshape(b, p, h * dh)


def _feat_major(cache):
    b, h, p, dh = cache.shape
    return cache.transpose(0, 1, 3, 2).reshape(b, h * dh, p)


def _head_major(x, n_heads):
    b, t, w = x.shape
    return x.reshape(b, t, n_heads, w // n_heads).transpose(0, 2, 1, 3)


def _even_layer(xp, xs, mod, g_pre, g_post, w_in, b_gates, g_hn, g_q, g_k, w_out, st_c, st_n, st_m,
                ck, cv, rope_tabs):
    o_v, o_o = 2 * W_A, 3 * W_A
    o_g = 4 * W_A
    o_qb = o_g + 4 * H_A
    o_kb = o_qb + W_B
    o_vb = o_kb + HKV_B * DH_B
    o_z = o_vb + HKV_B * DH_B
    w_qk = w_in[:, :o_v].astype(BF16)
    w_vta = w_in[:, o_v:o_o].T.astype(BF16)
    w_ota = w_in[:, o_o:o_g].T.astype(BF16)
    w_g = w_in[:, o_g:o_qb].astype(BF16)
    w_qkv = jnp.concatenate([_perm_heads(w_in[:, o_qb:o_kb], 1), w_in[:, o_kb:o_z]], axis=1).astype(BF16)
    w_z = jnp.concatenate([w_in[:, o_z:o_z + W_A], _perm_heads(w_in[:, o_z + W_A:], 1)], axis=1).astype(BF16)
    w_o = jnp.concatenate([w_out[:W_A], _perm_heads(w_out[W_A:], 0)], axis=0).astype(BF16)
    w_vt = w_in[:, o_vb:o_z].T.astype(BF16)

    def stream(x, row0, per_batch, init, cache, rope, emit):
        bsz, t, _ = x.shape
        tm = 512
        xf = x if per_batch else x.reshape(1, bsz * t, D_MODEL)
        qk, qkv, z, vta, ota, vbt, gtt = _proj(
            xf, g_pre, mod, [w_qk, w_qkv, w_z], [BF16, F32, BF16], [w_vta, w_ota, w_vt],
            [CHUNK_A, CHUNK_A, None if per_batch else t], (w_g, b_gates),
            row0=row0, per_batch=per_batch, tm=tm)
        if not per_batch:
            qk, qkv, z = (a.reshape(bsz, t, a.shape[-1]) for a in (qk, qkv, z))
            gtt = gtt.reshape(4 * H_A, bsz, t).transpose(1, 0, 2)
            vbt = vbt.reshape(bsz, HKV_B * DH_B, t)
            vta, ota = (a.reshape(bsz, t // CHUNK_A, W_A, CHUNK_A) for a in (vta, ota))
        res = _mlstm(qk, vta, ota, gtt, g_hn, init, emit_state=emit)
        ha = res[0]
        prep = _qkprep(qkv, g_q, g_k, rope, emit_kn=emit, tm=min(t, 512))
        qn, kn = prep[0], prep[1]
        vb = qkv[..., W_B + HKV_B * DH_B:]
        if per_batch:
            hb = _attention(qn, kn, vbt, cache, kv_shared=True, tq=512, n_tiles=1)
        else:
            hb = _attention(qn, kn, vbt, cache, kv_shared=True, tq=t, n_tiles=W_B // LANES)
        y = _outproj([ha.reshape(xf.shape[0], -1, W_A), hb.reshape(xf.shape[0], -1, W_B)],
                     z.reshape(xf.shape[0], -1, W_A + W_B), w_o, xf, g_post, mod,
                     row0=row0, per_batch=per_batch, tm=tm)
        return y.reshape(bsz, t, D_MODEL), res[1:], (prep[2] if emit else None), vb

    yp, st, kn_p, vb_p = stream(xp, 0, False, None, None, None, True)
    n0 = st_n.transpose(0, 2, 1, 3)
    m0 = jnp.broadcast_to(st_m.transpose(0, 2, 1)[..., None], n0.shape)
    cache = (_tok_major(ck).astype(BF16), _feat_major(cv).astype(BF16))
    ys, _, _, _ = stream(xs, 1, True, (st_c, n0, m0), cache, rope_tabs, False)
    c_out, n_out, m_out = st
    new_n = n_out.transpose(0, 2, 1, 3)[:, None]
    new_m = m_out[..., 0].transpose(0, 2, 1)[:, None]
    new_gk = _head_major(kn_p, HKV_B)[:, None]
    new_gv = _head_major(vb_p, HKV_B)[:, None]
    return yp, ys, c_out, new_n, new_m, new_gk, new_gv


def _odd_layer(xp, xs, mod, g_pre, g_post, w_in, rpb, w_out, ck, cv):
    ws = [(w_in[:, i * W_C:(i + 1) * W_C] * (QSCALE if i == 0 else 1.0)).astype(BF16) for i in range(4)]
    w_o = w_out.astype(BF16)
    tm = 512
    bsz, t, _ = xp.shape
    xf = xp.reshape(1, bsz * t, D_MODEL)
    w_vt = ws[2].T
    q, k, v, z, vt = _proj(xf, g_pre, mod, ws, [BF16, F32, F32, BF16], [w_vt], [t], None,
                           row0=0, per_batch=False, tm=tm)
    k = k.reshape(bsz, t, W_C)
    v = v.reshape(bsz, t, W_C)
    o = _attention(q.reshape(bsz, t, W_C), k, vt.reshape(bsz, W_C, t), None,
                   kv_shared=False, tq=t, n_tiles=W_C // LANES)
    yp = _outproj([o.reshape(1, bsz * t, W_C)], z, w_o, xf, g_post, mod, row0=0, per_batch=False, tm=tm)
    yp = yp.reshape(bsz, t, D_MODEL)
    new_nk = _head_major(k, H_C)[:, None]
    new_nv = _head_major(v, H_C)[:, None]
    q, k, z, vt = _proj(xs, g_pre, mod, [ws[0], ws[1], ws[3]], [BF16, BF16, BF16], [w_vt],
                        [NA_ROWS * GRID_W], None, row0=1, per_batch=True, tm=tm)
    o = _na_attention(q, k, vt, _tok_major(ck).astype(BF16), _feat_major(cv).astype(BF16), rpb)
    ys = _outproj([o], z, w_o, xs, g_post, mod, row0=1, per_batch=True, tm=tm)
    return yp, ys, new_nk, new_nv


def kernel(x_prompt, x_sample, state_mlstm_C, state_mlstm_n, state_mlstm_m, cache_gqa_k, cache_gqa_v,
           cache_na_k, cache_na_v, c, c_ctx, w_mod, b_mod, g_pre, g_post, w_in_ab, b_gates_ab, g_hnorm_a,
           g_qnorm_b, g_knorm_b, w_out_ab, w_in_c, rpb_c, w_out_c):
    depth = w_mod.shape[0]
    assert depth == 2 and c.shape[0] == 2
    cvec = jnp.zeros((8, D_MODEL), F32).at[0].set(c_ctx).at[1:1 + c.shape[0]].set(c)
    mod = _modulation(cvec, w_mod, b_mod)
    rope_tabs = _rope_tables(x_sample.shape[1])
    xp, xs, c_out, n_out, m_out, gk, gv = _even_layer(
        x_prompt, x_sample, mod[0], g_pre[0], g_post[0], w_in_ab[0], b_gates_ab[0], g_hnorm_a[0],
        g_qnorm_b[0], g_knorm_b[0], w_out_ab[0], state_mlstm_C[:, 0], state_mlstm_n[:, 0],
        state_mlstm_m[:, 0], cache_gqa_k[:, 0], cache_gqa_v[:, 0], rope_tabs)
    xp, xs, nk, nv = _odd_layer(xp, xs, mod[1], g_pre[1], g_post[1], w_in_c[0], rpb_c[0], w_out_c[0],
                                cache_na_k[:, 0], cache_na_v[:, 0])
    return (xp, xs, c_out, n_out, m_out, gk, gv, nk, nv)
```

```python
import functools

import jax
import jax.numpy as jnp
import numpy as np
from jax import lax
from jax.experimental import pallas as pl
from jax.experimental.pallas import tpu as pltpu

F32 = jnp.float32
BF16 = jnp.bfloat16

D_MODEL = 1024
GRID_W = 64
EPS = 1e-6
H_A = 4
DH_A = 128
W_A = H_A * DH_A
CHUNK_A = 128
HQ_B = 8
HKV_B = 2
DH_B = 64
W_B = HQ_B * DH_B
ROPE_THETA = 10000.0
H_C = 16
DH_C = 64
W_C = H_C * DH_C
WIN_R = 8
WIN_C = 16

LANES = 128
NA_ROWS = 4
NA_TILES = 2
NA_RBS = 2
NA_BLKS = -(-(NA_ROWS + WIN_R - 1) // NA_ROWS)
NA_WIN = NA_BLKS * NA_ROWS
assert WIN_R // 2 == NA_ROWS
BF16_ROWS = 16
KEY_CHUNK = 256
QK_AHEAD = 3
LOG2E = 1.4426950408889634
QSCALE = DH_B ** -0.5 * LOG2E
assert DH_B == DH_C
MASKED = 1e30
VMEM_LIMIT = 56 * 1024 * 1024


def _cparams(sem):
    return pltpu.CompilerParams(dimension_semantics=sem, vmem_limit_bytes=VMEM_LIMIT)


def _silu(x):
    return x / (1.0 + jnp.exp(-x))


def _sigmoid(x):
    return 1.0 / (1.0 + jnp.exp(-x))


def _log_sigmoid(x):
    return jnp.minimum(x, 0.0) - jnp.log1p(jnp.exp(-jnp.abs(x)))


def _dot_nt(a, b):
    return lax.dot_general(a, b, (((1,), (1,)), ((), ())), preferred_element_type=F32)


def _mod_kernel(c_ref, w_ref, b_ref, o_ref):
    s = _silu(c_ref[...])
    o_ref[0] = jnp.dot(s, w_ref[0], preferred_element_type=F32,
                       precision=lax.Precision.HIGHEST) + b_ref[0]


def _modulation(cvec, w_mod, b_mod):
    depth, d, n = w_mod.shape
    tn = n // 4
    return pl.pallas_call(
        _mod_kernel,
        grid=(depth, n // tn),
        in_specs=[pl.BlockSpec((8, d), lambda l, j: (0, 0)),
                  pl.BlockSpec((1, d, tn), lambda l, j: (l, 0, j)),
                  pl.BlockSpec((1, 1, tn), lambda l, j: (l, 0, j))],
        out_specs=pl.BlockSpec((1, 8, tn), lambda l, j: (l, 0, j)),
        out_shape=jax.ShapeDtypeStruct((depth, 8, n), F32),
        compiler_params=_cparams(("arbitrary", "arbitrary")),
        name="modulation",
    )(cvec, w_mod, b_mod.reshape(depth, 1, n))


def _head_norm(x, g):
    lo = lax.broadcasted_iota(jnp.int32, x.shape, 1) < DH_B
    x2 = x * x
    s_lo = jnp.sum(jnp.where(lo, x2, 0.0), axis=-1, keepdims=True)
    s_hi = jnp.sum(jnp.where(lo, 0.0, x2), axis=-1, keepdims=True)
    ms = jnp.where(lo, s_lo, s_hi) * (1.0 / DH_B)
    return x * lax.rsqrt(ms + EPS) * g


def _rope(x, cos, sin):
    quarter = DH_B // 4
    first = (lax.broadcasted_iota(jnp.int32, x.shape, 1) % (2 * quarter)) < quarter
    partner = jnp.where(first, pltpu.roll(x, LANES - quarter, 1), pltpu.roll(x, quarter, 1))
    return x * cos + partner * sin


def _store_t(ref, val, t_block):
    if t_block is None:
        ref[0] = val.astype(ref.dtype)
    else:
        for i in range(val.shape[1] // t_block):
            ref[0, i] = val[:, i * t_block:(i + 1) * t_block].astype(ref.dtype)


def _proj_kernel(*refs, n_seg, t_blocks, gate_rows, gqa, row0, per_batch):
    n_tseg = len(t_blocks) + (1 if gate_rows else 0)
    x_ref, g_ref, mod_ref = refs[:3]
    pos = 3
    w_refs = refs[pos:pos + n_seg]
    pos += n_seg
    wt_refs = refs[pos:pos + n_tseg]
    pos += n_tseg
    if gate_rows:
        bgt_ref = refs[pos]
        pos += 1
    if gqa is not None:
        rope, knt_block = gqa
        gq_ref, gk_ref = refs[pos:pos + 2]
        pos += 2
        if rope:
            cos_ref, sin_ref = refs[pos:pos + 2]
            pos += 2
    n_plain = n_seg - (1 if gqa is not None else 0)
    o_refs = refs[pos:pos + n_plain]
    pos += n_plain
    ot_refs = refs[pos:pos + len(t_blocks)]
    pos += len(t_blocks)
    if gate_rows:
        gto_ref = refs[pos]
        pos += 1
    if gqa is not None:
        q_out, k_out = refs[pos:pos + 2]
        pos += 2
        if knt_block is not None:
            knt_out = refs[pos]
            pos += 1
    wt_scs = refs[pos:pos + n_tseg]

    @pl.when(jnp.logical_and(pl.program_id(0) == 0, pl.program_id(1) == 0))
    def _():
        for wt_ref, sc in zip(wt_refs, wt_scs):
            sc[...] = wt_ref[...].T

    d = x_ref.shape[-1]
    row = row0 + (pl.program_id(0) if per_batch else 0)
    shift = mod_ref[pl.ds(row, 1), 0:d]
    scale = mod_ref[pl.ds(row, 1), d:2 * d]
    x = x_ref[0]
    r = lax.rsqrt(jnp.mean(x * x, axis=-1, keepdims=True) + EPS)
    h = (x * r * g_ref[...]) * (1.0 + scale) + shift
    hb = h.astype(BF16)
    for w_ref, o_ref in zip(w_refs[:n_plain], o_refs):
        o_ref[0] = jnp.dot(hb, w_ref[...], preferred_element_type=F32).astype(o_ref.dtype)
    for sc, ot_ref, tb in zip(wt_scs, ot_refs, t_blocks):
        _store_t(ot_ref, _dot_nt(sc[...], hb), tb)
    if gate_rows:
        gto_ref[0] = _dot_nt(wt_scs[-1][0:gate_rows, :], hb) + bgt_ref[...]
    if gqa is not None:
        res = jnp.dot(hb, w_refs[-1][...], preferred_element_type=F32)
        nq = W_B // LANES
        for j in range(nq + 1):
            xn = _head_norm(res[:, j * LANES:(j + 1) * LANES], gq_ref[...] if j < nq else gk_ref[...])
            if j == nq and knt_block is not None:
                _store_t(knt_out, xn.T, knt_block)
            if rope:
                xn = _rope(xn, cos_ref[...], sin_ref[...])
            if j < nq:
                q_out[0, :, j * LANES:(j + 1) * LANES] = (xn * QSCALE).astype(q_out.dtype)
            else:
                k_out[0] = xn.astype(k_out.dtype)


def _proj(x, g_pre, mod, segs, tsegs, gates, gqa, *, row0, per_batch, tm):
    bsz, t, d = x.shape
    grid = (bsz, t // tm)
    const = lambda b, i: (0, 0)
    in_specs = [pl.BlockSpec((1, tm, d), lambda b, i: (b, i, 0)),
                pl.BlockSpec((1, d), const),
                pl.BlockSpec(mod.shape, const)]
    args = [x, g_pre.reshape(1, d), mod]
    plain_ws = [w for w, _ in segs] + ([gqa["w"]] if gqa is not None else [])
    t_ws = [w for w, _, _ in tsegs] + ([gates[0]] if gates is not None else [])
    for w in plain_ws + t_ws:
        in_specs.append(pl.BlockSpec(w.shape, const))
        args.append(w)
    gate_rows = 0
    if gates is not None:
        gate_rows = gates[1].shape[0]
        in_specs.append(pl.BlockSpec((gate_rows, 1), const))
        args.append(gates[1].reshape(gate_rows, 1))
    if gqa is not None:
        in_specs += [pl.BlockSpec((1, LANES), const)] * 2
        args += [jnp.tile(gqa["g_q"], 2).reshape(1, LANES), jnp.tile(gqa["g_k"], 2).reshape(1, LANES)]
        if gqa["rope"] is not None:
            in_specs += [pl.BlockSpec((tm, LANES), lambda b, i: (i, 0))] * 2
            args += list(gqa["rope"])
    out_specs, out_shape = [], []
    for w, dt in segs:
        n = w.shape[1]
        out_specs.append(pl.BlockSpec((1, tm, n), lambda b, i: (b, i, 0)))
        out_shape.append(jax.ShapeDtypeStruct((bsz, t, n), dt))

    def add_t_out(n, dt, tb):
        if tb is None:
            out_specs.append(pl.BlockSpec((1, n, tm), lambda b, i: (b, 0, i)))
            out_shape.append(jax.ShapeDtypeStruct((bsz, n, t), dt))
        else:
            out_specs.append(pl.BlockSpec((1, tm // tb, n, tb), lambda b, i: (b, i, 0, 0)))
            out_shape.append(jax.ShapeDtypeStruct((bsz, t // tb, n, tb), dt))

    for w, dt, tb in tsegs:
        add_t_out(w.shape[1], dt, tb)
    if gates is not None:
        add_t_out(gate_rows, F32, None)
    gqa_static = None
    if gqa is not None:
        out_specs += [pl.BlockSpec((1, tm, W_B), lambda b, i: (b, i, 0)),
                      pl.BlockSpec((1, tm, LANES), lambda b, i: (b, i, 0))]
        out_shape += [jax.ShapeDtypeStruct((bsz, t, W_B), BF16), jax.ShapeDtypeStruct((bsz, t, LANES), BF16)]
        if gqa["knt_block"] is not None:
            add_t_out(LANES, F32, gqa["knt_block"])
        gqa_static = (gqa["rope"] is not None, gqa["knt_block"])
    kern = functools.partial(_proj_kernel, n_seg=len(plain_ws), t_blocks=tuple(tb for _, _, tb in tsegs),
                             gate_rows=gate_rows, gqa=gqa_static, row0=row0, per_batch=per_batch)
    return pl.pallas_call(
        kern, grid=grid, in_specs=in_specs, out_specs=out_specs, out_shape=out_shape,
        scratch_shapes=[pltpu.VMEM((w.shape[1], d), BF16) for w in t_ws],
        compiler_params=_cparams(("arbitrary", "arbitrary")), name="in_proj",
    )(*args)


def _outproj_kernel(*refs, n_in, row0, per_batch):
    a_refs = refs[:n_in]
    z_ref, w_ref, x_ref, gp_ref, mod_ref, o_ref = refs[n_in:n_in + 6]
    d = x_ref.shape[-1]
    row = row0 + (pl.program_id(0) if per_batch else 0)
    gate = mod_ref[pl.ds(row, 1), 2 * d:3 * d]
    z = z_ref[0].astype(F32)
    sz = _silu(z)
    acc = None
    off = 0
    for a_ref in a_refs:
        kk = a_ref.shape[-1]
        y = (a_ref[0].astype(F32) * sz[:, off:off + kk]).astype(BF16)
        part = jnp.dot(y, w_ref[off:off + kk, :], preferred_element_type=F32)
        acc = part if acc is None else acc + part
        off += kk
    r = lax.rsqrt(jnp.mean(acc * acc, axis=-1, keepdims=True) + EPS)
    o_ref[0] = x_ref[0] + gate * (acc * r * gp_ref[...])


def _outproj(parts, z, w_out, x, g_post, mod, *, row0, per_batch, tm):
    bsz, t, d = x.shape
    grid = (bsz, t // tm)
    in_specs, args = [], []
    for a in parts:
        in_specs.append(pl.BlockSpec((1, tm, a.shape[-1]), lambda b, i: (b, i, 0)))
        args.append(a)
    in_specs += [pl.BlockSpec((1, tm, z.shape[-1]), lambda b, i: (b, i, 0)),
                 pl.BlockSpec(w_out.shape, lambda b, i: (0, 0)),
                 pl.BlockSpec((1, tm, d), lambda b, i: (b, i, 0)),
                 pl.BlockSpec((1, d), lambda b, i: (0, 0)),
                 pl.BlockSpec(mod.shape, lambda b, i: (0, 0))]
    args += [z, w_out, x, g_post.reshape(1, d), mod]
    kern = functools.partial(_outproj_kernel, n_in=len(parts), row0=row0, per_batch=per_batch)
    return pl.pallas_call(
        kern, grid=grid, in_specs=in_specs,
        out_specs=pl.BlockSpec((1, tm, d), lambda b, i: (b, i, 0)),
        out_shape=jax.ShapeDtypeStruct((bsz, t, d), F32),
        compiler_params=_cparams(("arbitrary", "arbitrary")), name="out_proj",
    )(*args)


def _split3_bf16(x):
    hi = x.astype(BF16)
    r1 = x - hi.astype(F32)
    mid = r1.astype(BF16)
    lo = (r1 - mid.astype(F32)).astype(BF16)
    return hi, mid, lo


def _mlstm_gate_rows(ig_f, fg_f, ig_b, fg_b):
    rr, ll = fg_f.shape
    lf = _log_sigmoid(jnp.concatenate([fg_f, fg_b], axis=0))
    pieces = jnp.concatenate(_split3_bf16(lf), axis=0)
    u = lax.broadcasted_iota(jnp.int32, (ll, ll), 0)
    t = lax.broadcasted_iota(jnp.int32, (ll, ll), 1)
    out = []
    for d, ig in enumerate((ig_f, ig_b)):
        tri = jnp.where((u >= t) if d else (u <= t), 1.0, 0.0).astype(BF16)
        y = jnp.dot(pieces, tri, preferred_element_type=F32)
        rows = slice(d * rr, (d + 1) * rr)
        b = y[0:2 * rr][rows] + y[2 * rr:4 * rr][rows] + y[4 * rr:6 * rr][rows]
        c = ig - b
        cmax = jnp.broadcast_to(jnp.max(c, axis=-1, keepdims=True), c.shape)
        tot = jnp.broadcast_to(jnp.sum(lf[rows], axis=-1, keepdims=True), c.shape)
        out.append((c, b, cmax, tot))
    return out


def _mlstm_kernel(*refs, nc, hps, unroll, has_init, emit_state):
    q_ref, k_ref, vt_ref, ot_ref, gtt_ref, ghn_ref = refs[:6]
    pos = 6
    if has_init:
        c0_ref, n0_ref, m0_ref = refs[pos:pos + 3]
        pos += 3
    ha_ref = refs[pos]
    pos += 1
    if emit_state:
        cout_ref, nout_ref, mout_ref = refs[pos:pos + 3]
        pos += 3
    hs_sc, gate_sc, c_sc, n_sc, m_sc = refs[pos:pos + 5]
    ll, dh = CHUNK_A, DH_A
    kscale = dh ** -0.5
    hd0 = pl.program_id(1) * hps
    chains = [(i, d) for i in range(hps) for d in range(2)]

    for i, d in chains:
        if has_init:
            c_sc[i, d] = c0_ref[0, d, i]
            n_sc[i, d] = n0_ref[0, i, d:d + 1, :]
            m_sc[i, d] = m0_ref[0, i, d:d + 1, :]
        else:
            c_sc[i, d] = jnp.zeros((dh, dh), F32)
            n_sc[i, d] = jnp.zeros((1, dh), F32)
            m_sc[i, d] = jnp.zeros((1, dh), F32)

    def gate_rows(col0):
        return gtt_ref[0, pl.ds(pl.multiple_of((col0 + hd0) * nc, 8), hps * nc), :]

    gates = _mlstm_gate_rows(gate_rows(0), gate_rows(H_A), gate_rows(2 * H_A), gate_rows(3 * H_A))
    for d in range(2):
        for kind in range(4):
            gate_sc[d, kind] = gates[d][kind]

    ghn_t = [jnp.broadcast_to(ghn_ref[:, i * dh:(i + 1) * dh], (ll, dh)).T for i in range(hps)]
    si = lax.broadcasted_iota(jnp.int32, (ll, ll), 0)
    ti = lax.broadcasted_iota(jnp.int32, (ll, ll), 1)

    def run_trip(j, first_touch):
        jobs = []
        for u in range(unroll):
            step = j * unroll + u
            for i, d in chains:
                jobs.append((i, d, step if d == 0 else nc - 1 - step))
        state = {ch: [c_sc[ch], n_sc[ch], m_sc[ch]] for ch in chains}
        hcols = [slice(i * dh, (i + 1) * dh) for i in range(hps)]

        def tok_rows(cidx):
            return pl.ds(pl.multiple_of(cidx * ll, ll), ll)

        rows = []
        for i, d, cidx in jobs:
            c_r, b_r, cmax_r, tot_r = (gate_sc[d, kind, pl.ds(i * nc + cidx, 1), :] for kind in range(4))
            m_st = state[(i, d)][2]
            m_c = jnp.maximum(m_st, cmax_r)
            state[(i, d)][2] = tot_r + m_c
            rows.append((c_r, b_r, m_st, jnp.exp(m_st - m_c), jnp.exp(c_r - m_c) * kscale))
        start = []
        for (i, d, cidx), (_, _, _, a_st, wk) in zip(jobs, rows):
            k = k_ref[0, tok_rows(cidx), hcols[i]]
            vt = vt_ref[0, cidx, hcols[i], :]
            vw = jnp.concatenate([vt.astype(F32) * wk, jnp.broadcast_to(wk, (BF16_ROWS, ll))], axis=0)
            upd = jnp.dot(vw.astype(BF16), k, preferred_element_type=F32)
            c_st, n_st, _ = state[(i, d)]
            start.append((c_st, n_st))
            state[(i, d)][0] = a_st * c_st + upd[:dh]
            state[(i, d)][1] = a_st * n_st + upd[dh:dh + 1]
        prods = []
        for (i, d, cidx), (c_st, n_st) in zip(jobs, start):
            q = q_ref[0, tok_rows(cidx), hcols[i]]
            k = k_ref[0, tok_rows(cidx), hcols[i]]
            n16 = jnp.broadcast_to(n_st.astype(BF16), (BF16_ROWS, dh))
            prods.append(_dot_nt(jnp.concatenate([k, c_st.astype(BF16), n16], axis=0), q))
        for (i, d, cidx), (c_r, b_r, m_st, _, _), r in zip(jobs, rows, prods):
            allowed = (si >= ti) if d else (si <= ti)
            cb = jnp.where(allowed, jnp.broadcast_to(c_r, (ll, ll)).T, -jnp.inf)
            big_m = jnp.maximum(m_st, jnp.max(cb, axis=0, keepdims=True))
            p = jnp.exp(cb - (big_m - float(np.log(kscale)))) * r[:ll]
            w_inter = jnp.exp(m_st - big_m)
            vt = vt_ref[0, cidx, hcols[i], :]
            num = jnp.dot(vt, p.astype(BF16), preferred_element_type=F32) - r[ll:ll + dh] * (-w_inter)
            den = w_inter * r[ll + dh:ll + dh + 1] + jnp.sum(p, axis=0, keepdims=True)
            h_t = num / jnp.maximum(jnp.abs(den), jnp.exp(-(b_r + big_m)))
            if first_touch:
                hs_sc[i, cidx] = h_t
            else:
                hsum = h_t + hs_sc[i, cidx]
                rn = lax.rsqrt(jnp.mean(hsum * hsum, axis=0, keepdims=True) + EPS)
                out_t = (hsum * rn * ghn_t[i]) * _sigmoid(ot_ref[0, cidx, hcols[i], :].astype(F32))
                ha_ref[0, tok_rows(cidx), hcols[i]] = out_t.T.astype(ha_ref.dtype)
        for ch in chains:
            c_sc[ch], n_sc[ch], m_sc[ch] = state[ch]

    def make_body(first_touch):
        def body(j, carry):
            run_trip(j, first_touch)
            return carry
        return body

    trips = nc // unroll
    lax.fori_loop(0, trips // 2, make_body(True), 0)
    lax.fori_loop(trips // 2, trips, make_body(False), 0)

    if emit_state:
        for i, d in chains:
            cout_ref[0, 0, d, i] = c_sc[i, d]
            nout_ref[0, i, d:d + 1, :] = n_sc[i, d]
            mout_ref[0, i, d:d + 1, :] = m_sc[i, d]


def _mlstm(qk, vt, ot, gates_t, g_hn, init, *, emit_state):
    bsz, t, _ = qk.shape
    nc = t // CHUNK_A
    hps, unroll = (H_A, 1) if nc < 8 else (1, 2)
    assert (nc // 2) % unroll == 0 and nc % 2 == 0
    gtt = gates_t.reshape(bsz, 4 * H_A * nc, CHUNK_A)
    wh = hps * DH_A

    def tblk():
        return pl.BlockSpec((1, nc, wh, CHUNK_A), lambda b, g: (b, 0, g, 0))

    in_specs = [pl.BlockSpec((1, t, wh), lambda b, g: (b, 0, g)),
                pl.BlockSpec((1, t, wh), lambda b, g: (b, 0, H_A // hps + g)),
                tblk(), tblk(),
                pl.BlockSpec((1, 4 * H_A * nc, CHUNK_A), lambda b, g: (b, 0, 0)),
                pl.BlockSpec((1, wh), lambda b, g: (0, g))]
    args = [qk, qk, vt, ot, gtt, g_hn.reshape(1, W_A)]
    if init is not None:
        c0, n0, m0 = init
        in_specs += [pl.BlockSpec((1, 2, hps, DH_A, DH_A), lambda b, g: (b, 0, g, 0, 0)),
                     pl.BlockSpec((1, hps, 2, DH_A), lambda b, g: (b, g, 0, 0)),
                     pl.BlockSpec((1, hps, 2, DH_A), lambda b, g: (b, g, 0, 0))]
        args += [c0, n0, m0]
    out_specs = [pl.BlockSpec((1, t, wh), lambda b, g: (b, 0, g))]
    out_shape = [jax.ShapeDtypeStruct((bsz, t, W_A), BF16)]
    if emit_state:
        out_specs += [pl.BlockSpec((1, 1, 2, hps, DH_A, DH_A), lambda b, g: (b, 0, 0, g, 0, 0)),
                      pl.BlockSpec((1, hps, 2, DH_A), lambda b, g: (b, g, 0, 0)),
                      pl.BlockSpec((1, hps, 2, DH_A), lambda b, g: (b, g, 0, 0))]
        out_shape += [jax.ShapeDtypeStruct((bsz, 1, 2, H_A, DH_A, DH_A), F32),
                      jax.ShapeDtypeStruct((bsz, H_A, 2, DH_A), F32),
                      jax.ShapeDtypeStruct((bsz, H_A, 2, DH_A), F32)]
    kern = functools.partial(_mlstm_kernel, nc=nc, hps=hps, unroll=unroll, has_init=init is not None,
                             emit_state=emit_state)
    return pl.pallas_call(
        kern, grid=(bsz, H_A // hps), in_specs=in_specs, out_specs=out_specs, out_shape=out_shape,
        scratch_shapes=[pltpu.VMEM((hps, nc, DH_A, CHUNK_A), F32), pltpu.VMEM((2, 4, hps * nc, CHUNK_A), F32),
                        pltpu.VMEM((hps, 2, DH_A, DH_A), F32), pltpu.VMEM((hps, 2, 1, DH_A), F32),
                        pltpu.VMEM((hps, 2, 1, DH_A), F32)],
        compiler_params=_cparams(("arbitrary", "arbitrary")), name="mlstm",
    )(*args)


def _rope_tables(n_tok):
    t = np.arange(n_tok)
    row = (t // GRID_W).astype(np.float32)
    colp = (t % GRID_W).astype(np.float32)
    quarter = DH_B // 4
    freqs = (np.float32(ROPE_THETA) ** (-np.arange(quarter, dtype=np.float32) / np.float32(quarter))).astype(np.float32)
    ar = row[:, None] * freqs
    ac = colp[:, None] * freqs
    cos = np.concatenate([np.cos(ar), np.cos(ar), np.cos(ac), np.cos(ac)], axis=-1)
    sin = np.concatenate([-np.sin(ar), np.sin(ar), -np.sin(ac), np.sin(ac)], axis=-1)
    return jnp.asarray(np.tile(cos, (1, 2)), F32), jnp.asarray(np.tile(sin, (1, 2)), F32)


def _split_heads_q(q):
    lo = lax.broadcasted_iota(jnp.int32, q.shape, 1) < (LANES // 2)
    zero = jnp.zeros_like(q)
    return jnp.concatenate([jnp.where(lo, q, zero), jnp.where(lo, zero, q)], axis=0)


def _merge_heads_o(o_t, tq):
    half = LANES // 2
    return jnp.concatenate([o_t[:half, :tq], o_t[half:, tq:]], axis=0).T


def _with_ones_rows(vt):
    return jnp.concatenate([vt, jnp.ones((BF16_ROWS, vt.shape[1]), BF16)], axis=0)


def _online_softmax_pv_t(problems):
    seq = [(pi, ci) for pi, (_, chunks, _) in enumerate(problems) for ci in range(len(chunks))]
    scores = {}

    def issue(t):
        pi, ci = seq[t]
        qm, chunks, _ = problems[pi]
        scores[(pi, ci)] = _dot_nt(chunks[ci][0], qm)

    for t in range(min(QK_AHEAD, len(seq))):
        issue(t)
    m = acc = None
    for t, (pi, ci) in enumerate(seq):
        if t + QK_AHEAD < len(seq):
            issue(t + QK_AHEAD)
        _, chunks, emit = problems[pi]
        _, vt, penalty = chunks[ci]
        s = scores.pop((pi, ci))
        if penalty is not None:
            s = s - penalty
        mc = jnp.max(s, axis=0, keepdims=True)
        m_new = mc if ci == 0 else jnp.maximum(m, mc)
        p = jnp.exp2(s - m_new).astype(BF16)
        part = jnp.dot(_with_ones_rows(vt), p, preferred_element_type=F32)
        acc = part if ci == 0 else jnp.exp2(m - m_new) * acc + part
        m = m_new
        if ci == len(chunks) - 1:
            dv = vt.shape[0]
            emit(acc[:dv] / acc[dv:dv + 1])


def _attn_kernel(*refs, n_tiles, kv_shared, has_cache):
    q_ref, k_ref, vt_ref = refs[:3]
    pos = 3
    if has_cache:
        kc_ref, vtc_ref = refs[pos:pos + 2]
        pos += 2
    o_ref = refs[pos]
    tq = q_ref.shape[1]
    s_len = k_ref.shape[1]
    sc = min(s_len, KEY_CHUNK)
    problems = []
    for j in range(n_tiles):
        kj = 0 if kv_shared else j
        kcols = slice(kj * LANES, (kj + 1) * LANES)
        qm = _split_heads_q(q_ref[0, :, j * LANES:(j + 1) * LANES])
        chunks = [(k_ref[0, c * sc:(c + 1) * sc, kcols].astype(BF16),
                   vt_ref[0, kcols, c * sc:(c + 1) * sc].astype(BF16), None) for c in range(s_len // sc)]
        if has_cache:
            chunks.append((kc_ref[0], vtc_ref[0], None))

        def emit(o_t, j=j):
            o_ref[0, :, j * LANES:(j + 1) * LANES] = _merge_heads_o(o_t, tq).astype(o_ref.dtype)

        problems.append((qm, chunks, emit))
    _online_softmax_pv_t(problems)


def _attention(q, k, vt, cache, *, kv_shared, tq, n_tiles):
    bsz, t, w = q.shape
    s = k.shape[1]
    wt = n_tiles * LANES
    if kv_shared:
        k_spec = pl.BlockSpec((1, s, LANES), lambda b, i, g: (b, 0, 0))
        vt_spec = pl.BlockSpec((1, LANES, s), lambda b, i, g: (b, 0, 0))
    else:
        k_spec = pl.BlockSpec((1, s, wt), lambda b, i, g: (b, 0, g))
        vt_spec = pl.BlockSpec((1, wt, s), lambda b, i, g: (b, g, 0))
    in_specs = [pl.BlockSpec((1, tq, wt), lambda b, i, g: (b, i, g)), k_spec, vt_spec]
    args = [q, k, vt]
    if cache is not None:
        p = cache[0].shape[1]
        in_specs += [pl.BlockSpec((1, p, LANES), lambda b, i, g: (b, 0, 0)),
                     pl.BlockSpec((1, LANES, p), lambda b, i, g: (b, 0, 0))]
        args += list(cache)
    kern = functools.partial(_attn_kernel, n_tiles=n_tiles, kv_shared=kv_shared,
                             has_cache=cache is not None)
    return pl.pallas_call(
        kern, grid=(bsz, t // tq, w // wt), in_specs=in_specs,
        out_specs=pl.BlockSpec((1, tq, wt), lambda b, i, g: (b, i, g)),
        out_shape=jax.ShapeDtypeStruct((bsz, t, w), BF16),
        compiler_params=_cparams(("arbitrary", "arbitrary", "arbitrary")), name="attention",
    )(*args)


def _na_bias_blocks(variant):
    out = {}
    for qr in range(NA_ROWS):
        for kr in range(NA_WIN):
            if variant == 0:
                dr = kr - qr if kr < WIN_R else None
            elif variant == 1:
                dr = kr - qr - WIN_R // 2 if qr <= kr < qr + WIN_R else None
            else:
                dr = kr - qr - (NA_WIN - NA_ROWS) if kr >= NA_WIN - WIN_R else None
            out[(qr, kr)] = None if dr is None else dr + WIN_R - 1
    return out


def _na_kernel(q_ref, k_ref, vt_ref, kc_ref, vtc_ref, bc_ref, o_ref, bias_sc, *, n_rb):
    w = GRID_W
    tq = NA_ROWS * w

    @pl.when(jnp.logical_and(pl.program_id(1) == 0, pl.program_id(2) == 0))
    def _():
        for variant in range(3):
            for (qr, kr), di in _na_bias_blocks(variant).items():
                for hh in range(2 * NA_TILES):
                    val = jnp.full((w, w), MASKED, F32) if di is None else bc_ref[hh, di]
                    bias_sc[variant, kr * w:(kr + 1) * w, hh * tq + qr * w:hh * tq + (qr + 1) * w] = val

    n_rows = n_rb * NA_ROWS
    problems = []
    for rr in range(NA_RBS):
        rb = pl.program_id(2) * NA_RBS + rr
        variant = jnp.where(rb == 0, 0, jnp.where(rb == n_rb - 1, 2, 1))
        ws = jnp.clip(rb * NA_ROWS - WIN_R // 2, 0, n_rows - NA_WIN)
        blk0 = ws // NA_ROWS
        start = ws * w
        qrows = slice(rr * tq, (rr + 1) * tq)
        for j in range(NA_TILES):
            cols = slice(j * LANES, (j + 1) * LANES)
            qcols = slice(j * 2 * tq, (j + 1) * 2 * tq)
            qm = _split_heads_q(q_ref[0, qrows, cols])
            chunks = [(k_ref[0, pl.ds(pl.multiple_of(start + i * tq, tq), tq), cols],
                       vt_ref[0, blk0 + i, cols, :],
                       bias_sc[variant, i * tq:(i + 1) * tq, qcols]) for i in range(NA_BLKS)]
            chunks.append((kc_ref[0, :, cols], vtc_ref[0, cols, :], None))

            def emit(o_t, qrows=qrows, cols=cols):
                o_ref[0, qrows, cols] = _merge_heads_o(o_t, tq).astype(o_ref.dtype)

            problems.append((qm, chunks, emit))
    _online_softmax_pv_t(problems)


def _na_bias_table(rpb):
    c = np.arange(GRID_W)
    cs = np.clip(c - WIN_C // 2, 0, GRID_W - WIN_C)
    ck = np.arange(GRID_W)
    valid = (ck[:, None] >= cs[None, :]) & (ck[:, None] < cs[None, :] + WIN_C)
    idx = np.clip(ck[:, None] - c[None, :] + WIN_C - 1, 0, 2 * WIN_C - 2)
    onehot = (idx[..., None] == np.arange(2 * WIN_C - 1)).astype(np.float32)
    tab = jnp.einsum('hrx,kcx->hrkc', rpb, jnp.asarray(onehot), precision=lax.Precision.HIGHEST)
    return jnp.where(jnp.asarray(valid), tab * -LOG2E, MASKED)


def _na_attention(q, k, vt, kc, vtc, rpb):
    bsz, t, w = q.shape
    p = kc.shape[1]
    tq = NA_ROWS * GRID_W
    n_rb = t // tq
    bc = _na_bias_table(rpb)
    wt = NA_TILES * LANES
    tqs = NA_RBS * tq
    return pl.pallas_call(
        functools.partial(_na_kernel, n_rb=n_rb),
        grid=(w // wt, bsz, n_rb // NA_RBS),
        in_specs=[pl.BlockSpec((1, tqs, wt), lambda j, b, r: (b, r, j)),
                  pl.BlockSpec((1, t, wt), lambda j, b, r: (b, 0, j)),
                  pl.BlockSpec((1, n_rb, wt, tq), lambda j, b, r: (b, 0, j, 0)),
                  pl.BlockSpec((1, p, wt), lambda j, b, r: (b, 0, j)),
                  pl.BlockSpec((1, wt, p), lambda j, b, r: (b, j, 0)),
                  pl.BlockSpec((2 * NA_TILES, 2 * WIN_R - 1, GRID_W, GRID_W), lambda j, b, r: (j, 0, 0, 0))],
        out_specs=pl.BlockSpec((1, tqs, wt), lambda j, b, r: (b, r, j)),
        out_shape=jax.ShapeDtypeStruct((bsz, t, w), BF16),
        scratch_shapes=[pltpu.VMEM((3, NA_WIN * GRID_W, NA_TILES * 2 * tq), F32)],
        compiler_params=_cparams(("arbitrary", "arbitrary", "arbitrary")), name="na_attention",
    )(q, k, vt, kc, vtc, bc)


_GQA_PERM = np.array([0, 4, 1, 5, 2, 6, 3, 7])


def _perm_heads(w, axis):
    return jnp.concatenate([lax.slice_in_dim(w, h * DH_B, (h + 1) * DH_B, axis=axis) for h in _GQA_PERM],
                           axis=axis)


def _tok_major(cache):
    b, h, p, dh = cache.shape
    return cache.transpose(0, 2, 1, 3).reshape(b, p, h * dh)


def _feat_major(cache):
    b, h, p, dh = cache.shape
    return cache.transpose(0, 1, 3, 2).reshape(b, h * dh, p)


def _head_major_from_t(x_t, n_heads):
    b, w, t = x_t.shape
    return jnp.swapaxes(x_t.reshape(b, 1, n_heads, w // n_heads, t), -1, -2)


def _even_layer(xp, xs, mod, g_pre, g_post, w_in, b_gates, g_hn, g_q, g_k, w_out, st_c, st_n, st_m,
                ck, cv, rope_tabs):
    o_v, o_o = 2 * W_A, 3 * W_A
    o_g = 4 * W_A
    o_qb = o_g + 4 * H_A
    o_kb = o_qb + W_B
    o_vb = o_kb + HKV_B * DH_B
    o_z = o_vb + HKV_B * DH_B
    wb = w_in.astype(BF16)
    w_qk = wb[:, :o_v]
    w_va = wb[:, o_v:o_o]
    w_oa = wb[:, o_o:o_g]
    w_g = jnp.pad(wb[:, o_g:o_qb], ((0, 0), (0, LANES - 4 * H_A)))
    w_gqa = jnp.concatenate([_perm_heads(wb[:, o_qb:o_kb], 1), wb[:, o_kb:o_vb]], axis=1)
    w_vb = wb[:, o_vb:o_z]
    w_z = jnp.concatenate([wb[:, o_z:o_z + W_A], _perm_heads(wb[:, o_z + W_A:], 1)], axis=1)
    wob = w_out.astype(BF16)
    w_o = jnp.concatenate([wob[:W_A], _perm_heads(wob[W_A:], 0)], axis=0)

    def stream(x, row0, per_batch, init, cache, rope, emit):
        bsz, t, _ = x.shape
        tm = 512
        xf = x if per_batch else x.reshape(1, bsz * t, D_MODEL)
        vb_seg = (w_vb, BF16, None) if per_batch else (w_vb, F32, t)
        gqa = dict(w=w_gqa, g_q=g_q, g_k=g_k, rope=rope, knt_block=t if emit else None)
        outs = _proj(xf, g_pre, mod, [(w_qk, BF16), (w_z, BF16)],
                     [(w_va, BF16, CHUNK_A), (w_oa, BF16, CHUNK_A), vb_seg], (w_g, b_gates), gqa,
                     row0=row0, per_batch=per_batch, tm=tm)
        qk, z, vta, ota, vbt, gtt, qn, kn = outs[:8]
        if not per_batch:
            qk, z, qn, kn = (a.reshape(bsz, t, a.shape[-1]) for a in (qk, z, qn, kn))
            gtt = gtt.reshape(4 * H_A, bsz, t).transpose(1, 0, 2)
            vbt = vbt.reshape(bsz, HKV_B * DH_B, t)
            vta, ota = (a.reshape(bsz, t // CHUNK_A, W_A, CHUNK_A) for a in (vta, ota))
        res = _mlstm(qk, vta, ota, gtt, g_hn, init, emit_state=emit)
        ha = res[0]
        if per_batch:
            hb = _attention(qn, kn, vbt, cache, kv_shared=True, tq=512, n_tiles=1)
        else:
            hb = _attention(qn, kn, vbt, cache, kv_shared=True, tq=t, n_tiles=W_B // LANES)
        y = _outproj([ha.reshape(xf.shape[0], -1, W_A), hb.reshape(xf.shape[0], -1, W_B)],
                     z.reshape(xf.shape[0], -1, W_A + W_B), w_o, xf, g_post, mod,
                     row0=row0, per_batch=per_batch, tm=tm)
        knt = outs[8].reshape(bsz, HKV_B * DH_B, t) if emit else None
        return y.reshape(bsz, t, D_MODEL), res[1:], knt, vbt

    yp, st, knt_p, vbt_p = stream(xp, 0, False, None, None, None, True)
    n0 = st_n.transpose(0, 2, 1, 3)
    m0 = jnp.broadcast_to(st_m.transpose(0, 2, 1)[..., None], n0.shape)
    cache = (_tok_major(ck).astype(BF16), _feat_major(cv).astype(BF16))
    ys, _, _, _ = stream(xs, 1, True, (st_c, n0, m0), cache, rope_tabs, False)
    c_out, n_out, m_out = st
    new_n = n_out.transpose(0, 2, 1, 3)[:, None]
    new_m = m_out[..., 0].transpose(0, 2, 1)[:, None]
    return (yp, ys, c_out, new_n, new_m, _head_major_from_t(knt_p, HKV_B), _head_major_from_t(vbt_p, HKV_B))


def _odd_layer(xp, xs, mod, g_pre, g_post, w_in, rpb, w_out, ck, cv):
    w_q = (w_in[:, :W_C] * QSCALE).astype(BF16)
    w_k, w_v, w_z = (w_in[:, i * W_C:(i + 1) * W_C].astype(BF16) for i in range(1, 4))
    w_o = w_out.astype(BF16)
    tm = 512
    bsz, t, _ = xp.shape
    xf = xp.reshape(1, bsz * t, D_MODEL)
    q, k, z, kt, vt = _proj(xf, g_pre, mod, [(w_q, BF16), (w_k, BF16), (w_z, BF16)],
                            [(w_k, F32, t), (w_v, F32, t)], None, None, row0=0, per_batch=False, tm=tm)
    kt = kt.reshape(bsz, W_C, t)
    vt = vt.reshape(bsz, W_C, t)
    o = _attention(q.reshape(bsz, t, W_C), k.reshape(bsz, t, W_C), vt, None,
                   kv_shared=False, tq=t, n_tiles=W_C // LANES)
    yp = _outproj([o.reshape(1, bsz * t, W_C)], z, w_o, xf, g_post, mod, row0=0, per_batch=False, tm=tm)
    yp = yp.reshape(bsz, t, D_MODEL)
    q, k, z, vts = _proj(xs, g_pre, mod, [(w_q, BF16), (w_k, BF16), (w_z, BF16)],
                         [(w_v, BF16, NA_ROWS * GRID_W)], None, None, row0=1, per_batch=True, tm=tm)
    o = _na_attention(q, k, vts, _tok_major(ck).astype(BF16), _feat_major(cv).astype(BF16), rpb)
    ys = _outproj([o], z, w_o, xs, g_post, mod, row0=1, per_batch=True, tm=tm)
    return yp, ys, _head_major_from_t(kt, H_C), _head_major_from_t(vt, H_C)


def kernel(x_prompt, x_sample, state_mlstm_C, state_mlstm_n, state_mlstm_m, cache_gqa_k, cache_gqa_v,
           cache_na_k, cache_na_v, c, c_ctx, w_mod, b_mod, g_pre, g_post, w_in_ab, b_gates_ab, g_hnorm_a,
           g_qnorm_b, g_knorm_b, w_out_ab, w_in_c, rpb_c, w_out_c):
    depth = w_mod.shape[0]
    assert depth == 2 and c.shape[0] == 2
    cvec = jnp.concatenate([c_ctx[None], c, jnp.zeros((8 - 1 - c.shape[0], D_MODEL), F32)], axis=0)
    mod = _modulation(cvec, w_mod, b_mod)
    rope_tabs = _rope_tables(x_sample.shape[1])
    xp, xs, c_out, n_out, m_out, gk, gv = _even_layer(
        x_prompt, x_sample, mod[0], g_pre[0], g_post[0], w_in_ab[0], b_gates_ab[0], g_hnorm_a[0],
        g_qnorm_b[0], g_knorm_b[0], w_out_ab[0], state_mlstm_C[:, 0], state_mlstm_n[:, 0],
        state_mlstm_m[:, 0], cache_gqa_k[:, 0], cache_gqa_v[:, 0], rope_tabs)
    xp, xs, nk, nv = _odd_layer(xp, xs, mod[1], g_pre[1], g_post[1], w_in_c[0], rpb_c[0], w_out_c[0],
                                cache_na_k[:, 0], cache_na_v[:, 0])
    return (xp, xs, c_out, n_out, m_out, gk, gv, nk, nv)
```

```python
import functools

import jax
import jax.numpy as jnp
import numpy as np
from jax import lax
from jax.experimental import pallas as pl
from jax.experimental.pallas import tpu as pltpu

F32 = jnp.float32
BF16 = jnp.bfloat16

D_MODEL = 1024
GRID_W = 64
EPS = 1e-6
H_A = 4
DH_A = 128
W_A = H_A * DH_A
CHUNK_A = 128
HQ_B = 8
HKV_B = 2
DH_B = 64
W_B = HQ_B * DH_B
ROPE_THETA = 10000.0
H_C = 16
DH_C = 64
W_C = H_C * DH_C
WIN_R = 8
WIN_C = 16

LANES = 128
NA_ROWS = 4
NA_TILES = 2
NA_RBS = 2
NA_BLKS = -(-(NA_ROWS + WIN_R - 1) // NA_ROWS)
NA_WIN = NA_BLKS * NA_ROWS
assert WIN_R // 2 == NA_ROWS
BF16_ROWS = 16
PROJ_SUBTILES = 2
KEY_CHUNK = 256
QK_AHEAD = 3
LOG2E = 1.4426950408889634
QSCALE = DH_B ** -0.5 * LOG2E
assert DH_B == DH_C
MASKED = 1e30
VMEM_LIMIT = 56 * 1024 * 1024


def _cparams(sem):
    return pltpu.CompilerParams(dimension_semantics=sem, vmem_limit_bytes=VMEM_LIMIT)


def _silu(x):
    return x / (1.0 + jnp.exp(-x))


def _sigmoid(x):
    return 1.0 / (1.0 + jnp.exp(-x))


def _log_sigmoid(x):
    return jnp.minimum(x, 0.0) - jnp.log1p(jnp.exp(-jnp.abs(x)))


def _dot_nt(a, b):
    return lax.dot_general(a, b, (((1,), (1,)), ((), ())), preferred_element_type=F32)


def _mod_kernel(c_ref, w_ref, b_ref, o_ref):
    s = _silu(c_ref[...])
    o_ref[0] = jnp.dot(s, w_ref[0], preferred_element_type=F32,
                       precision=lax.Precision.HIGHEST) + b_ref[0]


def _modulation(cvec, w_mod, b_mod):
    depth, d, n = w_mod.shape
    tn = n // 4
    return pl.pallas_call(
        _mod_kernel,
        grid=(depth, n // tn),
        in_specs=[pl.BlockSpec((8, d), lambda l, j: (0, 0)),
                  pl.BlockSpec((1, d, tn), lambda l, j: (l, 0, j)),
                  pl.BlockSpec((1, 1, tn), lambda l, j: (l, 0, j))],
        out_specs=pl.BlockSpec((1, 8, tn), lambda l, j: (l, 0, j)),
        out_shape=jax.ShapeDtypeStruct((depth, 8, n), F32),
        compiler_params=_cparams(("arbitrary", "arbitrary")),
        name="modulation",
    )(cvec, w_mod, b_mod.reshape(depth, 1, n))


def _head_norm(x, g):
    lo = lax.broadcasted_iota(jnp.int32, x.shape, 1) < DH_B
    x2 = x * x
    s_lo = jnp.sum(jnp.where(lo, x2, 0.0), axis=-1, keepdims=True)
    s_hi = jnp.sum(jnp.where(lo, 0.0, x2), axis=-1, keepdims=True)
    ms = jnp.where(lo, s_lo, s_hi) * (1.0 / DH_B)
    return x * lax.rsqrt(ms + EPS) * g


def _rope(x, cos, sin):
    quarter = DH_B // 4
    first = (lax.broadcasted_iota(jnp.int32, x.shape, 1) % (2 * quarter)) < quarter
    partner = jnp.where(first, pltpu.roll(x, LANES - quarter, 1), pltpu.roll(x, quarter, 1))
    return x * cos + partner * sin


def _store_t(ref, val, t_block, tok0):
    n_tok = val.shape[1]
    if t_block is None:
        ref[0, :, tok0:tok0 + n_tok] = val.astype(ref.dtype)
    else:
        for i in range(n_tok // t_block):
            ref[0, tok0 // t_block + i] = val[:, i * t_block:(i + 1) * t_block].astype(ref.dtype)


def _proj_kernel(*refs, n_seg, t_blocks, gate_rows, gqa, row0, per_batch):
    n_tseg = len(t_blocks) + (1 if gate_rows else 0)
    x_ref, g_ref, mod_ref = refs[:3]
    pos = 3
    w_refs = refs[pos:pos + n_seg]
    pos += n_seg
    wt_refs = refs[pos:pos + n_tseg]
    pos += n_tseg
    if gate_rows:
        bgt_ref = refs[pos]
        pos += 1
    if gqa is not None:
        rope, knt_block = gqa
        gq_ref, gk_ref = refs[pos:pos + 2]
        pos += 2
        if rope:
            cos_ref, sin_ref = refs[pos:pos + 2]
            pos += 2
    n_plain = n_seg - (1 if gqa is not None else 0)
    o_refs = refs[pos:pos + n_plain]
    pos += n_plain
    ot_refs = refs[pos:pos + len(t_blocks)]
    pos += len(t_blocks)
    if gate_rows:
        gto_ref = refs[pos]
        pos += 1
    if gqa is not None:
        q_out, k_out = refs[pos:pos + 2]
        pos += 2
        if knt_block is not None:
            knt_out = refs[pos]
            pos += 1
    wt_sc = refs[pos] if n_tseg else None
    t_offs = [0]
    for wt_ref in wt_refs:
        t_offs.append(t_offs[-1] + wt_ref.shape[1])

    @pl.when(jnp.logical_and(pl.program_id(0) == 0, pl.program_id(1) == 0))
    def _():
        for wt_ref, off in zip(wt_refs, t_offs):
            wt_sc[off:off + wt_ref.shape[1], :] = wt_ref[...].T

    d = x_ref.shape[-1]
    tm = x_ref.shape[1]
    row = row0 + (pl.program_id(0) if per_batch else 0)
    shift = mod_ref[pl.ds(row, 1), 0:d]
    scale = mod_ref[pl.ds(row, 1), d:2 * d]
    sub = tm // PROJ_SUBTILES
    for s in range(PROJ_SUBTILES):
        rows = slice(s * sub, (s + 1) * sub)
        x = x_ref[0, rows, :]
        r = lax.rsqrt(jnp.mean(x * x, axis=-1, keepdims=True) + EPS)
        h = (x * r * g_ref[...]) * (1.0 + scale) + shift
        hb = h.astype(BF16)
        if gqa is not None:
            res = jnp.dot(hb, w_refs[-1][...], preferred_element_type=F32)
        for w_ref, o_ref in zip(w_refs[:n_plain], o_refs):
            o_ref[0, rows, :] = jnp.dot(hb, w_ref[...], preferred_element_type=F32).astype(o_ref.dtype)
        if n_tseg:
            res_t = _dot_nt(wt_sc[...], hb)
            for ot_ref, tb, off, end in zip(ot_refs, t_blocks, t_offs, t_offs[1:]):
                _store_t(ot_ref, res_t[off:end], tb, s * sub)
            if gate_rows:
                off = t_offs[len(t_blocks)]
                gto_ref[0, :, rows] = res_t[off:off + gate_rows] + bgt_ref[...]
        if gqa is not None:
            nq = W_B // LANES
            for j in range(nq + 1):
                xn = _head_norm(res[:, j * LANES:(j + 1) * LANES], gq_ref[...] if j < nq else gk_ref[...])
                if j == nq and knt_block is not None:
                    _store_t(knt_out, xn.T, knt_block, s * sub)
                if rope:
                    xn = _rope(xn, cos_ref[rows, :], sin_ref[rows, :])
                if j < nq:
                    q_out[0, rows, j * LANES:(j + 1) * LANES] = (xn * QSCALE).astype(q_out.dtype)
                else:
                    k_out[0, rows, :] = xn.astype(k_out.dtype)


def _proj(x, g_pre, mod, segs, tsegs, gates, gqa, *, row0, per_batch, tm):
    bsz, t, d = x.shape
    grid = (bsz, t // tm)
    const = lambda b, i: (0, 0)
    in_specs = [pl.BlockSpec((1, tm, d), lambda b, i: (b, i, 0)),
                pl.BlockSpec((1, d), const),
                pl.BlockSpec(mod.shape, const)]
    args = [x, g_pre.reshape(1, d), mod]
    plain_ws = [w for w, _ in segs] + ([gqa["w"]] if gqa is not None else [])
    t_ws = [w for w, _, _ in tsegs] + ([gates[0]] if gates is not None else [])
    for w in plain_ws + t_ws:
        in_specs.append(pl.BlockSpec(w.shape, const))
        args.append(w)
    gate_rows = 0
    if gates is not None:
        gate_rows = gates[1].shape[0]
        in_specs.append(pl.BlockSpec((gate_rows, 1), const))
        args.append(gates[1].reshape(gate_rows, 1))
    if gqa is not None:
        in_specs += [pl.BlockSpec((1, LANES), const)] * 2
        args += [jnp.tile(gqa["g_q"], 2).reshape(1, LANES), jnp.tile(gqa["g_k"], 2).reshape(1, LANES)]
        if gqa["rope"] is not None:
            in_specs += [pl.BlockSpec((tm, LANES), lambda b, i: (i, 0))] * 2
            args += list(gqa["rope"])
    out_specs, out_shape = [], []
    for w, dt in segs:
        n = w.shape[1]
        out_specs.append(pl.BlockSpec((1, tm, n), lambda b, i: (b, i, 0)))
        out_shape.append(jax.ShapeDtypeStruct((bsz, t, n), dt))

    def add_t_out(n, dt, tb):
        if tb is None:
            out_specs.append(pl.BlockSpec((1, n, tm), lambda b, i: (b, 0, i)))
            out_shape.append(jax.ShapeDtypeStruct((bsz, n, t), dt))
        else:
            out_specs.append(pl.BlockSpec((1, tm // tb, n, tb), lambda b, i: (b, i, 0, 0)))
            out_shape.append(jax.ShapeDtypeStruct((bsz, t // tb, n, tb), dt))

    for w, dt, tb in tsegs:
        add_t_out(w.shape[1], dt, tb)
    if gates is not None:
        add_t_out(gate_rows, F32, None)
    gqa_static = None
    if gqa is not None:
        out_specs += [pl.BlockSpec((1, tm, W_B), lambda b, i: (b, i, 0)),
                      pl.BlockSpec((1, tm, LANES), lambda b, i: (b, i, 0))]
        out_shape += [jax.ShapeDtypeStruct((bsz, t, W_B), BF16), jax.ShapeDtypeStruct((bsz, t, LANES), BF16)]
        if gqa["knt_block"] is not None:
            add_t_out(LANES, F32, gqa["knt_block"])
        gqa_static = (gqa["rope"] is not None, gqa["knt_block"])
    kern = functools.partial(_proj_kernel, n_seg=len(plain_ws), t_blocks=tuple(tb for _, _, tb in tsegs),
                             gate_rows=gate_rows, gqa=gqa_static, row0=row0, per_batch=per_batch)
    return pl.pallas_call(
        kern, grid=grid, in_specs=in_specs, out_specs=out_specs, out_shape=out_shape,
        scratch_shapes=[pltpu.VMEM((sum(w.shape[1] for w in t_ws), d), BF16)] if t_ws else [],
        compiler_params=_cparams(("arbitrary", "arbitrary")), name="in_proj",
    )(*args)


def _outproj_kernel(*refs, n_in, row0, per_batch):
    a_refs = refs[:n_in]
    z_ref, w_ref, x_ref, gp_ref, mod_ref, o_ref = refs[n_in:n_in + 6]
    d = x_ref.shape[-1]
    row = row0 + (pl.program_id(0) if per_batch else 0)
    gate = mod_ref[pl.ds(row, 1), 2 * d:3 * d]
    z = z_ref[0].astype(F32)
    sz = _silu(z)
    acc = None
    off = 0
    for a_ref in a_refs:
        kk = a_ref.shape[-1]
        y = (a_ref[0].astype(F32) * sz[:, off:off + kk]).astype(BF16)
        part = jnp.dot(y, w_ref[off:off + kk, :], preferred_element_type=F32)
        acc = part if acc is None else acc + part
        off += kk
    r = lax.rsqrt(jnp.mean(acc * acc, axis=-1, keepdims=True) + EPS)
    o_ref[0] = x_ref[0] + gate * (acc * r * gp_ref[...])


def _outproj(parts, z, w_out, x, g_post, mod, *, row0, per_batch, tm):
    bsz, t, d = x.shape
    grid = (bsz, t // tm)
    in_specs, args = [], []
    for a in parts:
        in_specs.append(pl.BlockSpec((1, tm, a.shape[-1]), lambda b, i: (b, i, 0)))
        args.append(a)
    in_specs += [pl.BlockSpec((1, tm, z.shape[-1]), lambda b, i: (b, i, 0)),
                 pl.BlockSpec(w_out.shape, lambda b, i: (0, 0)),
                 pl.BlockSpec((1, tm, d), lambda b, i: (b, i, 0)),
                 pl.BlockSpec((1, d), lambda b, i: (0, 0)),
                 pl.BlockSpec(mod.shape, lambda b, i: (0, 0))]
    args += [z, w_out, x, g_post.reshape(1, d), mod]
    kern = functools.partial(_outproj_kernel, n_in=len(parts), row0=row0, per_batch=per_batch)
    return pl.pallas_call(
        kern, grid=grid, in_specs=in_specs,
        out_specs=pl.BlockSpec((1, tm, d), lambda b, i: (b, i, 0)),
        out_shape=jax.ShapeDtypeStruct((bsz, t, d), F32),
        compiler_params=_cparams(("arbitrary", "arbitrary")), name="out_proj",
    )(*args)


def _split3_bf16(x):
    hi = x.astype(BF16)
    r1 = x - hi.astype(F32)
    mid = r1.astype(BF16)
    lo = (r1 - mid.astype(F32)).astype(BF16)
    return hi, mid, lo


def _mlstm_gate_rows(ig_f, fg_f, ig_b, fg_b):
    rr, ll = fg_f.shape
    lf = _log_sigmoid(jnp.concatenate([fg_f, fg_b], axis=0))
    pieces = jnp.concatenate(_split3_bf16(lf), axis=0)
    u = lax.broadcasted_iota(jnp.int32, (ll, ll), 0)
    t = lax.broadcasted_iota(jnp.int32, (ll, ll), 1)
    out = []
    for d, ig in enumerate((ig_f, ig_b)):
        tri = jnp.where((u >= t) if d else (u <= t), 1.0, 0.0).astype(BF16)
        y = jnp.dot(pieces, tri, preferred_element_type=F32)
        rows = slice(d * rr, (d + 1) * rr)
        b = y[0:2 * rr][rows] + y[2 * rr:4 * rr][rows] + y[4 * rr:6 * rr][rows]
        c = ig - b
        cmax = jnp.broadcast_to(jnp.max(c, axis=-1, keepdims=True), c.shape)
        tot = jnp.broadcast_to(jnp.sum(lf[rows], axis=-1, keepdims=True), c.shape)
        out.append((c, b, cmax, tot))
    return out


def _mlstm_kernel(*refs, nc, hps, unroll, has_init, emit_state):
    q_ref, k_ref, vt_ref, ot_ref, gtt_ref, ghn_ref = refs[:6]
    pos = 6
    if has_init:
        c0_ref, n0_ref, m0_ref = refs[pos:pos + 3]
        pos += 3
    ha_ref = refs[pos]
    pos += 1
    if emit_state:
        cout_ref, nout_ref, mout_ref = refs[pos:pos + 3]
        pos += 3
    hs_sc, gate_sc, c_sc, n_sc, m_sc = refs[pos:pos + 5]
    ll, dh = CHUNK_A, DH_A
    kscale = dh ** -0.5
    hd0 = pl.program_id(1) * hps
    chains = [(i, d) for i in range(hps) for d in range(2)]

    for i, d in chains:
        if has_init:
            c_sc[i, d] = c0_ref[0, d, i]
            n_sc[i, d] = n0_ref[0, i, d:d + 1, :]
            m_sc[i, d] = m0_ref[0, i, d:d + 1, :]
        else:
            c_sc[i, d] = jnp.zeros((dh, dh), F32)
            n_sc[i, d] = jnp.zeros((1, dh), F32)
            m_sc[i, d] = jnp.zeros((1, dh), F32)

    def gate_rows(col0):
        return gtt_ref[0, pl.ds(pl.multiple_of((col0 + hd0) * nc, 8), hps * nc), :]

    gates = _mlstm_gate_rows(gate_rows(0), gate_rows(H_A), gate_rows(2 * H_A), gate_rows(3 * H_A))
    for d in range(2):
        for kind in range(4):
            gate_sc[d, kind] = gates[d][kind]

    ghn_t = [jnp.broadcast_to(ghn_ref[:, i * dh:(i + 1) * dh], (ll, dh)).T for i in range(hps)]
    si = lax.broadcasted_iota(jnp.int32, (ll, ll), 0)
    ti = lax.broadcasted_iota(jnp.int32, (ll, ll), 1)

    def run_trip(j, first_touch):
        jobs = []
        for u in range(unroll):
            step = j * unroll + u
            for i, d in chains:
                jobs.append((i, d, step if d == 0 else nc - 1 - step))
        state = {ch: [c_sc[ch], n_sc[ch], m_sc[ch]] for ch in chains}
        hcols = [slice(i * dh, (i + 1) * dh) for i in range(hps)]

        def tok_rows(cidx):
            return pl.ds(pl.multiple_of(cidx * ll, ll), ll)

        rows = []
        for i, d, cidx in jobs:
            c_r, b_r, cmax_r, tot_r = (gate_sc[d, kind, pl.ds(i * nc + cidx, 1), :] for kind in range(4))
            m_st = state[(i, d)][2]
            m_c = jnp.maximum(m_st, cmax_r)
            state[(i, d)][2] = tot_r + m_c
            rows.append((c_r, b_r, m_st, jnp.exp(m_st - m_c), jnp.exp(c_r - m_c) * kscale))
        start = []
        for (i, d, cidx), (_, _, _, a_st, wk) in zip(jobs, rows):
            k = k_ref[0, tok_rows(cidx), hcols[i]]
            vt = vt_ref[0, cidx, hcols[i], :]
            vw = jnp.concatenate([vt.astype(F32) * wk, jnp.broadcast_to(wk, (BF16_ROWS, ll))], axis=0)
            upd = jnp.dot(vw.astype(BF16), k, preferred_element_type=F32)
            c_st, n_st, _ = state[(i, d)]
            start.append((c_st, n_st))
            state[(i, d)][0] = a_st * c_st + upd[:dh]
            state[(i, d)][1] = a_st * n_st + upd[dh:dh + 1]
        prods = []
        for (i, d, cidx), (c_st, n_st) in zip(jobs, start):
            q = q_ref[0, tok_rows(cidx), hcols[i]]
            k = k_ref[0, tok_rows(cidx), hcols[i]]
            n16 = jnp.broadcast_to(n_st.astype(BF16), (BF16_ROWS, dh))
            prods.append(_dot_nt(jnp.concatenate([k, c_st.astype(BF16), n16], axis=0), q))
        for (i, d, cidx), (c_r, b_r, m_st, _, _), r in zip(jobs, rows, prods):
            allowed = (si >= ti) if d else (si <= ti)
            cb = jnp.where(allowed, jnp.broadcast_to(c_r, (ll, ll)).T, -jnp.inf)
            big_m = jnp.maximum(m_st, jnp.max(cb, axis=0, keepdims=True))
            p = jnp.exp(cb - (big_m - float(np.log(kscale)))) * r[:ll]
            w_inter = jnp.exp(m_st - big_m)
            vt = vt_ref[0, cidx, hcols[i], :]
            num = jnp.dot(vt, p.astype(BF16), preferred_element_type=F32) - r[ll:ll + dh] * (-w_inter)
            den = w_inter * r[ll + dh:ll + dh + 1] + jnp.sum(p, axis=0, keepdims=True)
            h_t = num / jnp.maximum(jnp.abs(den), jnp.exp(-(b_r + big_m)))
            if first_touch:
                hs_sc[i, cidx] = h_t
            else:
                hsum = h_t + hs_sc[i, cidx]
                rn = lax.rsqrt(jnp.mean(hsum * hsum, axis=0, keepdims=True) + EPS)
                out_t = (hsum * rn * ghn_t[i]) * _sigmoid(ot_ref[0, cidx, hcols[i], :].astype(F32))
                ha_ref[0, tok_rows(cidx), hcols[i]] = out_t.T.astype(ha_ref.dtype)
        for ch in chains:
            c_sc[ch], n_sc[ch], m_sc[ch] = state[ch]

    def make_body(first_touch):
        def body(j, carry):
            run_trip(j, first_touch)
            return carry
        return body

    trips = nc // unroll
    lax.fori_loop(0, trips // 2, make_body(True), 0)
    lax.fori_loop(trips // 2, trips, make_body(False), 0)

    if emit_state:
        for i, d in chains:
            cout_ref[0, 0, d, i] = c_sc[i, d]
            nout_ref[0, i, d:d + 1, :] = n_sc[i, d]
            mout_ref[0, i, d:d + 1, :] = m_sc[i, d]


def _mlstm(qk, vt, ot, gates_t, g_hn, init, *, emit_state):
    bsz, t, _ = qk.shape
    nc = t // CHUNK_A
    hps, unroll = (H_A, 1) if nc < 8 else (1, 2)
    assert (nc // 2) % unroll == 0 and nc % 2 == 0
    gtt = gates_t.reshape(bsz, 4 * H_A * nc, CHUNK_A)
    wh = hps * DH_A

    def tblk():
        return pl.BlockSpec((1, nc, wh, CHUNK_A), lambda b, g: (b, 0, g, 0))

    in_specs = [pl.BlockSpec((1, t, wh), lambda b, g: (b, 0, g)),
                pl.BlockSpec((1, t, wh), lambda b, g: (b, 0, H_A // hps + g)),
                tblk(), tblk(),
                pl.BlockSpec((1, 4 * H_A * nc, CHUNK_A), lambda b, g: (b, 0, 0)),
                pl.BlockSpec((1, wh), lambda b, g: (0, g))]
    args = [qk, qk, vt, ot, gtt, g_hn.reshape(1, W_A)]
    if init is not None:
        c0, n0, m0 = init
        in_specs += [pl.BlockSpec((1, 2, hps, DH_A, DH_A), lambda b, g: (b, 0, g, 0, 0)),
                     pl.BlockSpec((1, hps, 2, DH_A), lambda b, g: (b, g, 0, 0)),
                     pl.BlockSpec((1, hps, 2, DH_A), lambda b, g: (b, g, 0, 0))]
        args += [c0, n0, m0]
    out_specs = [pl.BlockSpec((1, t, wh), lambda b, g: (b, 0, g))]
    out_shape = [jax.ShapeDtypeStruct((bsz, t, W_A), BF16)]
    if emit_state:
        out_specs += [pl.BlockSpec((1, 1, 2, hps, DH_A, DH_A), lambda b, g: (b, 0, 0, g, 0, 0)),
                      pl.BlockSpec((1, hps, 2, DH_A), lambda b, g: (b, g, 0, 0)),
                      pl.BlockSpec((1, hps, 2, DH_A), lambda b, g: (b, g, 0, 0))]
        out_shape += [jax.ShapeDtypeStruct((bsz, 1, 2, H_A, DH_A, DH_A), F32),
                      jax.ShapeDtypeStruct((bsz, H_A, 2, DH_A), F32),
                      jax.ShapeDtypeStruct((bsz, H_A, 2, DH_A), F32)]
    kern = functools.partial(_mlstm_kernel, nc=nc, hps=hps, unroll=unroll, has_init=init is not None,
                             emit_state=emit_state)
    return pl.pallas_call(
        kern, grid=(bsz, H_A // hps), in_specs=in_specs, out_specs=out_specs, out_shape=out_shape,
        scratch_shapes=[pltpu.VMEM((hps, nc, DH_A, CHUNK_A), F32), pltpu.VMEM((2, 4, hps * nc, CHUNK_A), F32),
                        pltpu.VMEM((hps, 2, DH_A, DH_A), F32), pltpu.VMEM((hps, 2, 1, DH_A), F32),
                        pltpu.VMEM((hps, 2, 1, DH_A), F32)],
        compiler_params=_cparams(("arbitrary", "arbitrary")), name="mlstm",
    )(*args)


def _rope_tables(n_tok):
    t = np.arange(n_tok)
    row = (t // GRID_W).astype(np.float32)
    colp = (t % GRID_W).astype(np.float32)
    quarter = DH_B // 4
    freqs = (np.float32(ROPE_THETA) ** (-np.arange(quarter, dtype=np.float32) / np.float32(quarter))).astype(np.float32)
    ar = row[:, None] * freqs
    ac = colp[:, None] * freqs
    cos = np.concatenate([np.cos(ar), np.cos(ar), np.cos(ac), np.cos(ac)], axis=-1)
    sin = np.concatenate([-np.sin(ar), np.sin(ar), -np.sin(ac), np.sin(ac)], axis=-1)
    return jnp.asarray(np.tile(cos, (1, 2)), F32), jnp.asarray(np.tile(sin, (1, 2)), F32)


def _split_heads_q(q):
    lo = lax.broadcasted_iota(jnp.int32, q.shape, 1) < (LANES // 2)
    zero = jnp.zeros_like(q)
    return jnp.concatenate([jnp.where(lo, q, zero), jnp.where(lo, zero, q)], axis=0)


def _merge_heads_o(o_t, tq):
    half = LANES // 2
    return jnp.concatenate([o_t[:half, :tq], o_t[half:, tq:]], axis=0).T


def _with_ones_rows(vt):
    return jnp.concatenate([vt, jnp.ones((BF16_ROWS, vt.shape[1]), BF16)], axis=0)


def _online_softmax_pv_t(problems):
    seq = [(pi, ci) for pi, (_, chunks, _) in enumerate(problems) for ci in range(len(chunks))]
    scores = {}

    def issue(t):
        pi, ci = seq[t]
        qm, chunks, _ = problems[pi]
        scores[(pi, ci)] = _dot_nt(chunks[ci][0], qm)

    for t in range(min(QK_AHEAD, len(seq))):
        issue(t)
    m = acc = None
    for t, (pi, ci) in enumerate(seq):
        if t + QK_AHEAD < len(seq):
            issue(t + QK_AHEAD)
        _, chunks, emit = problems[pi]
        _, vt, penalty = chunks[ci]
        s = scores.pop((pi, ci))
        if penalty is not None:
            s = s - penalty
        mc = jnp.max(s, axis=0, keepdims=True)
        m_new = mc if ci == 0 else jnp.maximum(m, mc)
        p = jnp.exp2(s - m_new).astype(BF16)
        part = jnp.dot(_with_ones_rows(vt), p, preferred_element_type=F32)
        acc = part if ci == 0 else jnp.exp2(m - m_new) * acc + part
        m = m_new
        if ci == len(chunks) - 1:
            dv = vt.shape[0]
            emit(acc[:dv] / acc[dv:dv + 1])


def _attn_kernel(*refs, n_tiles, kv_shared, has_cache):
    q_ref, k_ref, vt_ref = refs[:3]
    pos = 3
    if has_cache:
        kc_ref, vtc_ref = refs[pos:pos + 2]
        pos += 2
    o_ref = refs[pos]
    tq = q_ref.shape[1]
    s_len = k_ref.shape[1]
    sc = min(s_len, KEY_CHUNK)
    problems = []
    for j in range(n_tiles):
        kj = 0 if kv_shared else j
        kcols = slice(kj * LANES, (kj + 1) * LANES)
        qm = _split_heads_q(q_ref[0, :, j * LANES:(j + 1) * LANES])
        chunks = [(k_ref[0, c * sc:(c + 1) * sc, kcols].astype(BF16),
                   vt_ref[0, kcols, c * sc:(c + 1) * sc].astype(BF16), None) for c in range(s_len // sc)]
        if has_cache:
            chunks.append((kc_ref[0], vtc_ref[0], None))

        def emit(o_t, j=j):
            o_ref[0, :, j * LANES:(j + 1) * LANES] = _merge_heads_o(o_t, tq).astype(o_ref.dtype)

        problems.append((qm, chunks, emit))
    _online_softmax_pv_t(problems)


def _attention(q, k, vt, cache, *, kv_shared, tq, n_tiles):
    bsz, t, w = q.shape
    s = k.shape[1]
    wt = n_tiles * LANES
    if kv_shared:
        k_spec = pl.BlockSpec((1, s, LANES), lambda b, i, g: (b, 0, 0))
        vt_spec = pl.BlockSpec((1, LANES, s), lambda b, i, g: (b, 0, 0))
    else:
        k_spec = pl.BlockSpec((1, s, wt), lambda b, i, g: (b, 0, g))
        vt_spec = pl.BlockSpec((1, wt, s), lambda b, i, g: (b, g, 0))
    in_specs = [pl.BlockSpec((1, tq, wt), lambda b, i, g: (b, i, g)), k_spec, vt_spec]
    args = [q, k, vt]
    if cache is not None:
        p = cache[0].shape[1]
        in_specs += [pl.BlockSpec((1, p, LANES), lambda b, i, g: (b, 0, 0)),
                     pl.BlockSpec((1, LANES, p), lambda b, i, g: (b, 0, 0))]
        args += list(cache)
    kern = functools.partial(_attn_kernel, n_tiles=n_tiles, kv_shared=kv_shared,
                             has_cache=cache is not None)
    return pl.pallas_call(
        kern, grid=(bsz, t // tq, w // wt), in_specs=in_specs,
        out_specs=pl.BlockSpec((1, tq, wt), lambda b, i, g: (b, i, g)),
        out_shape=jax.ShapeDtypeStruct((bsz, t, w), BF16),
        compiler_params=_cparams(("arbitrary", "arbitrary", "arbitrary")), name="attention",
    )(*args)


def _na_bias_blocks(variant):
    out = {}
    for qr in range(NA_ROWS):
        for kr in range(NA_WIN):
            if variant == 0:
                dr = kr - qr if kr < WIN_R else None
            elif variant == 1:
                dr = kr - qr - WIN_R // 2 if qr <= kr < qr + WIN_R else None
            else:
                dr = kr - qr - (NA_WIN - NA_ROWS) if kr >= NA_WIN - WIN_R else None
            out[(qr, kr)] = None if dr is None else dr + WIN_R - 1
    return out


def _na_kernel(q_ref, k_ref, vt_ref, kc_ref, vtc_ref, bc_ref, o_ref, bias_sc, *, n_rb):
    w = GRID_W
    tq = NA_ROWS * w

    @pl.when(jnp.logical_and(pl.program_id(1) == 0, pl.program_id(2) == 0))
    def _():
        for variant in range(3):
            for (qr, kr), di in _na_bias_blocks(variant).items():
                for hh in range(2 * NA_TILES):
                    val = jnp.full((w, w), MASKED, F32) if di is None else bc_ref[hh, di]
                    bias_sc[variant, kr * w:(kr + 1) * w, hh * tq + qr * w:hh * tq + (qr + 1) * w] = val

    n_rows = n_rb * NA_ROWS
    problems = []
    for rr in range(NA_RBS):
        rb = pl.program_id(2) * NA_RBS + rr
        variant = jnp.where(rb == 0, 0, jnp.where(rb == n_rb - 1, 2, 1))
        ws = jnp.clip(rb * NA_ROWS - WIN_R // 2, 0, n_rows - NA_WIN)
        blk0 = ws // NA_ROWS
        start = ws * w
        qrows = slice(rr * tq, (rr + 1) * tq)
        for j in range(NA_TILES):
            cols = slice(j * LANES, (j + 1) * LANES)
            qcols = slice(j * 2 * tq, (j + 1) * 2 * tq)
            qm = _split_heads_q(q_ref[0, qrows, cols])
            chunks = [(k_ref[0, pl.ds(pl.multiple_of(start + i * tq, tq), tq), cols],
                       vt_ref[0, blk0 + i, cols, :],
                       bias_sc[variant, i * tq:(i + 1) * tq, qcols]) for i in range(NA_BLKS)]
            chunks.append((kc_ref[0, :, cols], vtc_ref[0, cols, :], None))

            def emit(o_t, qrows=qrows, cols=cols):
                o_ref[0, qrows, cols] = _merge_heads_o(o_t, tq).astype(o_ref.dtype)

            problems.append((qm, chunks, emit))
    _online_softmax_pv_t(problems)


def _na_bias_table(rpb):
    c = np.arange(GRID_W)
    cs = np.clip(c - WIN_C // 2, 0, GRID_W - WIN_C)
    ck = np.arange(GRID_W)
    valid = (ck[:, None] >= cs[None, :]) & (ck[:, None] < cs[None, :] + WIN_C)
    idx = np.clip(ck[:, None] - c[None, :] + WIN_C - 1, 0, 2 * WIN_C - 2)
    onehot = (idx[..., None] == np.arange(2 * WIN_C - 1)).astype(np.float32)
    tab = jnp.einsum('hrx,kcx->hrkc', rpb, jnp.asarray(onehot), precision=lax.Precision.HIGHEST)
    return jnp.where(jnp.asarray(valid), tab * -LOG2E, MASKED)


def _na_attention(q, k, vt, kc, vtc, rpb):
    bsz, t, w = q.shape
    p = kc.shape[1]
    tq = NA_ROWS * GRID_W
    n_rb = t // tq
    bc = _na_bias_table(rpb)
    wt = NA_TILES * LANES
    tqs = NA_RBS * tq
    return pl.pallas_call(
        functools.partial(_na_kernel, n_rb=n_rb),
        grid=(w // wt, bsz, n_rb // NA_RBS),
        in_specs=[pl.BlockSpec((1, tqs, wt), lambda j, b, r: (b, r, j)),
                  pl.BlockSpec((1, t, wt), lambda j, b, r: (b, 0, j)),
                  pl.BlockSpec((1, n_rb, wt, tq), lambda j, b, r: (b, 0, j, 0)),
                  pl.BlockSpec((1, p, wt), lambda j, b, r: (b, 0, j)),
                  pl.BlockSpec((1, wt, p), lambda j, b, r: (b, j, 0)),
                  pl.BlockSpec((2 * NA_TILES, 2 * WIN_R - 1, GRID_W, GRID_W), lambda j, b, r: (j, 0, 0, 0))],
        out_specs=pl.BlockSpec((1, tqs, wt), lambda j, b, r: (b, r, j)),
        out_shape=jax.ShapeDtypeStruct((bsz, t, w), BF16),
        scratch_shapes=[pltpu.VMEM((3, NA_WIN * GRID_W, NA_TILES * 2 * tq), F32)],
        compiler_params=_cparams(("arbitrary", "arbitrary", "arbitrary")), name="na_attention",
    )(q, k, vt, kc, vtc, bc)


_GQA_PERM = np.array([0, 4, 1, 5, 2, 6, 3, 7])


def _perm_heads(w, axis):
    return jnp.concatenate([lax.slice_in_dim(w, h * DH_B, (h + 1) * DH_B, axis=axis) for h in _GQA_PERM],
                           axis=axis)


def _tok_major(cache):
    b, h, p, dh = cache.shape
    return cache.transpose(0, 2, 1, 3).reshape(b, p, h * dh)


def _feat_major(cache):
    b, h, p, dh = cache.shape
    return cache.transpose(0, 1, 3, 2).reshape(b, h * dh, p)


def _head_major_from_t(x_t, n_heads):
    b, w, t = x_t.shape
    return jnp.swapaxes(x_t.reshape(b, 1, n_heads, w // n_heads, t), -1, -2)


def _even_layer(xp, xs, mod, g_pre, g_post, w_in, b_gates, g_hn, g_q, g_k, w_out, st_c, st_n, st_m,
                ck, cv, rope_tabs):
    o_v, o_o = 2 * W_A, 3 * W_A
    o_g = 4 * W_A
    o_qb = o_g + 4 * H_A
    o_kb = o_qb + W_B
    o_vb = o_kb + HKV_B * DH_B
    o_z = o_vb + HKV_B * DH_B
    wb = w_in.astype(BF16)
    w_qk = wb[:, :o_v]
    w_va = wb[:, o_v:o_o]
    w_oa = wb[:, o_o:o_g]
    w_g = jnp.pad(wb[:, o_g:o_qb], ((0, 0), (0, LANES - 4 * H_A)))
    w_gqa = jnp.concatenate([_perm_heads(wb[:, o_qb:o_kb], 1), wb[:, o_kb:o_vb]], axis=1)
    w_vb = wb[:, o_vb:o_z]
    w_z = jnp.concatenate([wb[:, o_z:o_z + W_A], _perm_heads(wb[:, o_z + W_A:], 1)], axis=1)
    wob = w_out.astype(BF16)
    w_o = jnp.concatenate([wob[:W_A], _perm_heads(wob[W_A:], 0)], axis=0)

    def stream(x, row0, per_batch, init, cache, rope, emit):
        bsz, t, _ = x.shape
        tm = 512
        xf = x if per_batch else x.reshape(1, bsz * t, D_MODEL)
        vb_seg = (w_vb, BF16, None) if per_batch else (w_vb, F32, t)
        gqa = dict(w=w_gqa, g_q=g_q, g_k=g_k, rope=rope, knt_block=t if emit else None)
        outs = _proj(xf, g_pre, mod, [(w_qk, BF16), (w_z, BF16)],
                     [(w_va, BF16, CHUNK_A), (w_oa, BF16, CHUNK_A), vb_seg], (w_g, b_gates), gqa,
                     row0=row0, per_batch=per_batch, tm=tm)
        qk, z, vta, ota, vbt, gtt, qn, kn = outs[:8]
        if not per_batch:
            qk, z, qn, kn = (a.reshape(bsz, t, a.shape[-1]) for a in (qk, z, qn, kn))
            gtt = gtt.reshape(4 * H_A, bsz, t).transpose(1, 0, 2)
            vbt = vbt.reshape(bsz, HKV_B * DH_B, t)
            vta, ota = (a.reshape(bsz, t // CHUNK_A, W_A, CHUNK_A) for a in (vta, ota))
        res = _mlstm(qk, vta, ota, gtt, g_hn, init, emit_state=emit)
        ha = res[0]
        if per_batch:
            hb = _attention(qn, kn, vbt, cache, kv_shared=True, tq=512, n_tiles=2)
        else:
            hb = _attention(qn, kn, vbt, cache, kv_shared=True, tq=t, n_tiles=W_B // LANES)
        y = _outproj([ha.reshape(xf.shape[0], -1, W_A), hb.reshape(xf.shape[0], -1, W_B)],
                     z.reshape(xf.shape[0], -1, W_A + W_B), w_o, xf, g_post, mod,
                     row0=row0, per_batch=per_batch, tm=tm)
        knt = outs[8].reshape(bsz, HKV_B * DH_B, t) if emit else None
        return y.reshape(bsz, t, D_MODEL), res[1:], knt, vbt

    yp, st, knt_p, vbt_p = stream(xp, 0, False, None, None, None, True)
    n0 = st_n.transpose(0, 2, 1, 3)
    m0 = jnp.broadcast_to(st_m.transpose(0, 2, 1)[..., None], n0.shape)
    cache = (_tok_major(ck).astype(BF16), _feat_major(cv).astype(BF16))
    ys, _, _, _ = stream(xs, 1, True, (st_c, n0, m0), cache, rope_tabs, False)
    c_out, n_out, m_out = st
    new_n = n_out.transpose(0, 2, 1, 3)[:, None]
    new_m = m_out[..., 0].transpose(0, 2, 1)[:, None]
    return (yp, ys, c_out, new_n, new_m, _head_major_from_t(knt_p, HKV_B), _head_major_from_t(vbt_p, HKV_B))


def _odd_layer(xp, xs, mod, g_pre, g_post, w_in, rpb, w_out, ck, cv):
    w_q = (w_in[:, :W_C] * QSCALE).astype(BF16)
    w_k, w_v, w_z = (w_in[:, i * W_C:(i + 1) * W_C].astype(BF16) for i in range(1, 4))
    w_o = w_out.astype(BF16)
    tm = 512
    bsz, t, _ = xp.shape
    xf = xp.reshape(1, bsz * t, D_MODEL)
    q, k, z, kt, vt = _proj(xf, g_pre, mod, [(w_q, BF16), (w_k, BF16), (w_z, BF16)],
                            [(w_k, F32, t), (w_v, F32, t)], None, None, row0=0, per_batch=False, tm=tm)
    kt = kt.reshape(bsz, W_C, t)
    vt = vt.reshape(bsz, W_C, t)
    o = _attention(q.reshape(bsz, t, W_C), k.reshape(bsz, t, W_C), vt, None,
                   kv_shared=False, tq=t, n_tiles=W_C // LANES)
    yp = _outproj([o.reshape(1, bsz * t, W_C)], z, w_o, xf, g_post, mod, row0=0, per_batch=False, tm=tm)
    yp = yp.reshape(bsz, t, D_MODEL)
    q, k, z, vts = _proj(xs, g_pre, mod, [(w_q, BF16), (w_k, BF16), (w_z, BF16)],
                         [(w_v, BF16, NA_ROWS * GRID_W)], None, None, row0=1, per_batch=True, tm=tm)
    o = _na_attention(q, k, vts, _tok_major(ck).astype(BF16), _feat_major(cv).astype(BF16), rpb)
    ys = _outproj([o], z, w_o, xs, g_post, mod, row0=1, per_batch=True, tm=tm)
    return yp, ys, _head_major_from_t(kt, H_C), _head_major_from_t(vt, H_C)


def kernel(x_prompt, x_sample, state_mlstm_C, state_mlstm_n, state_mlstm_m, cache_gqa_k, cache_gqa_v,
           cache_na_k, cache_na_v, c, c_ctx, w_mod, b_mod, g_pre, g_post, w_in_ab, b_gates_ab, g_hnorm_a,
           g_qnorm_b, g_knorm_b, w_out_ab, w_in_c, rpb_c, w_out_c):
    depth = w_mod.shape[0]
    assert depth == 2 and c.shape[0] == 2
    cvec = jnp.concatenate([c_ctx[None], c, jnp.zeros((8 - 1 - c.shape[0], D_MODEL), F32)], axis=0)
    mod = _modulation(cvec, w_mod, b_mod)
    rope_tabs = _rope_tables(x_sample.shape[1])
    xp, xs, c_out, n_out, m_out, gk, gv = _even_layer(
        x_prompt, x_sample, mod[0], g_pre[0], g_post[0], w_in_ab[0], b_gates_ab[0], g_hnorm_a[0],
        g_qnorm_b[0], g_knorm_b[0], w_out_ab[0], state_mlstm_C[:, 0], state_mlstm_n[:, 0],
        state_mlstm_m[:, 0], cache_gqa_k[:, 0], cache_gqa_v[:, 0], rope_tabs)
    xp, xs, nk, nv = _odd_layer(xp, xs, mod[1], g_pre[1], g_post[1], w_in_c[0], rpb_c[0], w_out_c[0],
                                cache_na_k[:, 0], cache_na_v[:, 0])
    return (xp, xs, c_out, n_out, m_out, gk, gv, nk, nv)
```

```python
import functools

import jax
import jax.numpy as jnp
import numpy as np
from jax import lax
from jax.experimental import pallas as pl
from jax.experimental.pallas import tpu as pltpu

F32 = jnp.float32
BF16 = jnp.bfloat16

D_MODEL = 1024
GRID_W = 64
EPS = 1e-6
H_A = 4
DH_A = 128
W_A = H_A * DH_A
CHUNK_A = 128
HQ_B = 8
HKV_B = 2
DH_B = 64
W_B = HQ_B * DH_B
ROPE_THETA = 10000.0
H_C = 16
DH_C = 64
W_C = H_C * DH_C
WIN_R = 8
WIN_C = 16

LANES = 128
NA_ROWS = 4
NA_TILES = 2
NA_RBS = 8
NA_BLKS = -(-(NA_ROWS + WIN_R - 1) // NA_ROWS)
NA_WIN = NA_BLKS * NA_ROWS
assert WIN_R // 2 == NA_ROWS
BF16_ROWS = 16
PROJ_SUBTILES = 2
KEY_CHUNK = 256
QK_AHEAD = 3
LOG2E = 1.4426950408889634
QSCALE = DH_B ** -0.5 * LOG2E
assert DH_B == DH_C
MASKED = 1e30
VMEM_LIMIT = 56 * 1024 * 1024


def _cparams(sem):
    return pltpu.CompilerParams(dimension_semantics=sem, vmem_limit_bytes=VMEM_LIMIT)


def _silu(x):
    return x / (1.0 + jnp.exp(-x))


def _sigmoid(x):
    return 1.0 / (1.0 + jnp.exp(-x))


def _log_sigmoid(x):
    return jnp.minimum(x, 0.0) - jnp.log1p(jnp.exp(-jnp.abs(x)))


def _dot_nt(a, b):
    return lax.dot_general(a, b, (((1,), (1,)), ((), ())), preferred_element_type=F32)


def _mod_kernel(c_ref, w_ref, b_ref, o_ref):
    s = _silu(c_ref[...])
    o_ref[0] = jnp.dot(s, w_ref[0], preferred_element_type=F32,
                       precision=lax.Precision.HIGHEST) + b_ref[0]


def _modulation(cvec, w_mod, b_mod):
    depth, d, n = w_mod.shape
    tn = n // 4
    return pl.pallas_call(
        _mod_kernel,
        grid=(depth, n // tn),
        in_specs=[pl.BlockSpec((8, d), lambda l, j: (0, 0)),
                  pl.BlockSpec((1, d, tn), lambda l, j: (l, 0, j)),
                  pl.BlockSpec((1, 1, tn), lambda l, j: (l, 0, j))],
        out_specs=pl.BlockSpec((1, 8, tn), lambda l, j: (l, 0, j)),
        out_shape=jax.ShapeDtypeStruct((depth, 8, n), F32),
        compiler_params=_cparams(("arbitrary", "arbitrary")),
        name="modulation",
    )(cvec, w_mod, b_mod.reshape(depth, 1, n))


def _head_norm(x, g):
    lo = lax.broadcasted_iota(jnp.int32, x.shape, 1) < DH_B
    x2 = x * x
    s_lo = jnp.sum(jnp.where(lo, x2, 0.0), axis=-1, keepdims=True)
    s_hi = jnp.sum(jnp.where(lo, 0.0, x2), axis=-1, keepdims=True)
    ms = jnp.where(lo, s_lo, s_hi) * (1.0 / DH_B)
    return x * lax.rsqrt(ms + EPS) * g


def _rope(x, cos, sin):
    quarter = DH_B // 4
    first = (lax.broadcasted_iota(jnp.int32, x.shape, 1) % (2 * quarter)) < quarter
    partner = jnp.where(first, pltpu.roll(x, LANES - quarter, 1), pltpu.roll(x, quarter, 1))
    return x * cos + partner * sin


def _store_t(ref, val, t_block, tok0):
    n_tok = val.shape[1]
    if t_block is None:
        ref[0, :, tok0:tok0 + n_tok] = val.astype(ref.dtype)
    else:
        for i in range(n_tok // t_block):
            ref[0, tok0 // t_block + i] = val[:, i * t_block:(i + 1) * t_block].astype(ref.dtype)


def _proj_kernel(*refs, n_seg, t_blocks, gate_rows, gqa, row0, per_batch):
    n_tseg = len(t_blocks) + (1 if gate_rows else 0)
    x_ref, g_ref, mod_ref = refs[:3]
    pos = 3
    w_refs = refs[pos:pos + n_seg]
    pos += n_seg
    wt_refs = refs[pos:pos + n_tseg]
    pos += n_tseg
    if gate_rows:
        bgt_ref = refs[pos]
        pos += 1
    if gqa is not None:
        rope, knt_block = gqa
        gq_ref, gk_ref = refs[pos:pos + 2]
        pos += 2
        if rope:
            cos_ref, sin_ref = refs[pos:pos + 2]
            pos += 2
    n_plain = n_seg - (1 if gqa is not None else 0)
    o_refs = refs[pos:pos + n_plain]
    pos += n_plain
    ot_refs = refs[pos:pos + len(t_blocks)]
    pos += len(t_blocks)
    if gate_rows:
        gto_ref = refs[pos]
        pos += 1
    if gqa is not None:
        q_out, k_out = refs[pos:pos + 2]
        pos += 2
        if knt_block is not None:
            knt_out = refs[pos]
            pos += 1
    wt_sc = refs[pos] if n_tseg else None
    t_offs = [0]
    for wt_ref in wt_refs:
        t_offs.append(t_offs[-1] + wt_ref.shape[1])

    @pl.when(jnp.logical_and(pl.program_id(0) == 0, pl.program_id(1) == 0))
    def _():
        for wt_ref, off in zip(wt_refs, t_offs):
            wt_sc[off:off + wt_ref.shape[1], :] = wt_ref[...].T

    d = x_ref.shape[-1]
    tm = x_ref.shape[1]
    row = row0 + (pl.program_id(0) if per_batch else 0)
    shift = mod_ref[pl.ds(row, 1), 0:d]
    scale = mod_ref[pl.ds(row, 1), d:2 * d]
    sub = tm // PROJ_SUBTILES
    for s in range(PROJ_SUBTILES):
        rows = slice(s * sub, (s + 1) * sub)
        x = x_ref[0, rows, :]
        r = lax.rsqrt(jnp.mean(x * x, axis=-1, keepdims=True) + EPS)
        h = (x * r * g_ref[...]) * (1.0 + scale) + shift
        hb = h.astype(BF16)
        if gqa is not None:
            res = jnp.dot(hb, w_refs[-1][...], preferred_element_type=F32)
        for w_ref, o_ref in zip(w_refs[:n_plain], o_refs):
            o_ref[0, rows, :] = jnp.dot(hb, w_ref[...], preferred_element_type=F32).astype(o_ref.dtype)
        if n_tseg:
            res_t = _dot_nt(wt_sc[...], hb)
            for ot_ref, tb, off, end in zip(ot_refs, t_blocks, t_offs, t_offs[1:]):
                _store_t(ot_ref, res_t[off:end], tb, s * sub)
            if gate_rows:
                off = t_offs[len(t_blocks)]
                gto_ref[0, :, rows] = res_t[off:off + gate_rows] + bgt_ref[...]
        if gqa is not None:
            nq = W_B // LANES
            for j in range(nq + 1):
                xn = _head_norm(res[:, j * LANES:(j + 1) * LANES], gq_ref[...] if j < nq else gk_ref[...])
                if j == nq and knt_block is not None:
                    _store_t(knt_out, xn.T, knt_block, s * sub)
                if rope:
                    xn = _rope(xn, cos_ref[rows, :], sin_ref[rows, :])
                if j < nq:
                    q_out[0, rows, j * LANES:(j + 1) * LANES] = (xn * QSCALE).astype(q_out.dtype)
                else:
                    k_out[0, rows, :] = xn.astype(k_out.dtype)


def _proj(x, g_pre, mod, segs, tsegs, gates, gqa, *, row0, per_batch, tm):
    bsz, t, d = x.shape
    grid = (bsz, t // tm)
    const = lambda b, i: (0, 0)
    in_specs = [pl.BlockSpec((1, tm, d), lambda b, i: (b, i, 0)),
                pl.BlockSpec((1, d), const),
                pl.BlockSpec(mod.shape, const)]
    args = [x, g_pre.reshape(1, d), mod]
    plain_ws = [w for w, _ in segs] + ([gqa["w"]] if gqa is not None else [])
    t_ws = [w for w, _, _ in tsegs] + ([gates[0]] if gates is not None else [])
    for w in plain_ws + t_ws:
        in_specs.append(pl.BlockSpec(w.shape, const))
        args.append(w)
    gate_rows = 0
    if gates is not None:
        gate_rows = gates[1].shape[0]
        in_specs.append(pl.BlockSpec((gate_rows, 1), const))
        args.append(gates[1].reshape(gate_rows, 1))
    if gqa is not None:
        in_specs += [pl.BlockSpec((1, LANES), const)] * 2
        args += [jnp.tile(gqa["g_q"], 2).reshape(1, LANES), jnp.tile(gqa["g_k"], 2).reshape(1, LANES)]
        if gqa["rope"] is not None:
            in_specs += [pl.BlockSpec((tm, LANES), lambda b, i: (i, 0))] * 2
            args += list(gqa["rope"])
    out_specs, out_shape = [], []
    for w, dt in segs:
        n = w.shape[1]
        out_specs.append(pl.BlockSpec((1, tm, n), lambda b, i: (b, i, 0)))
        out_shape.append(jax.ShapeDtypeStruct((bsz, t, n), dt))

    def add_t_out(n, dt, tb):
        if tb is None:
            out_specs.append(pl.BlockSpec((1, n, tm), lambda b, i: (b, 0, i)))
            out_shape.append(jax.ShapeDtypeStruct((bsz, n, t), dt))
        else:
            out_specs.append(pl.BlockSpec((1, tm // tb, n, tb), lambda b, i: (b, i, 0, 0)))
            out_shape.append(jax.ShapeDtypeStruct((bsz, t // tb, n, tb), dt))

    for w, dt, tb in tsegs:
        add_t_out(w.shape[1], dt, tb)
    if gates is not None:
        add_t_out(gate_rows, F32, None)
    gqa_static = None
    if gqa is not None:
        out_specs += [pl.BlockSpec((1, tm, W_B), lambda b, i: (b, i, 0)),
                      pl.BlockSpec((1, tm, LANES), lambda b, i: (b, i, 0))]
        out_shape += [jax.ShapeDtypeStruct((bsz, t, W_B), BF16), jax.ShapeDtypeStruct((bsz, t, LANES), BF16)]
        if gqa["knt_block"] is not None:
            add_t_out(LANES, F32, gqa["knt_block"])
        gqa_static = (gqa["rope"] is not None, gqa["knt_block"])
    kern = functools.partial(_proj_kernel, n_seg=len(plain_ws), t_blocks=tuple(tb for _, _, tb in tsegs),
                             gate_rows=gate_rows, gqa=gqa_static, row0=row0, per_batch=per_batch)
    return pl.pallas_call(
        kern, grid=grid, in_specs=in_specs, out_specs=out_specs, out_shape=out_shape,
        scratch_shapes=[pltpu.VMEM((sum(w.shape[1] for w in t_ws), d), BF16)] if t_ws else [],
        compiler_params=_cparams(("arbitrary", "arbitrary")), name="in_proj",
    )(*args)


def _outproj_kernel(*refs, n_in, row0, per_batch):
    a_refs = refs[:n_in]
    z_ref, w_ref, x_ref, gp_ref, mod_ref, o_ref = refs[n_in:n_in + 6]
    d = x_ref.shape[-1]
    row = row0 + (pl.program_id(0) if per_batch else 0)
    gate = mod_ref[pl.ds(row, 1), 2 * d:3 * d]
    z = z_ref[0].astype(F32)
    sz = _silu(z)
    acc = None
    off = 0
    for a_ref in a_refs:
        kk = a_ref.shape[-1]
        y = (a_ref[0].astype(F32) * sz[:, off:off + kk]).astype(BF16)
        part = jnp.dot(y, w_ref[off:off + kk, :], preferred_element_type=F32)
        acc = part if acc is None else acc + part
        off += kk
    r = lax.rsqrt(jnp.mean(acc * acc, axis=-1, keepdims=True) + EPS)
    o_ref[0] = x_ref[0] + gate * (acc * r * gp_ref[...])


def _outproj(parts, z, w_out, x, g_post, mod, *, row0, per_batch, tm):
    bsz, t, d = x.shape
    grid = (bsz, t // tm)
    in_specs, args = [], []
    for a in parts:
        in_specs.append(pl.BlockSpec((1, tm, a.shape[-1]), lambda b, i: (b, i, 0)))
        args.append(a)
    in_specs += [pl.BlockSpec((1, tm, z.shape[-1]), lambda b, i: (b, i, 0)),
                 pl.BlockSpec(w_out.shape, lambda b, i: (0, 0)),
                 pl.BlockSpec((1, tm, d), lambda b, i: (b, i, 0)),
                 pl.BlockSpec((1, d), lambda b, i: (0, 0)),
                 pl.BlockSpec(mod.shape, lambda b, i: (0, 0))]
    args += [z, w_out, x, g_post.reshape(1, d), mod]
    kern = functools.partial(_outproj_kernel, n_in=len(parts), row0=row0, per_batch=per_batch)
    return pl.pallas_call(
        kern, grid=grid, in_specs=in_specs,
        out_specs=pl.BlockSpec((1, tm, d), lambda b, i: (b, i, 0)),
        out_shape=jax.ShapeDtypeStruct((bsz, t, d), F32),
        compiler_params=_cparams(("arbitrary", "arbitrary")), name="out_proj",
    )(*args)


def _split3_bf16(x):
    hi = x.astype(BF16)
    r1 = x - hi.astype(F32)
    mid = r1.astype(BF16)
    lo = (r1 - mid.astype(F32)).astype(BF16)
    return hi, mid, lo


def _mlstm_gate_rows(ig_f, fg_f, ig_b, fg_b):
    rr, ll = fg_f.shape
    lf = _log_sigmoid(jnp.concatenate([fg_f, fg_b], axis=0))
    pieces = jnp.concatenate(_split3_bf16(lf), axis=0)
    u = lax.broadcasted_iota(jnp.int32, (ll, ll), 0)
    t = lax.broadcasted_iota(jnp.int32, (ll, ll), 1)
    out = []
    for d, ig in enumerate((ig_f, ig_b)):
        tri = jnp.where((u >= t) if d else (u <= t), 1.0, 0.0).astype(BF16)
        y = jnp.dot(pieces, tri, preferred_element_type=F32)
        rows = slice(d * rr, (d + 1) * rr)
        b = y[0:2 * rr][rows] + y[2 * rr:4 * rr][rows] + y[4 * rr:6 * rr][rows]
        c = ig - b
        cmax = jnp.broadcast_to(jnp.max(c, axis=-1, keepdims=True), c.shape)
        tot = jnp.broadcast_to(jnp.sum(lf[rows], axis=-1, keepdims=True), c.shape)
        out.append((c, b, cmax, tot))
    return out


def _mlstm_kernel(*refs, nc, hps, unroll, has_init, emit_state):
    q_ref, k_ref, vt_ref, ot_ref, gtt_ref, ghn_ref = refs[:6]
    pos = 6
    if has_init:
        c0_ref, n0_ref, m0_ref = refs[pos:pos + 3]
        pos += 3
    ha_ref = refs[pos]
    pos += 1
    if emit_state:
        cout_ref, nout_ref, mout_ref = refs[pos:pos + 3]
        pos += 3
    hs_sc, gate_sc, c_sc, n_sc, m_sc = refs[pos:pos + 5]
    ll, dh = CHUNK_A, DH_A
    kscale = dh ** -0.5
    hd0 = pl.program_id(1) * hps
    chains = [(i, d) for i in range(hps) for d in range(2)]

    for i, d in chains:
        if has_init:
            c_sc[i, d] = c0_ref[0, d, i]
            n_sc[i, d] = n0_ref[0, i, d:d + 1, :]
            m_sc[i, d] = m0_ref[0, i, d:d + 1, :]
        else:
            c_sc[i, d] = jnp.zeros((dh, dh), F32)
            n_sc[i, d] = jnp.zeros((1, dh), F32)
            m_sc[i, d] = jnp.zeros((1, dh), F32)

    def gate_rows(col0):
        return gtt_ref[0, pl.ds(pl.multiple_of((col0 + hd0) * nc, 8), hps * nc), :]

    gates = _mlstm_gate_rows(gate_rows(0), gate_rows(H_A), gate_rows(2 * H_A), gate_rows(3 * H_A))
    for d in range(2):
        for kind in range(4):
            gate_sc[d, kind] = gates[d][kind]

    ghn_t = [jnp.broadcast_to(ghn_ref[:, i * dh:(i + 1) * dh], (ll, dh)).T for i in range(hps)]
    si = lax.broadcasted_iota(jnp.int32, (ll, ll), 0)
    ti = lax.broadcasted_iota(jnp.int32, (ll, ll), 1)

    def run_trip(j, first_touch):
        jobs = []
        for u in range(unroll):
            step = j * unroll + u
            for i, d in chains:
                jobs.append((i, d, step if d == 0 else nc - 1 - step))
        state = {ch: [c_sc[ch], n_sc[ch], m_sc[ch]] for ch in chains}
        hcols = [slice(i * dh, (i + 1) * dh) for i in range(hps)]

        def tok_rows(cidx):
            return pl.ds(pl.multiple_of(cidx * ll, ll), ll)

        rows = []
        for i, d, cidx in jobs:
            c_r, b_r, cmax_r, tot_r = (gate_sc[d, kind, pl.ds(i * nc + cidx, 1), :] for kind in range(4))
            m_st = state[(i, d)][2]
            m_c = jnp.maximum(m_st, cmax_r)
            state[(i, d)][2] = tot_r + m_c
            rows.append((c_r, b_r, m_st, jnp.exp(m_st - m_c), jnp.exp(c_r - m_c) * kscale))
        start = []
        for (i, d, cidx), (_, _, _, a_st, wk) in zip(jobs, rows):
            k = k_ref[0, tok_rows(cidx), hcols[i]]
            vt = vt_ref[0, cidx, hcols[i], :]
            vw = jnp.concatenate([vt.astype(F32) * wk, jnp.broadcast_to(wk, (BF16_ROWS, ll))], axis=0)
            upd = jnp.dot(vw.astype(BF16), k, preferred_element_type=F32)
            c_st, n_st, _ = state[(i, d)]
            start.append((c_st, n_st))
            state[(i, d)][0] = a_st * c_st + upd[:dh]
            state[(i, d)][1] = a_st * n_st + upd[dh:dh + 1]
        prods = []
        for (i, d, cidx), (c_st, n_st) in zip(jobs, start):
            q = q_ref[0, tok_rows(cidx), hcols[i]]
            k = k_ref[0, tok_rows(cidx), hcols[i]]
            n16 = jnp.broadcast_to(n_st.astype(BF16), (BF16_ROWS, dh))
            prods.append(_dot_nt(jnp.concatenate([k, c_st.astype(BF16), n16], axis=0), q))
        for (i, d, cidx), (c_r, b_r, m_st, _, _), r in zip(jobs, rows, prods):
            allowed = (si >= ti) if d else (si <= ti)
            cb = jnp.where(allowed, jnp.broadcast_to(c_r, (ll, ll)).T, -jnp.inf)
            big_m = jnp.maximum(m_st, jnp.max(cb, axis=0, keepdims=True))
            p = jnp.exp(cb - (big_m - float(np.log(kscale)))) * r[:ll]
            w_inter = jnp.exp(m_st - big_m)
            vt = vt_ref[0, cidx, hcols[i], :]
            num = jnp.dot(vt, p.astype(BF16), preferred_element_type=F32) - r[ll:ll + dh] * (-w_inter)
            den = w_inter * r[ll + dh:ll + dh + 1] + jnp.sum(p, axis=0, keepdims=True)
            h_t = num / jnp.maximum(jnp.abs(den), jnp.exp(-(b_r + big_m)))
            if first_touch:
                hs_sc[i, cidx] = h_t
            else:
                hsum = h_t + hs_sc[i, cidx]
                rn = lax.rsqrt(jnp.mean(hsum * hsum, axis=0, keepdims=True) + EPS)
                out_t = (hsum * rn * ghn_t[i]) * _sigmoid(ot_ref[0, cidx, hcols[i], :].astype(F32))
                ha_ref[0, tok_rows(cidx), hcols[i]] = out_t.T.astype(ha_ref.dtype)
        for ch in chains:
            c_sc[ch], n_sc[ch], m_sc[ch] = state[ch]

    def make_body(first_touch):
        def body(j, carry):
            run_trip(j, first_touch)
            return carry
        return body

    trips = nc // unroll
    lax.fori_loop(0, trips // 2, make_body(True), 0)
    lax.fori_loop(trips // 2, trips, make_body(False), 0)

    if emit_state:
        for i, d in chains:
            cout_ref[0, 0, d, i] = c_sc[i, d]
            nout_ref[0, i, d:d + 1, :] = n_sc[i, d]
            mout_ref[0, i, d:d + 1, :] = m_sc[i, d]


def _mlstm(qk, vt, ot, gates_t, g_hn, init, *, emit_state):
    bsz, t, _ = qk.shape
    nc = t // CHUNK_A
    hps, unroll = (H_A, 1) if nc < 8 else (1, 4)
    assert (nc // 2) % unroll == 0 and nc % 2 == 0
    gtt = gates_t.reshape(bsz, 4 * H_A * nc, CHUNK_A)
    wh = hps * DH_A

    def tblk():
        return pl.BlockSpec((1, nc, wh, CHUNK_A), lambda b, g: (b, 0, g, 0))

    in_specs = [pl.BlockSpec((1, t, wh), lambda b, g: (b, 0, g)),
                pl.BlockSpec((1, t, wh), lambda b, g: (b, 0, H_A // hps + g)),
                tblk(), tblk(),
                pl.BlockSpec((1, 4 * H_A * nc, CHUNK_A), lambda b, g: (b, 0, 0)),
                pl.BlockSpec((1, wh), lambda b, g: (0, g))]
    args = [qk, qk, vt, ot, gtt, g_hn.reshape(1, W_A)]
    if init is not None:
        c0, n0, m0 = init
        in_specs += [pl.BlockSpec((1, 2, hps, DH_A, DH_A), lambda b, g: (b, 0, g, 0, 0)),
                     pl.BlockSpec((1, hps, 2, DH_A), lambda b, g: (b, g, 0, 0)),
                     pl.BlockSpec((1, hps, 2, DH_A), lambda b, g: (b, g, 0, 0))]
        args += [c0, n0, m0]
    out_specs = [pl.BlockSpec((1, t, wh), lambda b, g: (b, 0, g))]
    out_shape = [jax.ShapeDtypeStruct((bsz, t, W_A), BF16)]
    if emit_state:
        out_specs += [pl.BlockSpec((1, 1, 2, hps, DH_A, DH_A), lambda b, g: (b, 0, 0, g, 0, 0)),
                      pl.BlockSpec((1, hps, 2, DH_A), lambda b, g: (b, g, 0, 0)),
                      pl.BlockSpec((1, hps, 2, DH_A), lambda b, g: (b, g, 0, 0))]
        out_shape += [jax.ShapeDtypeStruct((bsz, 1, 2, H_A, DH_A, DH_A), F32),
                      jax.ShapeDtypeStruct((bsz, H_A, 2, DH_A), F32),
                      jax.ShapeDtypeStruct((bsz, H_A, 2, DH_A), F32)]
    kern = functools.partial(_mlstm_kernel, nc=nc, hps=hps, unroll=unroll, has_init=init is not None,
                             emit_state=emit_state)
    return pl.pallas_call(
        kern, grid=(bsz, H_A // hps), in_specs=in_specs, out_specs=out_specs, out_shape=out_shape,
        scratch_shapes=[pltpu.VMEM((hps, nc, DH_A, CHUNK_A), F32), pltpu.VMEM((2, 4, hps * nc, CHUNK_A), F32),
                        pltpu.VMEM((hps, 2, DH_A, DH_A), F32), pltpu.VMEM((hps, 2, 1, DH_A), F32),
                        pltpu.VMEM((hps, 2, 1, DH_A), F32)],
        compiler_params=_cparams(("arbitrary", "arbitrary")), name="mlstm",
    )(*args)


def _rope_tables(n_tok):
    t = np.arange(n_tok)
    row = (t // GRID_W).astype(np.float32)
    colp = (t % GRID_W).astype(np.float32)
    quarter = DH_B // 4
    freqs = (np.float32(ROPE_THETA) ** (-np.arange(quarter, dtype=np.float32) / np.float32(quarter))).astype(np.float32)
    ar = row[:, None] * freqs
    ac = colp[:, None] * freqs
    cos = np.concatenate([np.cos(ar), np.cos(ar), np.cos(ac), np.cos(ac)], axis=-1)
    sin = np.concatenate([-np.sin(ar), np.sin(ar), -np.sin(ac), np.sin(ac)], axis=-1)
    return jnp.asarray(np.tile(cos, (1, 2)), F32), jnp.asarray(np.tile(sin, (1, 2)), F32)


def _split_heads_q(q):
    lo = lax.broadcasted_iota(jnp.int32, q.shape, 1) < (LANES // 2)
    zero = jnp.zeros_like(q)
    return jnp.concatenate([jnp.where(lo, q, zero), jnp.where(lo, zero, q)], axis=0)


def _merge_heads_o(o_t, tq):
    half = LANES // 2
    return jnp.concatenate([o_t[:half, :tq], o_t[half:, tq:]], axis=0).T


def _with_ones_rows(vt):
    return jnp.concatenate([vt, jnp.ones((BF16_ROWS, vt.shape[1]), BF16)], axis=0)


def _online_softmax_pv_t(problems):
    seq = [(pi, ci) for pi, (_, chunks, _) in enumerate(problems) for ci in range(len(chunks))]
    scores = {}

    def issue(t):
        pi, ci = seq[t]
        qm, chunks, _ = problems[pi]
        scores[(pi, ci)] = _dot_nt(chunks[ci][0], qm)

    for t in range(min(QK_AHEAD, len(seq))):
        issue(t)
    m = acc = None
    for t, (pi, ci) in enumerate(seq):
        if t + QK_AHEAD < len(seq):
            issue(t + QK_AHEAD)
        _, chunks, emit = problems[pi]
        _, vt, penalty = chunks[ci]
        s = scores.pop((pi, ci))
        if penalty is not None:
            s = s - penalty
        mc = jnp.max(s, axis=0, keepdims=True)
        m_new = mc if ci == 0 else jnp.maximum(m, mc)
        p = jnp.exp2(s - m_new).astype(BF16)
        part = jnp.dot(_with_ones_rows(vt), p, preferred_element_type=F32)
        acc = part if ci == 0 else jnp.exp2(m - m_new) * acc + part
        m = m_new
        if ci == len(chunks) - 1:
            dv = vt.shape[0]
            emit(acc[:dv] / acc[dv:dv + 1])


def _attn_kernel(*refs, n_tiles, kv_shared, has_cache):
    q_ref, k_ref, vt_ref = refs[:3]
    pos = 3
    if has_cache:
        kc_ref, vtc_ref = refs[pos:pos + 2]
        pos += 2
    o_ref = refs[pos]
    tq = q_ref.shape[1]
    s_len = k_ref.shape[1]
    sc = min(s_len, KEY_CHUNK)
    problems = []
    for j in range(n_tiles):
        kj = 0 if kv_shared else j
        kcols = slice(kj * LANES, (kj + 1) * LANES)
        qm = _split_heads_q(q_ref[0, :, j * LANES:(j + 1) * LANES])
        chunks = [(k_ref[0, c * sc:(c + 1) * sc, kcols].astype(BF16),
                   vt_ref[0, kcols, c * sc:(c + 1) * sc].astype(BF16), None) for c in range(s_len // sc)]
        if has_cache:
            chunks.append((kc_ref[0], vtc_ref[0], None))

        def emit(o_t, j=j):
            o_ref[0, :, j * LANES:(j + 1) * LANES] = _merge_heads_o(o_t, tq).astype(o_ref.dtype)

        problems.append((qm, chunks, emit))
    _online_softmax_pv_t(problems)


def _attention(q, k, vt, cache, *, kv_shared, tq, n_tiles):
    bsz, t, w = q.shape
    s = k.shape[1]
    wt = n_tiles * LANES
    if kv_shared:
        k_spec = pl.BlockSpec((1, s, LANES), lambda b, i, g: (b, 0, 0))
        vt_spec = pl.BlockSpec((1, LANES, s), lambda b, i, g: (b, 0, 0))
    else:
        k_spec = pl.BlockSpec((1, s, wt), lambda b, i, g: (b, 0, g))
        vt_spec = pl.BlockSpec((1, wt, s), lambda b, i, g: (b, g, 0))
    in_specs = [pl.BlockSpec((1, tq, wt), lambda b, i, g: (b, i, g)), k_spec, vt_spec]
    args = [q, k, vt]
    if cache is not None:
        p = cache[0].shape[1]
        in_specs += [pl.BlockSpec((1, p, LANES), lambda b, i, g: (b, 0, 0)),
                     pl.BlockSpec((1, LANES, p), lambda b, i, g: (b, 0, 0))]
        args += list(cache)
    kern = functools.partial(_attn_kernel, n_tiles=n_tiles, kv_shared=kv_shared,
                             has_cache=cache is not None)
    return pl.pallas_call(
        kern, grid=(bsz, t // tq, w // wt), in_specs=in_specs,
        out_specs=pl.BlockSpec((1, tq, wt), lambda b, i, g: (b, i, g)),
        out_shape=jax.ShapeDtypeStruct((bsz, t, w), BF16),
        compiler_params=_cparams(("arbitrary", "arbitrary", "arbitrary")), name="attention",
    )(*args)


def _na_bias_blocks(variant):
    out = {}
    for qr in range(NA_ROWS):
        for kr in range(NA_WIN):
            if variant == 0:
                dr = kr - qr if kr < WIN_R else None
            elif variant == 1:
                dr = kr - qr - WIN_R // 2 if qr <= kr < qr + WIN_R else None
            else:
                dr = kr - qr - (NA_WIN - NA_ROWS) if kr >= NA_WIN - WIN_R else None
            out[(qr, kr)] = None if dr is None else dr + WIN_R - 1
    return out


def _na_kernel(q_ref, k_ref, vt_ref, kc_ref, vtc_ref, bc_ref, o_ref, bias_sc, *, n_rb):
    w = GRID_W
    tq = NA_ROWS * w

    @pl.when(jnp.logical_and(pl.program_id(1) == 0, pl.program_id(2) == 0))
    def _():
        for variant in range(3):
            for (qr, kr), di in _na_bias_blocks(variant).items():
                for hh in range(2 * NA_TILES):
                    val = jnp.full((w, w), MASKED, F32) if di is None else bc_ref[hh, di]
                    bias_sc[variant, kr * w:(kr + 1) * w, hh * tq + qr * w:hh * tq + (qr + 1) * w] = val

    n_rows = n_rb * NA_ROWS
    problems = []
    for rr in range(NA_RBS):
        rb = pl.program_id(2) * NA_RBS + rr
        variant = jnp.where(rb == 0, 0, jnp.where(rb == n_rb - 1, 2, 1))
        ws = jnp.clip(rb * NA_ROWS - WIN_R // 2, 0, n_rows - NA_WIN)
        blk0 = ws // NA_ROWS
        start = ws * w
        qrows = slice(rr * tq, (rr + 1) * tq)
        for j in range(NA_TILES):
            cols = slice(j * LANES, (j + 1) * LANES)
            qcols = slice(j * 2 * tq, (j + 1) * 2 * tq)
            qm = _split_heads_q(q_ref[0, qrows, cols])
            chunks = [(k_ref[0, pl.ds(pl.multiple_of(start + i * tq, tq), tq), cols],
                       vt_ref[0, blk0 + i, cols, :],
                       bias_sc[variant, i * tq:(i + 1) * tq, qcols]) for i in range(NA_BLKS)]
            chunks.append((kc_ref[0, :, cols], vtc_ref[0, cols, :], None))

            def emit(o_t, qrows=qrows, cols=cols):
                o_ref[0, qrows, cols] = _merge_heads_o(o_t, tq).astype(o_ref.dtype)

            problems.append((qm, chunks, emit))
    _online_softmax_pv_t(problems)


def _na_bias_table(rpb):
    c = np.arange(GRID_W)
    cs = np.clip(c - WIN_C // 2, 0, GRID_W - WIN_C)
    ck = np.arange(GRID_W)
    valid = (ck[:, None] >= cs[None, :]) & (ck[:, None] < cs[None, :] + WIN_C)
    idx = np.clip(ck[:, None] - c[None, :] + WIN_C - 1, 0, 2 * WIN_C - 2)
    onehot = (idx[..., None] == np.arange(2 * WIN_C - 1)).astype(np.float32)
    tab = jnp.einsum('hrx,kcx->hrkc', rpb, jnp.asarray(onehot), precision=lax.Precision.HIGHEST)
    return jnp.where(jnp.asarray(valid), tab * -LOG2E, MASKED)


def _na_attention(q, k, vt, kc, vtc, rpb):
    bsz, t, w = q.shape
    p = kc.shape[1]
    tq = NA_ROWS * GRID_W
    n_rb = t // tq
    bc = _na_bias_table(rpb)
    wt = NA_TILES * LANES
    tqs = NA_RBS * tq
    return pl.pallas_call(
        functools.partial(_na_kernel, n_rb=n_rb),
        grid=(w // wt, bsz, n_rb // NA_RBS),
        in_specs=[pl.BlockSpec((1, tqs, wt), lambda j, b, r: (b, r, j)),
                  pl.BlockSpec((1, t, wt), lambda j, b, r: (b, 0, j)),
                  pl.BlockSpec((1, n_rb, wt, tq), lambda j, b, r: (b, 0, j, 0)),
                  pl.BlockSpec((1, p, wt), lambda j, b, r: (b, 0, j)),
                  pl.BlockSpec((1, wt, p), lambda j, b, r: (b, j, 0)),
                  pl.BlockSpec((2 * NA_TILES, 2 * WIN_R - 1, GRID_W, GRID_W), lambda j, b, r: (j, 0, 0, 0))],
        out_specs=pl.BlockSpec((1, tqs, wt), lambda j, b, r: (b, r, j)),
        out_shape=jax.ShapeDtypeStruct((bsz, t, w), BF16),
        scratch_shapes=[pltpu.VMEM((3, NA_WIN * GRID_W, NA_TILES * 2 * tq), F32)],
        compiler_params=_cparams(("arbitrary", "arbitrary", "arbitrary")), name="na_attention",
    )(q, k, vt, kc, vtc, bc)


_GQA_PERM = np.array([0, 4, 1, 5, 2, 6, 3, 7])


def _perm_heads(w, axis):
    return jnp.concatenate([lax.slice_in_dim(w, h * DH_B, (h + 1) * DH_B, axis=axis) for h in _GQA_PERM],
                           axis=axis)


def _tok_major(cache):
    b, h, p, dh = cache.shape
    return cache.transpose(0, 2, 1, 3).reshape(b, p, h * dh)


def _feat_major(cache):
    b, h, p, dh = cache.shape
    return cache.transpose(0, 1, 3, 2).reshape(b, h * dh, p)


def _head_major_from_t(x_t, n_heads):
    b, w, t = x_t.shape
    return jnp.swapaxes(x_t.reshape(b, 1, n_heads, w // n_heads, t), -1, -2)


def _even_layer(xp, xs, mod, g_pre, g_post, w_in, b_gates, g_hn, g_q, g_k, w_out, st_c, st_n, st_m,
                ck, cv, rope_tabs):
    o_v, o_o = 2 * W_A, 3 * W_A
    o_g = 4 * W_A
    o_qb = o_g + 4 * H_A
    o_kb = o_qb + W_B
    o_vb = o_kb + HKV_B * DH_B
    o_z = o_vb + HKV_B * DH_B
    wb = w_in.astype(BF16)
    w_qk = wb[:, :o_v]
    w_va = wb[:, o_v:o_o]
    w_oa = wb[:, o_o:o_g]
    w_g = jnp.pad(wb[:, o_g:o_qb], ((0, 0), (0, LANES - 4 * H_A)))
    w_gqa = jnp.concatenate([_perm_heads(wb[:, o_qb:o_kb], 1), wb[:, o_kb:o_vb]], axis=1)
    w_vb = wb[:, o_vb:o_z]
    w_z = jnp.concatenate([wb[:, o_z:o_z + W_A], _perm_heads(wb[:, o_z + W_A:], 1)], axis=1)
    wob = w_out.astype(BF16)
    w_o = jnp.concatenate([wob[:W_A], _perm_heads(wob[W_A:], 0)], axis=0)

    def stream(x, row0, per_batch, init, cache, rope, emit):
        bsz, t, _ = x.shape
        tm = 512
        xf = x if per_batch else x.reshape(1, bsz * t, D_MODEL)
        vb_seg = (w_vb, BF16, None) if per_batch else (w_vb, F32, t)
        gqa = dict(w=w_gqa, g_q=g_q, g_k=g_k, rope=rope, knt_block=t if emit else None)
        outs = _proj(xf, g_pre, mod, [(w_qk, BF16), (w_z, BF16)],
                     [(w_va, BF16, CHUNK_A), (w_oa, BF16, CHUNK_A), vb_seg], (w_g, b_gates), gqa,
                     row0=row0, per_batch=per_batch, tm=tm)
        qk, z, vta, ota, vbt, gtt, qn, kn = outs[:8]
        if not per_batch:
            qk, z, qn, kn = (a.reshape(bsz, t, a.shape[-1]) for a in (qk, z, qn, kn))
            gtt = gtt.reshape(4 * H_A, bsz, t).transpose(1, 0, 2)
            vbt = vbt.reshape(bsz, HKV_B * DH_B, t)
            vta, ota = (a.reshape(bsz, t // CHUNK_A, W_A, CHUNK_A) for a in (vta, ota))
        res = _mlstm(qk, vta, ota, gtt, g_hn, init, emit_state=emit)
        ha = res[0]
        if per_batch:
            hb = _attention(qn, kn, vbt, cache, kv_shared=True, tq=512, n_tiles=4)
        else:
            hb = _attention(qn, kn, vbt, cache, kv_shared=True, tq=t, n_tiles=W_B // LANES)
        y = _outproj([ha.reshape(xf.shape[0], -1, W_A), hb.reshape(xf.shape[0], -1, W_B)],
                     z.reshape(xf.shape[0], -1, W_A + W_B), w_o, xf, g_post, mod,
                     row0=row0, per_batch=per_batch, tm=tm)
        knt = outs[8].reshape(bsz, HKV_B * DH_B, t) if emit else None
        return y.reshape(bsz, t, D_MODEL), res[1:], knt, vbt

    yp, st, knt_p, vbt_p = stream(xp, 0, False, None, None, None, True)
    n0 = st_n.transpose(0, 2, 1, 3)
    m0 = jnp.broadcast_to(st_m.transpose(0, 2, 1)[..., None], n0.shape)
    cache = (_tok_major(ck).astype(BF16), _feat_major(cv).astype(BF16))
    ys, _, _, _ = stream(xs, 1, True, (st_c, n0, m0), cache, rope_tabs, False)
    c_out, n_out, m_out = st
    new_n = n_out.transpose(0, 2, 1, 3)[:, None]
    new_m = m_out[..., 0].transpose(0, 2, 1)[:, None]
    return (yp, ys, c_out, new_n, new_m, _head_major_from_t(knt_p, HKV_B), _head_major_from_t(vbt_p, HKV_B))


def _odd_layer(xp, xs, mod, g_pre, g_post, w_in, rpb, w_out, ck, cv):
    w_q = (w_in[:, :W_C] * QSCALE).astype(BF16)
    w_k, w_v, w_z = (w_in[:, i * W_C:(i + 1) * W_C].astype(BF16) for i in range(1, 4))
    w_o = w_out.astype(BF16)
    tm = 512
    bsz, t, _ = xp.shape
    xf = xp.reshape(1, bsz * t, D_MODEL)
    q, k, z, kt, vt = _proj(xf, g_pre, mod, [(w_q, BF16), (w_k, BF16), (w_z, BF16)],
                            [(w_k, F32, t), (w_v, F32, t)], None, None, row0=0, per_batch=False, tm=tm)
    kt = kt.reshape(bsz, W_C, t)
    vt = vt.reshape(bsz, W_C, t)
    o = _attention(q.reshape(bsz, t, W_C), k.reshape(bsz, t, W_C), vt, None,
                   kv_shared=False, tq=t, n_tiles=W_C // LANES)
    yp = _outproj([o.reshape(1, bsz * t, W_C)], z, w_o, xf, g_post, mod, row0=0, per_batch=False, tm=tm)
    yp = yp.reshape(bsz, t, D_MODEL)
    q, k, z, vts = _proj(xs, g_pre, mod, [(w_q, BF16), (w_k, BF16), (w_z, BF16)],
                         [(w_v, BF16, NA_ROWS * GRID_W)], None, None, row0=1, per_batch=True, tm=tm)
    o = _na_attention(q, k, vts, _tok_major(ck).astype(BF16), _feat_major(cv).astype(BF16), rpb)
    ys = _outproj([o], z, w_o, xs, g_post, mod, row0=1, per_batch=True, tm=tm)
    return yp, ys, _head_major_from_t(kt, H_C), _head_major_from_t(vt, H_C)


def kernel(x_prompt, x_sample, state_mlstm_C, state_mlstm_n, state_mlstm_m, cache_gqa_k, cache_gqa_v,
           cache_na_k, cache_na_v, c, c_ctx, w_mod, b_mod, g_pre, g_post, w_in_ab, b_gates_ab, g_hnorm_a,
           g_qnorm_b, g_knorm_b, w_out_ab, w_in_c, rpb_c, w_out_c):
    depth = w_mod.shape[0]
    assert depth == 2 and c.shape[0] == 2
    cvec = jnp.concatenate([c_ctx[None], c, jnp.zeros((8 - 1 - c.shape[0], D_MODEL), F32)], axis=0)
    mod = _modulation(cvec, w_mod, b_mod)
    rope_tabs = _rope_tables(x_sample.shape[1])
    xp, xs, c_out, n_out, m_out, gk, gv = _even_layer(
        x_prompt, x_sample, mod[0], g_pre[0], g_post[0], w_in_ab[0], b_gates_ab[0], g_hnorm_a[0],
        g_qnorm_b[0], g_knorm_b[0], w_out_ab[0], state_mlstm_C[:, 0], state_mlstm_n[:, 0],
        state_mlstm_m[:, 0], cache_gqa_k[:, 0], cache_gqa_v[:, 0], rope_tabs)
    xp, xs, nk, nv = _odd_layer(xp, xs, mod[1], g_pre[1], g_post[1], w_in_c[0], rpb_c[0], w_out_c[0],
                                cache_na_k[:, 0], cache_na_v[:, 0])
    return (xp, xs, c_out, n_out, m_out, gk, gv, nk, nv)
```

```python
import functools

import jax
import jax.numpy as jnp
import numpy as np
from jax import lax
from jax.experimental import pallas as pl
from jax.experimental.pallas import tpu as pltpu

F32 = jnp.float32
BF16 = jnp.bfloat16

D_MODEL = 1024
GRID_W = 64
EPS = 1e-6
H_A = 4
DH_A = 128
W_A = H_A * DH_A
CHUNK_A = 128
HQ_B = 8
HKV_B = 2
DH_B = 64
W_B = HQ_B * DH_B
ROPE_THETA = 10000.0
H_C = 16
DH_C = 64
W_C = H_C * DH_C
WIN_R = 8
WIN_C = 16

LANES = 128
NA_ROWS = 4
NA_TILES = 2
NA_RBS = 8
NA_BLKS = -(-(NA_ROWS + WIN_R - 1) // NA_ROWS)
NA_WIN = NA_BLKS * NA_ROWS
assert WIN_R // 2 == NA_ROWS
BF16_ROWS = 16
PROJ_SUBTILES = 2
KEY_CHUNK = 256
QK_AHEAD = 3
LOG2E = 1.4426950408889634
QSCALE = DH_B ** -0.5 * LOG2E
assert DH_B == DH_C
MASKED = 1e30
VMEM_LIMIT = 56 * 1024 * 1024


def _cparams(sem):
    return pltpu.CompilerParams(dimension_semantics=sem, vmem_limit_bytes=VMEM_LIMIT)


def _silu(x):
    return x / (1.0 + jnp.exp(-x))


def _sigmoid(x):
    return 1.0 / (1.0 + jnp.exp(-x))


def _log_sigmoid(x):
    return jnp.minimum(x, 0.0) - jnp.log1p(jnp.exp(-jnp.abs(x)))


def _dot_nt(a, b):
    return lax.dot_general(a, b, (((1,), (1,)), ((), ())), preferred_element_type=F32)


def _mod_kernel(c_ref, w_ref, b_ref, o_ref):
    s = _silu(c_ref[...])
    o_ref[0] = jnp.dot(s, w_ref[0], preferred_element_type=F32,
                       precision=lax.Precision.HIGHEST) + b_ref[0]


def _modulation(cvec, w_mod, b_mod):
    depth, d, n = w_mod.shape
    tn = n // 4
    return pl.pallas_call(
        _mod_kernel,
        grid=(depth, n // tn),
        in_specs=[pl.BlockSpec((8, d), lambda l, j: (0, 0)),
                  pl.BlockSpec((1, d, tn), lambda l, j: (l, 0, j)),
                  pl.BlockSpec((1, 1, tn), lambda l, j: (l, 0, j))],
        out_specs=pl.BlockSpec((1, 8, tn), lambda l, j: (l, 0, j)),
        out_shape=jax.ShapeDtypeStruct((depth, 8, n), F32),
        compiler_params=_cparams(("arbitrary", "arbitrary")),
        name="modulation",
    )(cvec, w_mod, b_mod.reshape(depth, 1, n))


def _head_norm(x, g):
    lo = lax.broadcasted_iota(jnp.int32, x.shape, 1) < DH_B
    x2 = x * x
    s_lo = jnp.sum(jnp.where(lo, x2, 0.0), axis=-1, keepdims=True)
    s_hi = jnp.sum(jnp.where(lo, 0.0, x2), axis=-1, keepdims=True)
    ms = jnp.where(lo, s_lo, s_hi) * (1.0 / DH_B)
    return x * lax.rsqrt(ms + EPS) * g


def _rope(x, cos, sin):
    quarter = DH_B // 4
    first = (lax.broadcasted_iota(jnp.int32, x.shape, 1) % (2 * quarter)) < quarter
    partner = jnp.where(first, pltpu.roll(x, LANES - quarter, 1), pltpu.roll(x, quarter, 1))
    return x * cos + partner * sin


def _store_t(ref, val, t_block, tok0):
    n_tok = val.shape[1]
    if t_block is None:
        ref[0, :, tok0:tok0 + n_tok] = val.astype(ref.dtype)
    else:
        for i in range(n_tok // t_block):
            ref[0, tok0 // t_block + i] = val[:, i * t_block:(i + 1) * t_block].astype(ref.dtype)


def _proj_kernel(*refs, n_seg, t_blocks, gate_rows, gqa, row0, per_batch):
    n_tseg = len(t_blocks) + (1 if gate_rows else 0)
    x_ref, g_ref, mod_ref = refs[:3]
    pos = 3
    w_refs = refs[pos:pos + n_seg]
    pos += n_seg
    wt_refs = refs[pos:pos + n_tseg]
    pos += n_tseg
    if gate_rows:
        bgt_ref = refs[pos]
        pos += 1
    if gqa is not None:
        rope, knt_block = gqa
        gq_ref, gk_ref = refs[pos:pos + 2]
        pos += 2
        if rope:
            cos_ref, sin_ref = refs[pos:pos + 2]
            pos += 2
    n_plain = n_seg - (1 if gqa is not None else 0)
    o_refs = refs[pos:pos + n_plain]
    pos += n_plain
    ot_refs = refs[pos:pos + len(t_blocks)]
    pos += len(t_blocks)
    if gate_rows:
        gto_ref = refs[pos]
        pos += 1
    if gqa is not None:
        q_out, k_out = refs[pos:pos + 2]
        pos += 2
        if knt_block is not None:
            knt_out = refs[pos]
            pos += 1
    wt_sc = refs[pos] if n_tseg else None
    t_offs = [0]
    for wt_ref in wt_refs:
        t_offs.append(t_offs[-1] + wt_ref.shape[1])

    @pl.when(jnp.logical_and(pl.program_id(0) == 0, pl.program_id(1) == 0))
    def _():
        for wt_ref, off in zip(wt_refs, t_offs):
            wt_sc[off:off + wt_ref.shape[1], :] = wt_ref[...].T

    d = x_ref.shape[-1]
    tm = x_ref.shape[1]
    row = row0 + (pl.program_id(0) if per_batch else 0)
    shift = mod_ref[pl.ds(row, 1), 0:d]
    scale = mod_ref[pl.ds(row, 1), d:2 * d]
    sub = tm // PROJ_SUBTILES
    for s in range(PROJ_SUBTILES):
        rows = slice(s * sub, (s + 1) * sub)
        x = x_ref[0, rows, :]
        r = lax.rsqrt(jnp.mean(x * x, axis=-1, keepdims=True) + EPS)
        h = (x * r * g_ref[...]) * (1.0 + scale) + shift
        hb = h.astype(BF16)
        if gqa is not None:
            res = jnp.dot(hb, w_refs[-1][...], preferred_element_type=F32)
        for w_ref, o_ref in zip(w_refs[:n_plain], o_refs):
            o_ref[0, rows, :] = jnp.dot(hb, w_ref[...], preferred_element_type=F32).astype(o_ref.dtype)
        if n_tseg:
            res_t = _dot_nt(wt_sc[...], hb)
            for ot_ref, tb, off, end in zip(ot_refs, t_blocks, t_offs, t_offs[1:]):
                _store_t(ot_ref, res_t[off:end], tb, s * sub)
            if gate_rows:
                off = t_offs[len(t_blocks)]
                gto_ref[0, :, rows] = res_t[off:off + gate_rows] + bgt_ref[...]
        if gqa is not None:
            nq = W_B // LANES
            for j in range(nq + 1):
                xn = _head_norm(res[:, j * LANES:(j + 1) * LANES], gq_ref[...] if j < nq else gk_ref[...])
                if j == nq and knt_block is not None:
                    _store_t(knt_out, xn.T, knt_block, s * sub)
                if rope:
                    xn = _rope(xn, cos_ref[rows, :], sin_ref[rows, :])
                if j < nq:
                    q_out[0, rows, j * LANES:(j + 1) * LANES] = (xn * QSCALE).astype(q_out.dtype)
                else:
                    k_out[0, rows, :] = xn.astype(k_out.dtype)


def _proj(x, g_pre, mod, segs, tsegs, gates, gqa, *, row0, per_batch, tm):
    bsz, t, d = x.shape
    grid = (bsz, t // tm)
    const = lambda b, i: (0, 0)
    in_specs = [pl.BlockSpec((1, tm, d), lambda b, i: (b, i, 0)),
                pl.BlockSpec((1, d), const),
                pl.BlockSpec(mod.shape, const)]
    args = [x, g_pre.reshape(1, d), mod]
    plain_ws = [w for w, _ in segs] + ([gqa["w"]] if gqa is not None else [])
    t_ws = [w for w, _, _ in tsegs] + ([gates[0]] if gates is not None else [])
    for w in plain_ws + t_ws:
        in_specs.append(pl.BlockSpec(w.shape, const))
        args.append(w)
    gate_rows = 0
    if gates is not None:
        gate_rows = gates[1].shape[0]
        in_specs.append(pl.BlockSpec((gate_rows, 1), const))
        args.append(gates[1].reshape(gate_rows, 1))
    if gqa is not None:
        in_specs += [pl.BlockSpec((1, LANES), const)] * 2
        args += [jnp.tile(gqa["g_q"], 2).reshape(1, LANES), jnp.tile(gqa["g_k"], 2).reshape(1, LANES)]
        if gqa["rope"] is not None:
            in_specs += [pl.BlockSpec((tm, LANES), lambda b, i: (i, 0))] * 2
            args += list(gqa["rope"])
    out_specs, out_shape = [], []
    for w, dt in segs:
        n = w.shape[1]
        out_specs.append(pl.BlockSpec((1, tm, n), lambda b, i: (b, i, 0)))
        out_shape.append(jax.ShapeDtypeStruct((bsz, t, n), dt))

    def add_t_out(n, dt, tb):
        if tb is None:
            out_specs.append(pl.BlockSpec((1, n, tm), lambda b, i: (b, 0, i)))
            out_shape.append(jax.ShapeDtypeStruct((bsz, n, t), dt))
        else:
            out_specs.append(pl.BlockSpec((1, tm // tb, n, tb), lambda b, i: (b, i, 0, 0)))
            out_shape.append(jax.ShapeDtypeStruct((bsz, t // tb, n, tb), dt))

    for w, dt, tb in tsegs:
        add_t_out(w.shape[1], dt, tb)
    if gates is not None:
        add_t_out(gate_rows, F32, None)
    gqa_static = None
    if gqa is not None:
        out_specs += [pl.BlockSpec((1, tm, W_B), lambda b, i: (b, i, 0)),
                      pl.BlockSpec((1, tm, LANES), lambda b, i: (b, i, 0))]
        out_shape += [jax.ShapeDtypeStruct((bsz, t, W_B), BF16), jax.ShapeDtypeStruct((bsz, t, LANES), BF16)]
        if gqa["knt_block"] is not None:
            add_t_out(LANES, F32, gqa["knt_block"])
        gqa_static = (gqa["rope"] is not None, gqa["knt_block"])
    kern = functools.partial(_proj_kernel, n_seg=len(plain_ws), t_blocks=tuple(tb for _, _, tb in tsegs),
                             gate_rows=gate_rows, gqa=gqa_static, row0=row0, per_batch=per_batch)
    return pl.pallas_call(
        kern, grid=grid, in_specs=in_specs, out_specs=out_specs, out_shape=out_shape,
        scratch_shapes=[pltpu.VMEM((sum(w.shape[1] for w in t_ws), d), BF16)] if t_ws else [],
        compiler_params=_cparams(("arbitrary", "arbitrary")), name="in_proj",
    )(*args)


def _outproj_kernel(*refs, n_in, row0, per_batch):
    a_refs = refs[:n_in]
    z_ref, w_ref, x_ref, gp_ref, mod_ref, o_ref = refs[n_in:n_in + 6]
    d = x_ref.shape[-1]
    row = row0 + (pl.program_id(0) if per_batch else 0)
    gate = mod_ref[pl.ds(row, 1), 2 * d:3 * d]
    z = z_ref[0].astype(F32)
    sz = _silu(z)
    acc = None
    off = 0
    for a_ref in a_refs:
        kk = a_ref.shape[-1]
        y = (a_ref[0].astype(F32) * sz[:, off:off + kk]).astype(BF16)
        part = jnp.dot(y, w_ref[off:off + kk, :], preferred_element_type=F32)
        acc = part if acc is None else acc + part
        off += kk
    r = lax.rsqrt(jnp.mean(acc * acc, axis=-1, keepdims=True) + EPS)
    o_ref[0] = x_ref[0] + gate * (acc * r * gp_ref[...])


def _outproj(parts, z, w_out, x, g_post, mod, *, row0, per_batch, tm):
    bsz, t, d = x.shape
    grid = (bsz, t // tm)
    in_specs, args = [], []
    for a in parts:
        in_specs.append(pl.BlockSpec((1, tm, a.shape[-1]), lambda b, i: (b, i, 0)))
        args.append(a)
    in_specs += [pl.BlockSpec((1, tm, z.shape[-1]), lambda b, i: (b, i, 0)),
                 pl.BlockSpec(w_out.shape, lambda b, i: (0, 0)),
                 pl.BlockSpec((1, tm, d), lambda b, i: (b, i, 0)),
                 pl.BlockSpec((1, d), lambda b, i: (0, 0)),
                 pl.BlockSpec(mod.shape, lambda b, i: (0, 0))]
    args += [z, w_out, x, g_post.reshape(1, d), mod]
    kern = functools.partial(_outproj_kernel, n_in=len(parts), row0=row0, per_batch=per_batch)
    return pl.pallas_call(
        kern, grid=grid, in_specs=in_specs,
        out_specs=pl.BlockSpec((1, tm, d), lambda b, i: (b, i, 0)),
        out_shape=jax.ShapeDtypeStruct((bsz, t, d), F32),
        compiler_params=_cparams(("arbitrary", "arbitrary")), name="out_proj",
    )(*args)


def _split3_bf16(x):
    hi = x.astype(BF16)
    r1 = x - hi.astype(F32)
    mid = r1.astype(BF16)
    lo = (r1 - mid.astype(F32)).astype(BF16)
    return hi, mid, lo


def _mlstm_gate_rows(ig_f, fg_f, ig_b, fg_b):
    rr, ll = fg_f.shape
    lf = _log_sigmoid(jnp.concatenate([fg_f, fg_b], axis=0))
    pieces = jnp.concatenate(_split3_bf16(lf), axis=0)
    u = lax.broadcasted_iota(jnp.int32, (ll, ll), 0)
    t = lax.broadcasted_iota(jnp.int32, (ll, ll), 1)
    out = []
    for d, ig in enumerate((ig_f, ig_b)):
        tri = jnp.where((u >= t) if d else (u <= t), 1.0, 0.0).astype(BF16)
        y = jnp.dot(pieces, tri, preferred_element_type=F32)
        rows = slice(d * rr, (d + 1) * rr)
        b = y[0:2 * rr][rows] + y[2 * rr:4 * rr][rows] + y[4 * rr:6 * rr][rows]
        c = ig - b
        cmax = jnp.broadcast_to(jnp.max(c, axis=-1, keepdims=True), c.shape)
        tot = jnp.broadcast_to(jnp.sum(lf[rows], axis=-1, keepdims=True), c.shape)
        out.append((c, b, cmax, tot))
    return out


def _mlstm_kernel(*refs, nc, hps, unroll, has_init, emit_state):
    q_ref, k_ref, vt_ref, ot_ref, gtt_ref, ghn_ref = refs[:6]
    pos = 6
    if has_init:
        c0_ref, n0_ref, m0_ref = refs[pos:pos + 3]
        pos += 3
    ha_ref = refs[pos]
    pos += 1
    if emit_state:
        cout_ref, nout_ref, mout_ref = refs[pos:pos + 3]
        pos += 3
    hs_sc, gate_sc, c_sc, n_sc, m_sc = refs[pos:pos + 5]
    ll, dh = CHUNK_A, DH_A
    kscale = dh ** -0.5
    hd0 = pl.program_id(1) * hps
    chains = [(i, d) for i in range(hps) for d in range(2)]

    for i, d in chains:
        if has_init:
            c_sc[i, d] = c0_ref[0, d, i]
            n_sc[i, d] = n0_ref[0, i, d:d + 1, :]
            m_sc[i, d] = m0_ref[0, i, d:d + 1, :]
        else:
            c_sc[i, d] = jnp.zeros((dh, dh), F32)
            n_sc[i, d] = jnp.zeros((1, dh), F32)
            m_sc[i, d] = jnp.zeros((1, dh), F32)

    def gate_rows(col0):
        return gtt_ref[0, pl.ds(pl.multiple_of((col0 + hd0) * nc, 8), hps * nc), :]

    gates = _mlstm_gate_rows(gate_rows(0), gate_rows(H_A), gate_rows(2 * H_A), gate_rows(3 * H_A))
    for d in range(2):
        for kind in range(4):
            gate_sc[d, kind] = gates[d][kind]

    ghn_t = [jnp.broadcast_to(ghn_ref[:, i * dh:(i + 1) * dh], (ll, dh)).T for i in range(hps)]
    si = lax.broadcasted_iota(jnp.int32, (ll, ll), 0)
    ti = lax.broadcasted_iota(jnp.int32, (ll, ll), 1)

    def run_trip(j, first_touch):
        jobs = []
        for u in range(unroll):
            step = j * unroll + u
            for i, d in chains:
                jobs.append((i, d, step if d == 0 else nc - 1 - step))
        state = {ch: [c_sc[ch], n_sc[ch], m_sc[ch]] for ch in chains}
        hcols = [slice(i * dh, (i + 1) * dh) for i in range(hps)]

        def tok_rows(cidx):
            return pl.ds(pl.multiple_of(cidx * ll, ll), ll)

        rows = []
        for i, d, cidx in jobs:
            c_r, b_r, cmax_r, tot_r = (gate_sc[d, kind, pl.ds(i * nc + cidx, 1), :] for kind in range(4))
            m_st = state[(i, d)][2]
            m_c = jnp.maximum(m_st, cmax_r)
            state[(i, d)][2] = tot_r + m_c
            rows.append((c_r, b_r, m_st, jnp.exp(m_st - m_c), jnp.exp(c_r - m_c) * kscale))
        start = []
        for (i, d, cidx), (_, _, _, a_st, wk) in zip(jobs, rows):
            k = k_ref[0, tok_rows(cidx), hcols[i]]
            vt = vt_ref[0, cidx, hcols[i], :]
            vw = jnp.concatenate([vt.astype(F32) * wk, jnp.broadcast_to(wk, (BF16_ROWS, ll))], axis=0)
            upd = jnp.dot(vw.astype(BF16), k, preferred_element_type=F32)
            c_st, n_st, _ = state[(i, d)]
            start.append((c_st, n_st))
            state[(i, d)][0] = a_st * c_st + upd[:dh]
            state[(i, d)][1] = a_st * n_st + upd[dh:dh + 1]
        prods = []
        for (i, d, cidx), (c_st, n_st) in zip(jobs, start):
            q = q_ref[0, tok_rows(cidx), hcols[i]]
            k = k_ref[0, tok_rows(cidx), hcols[i]]
            n16 = jnp.broadcast_to(n_st.astype(BF16), (BF16_ROWS, dh))
            prods.append(_dot_nt(jnp.concatenate([k, c_st.astype(BF16), n16], axis=0), q))
        for (i, d, cidx), (c_r, b_r, m_st, _, _), r in zip(jobs, rows, prods):
            allowed = (si >= ti) if d else (si <= ti)
            cb = jnp.where(allowed, jnp.broadcast_to(c_r, (ll, ll)).T, -jnp.inf)
            big_m = jnp.maximum(m_st, jnp.max(cb, axis=0, keepdims=True))
            p = jnp.exp(cb - (big_m - float(np.log(kscale)))) * r[:ll]
            w_inter = jnp.exp(m_st - big_m)
            vt = vt_ref[0, cidx, hcols[i], :]
            num = jnp.dot(vt, p.astype(BF16), preferred_element_type=F32) - r[ll:ll + dh] * (-w_inter)
            den = w_inter * r[ll + dh:ll + dh + 1] + jnp.sum(p, axis=0, keepdims=True)
            h_t = num / jnp.maximum(jnp.abs(den), jnp.exp(-(b_r + big_m)))
            if first_touch:
                hs_sc[i, cidx] = h_t
            else:
                hsum = h_t + hs_sc[i, cidx]
                rn = lax.rsqrt(jnp.mean(hsum * hsum, axis=0, keepdims=True) + EPS)
                out_t = (hsum * rn * ghn_t[i]) * _sigmoid(ot_ref[0, cidx, hcols[i], :].astype(F32))
                ha_ref[0, tok_rows(cidx), hcols[i]] = out_t.T.astype(ha_ref.dtype)
        for ch in chains:
            c_sc[ch], n_sc[ch], m_sc[ch] = state[ch]

    def make_body(first_touch):
        def body(j, carry):
            run_trip(j, first_touch)
            return carry
        return body

    trips = nc // unroll
    lax.fori_loop(0, trips // 2, make_body(True), 0)
    lax.fori_loop(trips // 2, trips, make_body(False), 0)

    if emit_state:
        for i, d in chains:
            cout_ref[0, 0, d, i] = c_sc[i, d]
            nout_ref[0, i, d:d + 1, :] = n_sc[i, d]
            mout_ref[0, i, d:d + 1, :] = m_sc[i, d]


def _mlstm(qk, vt, ot, gates_t, g_hn, init, *, emit_state):
    bsz, t, _ = qk.shape
    nc = t // CHUNK_A
    hps, unroll = (H_A, 1) if nc < 8 else (1, 4)
    assert (nc // 2) % unroll == 0 and nc % 2 == 0
    gtt = gates_t.reshape(bsz, 4 * H_A * nc, CHUNK_A)
    wh = hps * DH_A

    def tblk():
        return pl.BlockSpec((1, nc, wh, CHUNK_A), lambda b, g: (b, 0, g, 0))

    in_specs = [pl.BlockSpec((1, t, wh), lambda b, g: (b, 0, g)),
                pl.BlockSpec((1, t, wh), lambda b, g: (b, 0, H_A // hps + g)),
                tblk(), tblk(),
                pl.BlockSpec((1, 4 * H_A * nc, CHUNK_A), lambda b, g: (b, 0, 0)),
                pl.BlockSpec((1, wh), lambda b, g: (0, g))]
    args = [qk, qk, vt, ot, gtt, g_hn.reshape(1, W_A)]
    if init is not None:
        c0, n0, m0 = init
        in_specs += [pl.BlockSpec((1, 2, hps, DH_A, DH_A), lambda b, g: (b, 0, g, 0, 0)),
                     pl.BlockSpec((1, hps, 2, DH_A), lambda b, g: (b, g, 0, 0)),
                     pl.BlockSpec((1, hps, 2, DH_A), lambda b, g: (b, g, 0, 0))]
        args += [c0, n0, m0]
    out_specs = [pl.BlockSpec((1, t, wh), lambda b, g: (b, 0, g))]
    out_shape = [jax.ShapeDtypeStruct((bsz, t, W_A), BF16)]
    if emit_state:
        out_specs += [pl.BlockSpec((1, 1, 2, hps, DH_A, DH_A), lambda b, g: (b, 0, 0, g, 0, 0)),
                      pl.BlockSpec((1, hps, 2, DH_A), lambda b, g: (b, g, 0, 0)),
                      pl.BlockSpec((1, hps, 2, DH_A), lambda b, g: (b, g, 0, 0))]
        out_shape += [jax.ShapeDtypeStruct((bsz, 1, 2, H_A, DH_A, DH_A), F32),
                      jax.ShapeDtypeStruct((bsz, H_A, 2, DH_A), F32),
                      jax.ShapeDtypeStruct((bsz, H_A, 2, DH_A), F32)]
    kern = functools.partial(_mlstm_kernel, nc=nc, hps=hps, unroll=unroll, has_init=init is not None,
                             emit_state=emit_state)
    return pl.pallas_call(
        kern, grid=(bsz, H_A // hps), in_specs=in_specs, out_specs=out_specs, out_shape=out_shape,
        scratch_shapes=[pltpu.VMEM((hps, nc, DH_A, CHUNK_A), F32), pltpu.VMEM((2, 4, hps * nc, CHUNK_A), F32),
                        pltpu.VMEM((hps, 2, DH_A, DH_A), F32), pltpu.VMEM((hps, 2, 1, DH_A), F32),
                        pltpu.VMEM((hps, 2, 1, DH_A), F32)],
        compiler_params=_cparams(("arbitrary", "arbitrary")), name="mlstm",
    )(*args)


def _rope_tables(n_tok):
    t = np.arange(n_tok)
    row = (t // GRID_W).astype(np.float32)
    colp = (t % GRID_W).astype(np.float32)
    quarter = DH_B // 4
    freqs = (np.float32(ROPE_THETA) ** (-np.arange(quarter, dtype=np.float32) / np.float32(quarter))).astype(np.float32)
    ar = row[:, None] * freqs
    ac = colp[:, None] * freqs
    cos = np.concatenate([np.cos(ar), np.cos(ar), np.cos(ac), np.cos(ac)], axis=-1)
    sin = np.concatenate([-np.sin(ar), np.sin(ar), -np.sin(ac), np.sin(ac)], axis=-1)
    return jnp.asarray(np.tile(cos, (1, 2)), F32), jnp.asarray(np.tile(sin, (1, 2)), F32)


def _split_heads_q(q):
    lo = lax.broadcasted_iota(jnp.int32, q.shape, 1) < (LANES // 2)
    zero = jnp.zeros_like(q)
    return jnp.concatenate([jnp.where(lo, q, zero), jnp.where(lo, zero, q)], axis=0)


def _merge_heads_o(o_t, tq):
    half = LANES // 2
    return jnp.concatenate([o_t[:half, :tq], o_t[half:, tq:]], axis=0).T


def _with_ones_rows(vt):
    return jnp.concatenate([vt, jnp.ones((BF16_ROWS, vt.shape[1]), BF16)], axis=0)


def _online_softmax_pv_t(problems):
    seq = [(pi, ci) for pi, (_, chunks, _) in enumerate(problems) for ci in range(len(chunks))]
    scores = {}

    def issue(t):
        pi, ci = seq[t]
        qm, chunks, _ = problems[pi]
        scores[(pi, ci)] = _dot_nt(chunks[ci][0], qm)

    for t in range(min(QK_AHEAD, len(seq))):
        issue(t)
    m = acc = None
    for t, (pi, ci) in enumerate(seq):
        if t + QK_AHEAD < len(seq):
            issue(t + QK_AHEAD)
        _, chunks, emit = problems[pi]
        _, vt, penalty = chunks[ci]
        s = scores.pop((pi, ci))
        if penalty is not None:
            s = s - penalty
        mc = jnp.max(s, axis=0, keepdims=True)
        m_new = mc if ci == 0 else jnp.maximum(m, mc)
        p = jnp.exp2(s - m_new).astype(BF16)
        part = jnp.dot(_with_ones_rows(vt), p, preferred_element_type=F32)
        acc = part if ci == 0 else jnp.exp2(m - m_new) * acc + part
        m = m_new
        if ci == len(chunks) - 1:
            dv = vt.shape[0]
            emit(acc[:dv] / acc[dv:dv + 1])


def _attn_kernel(*refs, n_tiles, kv_shared, has_cache):
    q_ref, k_ref, vt_ref = refs[:3]
    pos = 3
    if has_cache:
        kc_ref, vtc_ref = refs[pos:pos + 2]
        pos += 2
    o_ref = refs[pos]
    tq = q_ref.shape[1]
    s_len = k_ref.shape[1]
    sc = min(s_len, KEY_CHUNK)
    problems = []
    for j in range(n_tiles):
        kj = 0 if kv_shared else j
        kcols = slice(kj * LANES, (kj + 1) * LANES)
        qm = _split_heads_q(q_ref[0, :, j * LANES:(j + 1) * LANES])
        chunks = [(k_ref[0, c * sc:(c + 1) * sc, kcols].astype(BF16),
                   vt_ref[0, kcols, c * sc:(c + 1) * sc].astype(BF16), None) for c in range(s_len // sc)]
        if has_cache:
            chunks.append((kc_ref[0], vtc_ref[0], None))

        def emit(o_t, j=j):
            o_ref[0, :, j * LANES:(j + 1) * LANES] = _merge_heads_o(o_t, tq).astype(o_ref.dtype)

        problems.append((qm, chunks, emit))
    _online_softmax_pv_t(problems)


def _attention(q, k, vt, cache, *, kv_shared, tq, n_tiles):
    bsz, t, w = q.shape
    s = k.shape[1]
    wt = n_tiles * LANES
    if kv_shared:
        k_spec = pl.BlockSpec((1, s, LANES), lambda b, i, g: (b, 0, 0))
        vt_spec = pl.BlockSpec((1, LANES, s), lambda b, i, g: (b, 0, 0))
    else:
        k_spec = pl.BlockSpec((1, s, wt), lambda b, i, g: (b, 0, g))
        vt_spec = pl.BlockSpec((1, wt, s), lambda b, i, g: (b, g, 0))
    in_specs = [pl.BlockSpec((1, tq, wt), lambda b, i, g: (b, i, g)), k_spec, vt_spec]
    args = [q, k, vt]
    if cache is not None:
        p = cache[0].shape[1]
        in_specs += [pl.BlockSpec((1, p, LANES), lambda b, i, g: (b, 0, 0)),
                     pl.BlockSpec((1, LANES, p), lambda b, i, g: (b, 0, 0))]
        args += list(cache)
    kern = functools.partial(_attn_kernel, n_tiles=n_tiles, kv_shared=kv_shared,
                             has_cache=cache is not None)
    return pl.pallas_call(
        kern, grid=(bsz, t // tq, w // wt), in_specs=in_specs,
        out_specs=pl.BlockSpec((1, tq, wt), lambda b, i, g: (b, i, g)),
        out_shape=jax.ShapeDtypeStruct((bsz, t, w), BF16),
        compiler_params=_cparams(("arbitrary", "arbitrary", "arbitrary")), name="attention",
    )(*args)


def _na_bias_blocks(variant):
    out = {}
    for qr in range(NA_ROWS):
        for kr in range(NA_WIN):
            if variant == 0:
                dr = kr - qr if kr < WIN_R else None
            elif variant == 1:
                dr = kr - qr - WIN_R // 2 if qr <= kr < qr + WIN_R else None
            else:
                dr = kr - qr - (NA_WIN - NA_ROWS) if kr >= NA_WIN - WIN_R else None
            out[(qr, kr)] = None if dr is None else dr + WIN_R - 1
    return out


def _na_kernel(q_ref, k_ref, vt_ref, kc_ref, vtc_ref, bc_ref, o_ref, bias_sc, *, n_rb):
    w = GRID_W
    tq = NA_ROWS * w

    @pl.when(jnp.logical_and(pl.program_id(1) == 0, pl.program_id(2) == 0))
    def _():
        for variant in range(3):
            for (qr, kr), di in _na_bias_blocks(variant).items():
                for hh in range(2 * NA_TILES):
                    val = jnp.full((w, w), MASKED, F32) if di is None else bc_ref[hh, di]
                    bias_sc[variant, kr * w:(kr + 1) * w, hh * tq + qr * w:hh * tq + (qr + 1) * w] = val

    n_rows = n_rb * NA_ROWS
    problems = []
    for rr in range(NA_RBS):
        rb = pl.program_id(2) * NA_RBS + rr
        variant = jnp.where(rb == 0, 0, jnp.where(rb == n_rb - 1, 2, 1))
        ws = jnp.clip(rb * NA_ROWS - WIN_R // 2, 0, n_rows - NA_WIN)
        blk0 = ws // NA_ROWS
        start = ws * w
        qrows = slice(rr * tq, (rr + 1) * tq)
        for j in range(NA_TILES):
            cols = slice(j * LANES, (j + 1) * LANES)
            qcols = slice(j * 2 * tq, (j + 1) * 2 * tq)
            qm = _split_heads_q(q_ref[0, qrows, cols])
            chunks = [(k_ref[0, pl.ds(pl.multiple_of(start + i * tq, tq), tq), cols],
                       vt_ref[0, blk0 + i, cols, :],
                       bias_sc[variant, i * tq:(i + 1) * tq, qcols]) for i in range(NA_BLKS)]
            chunks.append((kc_ref[0, :, cols], vtc_ref[0, cols, :], None))

            def emit(o_t, qrows=qrows, cols=cols):
                o_ref[0, qrows, cols] = _merge_heads_o(o_t, tq).astype(o_ref.dtype)

            problems.append((qm, chunks, emit))
    _online_softmax_pv_t(problems)


def _na_bias_table(rpb):
    c = np.arange(GRID_W)
    cs = np.clip(c - WIN_C // 2, 0, GRID_W - WIN_C)
    ck = np.arange(GRID_W)
    valid = (ck[:, None] >= cs[None, :]) & (ck[:, None] < cs[None, :] + WIN_C)
    idx = np.clip(ck[:, None] - c[None, :] + WIN_C - 1, 0, 2 * WIN_C - 2)
    onehot = (idx[..., None] == np.arange(2 * WIN_C - 1)).astype(np.float32)
    tab = jnp.einsum('hrx,kcx->hrkc', rpb, jnp.asarray(onehot), precision=lax.Precision.HIGHEST)
    return jnp.where(jnp.asarray(valid), tab * -LOG2E, MASKED)


def _na_attention(q, k, vt, kc, vtc, rpb):
    bsz, t, w = q.shape
    p = kc.shape[1]
    tq = NA_ROWS * GRID_W
    n_rb = t // tq
    bc = _na_bias_table(rpb)
    wt = NA_TILES * LANES
    tqs = NA_RBS * tq
    return pl.pallas_call(
        functools.partial(_na_kernel, n_rb=n_rb),
        grid=(w // wt, bsz, n_rb // NA_RBS),
        in_specs=[pl.BlockSpec((1, tqs, wt), lambda j, b, r: (b, r, j)),
                  pl.BlockSpec((1, t, wt), lambda j, b, r: (b, 0, j)),
                  pl.BlockSpec((1, n_rb, wt, tq), lambda j, b, r: (b, 0, j, 0)),
                  pl.BlockSpec((1, p, wt), lambda j, b, r: (b, 0, j)),
                  pl.BlockSpec((1, wt, p), lambda j, b, r: (b, j, 0)),
                  pl.BlockSpec((2 * NA_TILES, 2 * WIN_R - 1, GRID_W, GRID_W), lambda j, b, r: (j, 0, 0, 0))],
        out_specs=pl.BlockSpec((1, tqs, wt), lambda j, b, r: (b, r, j)),
        out_shape=jax.ShapeDtypeStruct((bsz, t, w), BF16),
        scratch_shapes=[pltpu.VMEM((3, NA_WIN * GRID_W, NA_TILES * 2 * tq), F32)],
        compiler_params=_cparams(("arbitrary", "arbitrary", "arbitrary")), name="na_attention",
    )(q, k, vt, kc, vtc, bc)


def _relayout_kernel(w_ref, *o_refs, plans, axis):
    w = w_ref[...]
    for o_ref, pieces in zip(o_refs, plans):
        for dst, src, size, scale in pieces:
            if src is None:
                val = jnp.zeros(o_ref.shape[:axis] + (size,) + o_ref.shape[axis + 1:], F32)
            else:
                val = lax.slice_in_dim(w, src, src + size, axis=axis)
                if scale != 1.0:
                    val = val * scale
            idx = (slice(None),) * axis + (slice(dst, dst + size),)
            o_ref[idx] = val.astype(o_ref.dtype)


def _relayout(w, plans, sizes, *, axis, tile):
    r, c = w.shape
    other = 1 - axis
    grid = (w.shape[other] // tile,)

    def spec(extent):
        shape = (tile, extent) if axis == 1 else (extent, tile)
        return pl.BlockSpec(shape, (lambda i: (i, 0)) if axis == 1 else (lambda i: (0, i)))

    def full(extent):
        return (r, extent) if axis == 1 else (extent, c)

    return pl.pallas_call(
        functools.partial(_relayout_kernel, plans=plans, axis=axis),
        grid=grid, in_specs=[spec(w.shape[axis])], out_specs=[spec(n) for n in sizes],
        out_shape=[jax.ShapeDtypeStruct(full(n), BF16) for n in sizes],
        compiler_params=_cparams(("arbitrary",)), name="w_relayout",
    )(w)


_GQA_PERM = np.array([0, 4, 1, 5, 2, 6, 3, 7])


def _tok_major(cache):
    b, h, p, dh = cache.shape
    return cache.transpose(0, 2, 1, 3).reshape(b, p, h * dh)


def _feat_major(cache):
    b, h, p, dh = cache.shape
    return cache.transpose(0, 1, 3, 2).reshape(b, h * dh, p)


def _head_major_from_t(x_t, n_heads):
    b, w, t = x_t.shape
    return jnp.swapaxes(x_t.reshape(b, 1, n_heads, w // n_heads, t), -1, -2)


def _even_layer(xp, xs, mod, g_pre, g_post, w_in, b_gates, g_hn, g_q, g_k, w_out, st_c, st_n, st_m,
                ck, cv, rope_tabs):
    o_v, o_o = 2 * W_A, 3 * W_A
    o_g = 4 * W_A
    o_qb = o_g + 4 * H_A
    o_kb = o_qb + W_B
    o_vb = o_kb + HKV_B * DH_B
    o_z = o_vb + HKV_B * DH_B
    def heads(dst0, src0):
        return [(dst0 + p * DH_B, src0 + int(h) * DH_B, DH_B, 1.0) for p, h in enumerate(_GQA_PERM)]

    n_g = 4 * H_A
    w_qk, w_va, w_oa, w_g, w_gqa, w_vb, w_z = _relayout(
        w_in,
        [[(0, 0, o_v, 1.0)], [(0, o_v, W_A, 1.0)], [(0, o_o, W_A, 1.0)],
         [(0, o_g, n_g, 1.0), (n_g, None, LANES - n_g, 1.0)],
         heads(0, o_qb) + [(W_B, o_kb, HKV_B * DH_B, 1.0)],
         [(0, o_vb, HKV_B * DH_B, 1.0)],
         [(0, o_z, W_A, 1.0)] + heads(W_A, o_z + W_A)],
        [o_v, W_A, W_A, LANES, W_B + HKV_B * DH_B, HKV_B * DH_B, W_A + W_B], axis=1, tile=256)
    (w_o,) = _relayout(w_out, [[(0, 0, W_A, 1.0)] + heads(W_A, W_A)], [W_A + W_B], axis=0, tile=256)

    def stream(x, row0, per_batch, init, cache, rope, emit):
        bsz, t, _ = x.shape
        tm = 512
        xf = x if per_batch else x.reshape(1, bsz * t, D_MODEL)
        vb_seg = (w_vb, BF16, None) if per_batch else (w_vb, F32, t)
        gqa = dict(w=w_gqa, g_q=g_q, g_k=g_k, rope=rope, knt_block=t if emit else None)
        outs = _proj(xf, g_pre, mod, [(w_qk, BF16), (w_z, BF16)],
                     [(w_va, BF16, CHUNK_A), (w_oa, BF16, CHUNK_A), vb_seg], (w_g, b_gates), gqa,
                     row0=row0, per_batch=per_batch, tm=tm)
        qk, z, vta, ota, vbt, gtt, qn, kn = outs[:8]
        if not per_batch:
            qk, z, qn, kn = (a.reshape(bsz, t, a.shape[-1]) for a in (qk, z, qn, kn))
            gtt = gtt.reshape(4 * H_A, bsz, t).transpose(1, 0, 2)
            vbt = vbt.reshape(bsz, HKV_B * DH_B, t)
            vta, ota = (a.reshape(bsz, t // CHUNK_A, W_A, CHUNK_A) for a in (vta, ota))
        res = _mlstm(qk, vta, ota, gtt, g_hn, init, emit_state=emit)
        ha = res[0]
        if per_batch:
            hb = _attention(qn, kn, vbt, cache, kv_shared=True, tq=512, n_tiles=4)
        else:
            hb = _attention(qn, kn, vbt, cache, kv_shared=True, tq=t, n_tiles=W_B // LANES)
        y = _outproj([ha.reshape(xf.shape[0], -1, W_A), hb.reshape(xf.shape[0], -1, W_B)],
                     z.reshape(xf.shape[0], -1, W_A + W_B), w_o, xf, g_post, mod,
                     row0=row0, per_batch=per_batch, tm=tm)
        knt = outs[8].reshape(bsz, HKV_B * DH_B, t) if emit else None
        return y.reshape(bsz, t, D_MODEL), res[1:], knt, vbt

    yp, st, knt_p, vbt_p = stream(xp, 0, False, None, None, None, True)
    n0 = st_n.transpose(0, 2, 1, 3)
    m0 = jnp.broadcast_to(st_m.transpose(0, 2, 1)[..., None], n0.shape)
    cache = (_tok_major(ck).astype(BF16), _feat_major(cv).astype(BF16))
    ys, _, _, _ = stream(xs, 1, True, (st_c, n0, m0), cache, rope_tabs, False)
    c_out, n_out, m_out = st
    new_n = n_out.transpose(0, 2, 1, 3)[:, None]
    new_m = m_out[..., 0].transpose(0, 2, 1)[:, None]
    return (yp, ys, c_out, new_n, new_m, _head_major_from_t(knt_p, HKV_B), _head_major_from_t(vbt_p, HKV_B))


def _odd_layer(xp, xs, mod, g_pre, g_post, w_in, rpb, w_out, ck, cv):
    w_q, w_k, w_v, w_z = _relayout(
        w_in, [[(0, i * W_C, W_C, QSCALE if i == 0 else 1.0)] for i in range(4)], [W_C] * 4,
        axis=1, tile=256)
    (w_o,) = _relayout(w_out, [[(0, 0, W_C, 1.0)]], [W_C], axis=0, tile=256)
    tm = 512
    bsz, t, _ = xp.shape
    xf = xp.reshape(1, bsz * t, D_MODEL)
    q, k, z, kt, vt = _proj(xf, g_pre, mod, [(w_q, BF16), (w_k, BF16), (w_z, BF16)],
                            [(w_k, F32, t), (w_v, F32, t)], None, None, row0=0, per_batch=False, tm=tm)
    kt = kt.reshape(bsz, W_C, t)
    vt = vt.reshape(bsz, W_C, t)
    o = _attention(q.reshape(bsz, t, W_C), k.reshape(bsz, t, W_C), vt, None,
                   kv_shared=False, tq=t, n_tiles=W_C // LANES)
    yp = _outproj([o.reshape(1, bsz * t, W_C)], z, w_o, xf, g_post, mod, row0=0, per_batch=False, tm=tm)
    yp = yp.reshape(bsz, t, D_MODEL)
    q, k, z, vts = _proj(xs, g_pre, mod, [(w_q, BF16), (w_k, BF16), (w_z, BF16)],
                         [(w_v, BF16, NA_ROWS * GRID_W)], None, None, row0=1, per_batch=True, tm=tm)
    o = _na_attention(q, k, vts, _tok_major(ck).astype(BF16), _feat_major(cv).astype(BF16), rpb)
    ys = _outproj([o], z, w_o, xs, g_post, mod, row0=1, per_batch=True, tm=tm)
    return yp, ys, _head_major_from_t(kt, H_C), _head_major_from_t(vt, H_C)


def kernel(x_prompt, x_sample, state_mlstm_C, state_mlstm_n, state_mlstm_m, cache_gqa_k, cache_gqa_v,
           cache_na_k, cache_na_v, c, c_ctx, w_mod, b_mod, g_pre, g_post, w_in_ab, b_gates_ab, g_hnorm_a,
           g_qnorm_b, g_knorm_b, w_out_ab, w_in_c, rpb_c, w_out_c):
    depth = w_mod.shape[0]
    assert depth == 2 and c.shape[0] == 2
    cvec = jnp.concatenate([c_ctx[None], c, jnp.zeros((8 - 1 - c.shape[0], D_MODEL), F32)], axis=0)
    mod = _modulation(cvec, w_mod, b_mod)
    rope_tabs = _rope_tables(x_sample.shape[1])
    xp, xs, c_out, n_out, m_out, gk, gv = _even_layer(
        x_prompt, x_sample, mod[0], g_pre[0], g_post[0], w_in_ab[0], b_gates_ab[0], g_hnorm_a[0],
        g_qnorm_b[0], g_knorm_b[0], w_out_ab[0], state_mlstm_C[:, 0], state_mlstm_n[:, 0],
        state_mlstm_m[:, 0], cache_gqa_k[:, 0], cache_gqa_v[:, 0], rope_tabs)
    xp, xs, nk, nv = _odd_layer(xp, xs, mod[1], g_pre[1], g_post[1], w_in_c[0], rpb_c[0], w_out_c[0],
                                cache_na_k[:, 0], cache_na_v[:, 0])
    return (xp, xs, c_out, n_out, m_out, gk, gv, nk, nv)
```

```python
import functools

import jax
import jax.numpy as jnp
import numpy as np
from jax import lax
from jax.experimental import pallas as pl
from jax.experimental.pallas import tpu as pltpu

F32 = jnp.float32
BF16 = jnp.bfloat16

D_MODEL = 1024
GRID_W = 64
EPS = 1e-6
H_A = 4
DH_A = 128
W_A = H_A * DH_A
CHUNK_A = 128
HQ_B = 8
HKV_B = 2
DH_B = 64
W_B = HQ_B * DH_B
ROPE_THETA = 10000.0
H_C = 16
DH_C = 64
W_C = H_C * DH_C
WIN_R = 8
WIN_C = 16

LANES = 128
NA_ROWS = 4
NA_TILES = 2
NA_RBS = 8
NA_BLKS = -(-(NA_ROWS + WIN_R - 1) // NA_ROWS)
NA_WIN = NA_BLKS * NA_ROWS
assert WIN_R // 2 == NA_ROWS
BF16_ROWS = 16
PROJ_SUBTILES = 2
KEY_CHUNK = 256
QK_AHEAD = 3
LOG2E = 1.4426950408889634
QSCALE = DH_B ** -0.5 * LOG2E
assert DH_B == DH_C
MASKED = 1e30
VMEM_LIMIT = 56 * 1024 * 1024


def _cparams(sem):
    return pltpu.CompilerParams(dimension_semantics=sem, vmem_limit_bytes=VMEM_LIMIT)


def _silu(x):
    return x / (1.0 + jnp.exp(-x))


def _sigmoid(x):
    return 1.0 / (1.0 + jnp.exp(-x))


def _log_sigmoid(x):
    return jnp.minimum(x, 0.0) - jnp.log1p(jnp.exp(-jnp.abs(x)))


def _dot_nt(a, b):
    return lax.dot_general(a, b, (((1,), (1,)), ((), ())), preferred_element_type=F32)


def _mod_kernel(c_ref, w_ref, b_ref, o_ref):
    s = _silu(c_ref[...])
    o_ref[0] = jnp.dot(s, w_ref[0], preferred_element_type=F32,
                       precision=lax.Precision.HIGHEST) + b_ref[0]


def _modulation(cvec, w_mod, b_mod):
    depth, d, n = w_mod.shape
    tn = n // 4
    return pl.pallas_call(
        _mod_kernel,
        grid=(depth, n // tn),
        in_specs=[pl.BlockSpec((8, d), lambda l, j: (0, 0)),
                  pl.BlockSpec((1, d, tn), lambda l, j: (l, 0, j)),
                  pl.BlockSpec((1, 1, tn), lambda l, j: (l, 0, j))],
        out_specs=pl.BlockSpec((1, 8, tn), lambda l, j: (l, 0, j)),
        out_shape=jax.ShapeDtypeStruct((depth, 8, n), F32),
        compiler_params=_cparams(("arbitrary", "arbitrary")),
        name="modulation",
    )(cvec, w_mod, b_mod.reshape(depth, 1, n))


def _head_norm(x, g):
    lo = lax.broadcasted_iota(jnp.int32, x.shape, 1) < DH_B
    x2 = x * x
    s_lo = jnp.sum(jnp.where(lo, x2, 0.0), axis=-1, keepdims=True)
    s_hi = jnp.sum(jnp.where(lo, 0.0, x2), axis=-1, keepdims=True)
    ms = jnp.where(lo, s_lo, s_hi) * (1.0 / DH_B)
    return x * lax.rsqrt(ms + EPS) * g


def _rope(x, cos, sin):
    quarter = DH_B // 4
    first = (lax.broadcasted_iota(jnp.int32, x.shape, 1) % (2 * quarter)) < quarter
    partner = jnp.where(first, pltpu.roll(x, LANES - quarter, 1), pltpu.roll(x, quarter, 1))
    return x * cos + partner * sin


def _store_t(ref, val, t_block, tok0):
    n_tok = val.shape[1]
    if t_block is None:
        ref[0, :, tok0:tok0 + n_tok] = val.astype(ref.dtype)
    else:
        for i in range(n_tok // t_block):
            ref[0, tok0 // t_block + i] = val[:, i * t_block:(i + 1) * t_block].astype(ref.dtype)


def _proj_kernel(*refs, n_seg, t_sizes, t_blocks, gate_rows, gqa, row0, per_batch):
    x_ref, g_ref, mod_ref = refs[:3]
    pos = 3
    w_refs = refs[pos:pos + n_seg]
    pos += n_seg
    if t_sizes:
        wt_ref = refs[pos]
        pos += 1
    if gate_rows:
        bgt_ref = refs[pos]
        pos += 1
    if gqa is not None:
        rope, knt_block = gqa
        gq_ref, gk_ref = refs[pos:pos + 2]
        pos += 2
        if rope:
            cos_ref, sin_ref = refs[pos:pos + 2]
            pos += 2
    n_plain = n_seg - (1 if gqa is not None else 0)
    o_refs = refs[pos:pos + n_plain]
    pos += n_plain
    ot_refs = refs[pos:pos + len(t_blocks)]
    pos += len(t_blocks)
    if gate_rows:
        gto_ref = refs[pos]
        pos += 1
    if gqa is not None:
        q_out, k_out = refs[pos:pos + 2]
        pos += 2
        if knt_block is not None:
            knt_out = refs[pos]
            pos += 1
    t_offs = [0]
    for n in t_sizes:
        t_offs.append(t_offs[-1] + n)

    d = x_ref.shape[-1]
    tm = x_ref.shape[1]
    row = row0 + (pl.program_id(0) if per_batch else 0)
    shift = mod_ref[pl.ds(row, 1), 0:d]
    scale = mod_ref[pl.ds(row, 1), d:2 * d]
    sub = tm // PROJ_SUBTILES
    for s in range(PROJ_SUBTILES):
        rows = slice(s * sub, (s + 1) * sub)
        x = x_ref[0, rows, :]
        r = lax.rsqrt(jnp.mean(x * x, axis=-1, keepdims=True) + EPS)
        h = (x * r * g_ref[...]) * (1.0 + scale) + shift
        hb = h.astype(BF16)
        if gqa is not None:
            res = jnp.dot(hb, w_refs[-1][...], preferred_element_type=F32)
        for w_ref, o_ref in zip(w_refs[:n_plain], o_refs):
            o_ref[0, rows, :] = jnp.dot(hb, w_ref[...], preferred_element_type=F32).astype(o_ref.dtype)
        if t_sizes:
            res_t = _dot_nt(wt_ref[...], hb)
            for ot_ref, tb, off, end in zip(ot_refs, t_blocks, t_offs, t_offs[1:]):
                _store_t(ot_ref, res_t[off:end], tb, s * sub)
            if gate_rows:
                off = t_offs[len(t_blocks)]
                gto_ref[0, :, rows] = res_t[off:off + gate_rows] + bgt_ref[...]
        if gqa is not None:
            nq = W_B // LANES
            for j in range(nq + 1):
                xn = _head_norm(res[:, j * LANES:(j + 1) * LANES], gq_ref[...] if j < nq else gk_ref[...])
                if j == nq and knt_block is not None:
                    _store_t(knt_out, xn.T, knt_block, s * sub)
                if rope:
                    xn = _rope(xn, cos_ref[rows, :], sin_ref[rows, :])
                if j < nq:
                    q_out[0, rows, j * LANES:(j + 1) * LANES] = (xn * QSCALE).astype(q_out.dtype)
                else:
                    k_out[0, rows, :] = xn.astype(k_out.dtype)


def _proj(x, g_pre, mod, segs, wt_all, tsegs, b_gates, gqa, *, row0, per_batch, tm):
    bsz, t, d = x.shape
    grid = (bsz, t // tm)
    const = lambda b, i: (0, 0)
    in_specs = [pl.BlockSpec((1, tm, d), lambda b, i: (b, i, 0)),
                pl.BlockSpec((1, d), const),
                pl.BlockSpec(mod.shape, const)]
    args = [x, g_pre.reshape(1, d), mod]
    plain_ws = [w for w, _ in segs] + ([gqa["w"]] if gqa is not None else [])
    for w in plain_ws + ([wt_all] if wt_all is not None else []):
        in_specs.append(pl.BlockSpec(w.shape, const))
        args.append(w)
    gate_rows = 0
    t_sizes = [n for n, _, _ in tsegs]
    if b_gates is not None:
        gate_rows = b_gates.shape[0]
        t_sizes.append(LANES)
        in_specs.append(pl.BlockSpec((gate_rows, 1), const))
        args.append(b_gates.reshape(gate_rows, 1))
    assert sum(t_sizes) == (0 if wt_all is None else wt_all.shape[0])
    if gqa is not None:
        in_specs += [pl.BlockSpec((1, LANES), const)] * 2
        args += [jnp.tile(gqa["g_q"], 2).reshape(1, LANES), jnp.tile(gqa["g_k"], 2).reshape(1, LANES)]
        if gqa["rope"] is not None:
            in_specs += [pl.BlockSpec((tm, LANES), lambda b, i: (i, 0))] * 2
            args += list(gqa["rope"])
    out_specs, out_shape = [], []
    for w, dt in segs:
        n = w.shape[1]
        out_specs.append(pl.BlockSpec((1, tm, n), lambda b, i: (b, i, 0)))
        out_shape.append(jax.ShapeDtypeStruct((bsz, t, n), dt))

    def add_t_out(n, dt, tb):
        if tb is None:
            out_specs.append(pl.BlockSpec((1, n, tm), lambda b, i: (b, 0, i)))
            out_shape.append(jax.ShapeDtypeStruct((bsz, n, t), dt))
        else:
            out_specs.append(pl.BlockSpec((1, tm // tb, n, tb), lambda b, i: (b, i, 0, 0)))
            out_shape.append(jax.ShapeDtypeStruct((bsz, t // tb, n, tb), dt))

    for n, dt, tb in tsegs:
        add_t_out(n, dt, tb)
    if b_gates is not None:
        add_t_out(gate_rows, F32, None)
    gqa_static = None
    if gqa is not None:
        out_specs += [pl.BlockSpec((1, tm, W_B), lambda b, i: (b, i, 0)),
                      pl.BlockSpec((1, tm, LANES), lambda b, i: (b, i, 0))]
        out_shape += [jax.ShapeDtypeStruct((bsz, t, W_B), BF16), jax.ShapeDtypeStruct((bsz, t, LANES), BF16)]
        if gqa["knt_block"] is not None:
            add_t_out(LANES, F32, gqa["knt_block"])
        gqa_static = (gqa["rope"] is not None, gqa["knt_block"])
    kern = functools.partial(_proj_kernel, n_seg=len(plain_ws), t_sizes=tuple(t_sizes),
                             t_blocks=tuple(tb for _, _, tb in tsegs), gate_rows=gate_rows, gqa=gqa_static,
                             row0=row0, per_batch=per_batch)
    return pl.pallas_call(
        kern, grid=grid, in_specs=in_specs, out_specs=out_specs, out_shape=out_shape,
        compiler_params=_cparams(("arbitrary", "arbitrary")), name="in_proj",
    )(*args)


def _outproj_kernel(*refs, n_in, row0, per_batch):
    a_refs = refs[:n_in]
    z_ref, w_ref, x_ref, gp_ref, mod_ref, o_ref = refs[n_in:n_in + 6]
    d = x_ref.shape[-1]
    row = row0 + (pl.program_id(0) if per_batch else 0)
    gate = mod_ref[pl.ds(row, 1), 2 * d:3 * d]
    z = z_ref[0].astype(F32)
    sz = _silu(z)
    acc = None
    off = 0
    for a_ref in a_refs:
        kk = a_ref.shape[-1]
        y = (a_ref[0].astype(F32) * sz[:, off:off + kk]).astype(BF16)
        part = jnp.dot(y, w_ref[off:off + kk, :], preferred_element_type=F32)
        acc = part if acc is None else acc + part
        off += kk
    r = lax.rsqrt(jnp.mean(acc * acc, axis=-1, keepdims=True) + EPS)
    o_ref[0] = x_ref[0] + gate * (acc * r * gp_ref[...])


def _outproj(parts, z, w_out, x, g_post, mod, *, row0, per_batch, tm):
    bsz, t, d = x.shape
    grid = (bsz, t // tm)
    in_specs, args = [], []
    for a in parts:
        in_specs.append(pl.BlockSpec((1, tm, a.shape[-1]), lambda b, i: (b, i, 0)))
        args.append(a)
    in_specs += [pl.BlockSpec((1, tm, z.shape[-1]), lambda b, i: (b, i, 0)),
                 pl.BlockSpec(w_out.shape, lambda b, i: (0, 0)),
                 pl.BlockSpec((1, tm, d), lambda b, i: (b, i, 0)),
                 pl.BlockSpec((1, d), lambda b, i: (0, 0)),
                 pl.BlockSpec(mod.shape, lambda b, i: (0, 0))]
    args += [z, w_out, x, g_post.reshape(1, d), mod]
    kern = functools.partial(_outproj_kernel, n_in=len(parts), row0=row0, per_batch=per_batch)
    return pl.pallas_call(
        kern, grid=grid, in_specs=in_specs,
        out_specs=pl.BlockSpec((1, tm, d), lambda b, i: (b, i, 0)),
        out_shape=jax.ShapeDtypeStruct((bsz, t, d), F32),
        compiler_params=_cparams(("arbitrary", "arbitrary")), name="out_proj",
    )(*args)


def _split3_bf16(x):
    hi = x.astype(BF16)
    r1 = x - hi.astype(F32)
    mid = r1.astype(BF16)
    lo = (r1 - mid.astype(F32)).astype(BF16)
    return hi, mid, lo


def _mlstm_gate_rows(ig_f, fg_f, ig_b, fg_b):
    rr, ll = fg_f.shape
    lf = _log_sigmoid(jnp.concatenate([fg_f, fg_b], axis=0))
    pieces = jnp.concatenate(_split3_bf16(lf), axis=0)
    u = lax.broadcasted_iota(jnp.int32, (ll, ll), 0)
    t = lax.broadcasted_iota(jnp.int32, (ll, ll), 1)
    out = []
    for d, ig in enumerate((ig_f, ig_b)):
        tri = jnp.where((u >= t) if d else (u <= t), 1.0, 0.0).astype(BF16)
        y = jnp.dot(pieces, tri, preferred_element_type=F32)
        rows = slice(d * rr, (d + 1) * rr)
        b = y[0:2 * rr][rows] + y[2 * rr:4 * rr][rows] + y[4 * rr:6 * rr][rows]
        c = ig - b
        cmax = jnp.broadcast_to(jnp.max(c, axis=-1, keepdims=True), c.shape)
        tot = jnp.broadcast_to(jnp.sum(lf[rows], axis=-1, keepdims=True), c.shape)
        out.append((c, b, cmax, tot))
    return out


def _mlstm_kernel(*refs, nc, hps, unroll, has_init, emit_state):
    q_ref, k_ref, vt_ref, ot_ref, gtt_ref, ghn_ref = refs[:6]
    pos = 6
    if has_init:
        c0_ref, n0_ref, m0_ref = refs[pos:pos + 3]
        pos += 3
    ha_ref = refs[pos]
    pos += 1
    if emit_state:
        cout_ref, nout_ref, mout_ref = refs[pos:pos + 3]
        pos += 3
    hs_sc, gate_sc, c_sc, n_sc, m_sc = refs[pos:pos + 5]
    ll, dh = CHUNK_A, DH_A
    kscale = dh ** -0.5
    hd0 = pl.program_id(1) * hps
    chains = [(i, d) for i in range(hps) for d in range(2)]

    for i, d in chains:
        if has_init:
            c_sc[i, d] = c0_ref[0, d, i]
            n_sc[i, d] = n0_ref[0, i, d:d + 1, :]
            m_sc[i, d] = m0_ref[0, i, d:d + 1, :]
        else:
            c_sc[i, d] = jnp.zeros((dh, dh), F32)
            n_sc[i, d] = jnp.zeros((1, dh), F32)
            m_sc[i, d] = jnp.zeros((1, dh), F32)

    def gate_rows(col0):
        return gtt_ref[0, pl.ds(pl.multiple_of((col0 + hd0) * nc, 8), hps * nc), :]

    gates = _mlstm_gate_rows(gate_rows(0), gate_rows(H_A), gate_rows(2 * H_A), gate_rows(3 * H_A))
    for d in range(2):
        for kind in range(4):
            gate_sc[d, kind] = gates[d][kind]

    ghn_t = [jnp.broadcast_to(ghn_ref[:, i * dh:(i + 1) * dh], (ll, dh)).T for i in range(hps)]
    si = lax.broadcasted_iota(jnp.int32, (ll, ll), 0)
    ti = lax.broadcasted_iota(jnp.int32, (ll, ll), 1)

    def run_trip(j, first_touch):
        jobs = []
        for u in range(unroll):
            step = j * unroll + u
            for i, d in chains:
                jobs.append((i, d, step if d == 0 else nc - 1 - step))
        state = {ch: [c_sc[ch], n_sc[ch], m_sc[ch]] for ch in chains}
        hcols = [slice(i * dh, (i + 1) * dh) for i in range(hps)]

        def tok_rows(cidx):
            return pl.ds(pl.multiple_of(cidx * ll, ll), ll)

        rows = []
        for i, d, cidx in jobs:
            c_r, b_r, cmax_r, tot_r = (gate_sc[d, kind, pl.ds(i * nc + cidx, 1), :] for kind in range(4))
            m_st = state[(i, d)][2]
            m_c = jnp.maximum(m_st, cmax_r)
            state[(i, d)][2] = tot_r + m_c
            rows.append((c_r, b_r, m_st, jnp.exp(m_st - m_c), jnp.exp(c_r - m_c) * kscale))
        start = []
        for (i, d, cidx), (_, _, _, a_st, wk) in zip(jobs, rows):
            k = k_ref[0, tok_rows(cidx), hcols[i]]
            vt = vt_ref[0, cidx, hcols[i], :]
            vw = jnp.concatenate([vt.astype(F32) * wk, jnp.broadcast_to(wk, (BF16_ROWS, ll))], axis=0)
            upd = jnp.dot(vw.astype(BF16), k, preferred_element_type=F32)
            c_st, n_st, _ = state[(i, d)]
            start.append((c_st, n_st))
            state[(i, d)][0] = a_st * c_st + upd[:dh]
            state[(i, d)][1] = a_st * n_st + upd[dh:dh + 1]
        prods = []
        for (i, d, cidx), (c_st, n_st) in zip(jobs, start):
            q = q_ref[0, tok_rows(cidx), hcols[i]]
            k = k_ref[0, tok_rows(cidx), hcols[i]]
            n16 = jnp.broadcast_to(n_st.astype(BF16), (BF16_ROWS, dh))
            prods.append(_dot_nt(jnp.concatenate([k, c_st.astype(BF16), n16], axis=0), q))
        for (i, d, cidx), (c_r, b_r, m_st, _, _), r in zip(jobs, rows, prods):
            allowed = (si >= ti) if d else (si <= ti)
            cb = jnp.where(allowed, jnp.broadcast_to(c_r, (ll, ll)).T, -jnp.inf)
            big_m = jnp.maximum(m_st, jnp.max(cb, axis=0, keepdims=True))
            p = jnp.exp(cb - (big_m - float(np.log(kscale)))) * r[:ll]
            w_inter = jnp.exp(m_st - big_m)
            vt = vt_ref[0, cidx, hcols[i], :]
            num = jnp.dot(vt, p.astype(BF16), preferred_element_type=F32) - r[ll:ll + dh] * (-w_inter)
            den = w_inter * r[ll + dh:ll + dh + 1] + jnp.sum(p, axis=0, keepdims=True)
            h_t = num / jnp.maximum(jnp.abs(den), jnp.exp(-(b_r + big_m)))
            if first_touch:
                hs_sc[i, cidx] = h_t
            else:
                hsum = h_t + hs_sc[i, cidx]
                rn = lax.rsqrt(jnp.mean(hsum * hsum, axis=0, keepdims=True) + EPS)
                out_t = (hsum * rn * ghn_t[i]) * _sigmoid(ot_ref[0, cidx, hcols[i], :].astype(F32))
                ha_ref[0, tok_rows(cidx), hcols[i]] = out_t.T.astype(ha_ref.dtype)
        for ch in chains:
            c_sc[ch], n_sc[ch], m_sc[ch] = state[ch]

    def make_body(first_touch):
        def body(j, carry):
            run_trip(j, first_touch)
            return carry
        return body

    trips = nc // unroll
    lax.fori_loop(0, trips // 2, make_body(True), 0)
    lax.fori_loop(trips // 2, trips, make_body(False), 0)

    if emit_state:
        for i, d in chains:
            cout_ref[0, 0, d, i] = c_sc[i, d]
            nout_ref[0, i, d:d + 1, :] = n_sc[i, d]
            mout_ref[0, i, d:d + 1, :] = m_sc[i, d]


def _mlstm(qk, vt, ot, gates_t, g_hn, init, *, emit_state):
    bsz, t, _ = qk.shape
    nc = t // CHUNK_A
    hps, unroll = (H_A, 1) if nc < 8 else (1, 4)
    assert (nc // 2) % unroll == 0 and nc % 2 == 0
    gtt = gates_t.reshape(bsz, 4 * H_A * nc, CHUNK_A)
    wh = hps * DH_A

    def tblk():
        return pl.BlockSpec((1, nc, wh, CHUNK_A), lambda b, g: (b, 0, g, 0))

    in_specs = [pl.BlockSpec((1, t, wh), lambda b, g: (b, 0, g)),
                pl.BlockSpec((1, t, wh), lambda b, g: (b, 0, H_A // hps + g)),
                tblk(), tblk(),
                pl.BlockSpec((1, 4 * H_A * nc, CHUNK_A), lambda b, g: (b, 0, 0)),
                pl.BlockSpec((1, wh), lambda b, g: (0, g))]
    args = [qk, qk, vt, ot, gtt, g_hn.reshape(1, W_A)]
    if init is not None:
        c0, n0, m0 = init
        in_specs += [pl.BlockSpec((1, 2, hps, DH_A, DH_A), lambda b, g: (b, 0, g, 0, 0)),
                     pl.BlockSpec((1, hps, 2, DH_A), lambda b, g: (b, g, 0, 0)),
                     pl.BlockSpec((1, hps, 2, DH_A), lambda b, g: (b, g, 0, 0))]
        args += [c0, n0, m0]
    out_specs = [pl.BlockSpec((1, t, wh), lambda b, g: (b, 0, g))]
    out_shape = [jax.ShapeDtypeStruct((bsz, t, W_A), BF16)]
    if emit_state:
        out_specs += [pl.BlockSpec((1, 1, 2, hps, DH_A, DH_A), lambda b, g: (b, 0, 0, g, 0, 0)),
                      pl.BlockSpec((1, hps, 2, DH_A), lambda b, g: (b, g, 0, 0)),
                      pl.BlockSpec((1, hps, 2, DH_A), lambda b, g: (b, g, 0, 0))]
        out_shape += [jax.ShapeDtypeStruct((bsz, 1, 2, H_A, DH_A, DH_A), F32),
                      jax.ShapeDtypeStruct((bsz, H_A, 2, DH_A), F32),
                      jax.ShapeDtypeStruct((bsz, H_A, 2, DH_A), F32)]
    kern = functools.partial(_mlstm_kernel, nc=nc, hps=hps, unroll=unroll, has_init=init is not None,
                             emit_state=emit_state)
    return pl.pallas_call(
        kern, grid=(bsz, H_A // hps), in_specs=in_specs, out_specs=out_specs, out_shape=out_shape,
        scratch_shapes=[pltpu.VMEM((hps, nc, DH_A, CHUNK_A), F32), pltpu.VMEM((2, 4, hps * nc, CHUNK_A), F32),
                        pltpu.VMEM((hps, 2, DH_A, DH_A), F32), pltpu.VMEM((hps, 2, 1, DH_A), F32),
                        pltpu.VMEM((hps, 2, 1, DH_A), F32)],
        compiler_params=_cparams(("arbitrary", "arbitrary")), name="mlstm",
    )(*args)


def _rope_tables(n_tok):
    t = np.arange(n_tok)
    row = (t // GRID_W).astype(np.float32)
    colp = (t % GRID_W).astype(np.float32)
    quarter = DH_B // 4
    freqs = (np.float32(ROPE_THETA) ** (-np.arange(quarter, dtype=np.float32) / np.float32(quarter))).astype(np.float32)
    ar = row[:, None] * freqs
    ac = colp[:, None] * freqs
    cos = np.concatenate([np.cos(ar), np.cos(ar), np.cos(ac), np.cos(ac)], axis=-1)
    sin = np.concatenate([-np.sin(ar), np.sin(ar), -np.sin(ac), np.sin(ac)], axis=-1)
    return jnp.asarray(np.tile(cos, (1, 2)), F32), jnp.asarray(np.tile(sin, (1, 2)), F32)


def _split_heads_q(q):
    lo = lax.broadcasted_iota(jnp.int32, q.shape, 1) < (LANES // 2)
    zero = jnp.zeros_like(q)
    return jnp.concatenate([jnp.where(lo, q, zero), jnp.where(lo, zero, q)], axis=0)


def _merge_heads_o(o_t, tq):
    half = LANES // 2
    return jnp.concatenate([o_t[:half, :tq], o_t[half:, tq:]], axis=0).T


def _with_ones_rows(vt):
    return jnp.concatenate([vt, jnp.ones((BF16_ROWS, vt.shape[1]), BF16)], axis=0)


def _online_softmax_pv_t(problems):
    seq = [(pi, ci) for pi, (_, chunks, _) in enumerate(problems) for ci in range(len(chunks))]
    scores = {}

    def issue(t):
        pi, ci = seq[t]
        qm, chunks, _ = problems[pi]
        scores[(pi, ci)] = _dot_nt(chunks[ci][0], qm)

    for t in range(min(QK_AHEAD, len(seq))):
        issue(t)
    m = acc = None
    for t, (pi, ci) in enumerate(seq):
        if t + QK_AHEAD < len(seq):
            issue(t + QK_AHEAD)
        _, chunks, emit = problems[pi]
        _, vt, penalty = chunks[ci]
        s = scores.pop((pi, ci))
        if penalty is not None:
            s = s - penalty
        mc = jnp.max(s, axis=0, keepdims=True)
        m_new = mc if ci == 0 else jnp.maximum(m, mc)
        p = jnp.exp2(s - m_new).astype(BF16)
        part = jnp.dot(_with_ones_rows(vt), p, preferred_element_type=F32)
        acc = part if ci == 0 else jnp.exp2(m - m_new) * acc + part
        m = m_new
        if ci == len(chunks) - 1:
            dv = vt.shape[0]
            emit(acc[:dv] / acc[dv:dv + 1])


def _attn_kernel(*refs, n_tiles, kv_shared, has_cache):
    q_ref, k_ref, vt_ref = refs[:3]
    pos = 3
    if has_cache:
        kc_ref, vtc_ref = refs[pos:pos + 2]
        pos += 2
    o_ref = refs[pos]
    tq = q_ref.shape[1]
    s_len = k_ref.shape[1]
    sc = min(s_len, KEY_CHUNK)
    problems = []
    for j in range(n_tiles):
        kj = 0 if kv_shared else j
        kcols = slice(kj * LANES, (kj + 1) * LANES)
        qm = _split_heads_q(q_ref[0, :, j * LANES:(j + 1) * LANES])
        chunks = [(k_ref[0, c * sc:(c + 1) * sc, kcols].astype(BF16),
                   vt_ref[0, kcols, c * sc:(c + 1) * sc].astype(BF16), None) for c in range(s_len // sc)]
        if has_cache:
            chunks.append((kc_ref[0], vtc_ref[0], None))

        def emit(o_t, j=j):
            o_ref[0, :, j * LANES:(j + 1) * LANES] = _merge_heads_o(o_t, tq).astype(o_ref.dtype)

        problems.append((qm, chunks, emit))
    _online_softmax_pv_t(problems)


def _attention(q, k, vt, cache, *, kv_shared, tq, n_tiles):
    bsz, t, w = q.shape
    s = k.shape[1]
    wt = n_tiles * LANES
    if kv_shared:
        k_spec = pl.BlockSpec((1, s, LANES), lambda b, i, g: (b, 0, 0))
        vt_spec = pl.BlockSpec((1, LANES, s), lambda b, i, g: (b, 0, 0))
    else:
        k_spec = pl.BlockSpec((1, s, wt), lambda b, i, g: (b, 0, g))
        vt_spec = pl.BlockSpec((1, wt, s), lambda b, i, g: (b, g, 0))
    in_specs = [pl.BlockSpec((1, tq, wt), lambda b, i, g: (b, i, g)), k_spec, vt_spec]
    args = [q, k, vt]
    if cache is not None:
        p = cache[0].shape[1]
        in_specs += [pl.BlockSpec((1, p, LANES), lambda b, i, g: (b, 0, 0)),
                     pl.BlockSpec((1, LANES, p), lambda b, i, g: (b, 0, 0))]
        args += list(cache)
    kern = functools.partial(_attn_kernel, n_tiles=n_tiles, kv_shared=kv_shared,
                             has_cache=cache is not None)
    return pl.pallas_call(
        kern, grid=(bsz, t // tq, w // wt), in_specs=in_specs,
        out_specs=pl.BlockSpec((1, tq, wt), lambda b, i, g: (b, i, g)),
        out_shape=jax.ShapeDtypeStruct((bsz, t, w), BF16),
        compiler_params=_cparams(("arbitrary", "arbitrary", "arbitrary")), name="attention",
    )(*args)


def _na_bias_blocks(variant):
    out = {}
    for qr in range(NA_ROWS):
        for kr in range(NA_WIN):
            if variant == 0:
                dr = kr - qr if kr < WIN_R else None
            elif variant == 1:
                dr = kr - qr - WIN_R // 2 if qr <= kr < qr + WIN_R else None
            else:
                dr = kr - qr - (NA_WIN - NA_ROWS) if kr >= NA_WIN - WIN_R else None
            out[(qr, kr)] = None if dr is None else dr + WIN_R - 1
    return out


def _na_kernel(q_ref, k_ref, vt_ref, kc_ref, vtc_ref, bc_ref, o_ref, bias_sc, *, n_rb):
    w = GRID_W
    tq = NA_ROWS * w

    @pl.when(jnp.logical_and(pl.program_id(1) == 0, pl.program_id(2) == 0))
    def _():
        for variant in range(3):
            for (qr, kr), di in _na_bias_blocks(variant).items():
                for hh in range(2 * NA_TILES):
                    val = jnp.full((w, w), MASKED, F32) if di is None else bc_ref[hh, di]
                    bias_sc[variant, kr * w:(kr + 1) * w, hh * tq + qr * w:hh * tq + (qr + 1) * w] = val

    n_rows = n_rb * NA_ROWS
    problems = []
    for rr in range(NA_RBS):
        rb = pl.program_id(2) * NA_RBS + rr
        variant = jnp.where(rb == 0, 0, jnp.where(rb == n_rb - 1, 2, 1))
        ws = jnp.clip(rb * NA_ROWS - WIN_R // 2, 0, n_rows - NA_WIN)
        blk0 = ws // NA_ROWS
        start = ws * w
        qrows = slice(rr * tq, (rr + 1) * tq)
        for j in range(NA_TILES):
            cols = slice(j * LANES, (j + 1) * LANES)
            qcols = slice(j * 2 * tq, (j + 1) * 2 * tq)
            qm = _split_heads_q(q_ref[0, qrows, cols])
            chunks = [(k_ref[0, pl.ds(pl.multiple_of(start + i * tq, tq), tq), cols],
                       vt_ref[0, blk0 + i, cols, :],
                       bias_sc[variant, i * tq:(i + 1) * tq, qcols]) for i in range(NA_BLKS)]
            chunks.append((kc_ref[0, :, cols], vtc_ref[0, cols, :], None))

            def emit(o_t, qrows=qrows, cols=cols):
                o_ref[0, qrows, cols] = _merge_heads_o(o_t, tq).astype(o_ref.dtype)

            problems.append((qm, chunks, emit))
    _online_softmax_pv_t(problems)


def _na_bias_table(rpb):
    c = np.arange(GRID_W)
    cs = np.clip(c - WIN_C // 2, 0, GRID_W - WIN_C)
    ck = np.arange(GRID_W)
    valid = (ck[:, None] >= cs[None, :]) & (ck[:, None] < cs[None, :] + WIN_C)
    idx = np.clip(ck[:, None] - c[None, :] + WIN_C - 1, 0, 2 * WIN_C - 2)
    onehot = (idx[..., None] == np.arange(2 * WIN_C - 1)).astype(np.float32)
    tab = jnp.einsum('hrx,kcx->hrkc', rpb, jnp.asarray(onehot), precision=lax.Precision.HIGHEST)
    return jnp.where(jnp.asarray(valid), tab * -LOG2E, MASKED)


def _na_attention(q, k, vt, kc, vtc, rpb):
    bsz, t, w = q.shape
    p = kc.shape[1]
    tq = NA_ROWS * GRID_W
    n_rb = t // tq
    bc = _na_bias_table(rpb)
    wt = NA_TILES * LANES
    tqs = NA_RBS * tq
    return pl.pallas_call(
        functools.partial(_na_kernel, n_rb=n_rb),
        grid=(w // wt, bsz, n_rb // NA_RBS),
        in_specs=[pl.BlockSpec((1, tqs, wt), lambda j, b, r: (b, r, j)),
                  pl.BlockSpec((1, t, wt), lambda j, b, r: (b, 0, j)),
                  pl.BlockSpec((1, n_rb, wt, tq), lambda j, b, r: (b, 0, j, 0)),
                  pl.BlockSpec((1, p, wt), lambda j, b, r: (b, 0, j)),
                  pl.BlockSpec((1, wt, p), lambda j, b, r: (b, j, 0)),
                  pl.BlockSpec((2 * NA_TILES, 2 * WIN_R - 1, GRID_W, GRID_W), lambda j, b, r: (j, 0, 0, 0))],
        out_specs=pl.BlockSpec((1, tqs, wt), lambda j, b, r: (b, r, j)),
        out_shape=jax.ShapeDtypeStruct((bsz, t, w), BF16),
        scratch_shapes=[pltpu.VMEM((3, NA_WIN * GRID_W, NA_TILES * 2 * tq), F32)],
        compiler_params=_cparams(("arbitrary", "arbitrary", "arbitrary")), name="na_attention",
    )(q, k, vt, kc, vtc, bc)


def _relayout_kernel(w_ref, *o_refs, plans, transposed, axis):
    w = w_ref[...]
    tile = w.shape[1 - axis]
    for o_ref, pieces, tr in zip(o_refs, plans, transposed):
        vals = []
        for src, size, scale in pieces:
            if src is None:
                val = jnp.zeros((size, tile) if axis == 0 else (tile, size), F32)
            else:
                val = lax.slice_in_dim(w, src, src + size, axis=axis)
                if scale != 1.0:
                    val = val * scale
            vals.append(val)
        val = vals[0] if len(vals) == 1 else jnp.concatenate(vals, axis=axis)
        o_ref[...] = (val.T if tr else val).astype(o_ref.dtype)


def _relayout(w, plans, transposed, *, axis, tile):
    grid = (w.shape[1 - axis] // tile,)
    full = w.shape[1 - axis]

    def spec(extent, tr):
        along_rows = (axis == 0) != tr
        shape = (extent, tile) if along_rows else (tile, extent)
        return pl.BlockSpec(shape, (lambda i: (0, i)) if along_rows else (lambda i: (i, 0)))

    def shape(extent, tr):
        return (extent, full) if (axis == 0) != tr else (full, extent)

    sizes = [sum(size for _, size, _ in pieces) for pieces in plans]
    return pl.pallas_call(
        functools.partial(_relayout_kernel, plans=plans, transposed=tuple(transposed), axis=axis),
        grid=grid, in_specs=[spec(w.shape[axis], False)],
        out_specs=[spec(n, tr) for n, tr in zip(sizes, transposed)],
        out_shape=[jax.ShapeDtypeStruct(shape(n, tr), BF16) for n, tr in zip(sizes, transposed)],
        compiler_params=_cparams(("arbitrary",)), name="w_relayout",
    )(w)


_GQA_PERM = np.array([0, 4, 1, 5, 2, 6, 3, 7])


def _tok_major(cache):
    b, h, p, dh = cache.shape
    return cache.transpose(0, 2, 1, 3).reshape(b, p, h * dh)


def _feat_major(cache):
    b, h, p, dh = cache.shape
    return cache.transpose(0, 1, 3, 2).reshape(b, h * dh, p)


def _head_major_from_t(x_t, n_heads):
    b, w, t = x_t.shape
    return jnp.swapaxes(x_t.reshape(b, 1, n_heads, w // n_heads, t), -1, -2)


def _even_layer(xp, xs, mod, g_pre, g_post, w_in, b_gates, g_hn, g_q, g_k, w_out, st_c, st_n, st_m,
                ck, cv, rope_tabs):
    o_v, o_o = 2 * W_A, 3 * W_A
    o_g = 4 * W_A
    o_qb = o_g + 4 * H_A
    o_kb = o_qb + W_B
    o_vb = o_kb + HKV_B * DH_B
    o_z = o_vb + HKV_B * DH_B
    def heads(src0):
        return [(src0 + int(h) * DH_B, DH_B, 1.0) for h in _GQA_PERM]

    n_g = 4 * H_A
    w_qk, w_gqa, w_z, wt_all = _relayout(
        jnp.swapaxes(w_in, 0, 1),
        [[(0, o_v, 1.0)],
         heads(o_qb) + [(o_kb, HKV_B * DH_B, 1.0)],
         [(o_z, W_A, 1.0)] + heads(o_z + W_A),
         [(o_v, 2 * W_A, 1.0), (o_vb, HKV_B * DH_B, 1.0), (o_g, n_g, 1.0), (None, LANES - n_g, 1.0)]],
        [True, True, True, False], axis=0, tile=256)
    (w_o,) = _relayout(w_out, [[(0, W_A, 1.0)] + heads(W_A)], [False], axis=0, tile=256)

    def stream(x, row0, per_batch, init, cache, rope, emit):
        bsz, t, _ = x.shape
        tm = 512
        xf = x if per_batch else x.reshape(1, bsz * t, D_MODEL)
        n_vb = HKV_B * DH_B
        vb_seg = (n_vb, BF16, None) if per_batch else (n_vb, F32, t)
        gqa = dict(w=w_gqa, g_q=g_q, g_k=g_k, rope=rope, knt_block=t if emit else None)
        outs = _proj(xf, g_pre, mod, [(w_qk, BF16), (w_z, BF16)], wt_all,
                     [(W_A, BF16, CHUNK_A), (W_A, BF16, CHUNK_A), vb_seg], b_gates, gqa,
                     row0=row0, per_batch=per_batch, tm=tm)
        qk, z, vta, ota, vbt, gtt, qn, kn = outs[:8]
        if not per_batch:
            qk, z, qn, kn = (a.reshape(bsz, t, a.shape[-1]) for a in (qk, z, qn, kn))
            gtt = gtt.reshape(4 * H_A, bsz, t).transpose(1, 0, 2)
            vbt = vbt.reshape(bsz, HKV_B * DH_B, t)
            vta, ota = (a.reshape(bsz, t // CHUNK_A, W_A, CHUNK_A) for a in (vta, ota))
        res = _mlstm(qk, vta, ota, gtt, g_hn, init, emit_state=emit)
        ha = res[0]
        if per_batch:
            hb = _attention(qn, kn, vbt, cache, kv_shared=True, tq=512, n_tiles=4)
        else:
            hb = _attention(qn, kn, vbt, cache, kv_shared=True, tq=t, n_tiles=W_B // LANES)
        y = _outproj([ha.reshape(xf.shape[0], -1, W_A), hb.reshape(xf.shape[0], -1, W_B)],
                     z.reshape(xf.shape[0], -1, W_A + W_B), w_o, xf, g_post, mod,
                     row0=row0, per_batch=per_batch, tm=tm)
        knt = outs[8].reshape(bsz, HKV_B * DH_B, t) if emit else None
        return y.reshape(bsz, t, D_MODEL), res[1:], knt, vbt

    yp, st, knt_p, vbt_p = stream(xp, 0, False, None, None, None, True)
    n0 = st_n.transpose(0, 2, 1, 3)
    m0 = jnp.broadcast_to(st_m.transpose(0, 2, 1)[..., None], n0.shape)
    cache = (_tok_major(ck).astype(BF16), _feat_major(cv).astype(BF16))
    ys, _, _, _ = stream(xs, 1, True, (st_c, n0, m0), cache, rope_tabs, False)
    c_out, n_out, m_out = st
    new_n = n_out.transpose(0, 2, 1, 3)[:, None]
    new_m = m_out[..., 0].transpose(0, 2, 1)[:, None]
    return (yp, ys, c_out, new_n, new_m, _head_major_from_t(knt_p, HKV_B), _head_major_from_t(vbt_p, HKV_B))


def _odd_layer(xp, xs, mod, g_pre, g_post, w_in, rpb, w_out, ck, cv):
    w_q, w_k, w_z, wt_kv = _relayout(
        w_in, [[(0, W_C, QSCALE)], [(W_C, W_C, 1.0)], [(3 * W_C, W_C, 1.0)], [(W_C, 2 * W_C, 1.0)]],
        [False, False, False, True], axis=1, tile=256)
    (w_o,) = _relayout(w_out, [[(0, W_C, 1.0)]], [False], axis=0, tile=256)
    tm = 512
    bsz, t, _ = xp.shape
    xf = xp.reshape(1, bsz * t, D_MODEL)
    q, k, z, kt, vt = _proj(xf, g_pre, mod, [(w_q, BF16), (w_k, BF16), (w_z, BF16)], wt_kv,
                            [(W_C, F32, t), (W_C, F32, t)], None, None, row0=0, per_batch=False, tm=tm)
    kt = kt.reshape(bsz, W_C, t)
    vt = vt.reshape(bsz, W_C, t)
    o = _attention(q.reshape(bsz, t, W_C), k.reshape(bsz, t, W_C), vt, None,
                   kv_shared=False, tq=t, n_tiles=W_C // LANES)
    yp = _outproj([o.reshape(1, bsz * t, W_C)], z, w_o, xf, g_post, mod, row0=0, per_batch=False, tm=tm)
    yp = yp.reshape(bsz, t, D_MODEL)
    q, k, z, vts = _proj(xs, g_pre, mod, [(w_q, BF16), (w_k, BF16), (w_z, BF16)], wt_kv[W_C:],
                         [(W_C, BF16, NA_ROWS * GRID_W)], None, None, row0=1, per_batch=True, tm=tm)
    o = _na_attention(q, k, vts, _tok_major(ck).astype(BF16), _feat_major(cv).astype(BF16), rpb)
    ys = _outproj([o], z, w_o, xs, g_post, mod, row0=1, per_batch=True, tm=tm)
    return yp, ys, _head_major_from_t(kt, H_C), _head_major_from_t(vt, H_C)


def kernel(x_prompt, x_sample, state_mlstm_C, state_mlstm_n, state_mlstm_m, cache_gqa_k, cache_gqa_v,
           cache_na_k, cache_na_v, c, c_ctx, w_mod, b_mod, g_pre, g_post, w_in_ab, b_gates_ab, g_hnorm_a,
           g_qnorm_b, g_knorm_b, w_out_ab, w_in_c, rpb_c, w_out_c):
    depth = w_mod.shape[0]
    assert depth == 2 and c.shape[0] == 2
    cvec = jnp.concatenate([c_ctx[None], c, jnp.zeros((8 - 1 - c.shape[0], D_MODEL), F32)], axis=0)
    mod = _modulation(cvec, w_mod, b_mod)
    rope_tabs = _rope_tables(x_sample.shape[1])
    xp, xs, c_out, n_out, m_out, gk, gv = _even_layer(
        x_prompt, x_sample, mod[0], g_pre[0], g_post[0], w_in_ab[0], b_gates_ab[0], g_hnorm_a[0],
        g_qnorm_b[0], g_knorm_b[0], w_out_ab[0], state_mlstm_C[:, 0], state_mlstm_n[:, 0],
        state_mlstm_m[:, 0], cache_gqa_k[:, 0], cache_gqa_v[:, 0], rope_tabs)
    xp, xs, nk, nv = _odd_layer(xp, xs, mod[1], g_pre[1], g_post[1], w_in_c[0], rpb_c[0], w_out_c[0],
                                cache_na_k[:, 0], cache_na_v[:, 0])
    return (xp, xs, c_out, n_out, m_out, gk, gv, nk, nv)
```

```python
import functools

import jax
import jax.numpy as jnp
import numpy as np
from jax import lax
from jax.experimental import pallas as pl
from jax.experimental.pallas import tpu as pltpu

F32 = jnp.float32
BF16 = jnp.bfloat16

D_MODEL = 1024
GRID_W = 64
EPS = 1e-6
H_A = 4
DH_A = 128
W_A = H_A * DH_A
CHUNK_A = 128
HQ_B = 8
HKV_B = 2
DH_B = 64
W_B = HQ_B * DH_B
ROPE_THETA = 10000.0
H_C = 16
DH_C = 64
W_C = H_C * DH_C
WIN_R = 8
WIN_C = 16

LANES = 128
NA_ROWS = 4
NA_TILES = 2
NA_RBS = 8
NA_BLKS = -(-(NA_ROWS + WIN_R - 1) // NA_ROWS)
NA_WIN = NA_BLKS * NA_ROWS
assert WIN_R // 2 == NA_ROWS
BF16_ROWS = 16
PROJ_SUBTILES = 2
KEY_CHUNK = 256
QK_AHEAD = 3
LOG2E = 1.4426950408889634
QSCALE = DH_B ** -0.5 * LOG2E
assert DH_B == DH_C
MASKED = 1e30
VMEM_LIMIT = 56 * 1024 * 1024


def _cparams(sem):
    return pltpu.CompilerParams(dimension_semantics=sem, vmem_limit_bytes=VMEM_LIMIT)


def _silu(x):
    return x / (1.0 + jnp.exp(-x))


def _sigmoid(x):
    return 1.0 / (1.0 + jnp.exp(-x))


def _log_sigmoid(x):
    return jnp.minimum(x, 0.0) - jnp.log1p(jnp.exp(-jnp.abs(x)))


def _dot_nt(a, b):
    return lax.dot_general(a, b, (((1,), (1,)), ((), ())), preferred_element_type=F32)


def _mod_kernel(c_ref, w_ref, b_ref, o_ref):
    s = _silu(c_ref[...])
    o_ref[0] = jnp.dot(s, w_ref[0], preferred_element_type=F32,
                       precision=lax.Precision.HIGHEST) + b_ref[0]


def _modulation(cvec, w_mod, b_mod):
    depth, d, n = w_mod.shape
    tn = n // 4
    return pl.pallas_call(
        _mod_kernel,
        grid=(depth, n // tn),
        in_specs=[pl.BlockSpec((8, d), lambda l, j: (0, 0)),
                  pl.BlockSpec((1, d, tn), lambda l, j: (l, 0, j)),
                  pl.BlockSpec((1, 1, tn), lambda l, j: (l, 0, j))],
        out_specs=pl.BlockSpec((1, 8, tn), lambda l, j: (l, 0, j)),
        out_shape=jax.ShapeDtypeStruct((depth, 8, n), F32),
        compiler_params=_cparams(("arbitrary", "arbitrary")),
        name="modulation",
    )(cvec, w_mod, b_mod.reshape(depth, 1, n))


def _head_norm(x, g):
    lo = lax.broadcasted_iota(jnp.int32, x.shape, 1) < DH_B
    x2 = x * x
    s_lo = jnp.sum(jnp.where(lo, x2, 0.0), axis=-1, keepdims=True)
    s_hi = jnp.sum(jnp.where(lo, 0.0, x2), axis=-1, keepdims=True)
    ms = jnp.where(lo, s_lo, s_hi) * (1.0 / DH_B)
    return x * lax.rsqrt(ms + EPS) * g


def _rope(x, cos, sin):
    quarter = DH_B // 4
    first = (lax.broadcasted_iota(jnp.int32, x.shape, 1) % (2 * quarter)) < quarter
    partner = jnp.where(first, pltpu.roll(x, LANES - quarter, 1), pltpu.roll(x, quarter, 1))
    return x * cos + partner * sin


def _store_t(ref, val, t_block, tok0):
    n_tok = val.shape[1]
    if t_block is None:
        ref[0, :, tok0:tok0 + n_tok] = val.astype(ref.dtype)
    else:
        for i in range(n_tok // t_block):
            ref[0, tok0 // t_block + i] = val[:, i * t_block:(i + 1) * t_block].astype(ref.dtype)


def _proj_kernel(*refs, n_pre, n_seg, t_sizes, t_blocks, gate_rows, gqa, row0, per_batch):
    pos = 0
    if n_pre:
        pre_a = refs[:n_pre]
        pre_z, pre_w, x_ref, pre_gp, pre_mod = refs[n_pre:n_pre + 5]
        pos = n_pre + 5
    else:
        x_ref = refs[0]
        pos = 1
    g_ref, mod_ref = refs[pos:pos + 2]
    pos += 2
    w_refs = refs[pos:pos + n_seg]
    pos += n_seg
    if t_sizes:
        wt_ref = refs[pos]
        pos += 1
    if gate_rows:
        bgt_ref = refs[pos]
        pos += 1
    if gqa is not None:
        rope, knt_block = gqa
        gq_ref, gk_ref = refs[pos:pos + 2]
        pos += 2
        if rope:
            cos_ref, sin_ref = refs[pos:pos + 2]
            pos += 2
    n_plain = n_seg - (1 if gqa is not None else 0)
    if n_pre:
        xnew_ref = refs[pos]
        pos += 1
    o_refs = refs[pos:pos + n_plain]
    pos += n_plain
    ot_refs = refs[pos:pos + len(t_blocks)]
    pos += len(t_blocks)
    if gate_rows:
        gto_ref = refs[pos]
        pos += 1
    if gqa is not None:
        q_out, k_out = refs[pos:pos + 2]
        pos += 2
        if knt_block is not None:
            knt_out = refs[pos]
            pos += 1
    t_offs = [0]
    for n in t_sizes:
        t_offs.append(t_offs[-1] + n)

    d = x_ref.shape[-1]
    tm = x_ref.shape[1]
    row = row0 + (pl.program_id(0) if per_batch else 0)
    shift = mod_ref[pl.ds(row, 1), 0:d]
    scale = mod_ref[pl.ds(row, 1), d:2 * d]
    sub = tm // PROJ_SUBTILES
    for s in range(PROJ_SUBTILES):
        rows = slice(s * sub, (s + 1) * sub)
        if n_pre:
            x = _outproj_rows(pre_a, pre_z, pre_w, x_ref, pre_gp, pre_mod[pl.ds(row, 1), 2 * d:3 * d], rows)
            xnew_ref[0, rows, :] = x
        else:
            x = x_ref[0, rows, :]
        r = lax.rsqrt(jnp.mean(x * x, axis=-1, keepdims=True) + EPS)
        h = (x * r * g_ref[...]) * (1.0 + scale) + shift
        hb = h.astype(BF16)
        if gqa is not None:
            res = jnp.dot(hb, w_refs[-1][...], preferred_element_type=F32)
        for w_ref, o_ref in zip(w_refs[:n_plain], o_refs):
            o_ref[0, rows, :] = jnp.dot(hb, w_ref[...], preferred_element_type=F32).astype(o_ref.dtype)
        if t_sizes:
            res_t = _dot_nt(wt_ref[...], hb)
            for ot_ref, tb, off, end in zip(ot_refs, t_blocks, t_offs, t_offs[1:]):
                _store_t(ot_ref, res_t[off:end], tb, s * sub)
            if gate_rows:
                off = t_offs[len(t_blocks)]
                gto_ref[0, :, rows] = res_t[off:off + gate_rows] + bgt_ref[...]
        if gqa is not None:
            nq = W_B // LANES
            for j in range(nq + 1):
                xn = _head_norm(res[:, j * LANES:(j + 1) * LANES], gq_ref[...] if j < nq else gk_ref[...])
                if j == nq and knt_block is not None:
                    _store_t(knt_out, xn.T, knt_block, s * sub)
                if rope:
                    xn = _rope(xn, cos_ref[rows, :], sin_ref[rows, :])
                if j < nq:
                    q_out[0, rows, j * LANES:(j + 1) * LANES] = (xn * QSCALE).astype(q_out.dtype)
                else:
                    k_out[0, rows, :] = xn.astype(k_out.dtype)


def _proj(x, g_pre, mod, segs, wt_all, tsegs, b_gates, gqa, *, row0, per_batch, tm, pre=None):
    bsz, t, d = x.shape
    grid = (bsz, t // tm)
    const = lambda b, i: (0, 0)
    row_spec = lambda n: pl.BlockSpec((1, tm, n), lambda b, i: (b, i, 0))
    in_specs, args = [], []
    if pre is not None:
        for a in pre["parts"]:
            in_specs.append(row_spec(a.shape[-1]))
            args.append(a)
        in_specs += [row_spec(pre["z"].shape[-1]), pl.BlockSpec(pre["w_out"].shape, const), row_spec(d),
                     pl.BlockSpec((1, d), const), pl.BlockSpec(pre["mod"].shape, const)]
        args += [pre["z"], pre["w_out"], x, pre["g_post"].reshape(1, d), pre["mod"]]
    else:
        in_specs.append(row_spec(d))
        args.append(x)
    in_specs += [pl.BlockSpec((1, d), const), pl.BlockSpec(mod.shape, const)]
    args += [g_pre.reshape(1, d), mod]
    plain_ws = [w for w, _ in segs] + ([gqa["w"]] if gqa is not None else [])
    for w in plain_ws + ([wt_all] if wt_all is not None else []):
        in_specs.append(pl.BlockSpec(w.shape, const))
        args.append(w)
    gate_rows = 0
    t_sizes = [n for n, _, _ in tsegs]
    if b_gates is not None:
        gate_rows = b_gates.shape[0]
        t_sizes.append(LANES)
        in_specs.append(pl.BlockSpec((gate_rows, 1), const))
        args.append(b_gates.reshape(gate_rows, 1))
    assert sum(t_sizes) == (0 if wt_all is None else wt_all.shape[0])
    if gqa is not None:
        in_specs += [pl.BlockSpec((1, LANES), const)] * 2
        args += [jnp.tile(gqa["g_q"], 2).reshape(1, LANES), jnp.tile(gqa["g_k"], 2).reshape(1, LANES)]
        if gqa["rope"] is not None:
            in_specs += [pl.BlockSpec((tm, LANES), lambda b, i: (i, 0))] * 2
            args += list(gqa["rope"])
    out_specs, out_shape = [], []
    if pre is not None:
        out_specs.append(row_spec(d))
        out_shape.append(jax.ShapeDtypeStruct((bsz, t, d), F32))
    for w, dt in segs:
        n = w.shape[1]
        out_specs.append(row_spec(n))
        out_shape.append(jax.ShapeDtypeStruct((bsz, t, n), dt))

    def add_t_out(n, dt, tb):
        if tb is None:
            out_specs.append(pl.BlockSpec((1, n, tm), lambda b, i: (b, 0, i)))
            out_shape.append(jax.ShapeDtypeStruct((bsz, n, t), dt))
        else:
            out_specs.append(pl.BlockSpec((1, tm // tb, n, tb), lambda b, i: (b, i, 0, 0)))
            out_shape.append(jax.ShapeDtypeStruct((bsz, t // tb, n, tb), dt))

    for n, dt, tb in tsegs:
        add_t_out(n, dt, tb)
    if b_gates is not None:
        add_t_out(gate_rows, F32, None)
    gqa_static = None
    if gqa is not None:
        out_specs += [pl.BlockSpec((1, tm, W_B), lambda b, i: (b, i, 0)),
                      pl.BlockSpec((1, tm, LANES), lambda b, i: (b, i, 0))]
        out_shape += [jax.ShapeDtypeStruct((bsz, t, W_B), BF16), jax.ShapeDtypeStruct((bsz, t, LANES), BF16)]
        if gqa["knt_block"] is not None:
            add_t_out(LANES, F32, gqa["knt_block"])
        gqa_static = (gqa["rope"] is not None, gqa["knt_block"])
    kern = functools.partial(_proj_kernel, n_pre=0 if pre is None else len(pre["parts"]),
                             n_seg=len(plain_ws), t_sizes=tuple(t_sizes),
                             t_blocks=tuple(tb for _, _, tb in tsegs), gate_rows=gate_rows, gqa=gqa_static,
                             row0=row0, per_batch=per_batch)
    return pl.pallas_call(
        kern, grid=grid, in_specs=in_specs, out_specs=out_specs, out_shape=out_shape,
        compiler_params=_cparams(("arbitrary", "arbitrary")), name="in_proj",
    )(*args)


def _outproj_rows(a_refs, z_ref, w_ref, x_ref, gp_ref, gate, rows):
    sz = _silu(z_ref[0, rows, :].astype(F32))
    acc = None
    off = 0
    for a_ref in a_refs:
        kk = a_ref.shape[-1]
        y = (a_ref[0, rows, :].astype(F32) * sz[:, off:off + kk]).astype(BF16)
        part = jnp.dot(y, w_ref[off:off + kk, :], preferred_element_type=F32)
        acc = part if acc is None else acc + part
        off += kk
    r = lax.rsqrt(jnp.mean(acc * acc, axis=-1, keepdims=True) + EPS)
    return x_ref[0, rows, :] + gate * (acc * r * gp_ref[...])


def _outproj_kernel(*refs, n_in, row0, per_batch):
    a_refs = refs[:n_in]
    z_ref, w_ref, x_ref, gp_ref, mod_ref, o_ref = refs[n_in:n_in + 6]
    d = x_ref.shape[-1]
    row = row0 + (pl.program_id(0) if per_batch else 0)
    gate = mod_ref[pl.ds(row, 1), 2 * d:3 * d]
    o_ref[0] = _outproj_rows(a_refs, z_ref, w_ref, x_ref, gp_ref, gate, slice(None))


def _outproj(parts, z, w_out, x, g_post, mod, *, row0, per_batch, tm):
    bsz, t, d = x.shape
    grid = (bsz, t // tm)
    in_specs, args = [], []
    for a in parts:
        in_specs.append(pl.BlockSpec((1, tm, a.shape[-1]), lambda b, i: (b, i, 0)))
        args.append(a)
    in_specs += [pl.BlockSpec((1, tm, z.shape[-1]), lambda b, i: (b, i, 0)),
                 pl.BlockSpec(w_out.shape, lambda b, i: (0, 0)),
                 pl.BlockSpec((1, tm, d), lambda b, i: (b, i, 0)),
                 pl.BlockSpec((1, d), lambda b, i: (0, 0)),
                 pl.BlockSpec(mod.shape, lambda b, i: (0, 0))]
    args += [z, w_out, x, g_post.reshape(1, d), mod]
    kern = functools.partial(_outproj_kernel, n_in=len(parts), row0=row0, per_batch=per_batch)
    return pl.pallas_call(
        kern, grid=grid, in_specs=in_specs,
        out_specs=pl.BlockSpec((1, tm, d), lambda b, i: (b, i, 0)),
        out_shape=jax.ShapeDtypeStruct((bsz, t, d), F32),
        compiler_params=_cparams(("arbitrary", "arbitrary")), name="out_proj",
    )(*args)


def _split3_bf16(x):
    hi = x.astype(BF16)
    r1 = x - hi.astype(F32)
    mid = r1.astype(BF16)
    lo = (r1 - mid.astype(F32)).astype(BF16)
    return hi, mid, lo


def _mlstm_gate_rows(ig_f, fg_f, ig_b, fg_b):
    rr, ll = fg_f.shape
    lf = _log_sigmoid(jnp.concatenate([fg_f, fg_b], axis=0))
    pieces = jnp.concatenate(_split3_bf16(lf), axis=0)
    u = lax.broadcasted_iota(jnp.int32, (ll, ll), 0)
    t = lax.broadcasted_iota(jnp.int32, (ll, ll), 1)
    out = []
    for d, ig in enumerate((ig_f, ig_b)):
        tri = jnp.where((u >= t) if d else (u <= t), 1.0, 0.0).astype(BF16)
        y = jnp.dot(pieces, tri, preferred_element_type=F32)
        rows = slice(d * rr, (d + 1) * rr)
        b = y[0:2 * rr][rows] + y[2 * rr:4 * rr][rows] + y[4 * rr:6 * rr][rows]
        c = ig - b
        cmax = jnp.broadcast_to(jnp.max(c, axis=-1, keepdims=True), c.shape)
        tot = jnp.broadcast_to(jnp.sum(lf[rows], axis=-1, keepdims=True), c.shape)
        out.append((c, b, cmax, tot))
    return out


def _mlstm_kernel(*refs, nc, hps, unroll, has_init, emit_state):
    q_ref, k_ref, vt_ref, ot_ref, gtt_ref, ghn_ref = refs[:6]
    pos = 6
    if has_init:
        c0_ref, n0_ref, m0_ref = refs[pos:pos + 3]
        pos += 3
    ha_ref = refs[pos]
    pos += 1
    if emit_state:
        cout_ref, nout_ref, mout_ref = refs[pos:pos + 3]
        pos += 3
    hs_sc, gate_sc, c_sc, n_sc, m_sc = refs[pos:pos + 5]
    ll, dh = CHUNK_A, DH_A
    kscale = dh ** -0.5
    hd0 = pl.program_id(1) * hps
    chains = [(i, d) for i in range(hps) for d in range(2)]

    for i, d in chains:
        if has_init:
            c_sc[i, d] = c0_ref[0, d, i]
            n_sc[i, d] = n0_ref[0, i, d:d + 1, :]
            m_sc[i, d] = m0_ref[0, i, d:d + 1, :]
        else:
            c_sc[i, d] = jnp.zeros((dh, dh), F32)
            n_sc[i, d] = jnp.zeros((1, dh), F32)
            m_sc[i, d] = jnp.zeros((1, dh), F32)

    def gate_rows(col0):
        return gtt_ref[0, pl.ds(pl.multiple_of((col0 + hd0) * nc, 8), hps * nc), :]

    gates = _mlstm_gate_rows(gate_rows(0), gate_rows(H_A), gate_rows(2 * H_A), gate_rows(3 * H_A))
    for d in range(2):
        for kind in range(4):
            gate_sc[d, kind] = gates[d][kind]

    ghn_t = [jnp.broadcast_to(ghn_ref[:, i * dh:(i + 1) * dh], (ll, dh)).T for i in range(hps)]
    si = lax.broadcasted_iota(jnp.int32, (ll, ll), 0)
    ti = lax.broadcasted_iota(jnp.int32, (ll, ll), 1)

    def run_trip(j, first_touch):
        jobs = []
        for u in range(unroll):
            step = j * unroll + u
            for i, d in chains:
                jobs.append((i, d, step if d == 0 else nc - 1 - step))
        state = {ch: [c_sc[ch], n_sc[ch], m_sc[ch]] for ch in chains}
        hcols = [slice(i * dh, (i + 1) * dh) for i in range(hps)]

        def tok_rows(cidx):
            return pl.ds(pl.multiple_of(cidx * ll, ll), ll)

        rows = []
        for i, d, cidx in jobs:
            c_r, b_r, cmax_r, tot_r = (gate_sc[d, kind, pl.ds(i * nc + cidx, 1), :] for kind in range(4))
            m_st = state[(i, d)][2]
            m_c = jnp.maximum(m_st, cmax_r)
            state[(i, d)][2] = tot_r + m_c
            rows.append((c_r, b_r, m_st, jnp.exp(m_st - m_c), jnp.exp(c_r - m_c) * kscale))
        start = []
        for (i, d, cidx), (_, _, _, a_st, wk) in zip(jobs, rows):
            k = k_ref[0, tok_rows(cidx), hcols[i]]
            vt = vt_ref[0, cidx, hcols[i], :]
            vw = jnp.concatenate([vt.astype(F32) * wk, jnp.broadcast_to(wk, (BF16_ROWS, ll))], axis=0)
            upd = jnp.dot(vw.astype(BF16), k, preferred_element_type=F32)
            c_st, n_st, _ = state[(i, d)]
            start.append((c_st, n_st))
            state[(i, d)][0] = a_st * c_st + upd[:dh]
            state[(i, d)][1] = a_st * n_st + upd[dh:dh + 1]
        prods = []
        for (i, d, cidx), (c_st, n_st) in zip(jobs, start):
            q = q_ref[0, tok_rows(cidx), hcols[i]]
            k = k_ref[0, tok_rows(cidx), hcols[i]]
            n16 = jnp.broadcast_to(n_st.astype(BF16), (BF16_ROWS, dh))
            prods.append(_dot_nt(jnp.concatenate([k, c_st.astype(BF16), n16], axis=0), q))
        for (i, d, cidx), (c_r, b_r, m_st, _, _), r in zip(jobs, rows, prods):
            allowed = (si >= ti) if d else (si <= ti)
            cb = jnp.where(allowed, jnp.broadcast_to(c_r, (ll, ll)).T, -jnp.inf)
            big_m = jnp.maximum(m_st, jnp.max(cb, axis=0, keepdims=True))
            p = jnp.exp(cb - (big_m - float(np.log(kscale)))) * r[:ll]
            w_inter = jnp.exp(m_st - big_m)
            vt = vt_ref[0, cidx, hcols[i], :]
            num = jnp.dot(vt, p.astype(BF16), preferred_element_type=F32) - r[ll:ll + dh] * (-w_inter)
            den = w_inter * r[ll + dh:ll + dh + 1] + jnp.sum(p, axis=0, keepdims=True)
            h_t = num / jnp.maximum(jnp.abs(den), jnp.exp(-(b_r + big_m)))
            if first_touch:
                hs_sc[i, cidx] = h_t
            else:
                hsum = h_t + hs_sc[i, cidx]
                rn = lax.rsqrt(jnp.mean(hsum * hsum, axis=0, keepdims=True) + EPS)
                out_t = (hsum * rn * ghn_t[i]) * _sigmoid(ot_ref[0, cidx, hcols[i], :].astype(F32))
                ha_ref[0, tok_rows(cidx), hcols[i]] = out_t.T.astype(ha_ref.dtype)
        for ch in chains:
            c_sc[ch], n_sc[ch], m_sc[ch] = state[ch]

    def make_body(first_touch):
        def body(j, carry):
            run_trip(j, first_touch)
            return carry
        return body

    trips = nc // unroll
    lax.fori_loop(0, trips // 2, make_body(True), 0)
    lax.fori_loop(trips // 2, trips, make_body(False), 0)

    if emit_state:
        for i, d in chains:
            cout_ref[0, 0, d, i] = c_sc[i, d]
            nout_ref[0, i, d:d + 1, :] = n_sc[i, d]
            mout_ref[0, i, d:d + 1, :] = m_sc[i, d]


def _mlstm(qk, vt, ot, gates_t, g_hn, init, *, emit_state):
    bsz, t, _ = qk.shape
    nc = t // CHUNK_A
    hps, unroll = (H_A, 1) if nc < 8 else (1, 4)
    assert (nc // 2) % unroll == 0 and nc % 2 == 0
    gtt = gates_t.reshape(bsz, 4 * H_A * nc, CHUNK_A)
    wh = hps * DH_A

    def tblk():
        return pl.BlockSpec((1, nc, wh, CHUNK_A), lambda b, g: (b, 0, g, 0))

    in_specs = [pl.BlockSpec((1, t, wh), lambda b, g: (b, 0, g)),
                pl.BlockSpec((1, t, wh), lambda b, g: (b, 0, H_A // hps + g)),
                tblk(), tblk(),
                pl.BlockSpec((1, 4 * H_A * nc, CHUNK_A), lambda b, g: (b, 0, 0)),
                pl.BlockSpec((1, wh), lambda b, g: (0, g))]
    args = [qk, qk, vt, ot, gtt, g_hn.reshape(1, W_A)]
    if init is not None:
        c0, n0, m0 = init
        in_specs += [pl.BlockSpec((1, 2, hps, DH_A, DH_A), lambda b, g: (b, 0, g, 0, 0)),
                     pl.BlockSpec((1, hps, 2, DH_A), lambda b, g: (b, g, 0, 0)),
                     pl.BlockSpec((1, hps, 2, DH_A), lambda b, g: (b, g, 0, 0))]
        args += [c0, n0, m0]
    out_specs = [pl.BlockSpec((1, t, wh), lambda b, g: (b, 0, g))]
    out_shape = [jax.ShapeDtypeStruct((bsz, t, W_A), BF16)]
    if emit_state:
        out_specs += [pl.BlockSpec((1, 1, 2, hps, DH_A, DH_A), lambda b, g: (b, 0, 0, g, 0, 0)),
                      pl.BlockSpec((1, hps, 2, DH_A), lambda b, g: (b, g, 0, 0)),
                      pl.BlockSpec((1, hps, 2, DH_A), lambda b, g: (b, g, 0, 0))]
        out_shape += [jax.ShapeDtypeStruct((bsz, 1, 2, H_A, DH_A, DH_A), F32),
                      jax.ShapeDtypeStruct((bsz, H_A, 2, DH_A), F32),
                      jax.ShapeDtypeStruct((bsz, H_A, 2, DH_A), F32)]
    kern = functools.partial(_mlstm_kernel, nc=nc, hps=hps, unroll=unroll, has_init=init is not None,
                             emit_state=emit_state)
    return pl.pallas_call(
        kern, grid=(bsz, H_A // hps), in_specs=in_specs, out_specs=out_specs, out_shape=out_shape,
        scratch_shapes=[pltpu.VMEM((hps, nc, DH_A, CHUNK_A), F32), pltpu.VMEM((2, 4, hps * nc, CHUNK_A), F32),
                        pltpu.VMEM((hps, 2, DH_A, DH_A), F32), pltpu.VMEM((hps, 2, 1, DH_A), F32),
                        pltpu.VMEM((hps, 2, 1, DH_A), F32)],
        compiler_params=_cparams(("arbitrary", "arbitrary")), name="mlstm",
    )(*args)


def _rope_tables(n_tok):
    t = np.arange(n_tok)
    row = (t // GRID_W).astype(np.float32)
    colp = (t % GRID_W).astype(np.float32)
    quarter = DH_B // 4
    freqs = (np.float32(ROPE_THETA) ** (-np.arange(quarter, dtype=np.float32) / np.float32(quarter))).astype(np.float32)
    ar = row[:, None] * freqs
    ac = colp[:, None] * freqs
    cos = np.concatenate([np.cos(ar), np.cos(ar), np.cos(ac), np.cos(ac)], axis=-1)
    sin = np.concatenate([-np.sin(ar), np.sin(ar), -np.sin(ac), np.sin(ac)], axis=-1)
    return jnp.asarray(np.tile(cos, (1, 2)), F32), jnp.asarray(np.tile(sin, (1, 2)), F32)


def _split_heads_q(q):
    lo = lax.broadcasted_iota(jnp.int32, q.shape, 1) < (LANES // 2)
    zero = jnp.zeros_like(q)
    return jnp.concatenate([jnp.where(lo, q, zero), jnp.where(lo, zero, q)], axis=0)


def _merge_heads_o(o_t, tq):
    half = LANES // 2
    return jnp.concatenate([o_t[:half, :tq], o_t[half:, tq:]], axis=0).T


def _with_ones_rows(vt):
    return jnp.concatenate([vt, jnp.ones((BF16_ROWS, vt.shape[1]), BF16)], axis=0)


def _online_softmax_pv_t(problems):
    seq = [(pi, ci) for pi, (_, chunks, _) in enumerate(problems) for ci in range(len(chunks))]
    scores = {}

    def issue(t):
        pi, ci = seq[t]
        qm, chunks, _ = problems[pi]
        scores[(pi, ci)] = _dot_nt(chunks[ci][0], qm)

    for t in range(min(QK_AHEAD, len(seq))):
        issue(t)
    m = acc = None
    for t, (pi, ci) in enumerate(seq):
        if t + QK_AHEAD < len(seq):
            issue(t + QK_AHEAD)
        _, chunks, emit = problems[pi]
        _, vt, penalty = chunks[ci]
        s = scores.pop((pi, ci))
        if penalty is not None:
            s = s - penalty
        mc = jnp.max(s, axis=0, keepdims=True)
        m_new = mc if ci == 0 else jnp.maximum(m, mc)
        p = jnp.exp2(s - m_new).astype(BF16)
        part = jnp.dot(_with_ones_rows(vt), p, preferred_element_type=F32)
        acc = part if ci == 0 else jnp.exp2(m - m_new) * acc + part
        m = m_new
        if ci == len(chunks) - 1:
            dv = vt.shape[0]
            emit(acc[:dv] / acc[dv:dv + 1])


def _attn_kernel(*refs, n_tiles, kv_shared, has_cache):
    q_ref, k_ref, vt_ref = refs[:3]
    pos = 3
    if has_cache:
        kc_ref, vtc_ref = refs[pos:pos + 2]
        pos += 2
    o_ref = refs[pos]
    tq = q_ref.shape[1]
    s_len = k_ref.shape[1]
    sc = min(s_len, KEY_CHUNK)
    problems = []
    for j in range(n_tiles):
        kj = 0 if kv_shared else j
        kcols = slice(kj * LANES, (kj + 1) * LANES)
        qm = _split_heads_q(q_ref[0, :, j * LANES:(j + 1) * LANES])
        chunks = [(k_ref[0, c * sc:(c + 1) * sc, kcols].astype(BF16),
                   vt_ref[0, kcols, c * sc:(c + 1) * sc].astype(BF16), None) for c in range(s_len // sc)]
        if has_cache:
            chunks.append((kc_ref[0], vtc_ref[0], None))

        def emit(o_t, j=j):
            o_ref[0, :, j * LANES:(j + 1) * LANES] = _merge_heads_o(o_t, tq).astype(o_ref.dtype)

        problems.append((qm, chunks, emit))
    _online_softmax_pv_t(problems)


def _attention(q, k, vt, cache, *, kv_shared, tq, n_tiles):
    bsz, t, w = q.shape
    s = k.shape[1]
    wt = n_tiles * LANES
    if kv_shared:
        k_spec = pl.BlockSpec((1, s, LANES), lambda b, i, g: (b, 0, 0))
        vt_spec = pl.BlockSpec((1, LANES, s), lambda b, i, g: (b, 0, 0))
    else:
        k_spec = pl.BlockSpec((1, s, wt), lambda b, i, g: (b, 0, g))
        vt_spec = pl.BlockSpec((1, wt, s), lambda b, i, g: (b, g, 0))
    in_specs = [pl.BlockSpec((1, tq, wt), lambda b, i, g: (b, i, g)), k_spec, vt_spec]
    args = [q, k, vt]
    if cache is not None:
        p = cache[0].shape[1]
        in_specs += [pl.BlockSpec((1, p, LANES), lambda b, i, g: (b, 0, 0)),
                     pl.BlockSpec((1, LANES, p), lambda b, i, g: (b, 0, 0))]
        args += list(cache)
    kern = functools.partial(_attn_kernel, n_tiles=n_tiles, kv_shared=kv_shared,
                             has_cache=cache is not None)
    return pl.pallas_call(
        kern, grid=(bsz, t // tq, w // wt), in_specs=in_specs,
        out_specs=pl.BlockSpec((1, tq, wt), lambda b, i, g: (b, i, g)),
        out_shape=jax.ShapeDtypeStruct((bsz, t, w), BF16),
        compiler_params=_cparams(("arbitrary", "arbitrary", "arbitrary")), name="attention",
    )(*args)


def _na_bias_blocks(variant):
    out = {}
    for qr in range(NA_ROWS):
        for kr in range(NA_WIN):
            if variant == 0:
                dr = kr - qr if kr < WIN_R else None
            elif variant == 1:
                dr = kr - qr - WIN_R // 2 if qr <= kr < qr + WIN_R else None
            else:
                dr = kr - qr - (NA_WIN - NA_ROWS) if kr >= NA_WIN - WIN_R else None
            out[(qr, kr)] = None if dr is None else dr + WIN_R - 1
    return out


def _na_kernel(q_ref, k_ref, vt_ref, kc_ref, vtc_ref, bc_ref, o_ref, bias_sc, *, n_rb):
    w = GRID_W
    tq = NA_ROWS * w

    @pl.when(jnp.logical_and(pl.program_id(1) == 0, pl.program_id(2) == 0))
    def _():
        for variant in range(3):
            for (qr, kr), di in _na_bias_blocks(variant).items():
                for hh in range(2 * NA_TILES):
                    val = jnp.full((w, w), MASKED, F32) if di is None else bc_ref[hh, di]
                    bias_sc[variant, kr * w:(kr + 1) * w, hh * tq + qr * w:hh * tq + (qr + 1) * w] = val

    n_rows = n_rb * NA_ROWS
    problems = []
    for rr in range(NA_RBS):
        rb = pl.program_id(2) * NA_RBS + rr
        variant = jnp.where(rb == 0, 0, jnp.where(rb == n_rb - 1, 2, 1))
        ws = jnp.clip(rb * NA_ROWS - WIN_R // 2, 0, n_rows - NA_WIN)
        blk0 = ws // NA_ROWS
        start = ws * w
        qrows = slice(rr * tq, (rr + 1) * tq)
        for j in range(NA_TILES):
            cols = slice(j * LANES, (j + 1) * LANES)
            qcols = slice(j * 2 * tq, (j + 1) * 2 * tq)
            qm = _split_heads_q(q_ref[0, qrows, cols])
            chunks = [(k_ref[0, pl.ds(pl.multiple_of(start + i * tq, tq), tq), cols],
                       vt_ref[0, blk0 + i, cols, :],
                       bias_sc[variant, i * tq:(i + 1) * tq, qcols]) for i in range(NA_BLKS)]
            chunks.append((kc_ref[0, :, cols], vtc_ref[0, cols, :], None))

            def emit(o_t, qrows=qrows, cols=cols):
                o_ref[0, qrows, cols] = _merge_heads_o(o_t, tq).astype(o_ref.dtype)

            problems.append((qm, chunks, emit))
    _online_softmax_pv_t(problems)


def _na_bias_table(rpb):
    c = np.arange(GRID_W)
    cs = np.clip(c - WIN_C // 2, 0, GRID_W - WIN_C)
    ck = np.arange(GRID_W)
    valid = (ck[:, None] >= cs[None, :]) & (ck[:, None] < cs[None, :] + WIN_C)
    idx = np.clip(ck[:, None] - c[None, :] + WIN_C - 1, 0, 2 * WIN_C - 2)
    onehot = (idx[..., None] == np.arange(2 * WIN_C - 1)).astype(np.float32)
    tab = jnp.einsum('hrx,kcx->hrkc', rpb, jnp.asarray(onehot), precision=lax.Precision.HIGHEST)
    return jnp.where(jnp.asarray(valid), tab * -LOG2E, MASKED)


def _na_attention(q, k, vt, kc, vtc, rpb):
    bsz, t, w = q.shape
    p = kc.shape[1]
    tq = NA_ROWS * GRID_W
    n_rb = t // tq
    bc = _na_bias_table(rpb)
    wt = NA_TILES * LANES
    tqs = NA_RBS * tq
    return pl.pallas_call(
        functools.partial(_na_kernel, n_rb=n_rb),
        grid=(w // wt, bsz, n_rb // NA_RBS),
        in_specs=[pl.BlockSpec((1, tqs, wt), lambda j, b, r: (b, r, j)),
                  pl.BlockSpec((1, t, wt), lambda j, b, r: (b, 0, j)),
                  pl.BlockSpec((1, n_rb, wt, tq), lambda j, b, r: (b, 0, j, 0)),
                  pl.BlockSpec((1, p, wt), lambda j, b, r: (b, 0, j)),
                  pl.BlockSpec((1, wt, p), lambda j, b, r: (b, j, 0)),
                  pl.BlockSpec((2 * NA_TILES, 2 * WIN_R - 1, GRID_W, GRID_W), lambda j, b, r: (j, 0, 0, 0))],
        out_specs=pl.BlockSpec((1, tqs, wt), lambda j, b, r: (b, r, j)),
        out_shape=jax.ShapeDtypeStruct((bsz, t, w), BF16),
        scratch_shapes=[pltpu.VMEM((3, NA_WIN * GRID_W, NA_TILES * 2 * tq), F32)],
        compiler_params=_cparams(("arbitrary", "arbitrary", "arbitrary")), name="na_attention",
    )(q, k, vt, kc, vtc, bc)


def _relayout_kernel(w_ref, *o_refs, plans, transposed, axis):
    w = w_ref[...]
    tile = w.shape[1 - axis]
    for o_ref, pieces, tr in zip(o_refs, plans, transposed):
        vals = []
        for src, size, scale in pieces:
            if src is None:
                val = jnp.zeros((size, tile) if axis == 0 else (tile, size), F32)
            else:
                val = lax.slice_in_dim(w, src, src + size, axis=axis)
                if scale != 1.0:
                    val = val * scale
            vals.append(val)
        val = vals[0] if len(vals) == 1 else jnp.concatenate(vals, axis=axis)
        o_ref[...] = (val.T if tr else val).astype(o_ref.dtype)


def _relayout(w, plans, transposed, *, axis, tile):
    grid = (w.shape[1 - axis] // tile,)
    full = w.shape[1 - axis]

    def spec(extent, tr):
        along_rows = (axis == 0) != tr
        shape = (extent, tile) if along_rows else (tile, extent)
        return pl.BlockSpec(shape, (lambda i: (0, i)) if along_rows else (lambda i: (i, 0)))

    def shape(extent, tr):
        return (extent, full) if (axis == 0) != tr else (full, extent)

    sizes = [sum(size for _, size, _ in pieces) for pieces in plans]
    return pl.pallas_call(
        functools.partial(_relayout_kernel, plans=plans, transposed=tuple(transposed), axis=axis),
        grid=grid, in_specs=[spec(w.shape[axis], False)],
        out_specs=[spec(n, tr) for n, tr in zip(sizes, transposed)],
        out_shape=[jax.ShapeDtypeStruct(shape(n, tr), BF16) for n, tr in zip(sizes, transposed)],
        compiler_params=_cparams(("arbitrary",)), name="w_relayout",
    )(w)


_GQA_PERM = np.array([0, 4, 1, 5, 2, 6, 3, 7])


def _tok_major(cache):
    b, h, p, dh = cache.shape
    return cache.transpose(0, 2, 1, 3).reshape(b, p, h * dh)


def _feat_major(cache):
    b, h, p, dh = cache.shape
    return cache.transpose(0, 1, 3, 2).reshape(b, h * dh, p)


def _head_major_from_t(x_t, n_heads):
    b, w, t = x_t.shape
    return jnp.swapaxes(x_t.reshape(b, 1, n_heads, w // n_heads, t), -1, -2)


def _even_layer(xp, xs, mod, g_pre, g_post, w_in, b_gates, g_hn, g_q, g_k, w_out, st_c, st_n, st_m,
                ck, cv, rope_tabs):
    o_v, o_o = 2 * W_A, 3 * W_A
    o_g = 4 * W_A
    o_qb = o_g + 4 * H_A
    o_kb = o_qb + W_B
    o_vb = o_kb + HKV_B * DH_B
    o_z = o_vb + HKV_B * DH_B
    def heads(src0):
        return [(src0 + int(h) * DH_B, DH_B, 1.0) for h in _GQA_PERM]

    n_g = 4 * H_A
    w_qk, w_gqa, w_z, wt_all = _relayout(
        jnp.swapaxes(w_in, 0, 1),
        [[(0, o_v, 1.0)],
         heads(o_qb) + [(o_kb, HKV_B * DH_B, 1.0)],
         [(o_z, W_A, 1.0)] + heads(o_z + W_A),
         [(o_v, 2 * W_A, 1.0), (o_vb, HKV_B * DH_B, 1.0), (o_g, n_g, 1.0), (None, LANES - n_g, 1.0)]],
        [True, True, True, False], axis=0, tile=256)
    (w_o,) = _relayout(w_out, [[(0, W_A, 1.0)] + heads(W_A)], [False], axis=0, tile=256)

    def stream(x, row0, per_batch, init, cache, rope, emit):
        bsz, t, _ = x.shape
        tm = 512
        xf = x if per_batch else x.reshape(1, bsz * t, D_MODEL)
        n_vb = HKV_B * DH_B
        vb_seg = (n_vb, BF16, None) if per_batch else (n_vb, F32, t)
        gqa = dict(w=w_gqa, g_q=g_q, g_k=g_k, rope=rope, knt_block=t if emit else None)
        outs = _proj(xf, g_pre, mod, [(w_qk, BF16), (w_z, BF16)], wt_all,
                     [(W_A, BF16, CHUNK_A), (W_A, BF16, CHUNK_A), vb_seg], b_gates, gqa,
                     row0=row0, per_batch=per_batch, tm=tm)
        qk, z, vta, ota, vbt, gtt, qn, kn = outs[:8]
        if not per_batch:
            qk, z, qn, kn = (a.reshape(bsz, t, a.shape[-1]) for a in (qk, z, qn, kn))
            gtt = gtt.reshape(4 * H_A, bsz, t).transpose(1, 0, 2)
            vbt = vbt.reshape(bsz, HKV_B * DH_B, t)
            vta, ota = (a.reshape(bsz, t // CHUNK_A, W_A, CHUNK_A) for a in (vta, ota))
        res = _mlstm(qk, vta, ota, gtt, g_hn, init, emit_state=emit)
        ha = res[0]
        if per_batch:
            hb = _attention(qn, kn, vbt, cache, kv_shared=True, tq=512, n_tiles=4)
        else:
            hb = _attention(qn, kn, vbt, cache, kv_shared=True, tq=t, n_tiles=W_B // LANES)
        pre = dict(parts=[ha.reshape(xf.shape[0], -1, W_A), hb.reshape(xf.shape[0], -1, W_B)],
                   z=z.reshape(xf.shape[0], -1, W_A + W_B), w_out=w_o, x=xf, g_post=g_post, mod=mod)
        knt = outs[8].reshape(bsz, HKV_B * DH_B, t) if emit else None
        return pre, res[1:], knt, vbt

    pre_p, st, knt_p, vbt_p = stream(xp, 0, False, None, None, None, True)
    n0 = st_n.transpose(0, 2, 1, 3)
    m0 = jnp.broadcast_to(st_m.transpose(0, 2, 1)[..., None], n0.shape)
    cache = (_tok_major(ck).astype(BF16), _feat_major(cv).astype(BF16))
    pre_s, _, _, _ = stream(xs, 1, True, (st_c, n0, m0), cache, rope_tabs, False)
    c_out, n_out, m_out = st
    new_n = n_out.transpose(0, 2, 1, 3)[:, None]
    new_m = m_out[..., 0].transpose(0, 2, 1)[:, None]
    return (pre_p, pre_s, c_out, new_n, new_m, _head_major_from_t(knt_p, HKV_B),
            _head_major_from_t(vbt_p, HKV_B))


def _odd_layer(pre_p, pre_s, bsz, t, mod, g_pre, g_post, w_in, rpb, w_out, ck, cv):
    w_q, w_k, w_z, wt_kv = _relayout(
        w_in, [[(0, W_C, QSCALE)], [(W_C, W_C, 1.0)], [(3 * W_C, W_C, 1.0)], [(W_C, 2 * W_C, 1.0)]],
        [False, False, False, True], axis=1, tile=256)
    (w_o,) = _relayout(w_out, [[(0, W_C, 1.0)]], [False], axis=0, tile=256)
    tm = 512
    xf, q, k, z, kt, vt = _proj(pre_p["x"], g_pre, mod, [(w_q, BF16), (w_k, BF16), (w_z, BF16)], wt_kv,
                                [(W_C, F32, t), (W_C, F32, t)], None, None, row0=0, per_batch=False,
                                tm=tm, pre=pre_p)
    kt = kt.reshape(bsz, W_C, t)
    vt = vt.reshape(bsz, W_C, t)
    o = _attention(q.reshape(bsz, t, W_C), k.reshape(bsz, t, W_C), vt, None,
                   kv_shared=False, tq=t, n_tiles=W_C // LANES)
    yp = _outproj([o.reshape(1, bsz * t, W_C)], z, w_o, xf, g_post, mod, row0=0, per_batch=False, tm=tm)
    yp = yp.reshape(bsz, t, D_MODEL)
    xs, q, k, z, vts = _proj(pre_s["x"], g_pre, mod, [(w_q, BF16), (w_k, BF16), (w_z, BF16)], wt_kv[W_C:],
                             [(W_C, BF16, NA_ROWS * GRID_W)], None, None, row0=1, per_batch=True,
                             tm=tm, pre=pre_s)
    o = _na_attention(q, k, vts, _tok_major(ck).astype(BF16), _feat_major(cv).astype(BF16), rpb)
    ys = _outproj([o], z, w_o, xs, g_post, mod, row0=1, per_batch=True, tm=tm)
    return yp, ys, _head_major_from_t(kt, H_C), _head_major_from_t(vt, H_C)


def kernel(x_prompt, x_sample, state_mlstm_C, state_mlstm_n, state_mlstm_m, cache_gqa_k, cache_gqa_v,
           cache_na_k, cache_na_v, c, c_ctx, w_mod, b_mod, g_pre, g_post, w_in_ab, b_gates_ab, g_hnorm_a,
           g_qnorm_b, g_knorm_b, w_out_ab, w_in_c, rpb_c, w_out_c):
    depth = w_mod.shape[0]
    assert depth == 2 and c.shape[0] == 2
    cvec = jnp.concatenate([c_ctx[None], c, jnp.zeros((8 - 1 - c.shape[0], D_MODEL), F32)], axis=0)
    mod = _modulation(cvec, w_mod, b_mod)
    rope_tabs = _rope_tables(x_sample.shape[1])
    pre_p, pre_s, c_out, n_out, m_out, gk, gv = _even_layer(
        x_prompt, x_sample, mod[0], g_pre[0], g_post[0], w_in_ab[0], b_gates_ab[0], g_hnorm_a[0],
        g_qnorm_b[0], g_knorm_b[0], w_out_ab[0], state_mlstm_C[:, 0], state_mlstm_n[:, 0],
        state_mlstm_m[:, 0], cache_gqa_k[:, 0], cache_gqa_v[:, 0], rope_tabs)
    bsz, t, _ = x_prompt.shape
    xp, xs, nk, nv = _odd_layer(pre_p, pre_s, bsz, t, mod[1], g_pre[1], g_post[1], w_in_c[0], rpb_c[0],
                                w_out_c[0], cache_na_k[:, 0], cache_na_v[:, 0])
    return (xp, xs, c_out, n_out, m_out, gk, gv, nk, nv)
```

```python
import functools

import jax
import jax.numpy as jnp
import numpy as np
from jax import lax
from jax.experimental import pallas as pl
from jax.experimental.pallas import tpu as pltpu

F32 = jnp.float32
BF16 = jnp.bfloat16

D_MODEL = 1024
GRID_W = 64
EPS = 1e-6
H_A = 4
DH_A = 128
W_A = H_A * DH_A
CHUNK_A = 128
HQ_B = 8
HKV_B = 2
DH_B = 64
W_B = HQ_B * DH_B
ROPE_THETA = 10000.0
H_C = 16
DH_C = 64
W_C = H_C * DH_C
WIN_R = 8
WIN_C = 16

LANES = 128
NA_ROWS = 4
NA_TILES = 2
NA_RBS = 8
NA_BLKS = -(-(NA_ROWS + WIN_R - 1) // NA_ROWS)
NA_WIN = NA_BLKS * NA_ROWS
assert WIN_R // 2 == NA_ROWS
BF16_ROWS = 16
PROJ_SUBTILES = 2
KEY_CHUNK = 256
QK_AHEAD = 3
LOG2E = 1.4426950408889634
QSCALE = DH_B ** -0.5 * LOG2E
assert DH_B == DH_C
MASKED = 1e30
VMEM_LIMIT = 56 * 1024 * 1024


def _cparams(sem):
    return pltpu.CompilerParams(dimension_semantics=sem, vmem_limit_bytes=VMEM_LIMIT)


def _silu(x):
    return x / (1.0 + jnp.exp(-x))


def _sigmoid(x):
    return 1.0 / (1.0 + jnp.exp(-x))


def _log_sigmoid(x):
    return jnp.minimum(x, 0.0) - jnp.log1p(jnp.exp(-jnp.abs(x)))


def _dot_nt(a, b):
    return lax.dot_general(a, b, (((1,), (1,)), ((), ())), preferred_element_type=F32)


MOD_ROWS = 3


def _mod_kernel(c_ref, w_ref, b_ref, o_ref, sb_sc):
    d, tn = w_ref.shape[1], w_ref.shape[2]
    sub = 8

    @pl.when(jnp.logical_and(pl.program_id(0) == 0, pl.program_id(1) == 0))
    def _():
        s = _silu(c_ref[...])
        for r in range(MOD_ROWS):
            sb_sc[r] = jnp.broadcast_to(s[r:r + 1, :], (LANES, d)).T

    def body(g, acc):
        rows = pl.ds(pl.multiple_of(g * sub, sub), sub)
        w = w_ref[0, rows, :]
        out = []
        for r in range(MOD_ROWS):
            sb = sb_sc[r, rows, :]
            out.append([acc[r][j] + sb * w[:, j * LANES:(j + 1) * LANES] for j in range(tn // LANES)])
        return out

    zero = jnp.zeros((sub, LANES), F32)
    acc = lax.fori_loop(0, d // sub, body, [[zero] * (tn // LANES) for _ in range(MOD_ROWS)], unroll=8)
    rows = [jnp.sum(jnp.concatenate(a, axis=1), axis=0, keepdims=True) + b_ref[0] for a in acc]
    o_ref[0] = jnp.concatenate(rows + [jnp.zeros((8 - MOD_ROWS, tn), F32)], axis=0)


def _modulation(cvec, w_mod, b_mod):
    depth, d, n = w_mod.shape
    tn = n // 4
    return pl.pallas_call(
        _mod_kernel,
        grid=(depth, n // tn),
        in_specs=[pl.BlockSpec((8, d), lambda l, j: (0, 0)),
                  pl.BlockSpec((1, d, tn), lambda l, j: (l, 0, j)),
                  pl.BlockSpec((1, 1, tn), lambda l, j: (l, 0, j))],
        out_specs=pl.BlockSpec((1, 8, tn), lambda l, j: (l, 0, j)),
        out_shape=jax.ShapeDtypeStruct((depth, 8, n), F32),
        scratch_shapes=[pltpu.VMEM((MOD_ROWS, d, LANES), F32)],
        compiler_params=_cparams(("arbitrary", "arbitrary")),
        name="modulation",
    )(cvec, w_mod, b_mod.reshape(depth, 1, n))


def _head_norm(x, g):
    lo = lax.broadcasted_iota(jnp.int32, x.shape, 1) < DH_B
    x2 = x * x
    s_lo = jnp.sum(jnp.where(lo, x2, 0.0), axis=-1, keepdims=True)
    s_hi = jnp.sum(jnp.where(lo, 0.0, x2), axis=-1, keepdims=True)
    ms = jnp.where(lo, s_lo, s_hi) * (1.0 / DH_B)
    return x * lax.rsqrt(ms + EPS) * g


def _rope(x, cos, sin):
    quarter = DH_B // 4
    first = (lax.broadcasted_iota(jnp.int32, x.shape, 1) % (2 * quarter)) < quarter
    partner = jnp.where(first, pltpu.roll(x, LANES - quarter, 1), pltpu.roll(x, quarter, 1))
    return x * cos + partner * sin


def _store_t(ref, val, t_block, tok0):
    n_tok = val.shape[1]
    if t_block is None:
        ref[0, :, tok0:tok0 + n_tok] = val.astype(ref.dtype)
    else:
        for i in range(n_tok // t_block):
            ref[0, tok0 // t_block + i] = val[:, i * t_block:(i + 1) * t_block].astype(ref.dtype)


def _proj_kernel(*refs, n_seg, t_sizes, t_blocks, t_plain, gate_rows, gqa, row0, per_batch):
    x_ref, g_ref, mod_ref = refs[:3]
    pos = 3
    w_refs = refs[pos:pos + n_seg]
    pos += n_seg
    if t_sizes:
        wt_ref = refs[pos]
        pos += 1
    if gate_rows:
        bgt_ref = refs[pos]
        pos += 1
    if gqa is not None:
        rope, knt_block = gqa
        gq_ref, gk_ref = refs[pos:pos + 2]
        pos += 2
        if rope:
            cos_ref, sin_ref = refs[pos:pos + 2]
            pos += 2
    n_plain = n_seg - (1 if gqa is not None else 0)
    o_refs = refs[pos:pos + n_plain]
    pos += n_plain
    ot_refs = refs[pos:pos + len(t_blocks)]
    pos += len(t_blocks)
    otp_refs = {}
    for i, flag in enumerate(t_plain):
        if flag:
            otp_refs[i] = refs[pos]
            pos += 1
    if gate_rows:
        gto_ref = refs[pos]
        pos += 1
    if gqa is not None:
        q_out, k_out = refs[pos:pos + 2]
        pos += 2
        if knt_block is not None:
            knt_out = refs[pos]
            pos += 1
    t_offs = [0]
    for n in t_sizes:
        t_offs.append(t_offs[-1] + n)

    d = x_ref.shape[-1]
    tm = x_ref.shape[1]
    row = row0 + (pl.program_id(0) if per_batch else 0)
    shift = mod_ref[pl.ds(row, 1), 0:d]
    scale = mod_ref[pl.ds(row, 1), d:2 * d]
    sub = tm // PROJ_SUBTILES
    for s in range(PROJ_SUBTILES):
        rows = slice(s * sub, (s + 1) * sub)
        x = x_ref[0, rows, :]
        r = lax.rsqrt(jnp.mean(x * x, axis=-1, keepdims=True) + EPS)
        h = (x * r * g_ref[...]) * (1.0 + scale) + shift
        hb = h.astype(BF16)
        if gqa is not None:
            res = jnp.dot(hb, w_refs[-1][...], preferred_element_type=F32)
        for w_ref, o_ref in zip(w_refs[:n_plain], o_refs):
            o_ref[0, rows, :] = jnp.dot(hb, w_ref[...], preferred_element_type=F32).astype(o_ref.dtype)
        if t_sizes:
            res_t = _dot_nt(wt_ref[...], hb)
            for i, (ot_ref, tb, off, end) in enumerate(zip(ot_refs, t_blocks, t_offs, t_offs[1:])):
                _store_t(ot_ref, res_t[off:end], tb, s * sub)
                if i in otp_refs:
                    otp_refs[i][0, rows, :] = res_t[off:end].T.astype(otp_refs[i].dtype)
            if gate_rows:
                off = t_offs[len(t_blocks)]
                gto_ref[0, :, rows] = res_t[off:off + gate_rows] + bgt_ref[...]
        if gqa is not None:
            nq = W_B // LANES
            for j in range(nq + 1):
                xn = _head_norm(res[:, j * LANES:(j + 1) * LANES], gq_ref[...] if j < nq else gk_ref[...])
                if j == nq and knt_block is not None:
                    _store_t(knt_out, xn.T, knt_block, s * sub)
                if rope:
                    xn = _rope(xn, cos_ref[rows, :], sin_ref[rows, :])
                if j < nq:
                    q_out[0, rows, j * LANES:(j + 1) * LANES] = (xn * QSCALE).astype(q_out.dtype)
                else:
                    k_out[0, rows, :] = xn.astype(k_out.dtype)


def _proj(x, g_pre, mod, segs, wt_all, tsegs, b_gates, gqa, *, row0, per_batch, tm):
    bsz, t, d = x.shape
    grid = (bsz, t // tm)
    const = lambda b, i: (0, 0)
    in_specs = [pl.BlockSpec((1, tm, d), lambda b, i: (b, i, 0)),
                pl.BlockSpec((1, d), const),
                pl.BlockSpec(mod.shape, const)]
    args = [x, g_pre.reshape(1, d), mod]
    plain_ws = [w for w, _ in segs] + ([gqa["w"]] if gqa is not None else [])
    for w in plain_ws + ([wt_all] if wt_all is not None else []):
        in_specs.append(pl.BlockSpec(w.shape, const))
        args.append(w)
    gate_rows = 0
    t_plain = tuple(len(ts) > 3 and ts[3] for ts in tsegs)
    tsegs = [ts[:3] for ts in tsegs]
    t_sizes = [n for n, _, _ in tsegs]
    if b_gates is not None:
        gate_rows = b_gates.shape[0]
        t_sizes.append(LANES)
        in_specs.append(pl.BlockSpec((gate_rows, 1), const))
        args.append(b_gates.reshape(gate_rows, 1))
    assert sum(t_sizes) == (0 if wt_all is None else wt_all.shape[0])
    if gqa is not None:
        in_specs += [pl.BlockSpec((1, LANES), const)] * 2
        args += [jnp.tile(gqa["g_q"], 2).reshape(1, LANES), jnp.tile(gqa["g_k"], 2).reshape(1, LANES)]
        if gqa["rope"] is not None:
            in_specs += [pl.BlockSpec((tm, LANES), lambda b, i: (i, 0))] * 2
            args += list(gqa["rope"])
    out_specs, out_shape = [], []
    for w, dt in segs:
        n = w.shape[1]
        out_specs.append(pl.BlockSpec((1, tm, n), lambda b, i: (b, i, 0)))
        out_shape.append(jax.ShapeDtypeStruct((bsz, t, n), dt))

    def add_t_out(n, dt, tb):
        if tb is None:
            out_specs.append(pl.BlockSpec((1, n, tm), lambda b, i: (b, 0, i)))
            out_shape.append(jax.ShapeDtypeStruct((bsz, n, t), dt))
        else:
            out_specs.append(pl.BlockSpec((1, tm // tb, n, tb), lambda b, i: (b, i, 0, 0)))
            out_shape.append(jax.ShapeDtypeStruct((bsz, t // tb, n, tb), dt))

    for n, dt, tb in tsegs:
        add_t_out(n, dt, tb)
    for (n, _, _), flag in zip(tsegs, t_plain):
        if flag:
            out_specs.append(pl.BlockSpec((1, tm, n), lambda b, i: (b, i, 0)))
            out_shape.append(jax.ShapeDtypeStruct((bsz, t, n), BF16))
    if b_gates is not None:
        add_t_out(gate_rows, F32, None)
    gqa_static = None
    if gqa is not None:
        out_specs += [pl.BlockSpec((1, tm, W_B), lambda b, i: (b, i, 0)),
                      pl.BlockSpec((1, tm, LANES), lambda b, i: (b, i, 0))]
        out_shape += [jax.ShapeDtypeStruct((bsz, t, W_B), BF16), jax.ShapeDtypeStruct((bsz, t, LANES), BF16)]
        if gqa["knt_block"] is not None:
            add_t_out(LANES, F32, gqa["knt_block"])
        gqa_static = (gqa["rope"] is not None, gqa["knt_block"])
    kern = functools.partial(_proj_kernel, n_seg=len(plain_ws), t_sizes=tuple(t_sizes),
                             t_blocks=tuple(tb for _, _, tb in tsegs), t_plain=t_plain,
                             gate_rows=gate_rows, gqa=gqa_static,
                             row0=row0, per_batch=per_batch)
    return pl.pallas_call(
        kern, grid=grid, in_specs=in_specs, out_specs=out_specs, out_shape=out_shape,
        compiler_params=_cparams(("arbitrary", "arbitrary")), name="in_proj",
    )(*args)


def _outproj_kernel(*refs, n_in, row0, per_batch):
    a_refs = refs[:n_in]
    z_ref, w_ref, x_ref, gp_ref, mod_ref, o_ref = refs[n_in:n_in + 6]
    d = x_ref.shape[-1]
    row = row0 + (pl.program_id(0) if per_batch else 0)
    gate = mod_ref[pl.ds(row, 1), 2 * d:3 * d]
    z = z_ref[0].astype(F32)
    sz = _silu(z)
    acc = None
    off = 0
    for a_ref in a_refs:
        kk = a_ref.shape[-1]
        y = (a_ref[0].astype(F32) * sz[:, off:off + kk]).astype(BF16)
        part = jnp.dot(y, w_ref[off:off + kk, :], preferred_element_type=F32)
        acc = part if acc is None else acc + part
        off += kk
    r = lax.rsqrt(jnp.mean(acc * acc, axis=-1, keepdims=True) + EPS)
    o_ref[0] = x_ref[0] + gate * (acc * r * gp_ref[...])


def _outproj(parts, z, w_out, x, g_post, mod, *, row0, per_batch, tm):
    bsz, t, d = x.shape
    grid = (bsz, t // tm)
    in_specs, args = [], []
    for a in parts:
        in_specs.append(pl.BlockSpec((1, tm, a.shape[-1]), lambda b, i: (b, i, 0)))
        args.append(a)
    in_specs += [pl.BlockSpec((1, tm, z.shape[-1]), lambda b, i: (b, i, 0)),
                 pl.BlockSpec(w_out.shape, lambda b, i: (0, 0)),
                 pl.BlockSpec((1, tm, d), lambda b, i: (b, i, 0)),
                 pl.BlockSpec((1, d), lambda b, i: (0, 0)),
                 pl.BlockSpec(mod.shape, lambda b, i: (0, 0))]
    args += [z, w_out, x, g_post.reshape(1, d), mod]
    kern = functools.partial(_outproj_kernel, n_in=len(parts), row0=row0, per_batch=per_batch)
    return pl.pallas_call(
        kern, grid=grid, in_specs=in_specs,
        out_specs=pl.BlockSpec((1, tm, d), lambda b, i: (b, i, 0)),
        out_shape=jax.ShapeDtypeStruct((bsz, t, d), F32),
        compiler_params=_cparams(("arbitrary", "arbitrary")), name="out_proj",
    )(*args)


def _split3_bf16(x):
    hi = x.astype(BF16)
    r1 = x - hi.astype(F32)
    mid = r1.astype(BF16)
    lo = (r1 - mid.astype(F32)).astype(BF16)
    return hi, mid, lo


def _mlstm_gate_rows(ig_f, fg_f, ig_b, fg_b):
    rr, ll = fg_f.shape
    lf = _log_sigmoid(jnp.concatenate([fg_f, fg_b], axis=0))
    pieces = jnp.concatenate(_split3_bf16(lf), axis=0)
    u = lax.broadcasted_iota(jnp.int32, (ll, ll), 0)
    t = lax.broadcasted_iota(jnp.int32, (ll, ll), 1)
    out = []
    for d, ig in enumerate((ig_f, ig_b)):
        tri = jnp.where((u >= t) if d else (u <= t), 1.0, 0.0).astype(BF16)
        y = jnp.dot(pieces, tri, preferred_element_type=F32)
        rows = slice(d * rr, (d + 1) * rr)
        b = y[0:2 * rr][rows] + y[2 * rr:4 * rr][rows] + y[4 * rr:6 * rr][rows]
        c = ig - b
        cmax = jnp.broadcast_to(jnp.max(c, axis=-1, keepdims=True), c.shape)
        tot = jnp.broadcast_to(jnp.sum(lf[rows], axis=-1, keepdims=True), c.shape)
        out.append((c, b, cmax, tot))
    return out


def _mlstm_kernel(*refs, nc, hps, unroll, has_init, emit_state):
    q_ref, k_ref, vt_ref, ot_ref, gtt_ref, ghn_ref = refs[:6]
    pos = 6
    if has_init:
        c0_ref, n0_ref, m0_ref = refs[pos:pos + 3]
        pos += 3
    ha_ref = refs[pos]
    pos += 1
    if emit_state:
        cout_ref, nout_ref, mout_ref = refs[pos:pos + 3]
        pos += 3
    hs_sc, gate_sc, c_sc, n_sc, m_sc = refs[pos:pos + 5]
    ll, dh = CHUNK_A, DH_A
    kscale = dh ** -0.5
    hd0 = pl.program_id(1) * hps
    chains = [(i, d) for i in range(hps) for d in range(2)]

    for i, d in chains:
        if has_init:
            c_sc[i, d] = c0_ref[0, d, i]
            n_sc[i, d] = n0_ref[0, i, d:d + 1, :]
            m_sc[i, d] = m0_ref[0, i, d:d + 1, :]
        else:
            c_sc[i, d] = jnp.zeros((dh, dh), F32)
            n_sc[i, d] = jnp.zeros((1, dh), F32)
            m_sc[i, d] = jnp.zeros((1, dh), F32)

    def gate_rows(col0):
        return gtt_ref[0, pl.ds(pl.multiple_of((col0 + hd0) * nc, 8), hps * nc), :]

    gates = _mlstm_gate_rows(gate_rows(0), gate_rows(H_A), gate_rows(2 * H_A), gate_rows(3 * H_A))
    for d in range(2):
        for kind in range(4):
            gate_sc[d, kind] = gates[d][kind]

    ghn_t = [jnp.broadcast_to(ghn_ref[:, i * dh:(i + 1) * dh], (ll, dh)).T for i in range(hps)]
    si = lax.broadcasted_iota(jnp.int32, (ll, ll), 0)
    ti = lax.broadcasted_iota(jnp.int32, (ll, ll), 1)

    def run_trip(j, first_touch):
        jobs = []
        for u in range(unroll):
            step = j * unroll + u
            for i, d in chains:
                jobs.append((i, d, step if d == 0 else nc - 1 - step))
        state = {ch: [c_sc[ch], n_sc[ch], m_sc[ch]] for ch in chains}
        hcols = [slice(i * dh, (i + 1) * dh) for i in range(hps)]

        def tok_rows(cidx):
            return pl.ds(pl.multiple_of(cidx * ll, ll), ll)

        rows = []
        for i, d, cidx in jobs:
            c_r, b_r, cmax_r, tot_r = (gate_sc[d, kind, pl.ds(i * nc + cidx, 1), :] for kind in range(4))
            m_st = state[(i, d)][2]
            m_c = jnp.maximum(m_st, cmax_r)
            state[(i, d)][2] = tot_r + m_c
            rows.append((c_r, b_r, m_st, jnp.exp(m_st - m_c), jnp.exp(c_r - m_c) * kscale))
        start = []
        for (i, d, cidx), (_, _, _, a_st, wk) in zip(jobs, rows):
            k = k_ref[0, tok_rows(cidx), hcols[i]]
            vt = vt_ref[0, cidx, hcols[i], :]
            vw = jnp.concatenate([vt.astype(F32) * wk, jnp.broadcast_to(wk, (BF16_ROWS, ll))], axis=0)
            upd = jnp.dot(vw.astype(BF16), k, preferred_element_type=F32)
            c_st, n_st, _ = state[(i, d)]
            start.append((c_st, n_st))
            state[(i, d)][0] = a_st * c_st + upd[:dh]
            state[(i, d)][1] = a_st * n_st + upd[dh:dh + 1]
        prods = []
        for (i, d, cidx), (c_st, n_st) in zip(jobs, start):
            q = q_ref[0, tok_rows(cidx), hcols[i]]
            k = k_ref[0, tok_rows(cidx), hcols[i]]
            n16 = jnp.broadcast_to(n_st.astype(BF16), (BF16_ROWS, dh))
            prods.append(_dot_nt(jnp.concatenate([k, c_st.astype(BF16), n16], axis=0), q))
        for (i, d, cidx), (c_r, b_r, m_st, _, _), r in zip(jobs, rows, prods):
            allowed = (si >= ti) if d else (si <= ti)
            cb = jnp.where(allowed, jnp.broadcast_to(c_r, (ll, ll)).T, -jnp.inf)
            big_m = jnp.maximum(m_st, jnp.max(cb, axis=0, keepdims=True))
            p = jnp.exp(cb - (big_m - float(np.log(kscale)))) * r[:ll]
            w_inter = jnp.exp(m_st - big_m)
            vt = vt_ref[0, cidx, hcols[i], :]
            num = jnp.dot(vt, p.astype(BF16), preferred_element_type=F32) - r[ll:ll + dh] * (-w_inter)
            den = w_inter * r[ll + dh:ll + dh + 1] + jnp.sum(p, axis=0, keepdims=True)
            h_t = num / jnp.maximum(jnp.abs(den), jnp.exp(-(b_r + big_m)))
            if first_touch:
                hs_sc[i, cidx] = h_t
            else:
                hsum = h_t + hs_sc[i, cidx]
                rn = lax.rsqrt(jnp.mean(hsum * hsum, axis=0, keepdims=True) + EPS)
                out_t = (hsum * rn * ghn_t[i]) * _sigmoid(ot_ref[0, cidx, hcols[i], :].astype(F32))
                ha_ref[0, tok_rows(cidx), hcols[i]] = out_t.T.astype(ha_ref.dtype)
        for ch in chains:
            c_sc[ch], n_sc[ch], m_sc[ch] = state[ch]

    def make_body(first_touch):
        def body(j, carry):
            run_trip(j, first_touch)
            return carry
        return body

    trips = nc // unroll
    lax.fori_loop(0, trips // 2, make_body(True), 0)
    lax.fori_loop(trips // 2, trips, make_body(False), 0)

    if emit_state:
        for i, d in chains:
            cout_ref[0, 0, d, i] = c_sc[i, d]
            nout_ref[0, i, d:d + 1, :] = n_sc[i, d]
            mout_ref[0, i, d:d + 1, :] = m_sc[i, d]


def _mlstm(qk, vt, ot, gates_t, g_hn, init, *, emit_state):
    bsz, t, _ = qk.shape
    nc = t // CHUNK_A
    hps, unroll = (H_A, 1) if nc < 8 else (1, 4)
    assert (nc // 2) % unroll == 0 and nc % 2 == 0
    gtt = gates_t.reshape(bsz, 4 * H_A * nc, CHUNK_A)
    wh = hps * DH_A

    def tblk():
        return pl.BlockSpec((1, nc, wh, CHUNK_A), lambda b, g: (b, 0, g, 0))

    in_specs = [pl.BlockSpec((1, t, wh), lambda b, g: (b, 0, g)),
                pl.BlockSpec((1, t, wh), lambda b, g: (b, 0, H_A // hps + g)),
                tblk(), tblk(),
                pl.BlockSpec((1, 4 * H_A * nc, CHUNK_A), lambda b, g: (b, 0, 0)),
                pl.BlockSpec((1, wh), lambda b, g: (0, g))]
    args = [qk, qk, vt, ot, gtt, g_hn.reshape(1, W_A)]
    if init is not None:
        c0, n0, m0 = init
        in_specs += [pl.BlockSpec((1, 2, hps, DH_A, DH_A), lambda b, g: (b, 0, g, 0, 0)),
                     pl.BlockSpec((1, hps, 2, DH_A), lambda b, g: (b, g, 0, 0)),
                     pl.BlockSpec((1, hps, 2, DH_A), lambda b, g: (b, g, 0, 0))]
        args += [c0, n0, m0]
    out_specs = [pl.BlockSpec((1, t, wh), lambda b, g: (b, 0, g))]
    out_shape = [jax.ShapeDtypeStruct((bsz, t, W_A), BF16)]
    if emit_state:
        out_specs += [pl.BlockSpec((1, 1, 2, hps, DH_A, DH_A), lambda b, g: (b, 0, 0, g, 0, 0)),
                      pl.BlockSpec((1, hps, 2, DH_A), lambda b, g: (b, g, 0, 0)),
                      pl.BlockSpec((1, hps, 2, DH_A), lambda b, g: (b, g, 0, 0))]
        out_shape += [jax.ShapeDtypeStruct((bsz, 1, 2, H_A, DH_A, DH_A), F32),
                      jax.ShapeDtypeStruct((bsz, H_A, 2, DH_A), F32),
                      jax.ShapeDtypeStruct((bsz, H_A, 2, DH_A), F32)]
    kern = functools.partial(_mlstm_kernel, nc=nc, hps=hps, unroll=unroll, has_init=init is not None,
                             emit_state=emit_state)
    return pl.pallas_call(
        kern, grid=(bsz, H_A // hps), in_specs=in_specs, out_specs=out_specs, out_shape=out_shape,
        scratch_shapes=[pltpu.VMEM((hps, nc, DH_A, CHUNK_A), F32), pltpu.VMEM((2, 4, hps * nc, CHUNK_A), F32),
                        pltpu.VMEM((hps, 2, DH_A, DH_A), F32), pltpu.VMEM((hps, 2, 1, DH_A), F32),
                        pltpu.VMEM((hps, 2, 1, DH_A), F32)],
        compiler_params=_cparams(("arbitrary", "arbitrary")), name="mlstm",
    )(*args)


def _rope_tables(n_tok):
    t = np.arange(n_tok)
    row = (t // GRID_W).astype(np.float32)
    colp = (t % GRID_W).astype(np.float32)
    quarter = DH_B // 4
    freqs = (np.float32(ROPE_THETA) ** (-np.arange(quarter, dtype=np.float32) / np.float32(quarter))).astype(np.float32)
    ar = row[:, None] * freqs
    ac = colp[:, None] * freqs
    cos = np.concatenate([np.cos(ar), np.cos(ar), np.cos(ac), np.cos(ac)], axis=-1)
    sin = np.concatenate([-np.sin(ar), np.sin(ar), -np.sin(ac), np.sin(ac)], axis=-1)
    return jnp.asarray(np.tile(cos, (1, 2)), F32), jnp.asarray(np.tile(sin, (1, 2)), F32)


def _split_heads_q(q):
    lo = lax.broadcasted_iota(jnp.int32, q.shape, 1) < (LANES // 2)
    zero = jnp.zeros_like(q)
    return jnp.concatenate([jnp.where(lo, q, zero), jnp.where(lo, zero, q)], axis=0)


def _merge_heads_o(o_t, tq):
    half = LANES // 2
    return jnp.concatenate([o_t[:half, :tq], o_t[half:, tq:]], axis=0).T


def _with_ones_rows(vt):
    return jnp.concatenate([vt, jnp.ones((BF16_ROWS, vt.shape[1]), BF16)], axis=0)


def _online_softmax_pv_t(problems):
    seq = [(pi, ci) for pi, (_, chunks, _) in enumerate(problems) for ci in range(len(chunks))]
    scores = {}

    def issue(t):
        pi, ci = seq[t]
        qm, chunks, _ = problems[pi]
        scores[(pi, ci)] = _dot_nt(chunks[ci][0], qm)

    for t in range(min(QK_AHEAD, len(seq))):
        issue(t)
    m = acc = None
    for t, (pi, ci) in enumerate(seq):
        if t + QK_AHEAD < len(seq):
            issue(t + QK_AHEAD)
        _, chunks, emit = problems[pi]
        _, vt, penalty = chunks[ci]
        s = scores.pop((pi, ci))
        if penalty is not None:
            s = s - penalty
        mc = jnp.max(s, axis=0, keepdims=True)
        m_new = mc if ci == 0 else jnp.maximum(m, mc)
        p = jnp.exp2(s - m_new).astype(BF16)
        part = jnp.dot(_with_ones_rows(vt), p, preferred_element_type=F32)
        acc = part if ci == 0 else jnp.exp2(m - m_new) * acc + part
        m = m_new
        if ci == len(chunks) - 1:
            dv = vt.shape[0]
            emit(acc[:dv] / acc[dv:dv + 1])


def _attn_kernel(*refs, n_tiles, kv_shared, has_cache):
    q_ref, k_ref, vt_ref = refs[:3]
    pos = 3
    if has_cache:
        kc_ref, vtc_ref = refs[pos:pos + 2]
        pos += 2
    o_ref = refs[pos]
    tq = q_ref.shape[1]
    s_len = k_ref.shape[1]
    sc = min(s_len, KEY_CHUNK)
    problems = []
    for j in range(n_tiles):
        kj = 0 if kv_shared else j
        kcols = slice(kj * LANES, (kj + 1) * LANES)
        qm = _split_heads_q(q_ref[0, :, j * LANES:(j + 1) * LANES])
        chunks = [(k_ref[0, c * sc:(c + 1) * sc, kcols].astype(BF16),
                   vt_ref[0, kcols, c * sc:(c + 1) * sc].astype(BF16), None) for c in range(s_len // sc)]
        if has_cache:
            chunks.append((kc_ref[0], vtc_ref[0], None))

        def emit(o_t, j=j):
            o_ref[0, :, j * LANES:(j + 1) * LANES] = _merge_heads_o(o_t, tq).astype(o_ref.dtype)

        problems.append((qm, chunks, emit))
    _online_softmax_pv_t(problems)


def _attention(q, k, vt, cache, *, kv_shared, tq, n_tiles):
    bsz, t, w = q.shape
    s = k.shape[1]
    wt = n_tiles * LANES
    if kv_shared:
        k_spec = pl.BlockSpec((1, s, LANES), lambda b, i, g: (b, 0, 0))
        vt_spec = pl.BlockSpec((1, LANES, s), lambda b, i, g: (b, 0, 0))
    else:
        k_spec = pl.BlockSpec((1, s, wt), lambda b, i, g: (b, 0, g))
        vt_spec = pl.BlockSpec((1, wt, s), lambda b, i, g: (b, g, 0))
    in_specs = [pl.BlockSpec((1, tq, wt), lambda b, i, g: (b, i, g)), k_spec, vt_spec]
    args = [q, k, vt]
    if cache is not None:
        p = cache[0].shape[1]
        in_specs += [pl.BlockSpec((1, p, LANES), lambda b, i, g: (b, 0, 0)),
                     pl.BlockSpec((1, LANES, p), lambda b, i, g: (b, 0, 0))]
        args += list(cache)
    kern = functools.partial(_attn_kernel, n_tiles=n_tiles, kv_shared=kv_shared,
                             has_cache=cache is not None)
    return pl.pallas_call(
        kern, grid=(bsz, t // tq, w // wt), in_specs=in_specs,
        out_specs=pl.BlockSpec((1, tq, wt), lambda b, i, g: (b, i, g)),
        out_shape=jax.ShapeDtypeStruct((bsz, t, w), BF16),
        compiler_params=_cparams(("arbitrary", "arbitrary", "arbitrary")), name="attention",
    )(*args)


def _na_bias_blocks(variant):
    out = {}
    for qr in range(NA_ROWS):
        for kr in range(NA_WIN):
            if variant == 0:
                dr = kr - qr if kr < WIN_R else None
            elif variant == 1:
                dr = kr - qr - WIN_R // 2 if qr <= kr < qr + WIN_R else None
            else:
                dr = kr - qr - (NA_WIN - NA_ROWS) if kr >= NA_WIN - WIN_R else None
            out[(qr, kr)] = None if dr is None else dr + WIN_R - 1
    return out


def _na_kernel(q_ref, k_ref, vt_ref, kc_ref, vtc_ref, bc_ref, o_ref, bias_sc, *, n_rb):
    w = GRID_W
    tq = NA_ROWS * w

    @pl.when(jnp.logical_and(pl.program_id(1) == 0, pl.program_id(2) == 0))
    def _():
        for variant in range(3):
            for (qr, kr), di in _na_bias_blocks(variant).items():
                for hh in range(2 * NA_TILES):
                    val = jnp.full((w, w), MASKED, F32) if di is None else bc_ref[hh, di]
                    bias_sc[variant, kr * w:(kr + 1) * w, hh * tq + qr * w:hh * tq + (qr + 1) * w] = val

    n_rows = n_rb * NA_ROWS
    problems = []
    for rr in range(NA_RBS):
        rb = pl.program_id(2) * NA_RBS + rr
        variant = jnp.where(rb == 0, 0, jnp.where(rb == n_rb - 1, 2, 1))
        ws = jnp.clip(rb * NA_ROWS - WIN_R // 2, 0, n_rows - NA_WIN)
        blk0 = ws // NA_ROWS
        start = ws * w
        qrows = slice(rr * tq, (rr + 1) * tq)
        for j in range(NA_TILES):
            cols = slice(j * LANES, (j + 1) * LANES)
            qcols = slice(j * 2 * tq, (j + 1) * 2 * tq)
            qm = _split_heads_q(q_ref[0, qrows, cols])
            chunks = [(k_ref[0, pl.ds(pl.multiple_of(start + i * tq, tq), tq), cols],
                       vt_ref[0, blk0 + i, cols, :],
                       bias_sc[variant, i * tq:(i + 1) * tq, qcols]) for i in range(NA_BLKS)]
            chunks.append((kc_ref[0, :, cols], vtc_ref[0, cols, :], None))

            def emit(o_t, qrows=qrows, cols=cols):
                o_ref[0, qrows, cols] = _merge_heads_o(o_t, tq).astype(o_ref.dtype)

            problems.append((qm, chunks, emit))
    _online_softmax_pv_t(problems)


def _na_bias_table(rpb):
    c = np.arange(GRID_W)
    cs = np.clip(c - WIN_C // 2, 0, GRID_W - WIN_C)
    ck = np.arange(GRID_W)
    valid = (ck[:, None] >= cs[None, :]) & (ck[:, None] < cs[None, :] + WIN_C)
    idx = np.clip(ck[:, None] - c[None, :] + WIN_C - 1, 0, 2 * WIN_C - 2)
    onehot = (idx[..., None] == np.arange(2 * WIN_C - 1)).astype(np.float32)
    tab = jnp.einsum('hrx,kcx->hrkc', rpb, jnp.asarray(onehot), precision=lax.Precision.HIGHEST)
    return jnp.where(jnp.asarray(valid), tab * -LOG2E, MASKED)


def _na_attention(q, k, vt, kc, vtc, rpb):
    bsz, t, w = q.shape
    p = kc.shape[1]
    tq = NA_ROWS * GRID_W
    n_rb = t // tq
    bc = _na_bias_table(rpb)
    wt = NA_TILES * LANES
    tqs = NA_RBS * tq
    return pl.pallas_call(
        functools.partial(_na_kernel, n_rb=n_rb),
        grid=(w // wt, bsz, n_rb // NA_RBS),
        in_specs=[pl.BlockSpec((1, tqs, wt), lambda j, b, r: (b, r, j)),
                  pl.BlockSpec((1, t, wt), lambda j, b, r: (b, 0, j)),
                  pl.BlockSpec((1, n_rb, wt, tq), lambda j, b, r: (b, 0, j, 0)),
                  pl.BlockSpec((1, p, wt), lambda j, b, r: (b, 0, j)),
                  pl.BlockSpec((1, wt, p), lambda j, b, r: (b, j, 0)),
                  pl.BlockSpec((2 * NA_TILES, 2 * WIN_R - 1, GRID_W, GRID_W), lambda j, b, r: (j, 0, 0, 0))],
        out_specs=pl.BlockSpec((1, tqs, wt), lambda j, b, r: (b, r, j)),
        out_shape=jax.ShapeDtypeStruct((bsz, t, w), BF16),
        scratch_shapes=[pltpu.VMEM((3, NA_WIN * GRID_W, NA_TILES * 2 * tq), F32)],
        compiler_params=_cparams(("arbitrary", "arbitrary", "arbitrary")), name="na_attention",
    )(q, k, vt, kc, vtc, bc)


def _relayout_kernel(w_ref, *o_refs, plans, transposed, axis):
    w = w_ref[...]
    tile = w.shape[1 - axis]
    for o_ref, pieces, tr in zip(o_refs, plans, transposed):
        vals = []
        for src, size, scale in pieces:
            if src is None:
                val = jnp.zeros((size, tile) if axis == 0 else (tile, size), F32)
            else:
                val = lax.slice_in_dim(w, src, src + size, axis=axis)
                if scale != 1.0:
                    val = val * scale
            vals.append(val)
        val = vals[0] if len(vals) == 1 else jnp.concatenate(vals, axis=axis)
        o_ref[...] = (val.T if tr else val).astype(o_ref.dtype)


def _relayout(w, plans, transposed, *, axis, tile):
    grid = (w.shape[1 - axis] // tile,)
    full = w.shape[1 - axis]

    def spec(extent, tr):
        along_rows = (axis == 0) != tr
        shape = (extent, tile) if along_rows else (tile, extent)
        return pl.BlockSpec(shape, (lambda i: (0, i)) if along_rows else (lambda i: (i, 0)))

    def shape(extent, tr):
        return (extent, full) if (axis == 0) != tr else (full, extent)

    sizes = [sum(size for _, size, _ in pieces) for pieces in plans]
    return pl.pallas_call(
        functools.partial(_relayout_kernel, plans=plans, transposed=tuple(transposed), axis=axis),
        grid=grid, in_specs=[spec(w.shape[axis], False)],
        out_specs=[spec(n, tr) for n, tr in zip(sizes, transposed)],
        out_shape=[jax.ShapeDtypeStruct(shape(n, tr), BF16) for n, tr in zip(sizes, transposed)],
        compiler_params=_cparams(("arbitrary",)), name="w_relayout",
    )(w)


_GQA_PERM = np.array([0, 4, 1, 5, 2, 6, 3, 7])


def _tok_major(cache):
    b, h, p, dh = cache.shape
    return cache.transpose(0, 2, 1, 3).reshape(b, p, h * dh)


def _feat_major(cache):
    b, h, p, dh = cache.shape
    return cache.transpose(0, 1, 3, 2).reshape(b, h * dh, p)


def _head_major_from_t(x_t, n_heads):
    b, w, t = x_t.shape
    return jnp.swapaxes(x_t.reshape(b, 1, n_heads, w // n_heads, t), -1, -2)


def _even_layer(xp, xs, mod, g_pre, g_post, w_in, b_gates, g_hn, g_q, g_k, w_out, st_c, st_n, st_m,
                ck, cv, rope_tabs):
    o_v, o_o = 2 * W_A, 3 * W_A
    o_g = 4 * W_A
    o_qb = o_g + 4 * H_A
    o_kb = o_qb + W_B
    o_vb = o_kb + HKV_B * DH_B
    o_z = o_vb + HKV_B * DH_B
    def heads(src0):
        return [(src0 + int(h) * DH_B, DH_B, 1.0) for h in _GQA_PERM]

    n_g = 4 * H_A
    w_qk, w_gqa, w_z, wt_all = _relayout(
        jnp.swapaxes(w_in, 0, 1),
        [[(0, o_v, 1.0)],
         heads(o_qb) + [(o_kb, HKV_B * DH_B, 1.0)],
         [(o_z, W_A, 1.0)] + heads(o_z + W_A),
         [(o_v, 2 * W_A, 1.0), (o_vb, HKV_B * DH_B, 1.0), (o_g, n_g, 1.0), (None, LANES - n_g, 1.0)]],
        [True, True, True, False], axis=0, tile=256)
    (w_o,) = _relayout(w_out, [[(0, W_A, 1.0)] + heads(W_A)], [False], axis=0, tile=256)

    def stream(x, row0, per_batch, init, cache, rope, emit):
        bsz, t, _ = x.shape
        tm = 512
        xf = x if per_batch else x.reshape(1, bsz * t, D_MODEL)
        n_vb = HKV_B * DH_B
        vb_seg = (n_vb, BF16, None) if per_batch else (n_vb, F32, t)
        gqa = dict(w=w_gqa, g_q=g_q, g_k=g_k, rope=rope, knt_block=t if emit else None)
        outs = _proj(xf, g_pre, mod, [(w_qk, BF16), (w_z, BF16)], wt_all,
                     [(W_A, BF16, CHUNK_A), (W_A, BF16, CHUNK_A), vb_seg], b_gates, gqa,
                     row0=row0, per_batch=per_batch, tm=tm)
        qk, z, vta, ota, vbt, gtt, qn, kn = outs[:8]
        if not per_batch:
            qk, z, qn, kn = (a.reshape(bsz, t, a.shape[-1]) for a in (qk, z, qn, kn))
            gtt = gtt.reshape(4 * H_A, bsz, t).transpose(1, 0, 2)
            vbt = vbt.reshape(bsz, HKV_B * DH_B, t)
            vta, ota = (a.reshape(bsz, t // CHUNK_A, W_A, CHUNK_A) for a in (vta, ota))
        res = _mlstm(qk, vta, ota, gtt, g_hn, init, emit_state=emit)
        ha = res[0]
        if per_batch:
            hb = _attention(qn, kn, vbt, cache, kv_shared=True, tq=512, n_tiles=4)
        else:
            hb = _attention(qn, kn, vbt, cache, kv_shared=True, tq=t, n_tiles=W_B // LANES)
        y = _outproj([ha.reshape(xf.shape[0], -1, W_A), hb.reshape(xf.shape[0], -1, W_B)],
                     z.reshape(xf.shape[0], -1, W_A + W_B), w_o, xf, g_post, mod,
                     row0=row0, per_batch=per_batch, tm=tm)
        knt = outs[8].reshape(bsz, HKV_B * DH_B, t) if emit else None
        return y.reshape(bsz, t, D_MODEL), res[1:], knt, vbt

    yp, st, knt_p, vbt_p = stream(xp, 0, False, None, None, None, True)
    n0 = st_n.transpose(0, 2, 1, 3)
    m0 = jnp.broadcast_to(st_m.transpose(0, 2, 1)[..., None], n0.shape)
    cache = (_tok_major(ck).astype(BF16), _feat_major(cv).astype(BF16))
    ys, _, _, _ = stream(xs, 1, True, (st_c, n0, m0), cache, rope_tabs, False)
    c_out, n_out, m_out = st
    new_n = n_out.transpose(0, 2, 1, 3)[:, None]
    new_m = m_out[..., 0].transpose(0, 2, 1)[:, None]
    return (yp, ys, c_out, new_n, new_m, _head_major_from_t(knt_p, HKV_B), _head_major_from_t(vbt_p, HKV_B))


def _odd_layer(xp, xs, mod, g_pre, g_post, w_in, rpb, w_out, ck, cv):
    w_q, w_k, w_z, wt_kv = _relayout(
        w_in, [[(0, W_C, QSCALE)], [(W_C, W_C, 1.0)], [(3 * W_C, W_C, 1.0)], [(W_C, 2 * W_C, 1.0)]],
        [False, False, False, True], axis=1, tile=256)
    (w_o,) = _relayout(w_out, [[(0, W_C, 1.0)]], [False], axis=0, tile=256)
    tm = 512
    bsz, t, _ = xp.shape
    xf = xp.reshape(1, bsz * t, D_MODEL)
    q, z, kt, vt, k = _proj(xf, g_pre, mod, [(w_q, BF16), (w_z, BF16)], wt_kv,
                            [(W_C, F32, t, True), (W_C, F32, t)], None, None, row0=0, per_batch=False, tm=tm)
    kt = kt.reshape(bsz, W_C, t)
    vt = vt.reshape(bsz, W_C, t)
    o = _attention(q.reshape(bsz, t, W_C), k.reshape(bsz, t, W_C), vt, None,
                   kv_shared=False, tq=t, n_tiles=W_C // LANES)
    yp = _outproj([o.reshape(1, bsz * t, W_C)], z, w_o, xf, g_post, mod, row0=0, per_batch=False, tm=tm)
    yp = yp.reshape(bsz, t, D_MODEL)
    q, k, z, vts = _proj(xs, g_pre, mod, [(w_q, BF16), (w_k, BF16), (w_z, BF16)], wt_kv[W_C:],
                         [(W_C, BF16, NA_ROWS * GRID_W)], None, None, row0=1, per_batch=True, tm=tm)
    o = _na_attention(q, k, vts, _tok_major(ck).astype(BF16), _feat_major(cv).astype(BF16), rpb)
    ys = _outproj([o], z, w_o, xs, g_post, mod, row0=1, per_batch=True, tm=tm)
    return yp, ys, _head_major_from_t(kt, H_C), _head_major_from_t(vt, H_C)


def kernel(x_prompt, x_sample, state_mlstm_C, state_mlstm_n, state_mlstm_m, cache_gqa_k, cache_gqa_v,
           cache_na_k, cache_na_v, c, c_ctx, w_mod, b_mod, g_pre, g_post, w_in_ab, b_gates_ab, g_hnorm_a,
           g_qnorm_b, g_knorm_b, w_out_ab, w_in_c, rpb_c, w_out_c):
    depth = w_mod.shape[0]
    assert depth == 2 and c.shape[0] == 2
    cvec = jnp.concatenate([c_ctx[None], c, jnp.zeros((8 - 1 - c.shape[0], D_MODEL), F32)], axis=0)
    mod = _modulation(cvec, w_mod, b_mod)
    rope_tabs = _rope_tables(x_sample.shape[1])
    xp, xs, c_out, n_out, m_out, gk, gv = _even_layer(
        x_prompt, x_sample, mod[0], g_pre[0], g_post[0], w_in_ab[0], b_gates_ab[0], g_hnorm_a[0],
        g_qnorm_b[0], g_knorm_b[0], w_out_ab[0], state_mlstm_C[:, 0], state_mlstm_n[:, 0],
        state_mlstm_m[:, 0], cache_gqa_k[:, 0], cache_gqa_v[:, 0], rope_tabs)
    xp, xs, nk, nv = _odd_layer(xp, xs, mod[1], g_pre[1], g_post[1], w_in_c[0], rpb_c[0], w_out_c[0],
                                cache_na_k[:, 0], cache_na_v[:, 0])
    return (xp, xs, c_out, n_out, m_out, gk, gv, nk, nv)
```

```python
import functools

import jax
import jax.numpy as jnp
import numpy as np
from jax import lax
from jax.experimental import pallas as pl
from jax.experimental.pallas import tpu as pltpu

F32 = jnp.float32
BF16 = jnp.bfloat16

D_MODEL = 1024
GRID_W = 64
EPS = 1e-6
H_A = 4
DH_A = 128
W_A = H_A * DH_A
CHUNK_A = 128
HQ_B = 8
HKV_B = 2
DH_B = 64
W_B = HQ_B * DH_B
ROPE_THETA = 10000.0
H_C = 16
DH_C = 64
W_C = H_C * DH_C
WIN_R = 8
WIN_C = 16

LANES = 128
NA_ROWS = 4
NA_TILES = 2
NA_RBS = 8
NA_BLKS = -(-(NA_ROWS + WIN_R - 1) // NA_ROWS)
NA_WIN = NA_BLKS * NA_ROWS
assert WIN_R // 2 == NA_ROWS
BF16_ROWS = 16
PROJ_SUBTILE = 256
OUTPROJ_TM = 1024
KEY_CHUNK = 256
QK_AHEAD = 3
LOG2E = 1.4426950408889634
QSCALE = DH_B ** -0.5 * LOG2E
assert DH_B == DH_C
MASKED = 1e30
VMEM_LIMIT = 56 * 1024 * 1024


def _cparams(sem):
    return pltpu.CompilerParams(dimension_semantics=sem, vmem_limit_bytes=VMEM_LIMIT)


def _silu(x):
    return x / (1.0 + jnp.exp(-x))


def _sigmoid(x):
    return 1.0 / (1.0 + jnp.exp(-x))


def _log_sigmoid(x):
    return jnp.minimum(x, 0.0) - jnp.log1p(jnp.exp(-jnp.abs(x)))


def _dot_nt(a, b):
    return lax.dot_general(a, b, (((1,), (1,)), ((), ())), preferred_element_type=F32)


MOD_ROWS = 3


def _mod_kernel(c_ref, w_ref, b_ref, o_ref, sb_sc):
    d, tn = w_ref.shape[1], w_ref.shape[2]
    sub = 8

    @pl.when(jnp.logical_and(pl.program_id(0) == 0, pl.program_id(1) == 0))
    def _():
        s = _silu(c_ref[...])
        for r in range(MOD_ROWS):
            sb_sc[r] = jnp.broadcast_to(s[r:r + 1, :], (LANES, d)).T

    def body(g, acc):
        rows = pl.ds(pl.multiple_of(g * sub, sub), sub)
        w = w_ref[0, rows, :]
        out = []
        for r in range(MOD_ROWS):
            sb = sb_sc[r, rows, :]
            out.append([acc[r][j] + sb * w[:, j * LANES:(j + 1) * LANES] for j in range(tn // LANES)])
        return out

    zero = jnp.zeros((sub, LANES), F32)
    acc = lax.fori_loop(0, d // sub, body, [[zero] * (tn // LANES) for _ in range(MOD_ROWS)], unroll=8)
    rows = [jnp.sum(jnp.concatenate(a, axis=1), axis=0, keepdims=True) + b_ref[0] for a in acc]
    o_ref[0] = jnp.concatenate(rows + [jnp.zeros((8 - MOD_ROWS, tn), F32)], axis=0)


def _modulation(cvec, w_mod, b_mod):
    depth, d, n = w_mod.shape
    tn = n // 4
    return pl.pallas_call(
        _mod_kernel,
        grid=(depth, n // tn),
        in_specs=[pl.BlockSpec((8, d), lambda l, j: (0, 0)),
                  pl.BlockSpec((1, d, tn), lambda l, j: (l, 0, j)),
                  pl.BlockSpec((1, 1, tn), lambda l, j: (l, 0, j))],
        out_specs=pl.BlockSpec((1, 8, tn), lambda l, j: (l, 0, j)),
        out_shape=jax.ShapeDtypeStruct((depth, 8, n), F32),
        scratch_shapes=[pltpu.VMEM((MOD_ROWS, d, LANES), F32)],
        compiler_params=_cparams(("arbitrary", "arbitrary")),
        name="modulation",
    )(cvec, w_mod, b_mod.reshape(depth, 1, n))


def _head_norm(x, g):
    lo = lax.broadcasted_iota(jnp.int32, x.shape, 1) < DH_B
    x2 = x * x
    s_lo = jnp.sum(jnp.where(lo, x2, 0.0), axis=-1, keepdims=True)
    s_hi = jnp.sum(jnp.where(lo, 0.0, x2), axis=-1, keepdims=True)
    ms = jnp.where(lo, s_lo, s_hi) * (1.0 / DH_B)
    return x * lax.rsqrt(ms + EPS) * g


def _rope(x, cos, sin):
    quarter = DH_B // 4
    first = (lax.broadcasted_iota(jnp.int32, x.shape, 1) % (2 * quarter)) < quarter
    partner = jnp.where(first, pltpu.roll(x, LANES - quarter, 1), pltpu.roll(x, quarter, 1))
    return x * cos + partner * sin


def _store_t(ref, val, t_block, tok0):
    n_tok = val.shape[1]
    if t_block is None:
        ref[0, :, tok0:tok0 + n_tok] = val.astype(ref.dtype)
    else:
        for i in range(n_tok // t_block):
            ref[0, tok0 // t_block + i] = val[:, i * t_block:(i + 1) * t_block].astype(ref.dtype)


def _proj_kernel(*refs, n_seg, t_sizes, t_blocks, t_plain, gate_rows, gqa, row0, per_batch):
    x_ref, g_ref, mod_ref = refs[:3]
    pos = 3
    w_refs = refs[pos:pos + n_seg]
    pos += n_seg
    if t_sizes:
        wt_ref = refs[pos]
        pos += 1
    if gate_rows:
        bgt_ref = refs[pos]
        pos += 1
    if gqa is not None:
        rope, knt_block = gqa
        gq_ref, gk_ref = refs[pos:pos + 2]
        pos += 2
        if rope:
            cos_ref, sin_ref = refs[pos:pos + 2]
            pos += 2
    n_plain = n_seg - (1 if gqa is not None else 0)
    o_refs = refs[pos:pos + n_plain]
    pos += n_plain
    ot_refs = refs[pos:pos + len(t_blocks)]
    pos += len(t_blocks)
    otp_refs = {}
    for i, flag in enumerate(t_plain):
        if flag:
            otp_refs[i] = refs[pos]
            pos += 1
    if gate_rows:
        gto_ref = refs[pos]
        pos += 1
    if gqa is not None:
        q_out, k_out = refs[pos:pos + 2]
        pos += 2
        if knt_block is not None:
            knt_out = refs[pos]
            pos += 1
    t_offs = [0]
    for n in t_sizes:
        t_offs.append(t_offs[-1] + n)

    d = x_ref.shape[-1]
    tm = x_ref.shape[1]
    row = row0 + (pl.program_id(0) if per_batch else 0)
    shift = mod_ref[pl.ds(row, 1), 0:d]
    scale = mod_ref[pl.ds(row, 1), d:2 * d]
    sub = PROJ_SUBTILE
    for s in range(tm // sub):
        rows = slice(s * sub, (s + 1) * sub)
        x = x_ref[0, rows, :]
        r = lax.rsqrt(jnp.mean(x * x, axis=-1, keepdims=True) + EPS)
        h = (x * r * g_ref[...]) * (1.0 + scale) + shift
        hb = h.astype(BF16)
        if gqa is not None:
            res = jnp.dot(hb, w_refs[-1][...], preferred_element_type=F32)
        for w_ref, o_ref in zip(w_refs[:n_plain], o_refs):
            o_ref[0, rows, :] = jnp.dot(hb, w_ref[...], preferred_element_type=F32).astype(o_ref.dtype)
        if t_sizes:
            res_t = _dot_nt(wt_ref[...], hb)
            for i, (ot_ref, tb, off, end) in enumerate(zip(ot_refs, t_blocks, t_offs, t_offs[1:])):
                _store_t(ot_ref, res_t[off:end], tb, s * sub)
                if i in otp_refs:
                    otp_refs[i][0, rows, :] = res_t[off:end].T.astype(otp_refs[i].dtype)
            if gate_rows:
                off = t_offs[len(t_blocks)]
                gto_ref[0, :, rows] = res_t[off:off + gate_rows] + bgt_ref[...]
        if gqa is not None:
            nq = W_B // LANES
            for j in range(nq + 1):
                xn = _head_norm(res[:, j * LANES:(j + 1) * LANES], gq_ref[...] if j < nq else gk_ref[...])
                if j == nq and knt_block is not None:
                    _store_t(knt_out, xn.T, knt_block, s * sub)
                if rope:
                    xn = _rope(xn, cos_ref[rows, :], sin_ref[rows, :])
                if j < nq:
                    q_out[0, rows, j * LANES:(j + 1) * LANES] = (xn * QSCALE).astype(q_out.dtype)
                else:
                    k_out[0, rows, :] = xn.astype(k_out.dtype)


def _proj(x, g_pre, mod, segs, wt_all, tsegs, b_gates, gqa, *, row0, per_batch, tm):
    bsz, t, d = x.shape
    grid = (bsz, t // tm)
    const = lambda b, i: (0, 0)
    in_specs = [pl.BlockSpec((1, tm, d), lambda b, i: (b, i, 0)),
                pl.BlockSpec((1, d), const),
                pl.BlockSpec(mod.shape, const)]
    args = [x, g_pre.reshape(1, d), mod]
    plain_ws = [w for w, _ in segs] + ([gqa["w"]] if gqa is not None else [])
    for w in plain_ws + ([wt_all] if wt_all is not None else []):
        in_specs.append(pl.BlockSpec(w.shape, const))
        args.append(w)
    gate_rows = 0
    t_plain = tuple(len(ts) > 3 and ts[3] for ts in tsegs)
    tsegs = [ts[:3] for ts in tsegs]
    t_sizes = [n for n, _, _ in tsegs]
    if b_gates is not None:
        gate_rows = b_gates.shape[0]
        t_sizes.append(LANES)
        in_specs.append(pl.BlockSpec((gate_rows, 1), const))
        args.append(b_gates.reshape(gate_rows, 1))
    assert sum(t_sizes) == (0 if wt_all is None else wt_all.shape[0])
    if gqa is not None:
        in_specs += [pl.BlockSpec((1, LANES), const)] * 2
        args += [jnp.tile(gqa["g_q"], 2).reshape(1, LANES), jnp.tile(gqa["g_k"], 2).reshape(1, LANES)]
        if gqa["rope"] is not None:
            in_specs += [pl.BlockSpec((tm, LANES), lambda b, i: (i, 0))] * 2
            args += list(gqa["rope"])
    out_specs, out_shape = [], []
    for w, dt in segs:
        n = w.shape[1]
        out_specs.append(pl.BlockSpec((1, tm, n), lambda b, i: (b, i, 0)))
        out_shape.append(jax.ShapeDtypeStruct((bsz, t, n), dt))

    def add_t_out(n, dt, tb):
        if tb is None:
            out_specs.append(pl.BlockSpec((1, n, tm), lambda b, i: (b, 0, i)))
            out_shape.append(jax.ShapeDtypeStruct((bsz, n, t), dt))
        else:
            out_specs.append(pl.BlockSpec((1, tm // tb, n, tb), lambda b, i: (b, i, 0, 0)))
            out_shape.append(jax.ShapeDtypeStruct((bsz, t // tb, n, tb), dt))

    for n, dt, tb in tsegs:
        add_t_out(n, dt, tb)
    for (n, _, _), flag in zip(tsegs, t_plain):
        if flag:
            out_specs.append(pl.BlockSpec((1, tm, n), lambda b, i: (b, i, 0)))
            out_shape.append(jax.ShapeDtypeStruct((bsz, t, n), BF16))
    if b_gates is not None:
        add_t_out(gate_rows, F32, None)
    gqa_static = None
    if gqa is not None:
        out_specs += [pl.BlockSpec((1, tm, W_B), lambda b, i: (b, i, 0)),
                      pl.BlockSpec((1, tm, LANES), lambda b, i: (b, i, 0))]
        out_shape += [jax.ShapeDtypeStruct((bsz, t, W_B), BF16), jax.ShapeDtypeStruct((bsz, t, LANES), BF16)]
        if gqa["knt_block"] is not None:
            add_t_out(LANES, F32, gqa["knt_block"])
        gqa_static = (gqa["rope"] is not None, gqa["knt_block"])
    kern = functools.partial(_proj_kernel, n_seg=len(plain_ws), t_sizes=tuple(t_sizes),
                             t_blocks=tuple(tb for _, _, tb in tsegs), t_plain=t_plain,
                             gate_rows=gate_rows, gqa=gqa_static,
                             row0=row0, per_batch=per_batch)
    return pl.pallas_call(
        kern, grid=grid, in_specs=in_specs, out_specs=out_specs, out_shape=out_shape,
        compiler_params=_cparams(("arbitrary", "arbitrary")), name="in_proj",
    )(*args)


def _outproj_kernel(*refs, n_in, row0, per_batch):
    a_refs = refs[:n_in]
    z_ref, w_ref, x_ref, gp_ref, mod_ref, o_ref = refs[n_in:n_in + 6]
    d = x_ref.shape[-1]
    row = row0 + (pl.program_id(0) if per_batch else 0)
    gate = mod_ref[pl.ds(row, 1), 2 * d:3 * d]
    z = z_ref[0].astype(F32)
    sz = _silu(z)
    acc = None
    off = 0
    for a_ref in a_refs:
        kk = a_ref.shape[-1]
        y = (a_ref[0].astype(F32) * sz[:, off:off + kk]).astype(BF16)
        part = jnp.dot(y, w_ref[off:off + kk, :], preferred_element_type=F32)
        acc = part if acc is None else acc + part
        off += kk
    r = lax.rsqrt(jnp.mean(acc * acc, axis=-1, keepdims=True) + EPS)
    o_ref[0] = x_ref[0] + gate * (acc * r * gp_ref[...])


def _outproj(parts, z, w_out, x, g_post, mod, *, row0, per_batch, tm):
    bsz, t, d = x.shape
    grid = (bsz, t // tm)
    in_specs, args = [], []
    for a in parts:
        in_specs.append(pl.BlockSpec((1, tm, a.shape[-1]), lambda b, i: (b, i, 0)))
        args.append(a)
    in_specs += [pl.BlockSpec((1, tm, z.shape[-1]), lambda b, i: (b, i, 0)),
                 pl.BlockSpec(w_out.shape, lambda b, i: (0, 0)),
                 pl.BlockSpec((1, tm, d), lambda b, i: (b, i, 0)),
                 pl.BlockSpec((1, d), lambda b, i: (0, 0)),
                 pl.BlockSpec(mod.shape, lambda b, i: (0, 0))]
    args += [z, w_out, x, g_post.reshape(1, d), mod]
    kern = functools.partial(_outproj_kernel, n_in=len(parts), row0=row0, per_batch=per_batch)
    return pl.pallas_call(
        kern, grid=grid, in_specs=in_specs,
        out_specs=pl.BlockSpec((1, tm, d), lambda b, i: (b, i, 0)),
        out_shape=jax.ShapeDtypeStruct((bsz, t, d), F32),
        compiler_params=_cparams(("arbitrary", "arbitrary")), name="out_proj",
    )(*args)


def _split3_bf16(x):
    hi = x.astype(BF16)
    r1 = x - hi.astype(F32)
    mid = r1.astype(BF16)
    lo = (r1 - mid.astype(F32)).astype(BF16)
    return hi, mid, lo


def _mlstm_gate_rows(ig_f, fg_f, ig_b, fg_b):
    rr, ll = fg_f.shape
    lf = _log_sigmoid(jnp.concatenate([fg_f, fg_b], axis=0))
    pieces = jnp.concatenate(_split3_bf16(lf), axis=0)
    u = lax.broadcasted_iota(jnp.int32, (ll, ll), 0)
    t = lax.broadcasted_iota(jnp.int32, (ll, ll), 1)
    out = []
    for d, ig in enumerate((ig_f, ig_b)):
        tri = jnp.where((u >= t) if d else (u <= t), 1.0, 0.0).astype(BF16)
        y = jnp.dot(pieces, tri, preferred_element_type=F32)
        rows = slice(d * rr, (d + 1) * rr)
        b = y[0:2 * rr][rows] + y[2 * rr:4 * rr][rows] + y[4 * rr:6 * rr][rows]
        c = ig - b
        cmax = jnp.broadcast_to(jnp.max(c, axis=-1, keepdims=True), c.shape)
        tot = jnp.broadcast_to(jnp.sum(lf[rows], axis=-1, keepdims=True), c.shape)
        out.append((c, b, cmax, tot))
    return out


def _mlstm_kernel(*refs, nc, hps, unroll, has_init, emit_state):
    q_ref, k_ref, vt_ref, ot_ref, gtt_ref, ghn_ref = refs[:6]
    pos = 6
    if has_init:
        c0_ref, n0_ref, m0_ref = refs[pos:pos + 3]
        pos += 3
    ha_ref = refs[pos]
    pos += 1
    if emit_state:
        cout_ref, nout_ref, mout_ref = refs[pos:pos + 3]
        pos += 3
    hs_sc, gate_sc, c_sc, n_sc, m_sc = refs[pos:pos + 5]
    ll, dh = CHUNK_A, DH_A
    kscale = dh ** -0.5
    hd0 = pl.program_id(1) * hps
    chains = [(i, d) for i in range(hps) for d in range(2)]

    for i, d in chains:
        if has_init:
            c_sc[i, d] = c0_ref[0, d, i]
            n_sc[i, d] = n0_ref[0, i, d:d + 1, :]
            m_sc[i, d] = m0_ref[0, i, d:d + 1, :]
        else:
            c_sc[i, d] = jnp.zeros((dh, dh), F32)
            n_sc[i, d] = jnp.zeros((1, dh), F32)
            m_sc[i, d] = jnp.zeros((1, dh), F32)

    def gate_rows(col0):
        return gtt_ref[0, pl.ds(pl.multiple_of((col0 + hd0) * nc, 8), hps * nc), :]

    gates = _mlstm_gate_rows(gate_rows(0), gate_rows(H_A), gate_rows(2 * H_A), gate_rows(3 * H_A))
    for d in range(2):
        for kind in range(4):
            gate_sc[d, kind] = gates[d][kind]

    ghn_t = [jnp.broadcast_to(ghn_ref[:, i * dh:(i + 1) * dh], (ll, dh)).T for i in range(hps)]
    si = lax.broadcasted_iota(jnp.int32, (ll, ll), 0)
    ti = lax.broadcasted_iota(jnp.int32, (ll, ll), 1)

    def run_trip(j, first_touch):
        jobs = []
        for u in range(unroll):
            step = j * unroll + u
            for i, d in chains:
                jobs.append((i, d, step if d == 0 else nc - 1 - step))
        state = {ch: [c_sc[ch], n_sc[ch], m_sc[ch]] for ch in chains}
        hcols = [slice(i * dh, (i + 1) * dh) for i in range(hps)]

        def tok_rows(cidx):
            return pl.ds(pl.multiple_of(cidx * ll, ll), ll)

        rows = []
        for i, d, cidx in jobs:
            c_r, b_r, cmax_r, tot_r = (gate_sc[d, kind, pl.ds(i * nc + cidx, 1), :] for kind in range(4))
            m_st = state[(i, d)][2]
            m_c = jnp.maximum(m_st, cmax_r)
            state[(i, d)][2] = tot_r + m_c
            rows.append((c_r, b_r, m_st, jnp.exp(m_st - m_c), jnp.exp(c_r - m_c) * kscale))
        start = []
        for (i, d, cidx), (_, _, _, a_st, wk) in zip(jobs, rows):
            k = k_ref[0, tok_rows(cidx), hcols[i]]
            vt = vt_ref[0, cidx, hcols[i], :]
            vw = jnp.concatenate([vt.astype(F32) * wk, jnp.broadcast_to(wk, (BF16_ROWS, ll))], axis=0)
            upd = jnp.dot(vw.astype(BF16), k, preferred_element_type=F32)
            c_st, n_st, _ = state[(i, d)]
            start.append((c_st, n_st))
            state[(i, d)][0] = a_st * c_st + upd[:dh]
            state[(i, d)][1] = a_st * n_st + upd[dh:dh + 1]
        prods = []
        for (i, d, cidx), (c_st, n_st) in zip(jobs, start):
            q = q_ref[0, tok_rows(cidx), hcols[i]]
            k = k_ref[0, tok_rows(cidx), hcols[i]]
            n16 = jnp.broadcast_to(n_st.astype(BF16), (BF16_ROWS, dh))
            prods.append(_dot_nt(jnp.concatenate([k, c_st.astype(BF16), n16], axis=0), q))
        for (i, d, cidx), (c_r, b_r, m_st, _, _), r in zip(jobs, rows, prods):
            allowed = (si >= ti) if d else (si <= ti)
            cb = jnp.where(allowed, jnp.broadcast_to(c_r, (ll, ll)).T, -jnp.inf)
            big_m = jnp.maximum(m_st, jnp.max(cb, axis=0, keepdims=True))
            p = jnp.exp(cb - (big_m - float(np.log(kscale)))) * r[:ll]
            w_inter = jnp.exp(m_st - big_m)
            vt = vt_ref[0, cidx, hcols[i], :]
            num = jnp.dot(vt, p.astype(BF16), preferred_element_type=F32) - r[ll:ll + dh] * (-w_inter)
            den = w_inter * r[ll + dh:ll + dh + 1] + jnp.sum(p, axis=0, keepdims=True)
            h_t = num / jnp.maximum(jnp.abs(den), jnp.exp(-(b_r + big_m)))
            if first_touch:
                hs_sc[i, cidx] = h_t
            else:
                hsum = h_t + hs_sc[i, cidx]
                rn = lax.rsqrt(jnp.mean(hsum * hsum, axis=0, keepdims=True) + EPS)
                out_t = (hsum * rn * ghn_t[i]) * _sigmoid(ot_ref[0, cidx, hcols[i], :].astype(F32))
                ha_ref[0, tok_rows(cidx), hcols[i]] = out_t.T.astype(ha_ref.dtype)
        for ch in chains:
            c_sc[ch], n_sc[ch], m_sc[ch] = state[ch]

    def make_body(first_touch):
        def body(j, carry):
            run_trip(j, first_touch)
            return carry
        return body

    trips = nc // unroll
    lax.fori_loop(0, trips // 2, make_body(True), 0)
    lax.fori_loop(trips // 2, trips, make_body(False), 0)

    if emit_state:
        for i, d in chains:
            cout_ref[0, 0, d, i] = c_sc[i, d]
            nout_ref[0, i, d:d + 1, :] = n_sc[i, d]
            mout_ref[0, i, d:d + 1, :] = m_sc[i, d]


def _mlstm(qk, vt, ot, gates_t, g_hn, init, *, emit_state):
    bsz, t, _ = qk.shape
    nc = t // CHUNK_A
    hps, unroll = (H_A, 1) if nc < 8 else (1, 4)
    assert (nc // 2) % unroll == 0 and nc % 2 == 0
    gtt = gates_t.reshape(bsz, 4 * H_A * nc, CHUNK_A)
    wh = hps * DH_A

    def tblk():
        return pl.BlockSpec((1, nc, wh, CHUNK_A), lambda b, g: (b, 0, g, 0))

    in_specs = [pl.BlockSpec((1, t, wh), lambda b, g: (b, 0, g)),
                pl.BlockSpec((1, t, wh), lambda b, g: (b, 0, H_A // hps + g)),
                tblk(), tblk(),
                pl.BlockSpec((1, 4 * H_A * nc, CHUNK_A), lambda b, g: (b, 0, 0)),
                pl.BlockSpec((1, wh), lambda b, g: (0, g))]
    args = [qk, qk, vt, ot, gtt, g_hn.reshape(1, W_A)]
    if init is not None:
        c0, n0, m0 = init
        in_specs += [pl.BlockSpec((1, 2, hps, DH_A, DH_A), lambda b, g: (b, 0, g, 0, 0)),
                     pl.BlockSpec((1, hps, 2, DH_A), lambda b, g: (b, g, 0, 0)),
                     pl.BlockSpec((1, hps, 2, DH_A), lambda b, g: (b, g, 0, 0))]
        args += [c0, n0, m0]
    out_specs = [pl.BlockSpec((1, t, wh), lambda b, g: (b, 0, g))]
    out_shape = [jax.ShapeDtypeStruct((bsz, t, W_A), BF16)]
    if emit_state:
        out_specs += [pl.BlockSpec((1, 1, 2, hps, DH_A, DH_A), lambda b, g: (b, 0, 0, g, 0, 0)),
                      pl.BlockSpec((1, hps, 2, DH_A), lambda b, g: (b, g, 0, 0)),
                      pl.BlockSpec((1, hps, 2, DH_A), lambda b, g: (b, g, 0, 0))]
        out_shape += [jax.ShapeDtypeStruct((bsz, 1, 2, H_A, DH_A, DH_A), F32),
                      jax.ShapeDtypeStruct((bsz, H_A, 2, DH_A), F32),
                      jax.ShapeDtypeStruct((bsz, H_A, 2, DH_A), F32)]
    kern = functools.partial(_mlstm_kernel, nc=nc, hps=hps, unroll=unroll, has_init=init is not None,
                             emit_state=emit_state)
    return pl.pallas_call(
        kern, grid=(bsz, H_A // hps), in_specs=in_specs, out_specs=out_specs, out_shape=out_shape,
        scratch_shapes=[pltpu.VMEM((hps, nc, DH_A, CHUNK_A), F32), pltpu.VMEM((2, 4, hps * nc, CHUNK_A), F32),
                        pltpu.VMEM((hps, 2, DH_A, DH_A), F32), pltpu.VMEM((hps, 2, 1, DH_A), F32),
                        pltpu.VMEM((hps, 2, 1, DH_A), F32)],
        compiler_params=_cparams(("arbitrary", "arbitrary")), name="mlstm",
    )(*args)


def _rope_tables(n_tok):
    t = np.arange(n_tok)
    row = (t // GRID_W).astype(np.float32)
    colp = (t % GRID_W).astype(np.float32)
    quarter = DH_B // 4
    freqs = (np.float32(ROPE_THETA) ** (-np.arange(quarter, dtype=np.float32) / np.float32(quarter))).astype(np.float32)
    ar = row[:, None] * freqs
    ac = colp[:, None] * freqs
    cos = np.concatenate([np.cos(ar), np.cos(ar), np.cos(ac), np.cos(ac)], axis=-1)
    sin = np.concatenate([-np.sin(ar), np.sin(ar), -np.sin(ac), np.sin(ac)], axis=-1)
    return jnp.asarray(np.tile(cos, (1, 2)), F32), jnp.asarray(np.tile(sin, (1, 2)), F32)


def _split_heads_q(q):
    lo = lax.broadcasted_iota(jnp.int32, q.shape, 1) < (LANES // 2)
    zero = jnp.zeros_like(q)
    return jnp.concatenate([jnp.where(lo, q, zero), jnp.where(lo, zero, q)], axis=0)


def _merge_heads_o(o_t, tq):
    half = LANES // 2
    return jnp.concatenate([o_t[:half, :tq], o_t[half:, tq:]], axis=0).T


def _with_ones_rows(vt):
    return jnp.concatenate([vt, jnp.ones((BF16_ROWS, vt.shape[1]), BF16)], axis=0)


def _online_softmax_pv_t(problems):
    seq = [(pi, ci) for pi, (_, chunks, _) in enumerate(problems) for ci in range(len(chunks))]
    scores = {}

    def issue(t):
        pi, ci = seq[t]
        qm, chunks, _ = problems[pi]
        scores[(pi, ci)] = _dot_nt(chunks[ci][0], qm)

    for t in range(min(QK_AHEAD, len(seq))):
        issue(t)
    m = acc = None
    for t, (pi, ci) in enumerate(seq):
        if t + QK_AHEAD < len(seq):
            issue(t + QK_AHEAD)
        _, chunks, emit = problems[pi]
        _, vt, penalty = chunks[ci]
        s = scores.pop((pi, ci))
        if penalty is not None:
            s = s - penalty
        mc = jnp.max(s, axis=0, keepdims=True)
        m_new = mc if ci == 0 else jnp.maximum(m, mc)
        p = jnp.exp2(s - m_new).astype(BF16)
        part = jnp.dot(_with_ones_rows(vt), p, preferred_element_type=F32)
        acc = part if ci == 0 else jnp.exp2(m - m_new) * acc + part
        m = m_new
        if ci == len(chunks) - 1:
            dv = vt.shape[0]
            emit(acc[:dv] / acc[dv:dv + 1])


def _attn_kernel(*refs, n_tiles, kv_shared, has_cache):
    q_ref, k_ref, vt_ref = refs[:3]
    pos = 3
    if has_cache:
        kc_ref, vtc_ref = refs[pos:pos + 2]
        pos += 2
    o_ref = refs[pos]
    tq = q_ref.shape[1]
    s_len = k_ref.shape[1]
    sc = min(s_len, KEY_CHUNK)
    problems = []
    for j in range(n_tiles):
        kj = 0 if kv_shared else j
        kcols = slice(kj * LANES, (kj + 1) * LANES)
        qm = _split_heads_q(q_ref[0, :, j * LANES:(j + 1) * LANES])
        chunks = [(k_ref[0, c * sc:(c + 1) * sc, kcols].astype(BF16),
                   vt_ref[0, kcols, c * sc:(c + 1) * sc].astype(BF16), None) for c in range(s_len // sc)]
        if has_cache:
            chunks.append((kc_ref[0], vtc_ref[0], None))

        def emit(o_t, j=j):
            o_ref[0, :, j * LANES:(j + 1) * LANES] = _merge_heads_o(o_t, tq).astype(o_ref.dtype)

        problems.append((qm, chunks, emit))
    _online_softmax_pv_t(problems)


def _attention(q, k, vt, cache, *, kv_shared, tq, n_tiles):
    bsz, t, w = q.shape
    s = k.shape[1]
    wt = n_tiles * LANES
    if kv_shared:
        k_spec = pl.BlockSpec((1, s, LANES), lambda b, i, g: (b, 0, 0))
        vt_spec = pl.BlockSpec((1, LANES, s), lambda b, i, g: (b, 0, 0))
    else:
        k_spec = pl.BlockSpec((1, s, wt), lambda b, i, g: (b, 0, g))
        vt_spec = pl.BlockSpec((1, wt, s), lambda b, i, g: (b, g, 0))
    in_specs = [pl.BlockSpec((1, tq, wt), lambda b, i, g: (b, i, g)), k_spec, vt_spec]
    args = [q, k, vt]
    if cache is not None:
        p = cache[0].shape[1]
        in_specs += [pl.BlockSpec((1, p, LANES), lambda b, i, g: (b, 0, 0)),
                     pl.BlockSpec((1, LANES, p), lambda b, i, g: (b, 0, 0))]
        args += list(cache)
    kern = functools.partial(_attn_kernel, n_tiles=n_tiles, kv_shared=kv_shared,
                             has_cache=cache is not None)
    return pl.pallas_call(
        kern, grid=(bsz, t // tq, w // wt), in_specs=in_specs,
        out_specs=pl.BlockSpec((1, tq, wt), lambda b, i, g: (b, i, g)),
        out_shape=jax.ShapeDtypeStruct((bsz, t, w), BF16),
        compiler_params=_cparams(("arbitrary", "arbitrary", "arbitrary")), name="attention",
    )(*args)


def _na_bias_blocks(variant):
    out = {}
    for qr in range(NA_ROWS):
        for kr in range(NA_WIN):
            if variant == 0:
                dr = kr - qr if kr < WIN_R else None
            elif variant == 1:
                dr = kr - qr - WIN_R // 2 if qr <= kr < qr + WIN_R else None
            else:
                dr = kr - qr - (NA_WIN - NA_ROWS) if kr >= NA_WIN - WIN_R else None
            out[(qr, kr)] = None if dr is None else dr + WIN_R - 1
    return out


def _na_kernel(q_ref, k_ref, vt_ref, kc_ref, vtc_ref, bc_ref, o_ref, bias_sc, *, n_rb):
    w = GRID_W
    tq = NA_ROWS * w

    @pl.when(jnp.logical_and(pl.program_id(1) == 0, pl.program_id(2) == 0))
    def _():
        for variant in range(3):
            for (qr, kr), di in _na_bias_blocks(variant).items():
                for hh in range(2 * NA_TILES):
                    val = jnp.full((w, w), MASKED, F32) if di is None else bc_ref[hh, di]
                    bias_sc[variant, kr * w:(kr + 1) * w, hh * tq + qr * w:hh * tq + (qr + 1) * w] = val

    n_rows = n_rb * NA_ROWS
    problems = []
    for rr in range(NA_RBS):
        rb = pl.program_id(2) * NA_RBS + rr
        variant = jnp.where(rb == 0, 0, jnp.where(rb == n_rb - 1, 2, 1))
        ws = jnp.clip(rb * NA_ROWS - WIN_R // 2, 0, n_rows - NA_WIN)
        blk0 = ws // NA_ROWS
        start = ws * w
        qrows = slice(rr * tq, (rr + 1) * tq)
        for j in range(NA_TILES):
            cols = slice(j * LANES, (j + 1) * LANES)
            qcols = slice(j * 2 * tq, (j + 1) * 2 * tq)
            qm = _split_heads_q(q_ref[0, qrows, cols])
            chunks = [(k_ref[0, pl.ds(pl.multiple_of(start + i * tq, tq), tq), cols],
                       vt_ref[0, blk0 + i, cols, :],
                       bias_sc[variant, i * tq:(i + 1) * tq, qcols]) for i in range(NA_BLKS)]
            chunks.append((kc_ref[0, :, cols], vtc_ref[0, cols, :], None))

            def emit(o_t, qrows=qrows, cols=cols):
                o_ref[0, qrows, cols] = _merge_heads_o(o_t, tq).astype(o_ref.dtype)

            problems.append((qm, chunks, emit))
    _online_softmax_pv_t(problems)


def _na_bias_table(rpb):
    c = np.arange(GRID_W)
    cs = np.clip(c - WIN_C // 2, 0, GRID_W - WIN_C)
    ck = np.arange(GRID_W)
    valid = (ck[:, None] >= cs[None, :]) & (ck[:, None] < cs[None, :] + WIN_C)
    idx = np.clip(ck[:, None] - c[None, :] + WIN_C - 1, 0, 2 * WIN_C - 2)
    onehot = (idx[..., None] == np.arange(2 * WIN_C - 1)).astype(np.float32)
    tab = jnp.einsum('hrx,kcx->hrkc', rpb, jnp.asarray(onehot), precision=lax.Precision.HIGHEST)
    return jnp.where(jnp.asarray(valid), tab * -LOG2E, MASKED)


def _na_attention(q, k, vt, kc, vtc, rpb):
    bsz, t, w = q.shape
    p = kc.shape[1]
    tq = NA_ROWS * GRID_W
    n_rb = t // tq
    bc = _na_bias_table(rpb)
    wt = NA_TILES * LANES
    tqs = NA_RBS * tq
    return pl.pallas_call(
        functools.partial(_na_kernel, n_rb=n_rb),
        grid=(w // wt, bsz, n_rb // NA_RBS),
        in_specs=[pl.BlockSpec((1, tqs, wt), lambda j, b, r: (b, r, j)),
                  pl.BlockSpec((1, t, wt), lambda j, b, r: (b, 0, j)),
                  pl.BlockSpec((1, n_rb, wt, tq), lambda j, b, r: (b, 0, j, 0)),
                  pl.BlockSpec((1, p, wt), lambda j, b, r: (b, 0, j)),
                  pl.BlockSpec((1, wt, p), lambda j, b, r: (b, j, 0)),
                  pl.BlockSpec((2 * NA_TILES, 2 * WIN_R - 1, GRID_W, GRID_W), lambda j, b, r: (j, 0, 0, 0))],
        out_specs=pl.BlockSpec((1, tqs, wt), lambda j, b, r: (b, r, j)),
        out_shape=jax.ShapeDtypeStruct((bsz, t, w), BF16),
        scratch_shapes=[pltpu.VMEM((3, NA_WIN * GRID_W, NA_TILES * 2 * tq), F32)],
        compiler_params=_cparams(("arbitrary", "arbitrary", "arbitrary")), name="na_attention",
    )(q, k, vt, kc, vtc, bc)


def _relayout_kernel(w_ref, *o_refs, plans, transposed, axis):
    w = w_ref[...]
    tile = w.shape[1 - axis]
    for o_ref, pieces, tr in zip(o_refs, plans, transposed):
        vals = []
        for src, size, scale in pieces:
            if src is None:
                val = jnp.zeros((size, tile) if axis == 0 else (tile, size), F32)
            else:
                val = lax.slice_in_dim(w, src, src + size, axis=axis)
                if scale != 1.0:
                    val = val * scale
            vals.append(val)
        val = vals[0] if len(vals) == 1 else jnp.concatenate(vals, axis=axis)
        o_ref[...] = (val.T if tr else val).astype(o_ref.dtype)


def _relayout(w, plans, transposed, *, axis, tile):
    grid = (w.shape[1 - axis] // tile,)
    full = w.shape[1 - axis]

    def spec(extent, tr):
        along_rows = (axis == 0) != tr
        shape = (extent, tile) if along_rows else (tile, extent)
        return pl.BlockSpec(shape, (lambda i: (0, i)) if along_rows else (lambda i: (i, 0)))

    def shape(extent, tr):
        return (extent, full) if (axis == 0) != tr else (full, extent)

    sizes = [sum(size for _, size, _ in pieces) for pieces in plans]
    return pl.pallas_call(
        functools.partial(_relayout_kernel, plans=plans, transposed=tuple(transposed), axis=axis),
        grid=grid, in_specs=[spec(w.shape[axis], False)],
        out_specs=[spec(n, tr) for n, tr in zip(sizes, transposed)],
        out_shape=[jax.ShapeDtypeStruct(shape(n, tr), BF16) for n, tr in zip(sizes, transposed)],
        compiler_params=_cparams(("arbitrary",)), name="w_relayout",
    )(w)


_GQA_PERM = np.array([0, 4, 1, 5, 2, 6, 3, 7])


def _tok_major(cache):
    b, h, p, dh = cache.shape
    return cache.transpose(0, 2, 1, 3).reshape(b, p, h * dh)


def _feat_major(cache):
    b, h, p, dh = cache.shape
    return cache.transpose(0, 1, 3, 2).reshape(b, h * dh, p)


def _head_major_from_t(x_t, n_heads):
    b, w, t = x_t.shape
    return jnp.swapaxes(x_t.reshape(b, 1, n_heads, w // n_heads, t), -1, -2)


def _even_layer(xp, xs, mod, g_pre, g_post, w_in, b_gates, g_hn, g_q, g_k, w_out, st_c, st_n, st_m,
                ck, cv, rope_tabs):
    o_v, o_o = 2 * W_A, 3 * W_A
    o_g = 4 * W_A
    o_qb = o_g + 4 * H_A
    o_kb = o_qb + W_B
    o_vb = o_kb + HKV_B * DH_B
    o_z = o_vb + HKV_B * DH_B
    def heads(src0):
        return [(src0 + int(h) * DH_B, DH_B, 1.0) for h in _GQA_PERM]

    n_g = 4 * H_A
    w_qk, w_gqa, w_z, wt_all = _relayout(
        jnp.swapaxes(w_in, 0, 1),
        [[(0, o_v, 1.0)],
         heads(o_qb) + [(o_kb, HKV_B * DH_B, 1.0)],
         [(o_z, W_A, 1.0)] + heads(o_z + W_A),
         [(o_v, 2 * W_A, 1.0), (o_vb, HKV_B * DH_B, 1.0), (o_g, n_g, 1.0), (None, LANES - n_g, 1.0)]],
        [True, True, True, False], axis=0, tile=256)
    (w_o,) = _relayout(w_out, [[(0, W_A, 1.0)] + heads(W_A)], [False], axis=0, tile=256)

    def stream(x, row0, per_batch, init, cache, rope, emit):
        bsz, t, _ = x.shape
        tm = 1024
        xf = x if per_batch else x.reshape(1, bsz * t, D_MODEL)
        n_vb = HKV_B * DH_B
        vb_seg = (n_vb, BF16, None) if per_batch else (n_vb, F32, t)
        gqa = dict(w=w_gqa, g_q=g_q, g_k=g_k, rope=rope, knt_block=t if emit else None)
        outs = _proj(xf, g_pre, mod, [(w_qk, BF16), (w_z, BF16)], wt_all,
                     [(W_A, BF16, CHUNK_A), (W_A, BF16, CHUNK_A), vb_seg], b_gates, gqa,
                     row0=row0, per_batch=per_batch, tm=tm)
        qk, z, vta, ota, vbt, gtt, qn, kn = outs[:8]
        if not per_batch:
            qk, z, qn, kn = (a.reshape(bsz, t, a.shape[-1]) for a in (qk, z, qn, kn))
            gtt = gtt.reshape(4 * H_A, bsz, t).transpose(1, 0, 2)
            vbt = vbt.reshape(bsz, HKV_B * DH_B, t)
            vta, ota = (a.reshape(bsz, t // CHUNK_A, W_A, CHUNK_A) for a in (vta, ota))
        res = _mlstm(qk, vta, ota, gtt, g_hn, init, emit_state=emit)
        ha = res[0]
        if per_batch:
            hb = _attention(qn, kn, vbt, cache, kv_shared=True, tq=512, n_tiles=4)
        else:
            hb = _attention(qn, kn, vbt, cache, kv_shared=True, tq=t, n_tiles=W_B // LANES)
        y = _outproj([ha.reshape(xf.shape[0], -1, W_A), hb.reshape(xf.shape[0], -1, W_B)],
                     z.reshape(xf.shape[0], -1, W_A + W_B), w_o, xf, g_post, mod,
                     row0=row0, per_batch=per_batch, tm=OUTPROJ_TM)
        knt = outs[8].reshape(bsz, HKV_B * DH_B, t) if emit else None
        return y.reshape(bsz, t, D_MODEL), res[1:], knt, vbt

    yp, st, knt_p, vbt_p = stream(xp, 0, False, None, None, None, True)
    n0 = st_n.transpose(0, 2, 1, 3)
    m0 = jnp.broadcast_to(st_m.transpose(0, 2, 1)[..., None], n0.shape)
    cache = (_tok_major(ck).astype(BF16), _feat_major(cv).astype(BF16))
    ys, _, _, _ = stream(xs, 1, True, (st_c, n0, m0), cache, rope_tabs, False)
    c_out, n_out, m_out = st
    new_n = n_out.transpose(0, 2, 1, 3)[:, None]
    new_m = m_out[..., 0].transpose(0, 2, 1)[:, None]
    return (yp, ys, c_out, new_n, new_m, _head_major_from_t(knt_p, HKV_B), _head_major_from_t(vbt_p, HKV_B))


def _odd_layer(xp, xs, mod, g_pre, g_post, w_in, rpb, w_out, ck, cv):
    w_q, w_k, w_z, wt_kv = _relayout(
        w_in, [[(0, W_C, QSCALE)], [(W_C, W_C, 1.0)], [(3 * W_C, W_C, 1.0)], [(W_C, 2 * W_C, 1.0)]],
        [False, False, False, True], axis=1, tile=256)
    (w_o,) = _relayout(w_out, [[(0, W_C, 1.0)]], [False], axis=0, tile=256)
    tm = 512
    bsz, t, _ = xp.shape
    xf = xp.reshape(1, bsz * t, D_MODEL)
    q, z, kt, vt, k = _proj(xf, g_pre, mod, [(w_q, BF16), (w_z, BF16)], wt_kv,
                            [(W_C, F32, t, True), (W_C, F32, t)], None, None, row0=0, per_batch=False, tm=tm)
    kt = kt.reshape(bsz, W_C, t)
    vt = vt.reshape(bsz, W_C, t)
    o = _attention(q.reshape(bsz, t, W_C), k.reshape(bsz, t, W_C), vt, None,
                   kv_shared=False, tq=t, n_tiles=W_C // LANES)
    yp = _outproj([o.reshape(1, bsz * t, W_C)], z, w_o, xf, g_post, mod, row0=0, per_batch=False,
                  tm=OUTPROJ_TM)
    yp = yp.reshape(bsz, t, D_MODEL)
    q, k, z, vts = _proj(xs, g_pre, mod, [(w_q, BF16), (w_k, BF16), (w_z, BF16)], wt_kv[W_C:],
                         [(W_C, BF16, NA_ROWS * GRID_W)], None, None, row0=1, per_batch=True, tm=2 * tm)
    o = _na_attention(q, k, vts, _tok_major(ck).astype(BF16), _feat_major(cv).astype(BF16), rpb)
    ys = _outproj([o], z, w_o, xs, g_post, mod, row0=1, per_batch=True, tm=OUTPROJ_TM)
    return yp, ys, _head_major_from_t(kt, H_C), _head_major_from_t(vt, H_C)


def kernel(x_prompt, x_sample, state_mlstm_C, state_mlstm_n, state_mlstm_m, cache_gqa_k, cache_gqa_v,
           cache_na_k, cache_na_v, c, c_ctx, w_mod, b_mod, g_pre, g_post, w_in_ab, b_gates_ab, g_hnorm_a,
           g_qnorm_b, g_knorm_b, w_out_ab, w_in_c, rpb_c, w_out_c):
    depth = w_mod.shape[0]
    assert depth == 2 and c.shape[0] == 2
    cvec = jnp.concatenate([c_ctx[None], c, jnp.zeros((8 - 1 - c.shape[0], D_MODEL), F32)], axis=0)
    mod = _modulation(cvec, w_mod, b_mod)
    rope_tabs = _rope_tables(x_sample.shape[1])
    xp, xs, c_out, n_out, m_out, gk, gv = _even_layer(
        x_prompt, x_sample, mod[0], g_pre[0], g_post[0], w_in_ab[0], b_gates_ab[0], g_hnorm_a[0],
        g_qnorm_b[0], g_knorm_b[0], w_out_ab[0], state_mlstm_C[:, 0], state_mlstm_n[:, 0],
        state_mlstm_m[:, 0], cache_gqa_k[:, 0], cache_gqa_v[:, 0], rope_tabs)
    xp, xs, nk, nv = _odd_layer(xp, xs, mod[1], g_pre[1], g_post[1], w_in_c[0], rpb_c[0], w_out_c[0],
                                cache_na_k[:, 0], cache_na_v[:, 0])
    return (xp, xs, c_out, n_out, m_out, gk, gv, nk, nv)
```

```python
import functools

import jax
import jax.numpy as jnp
import numpy as np
from jax import lax
from jax.experimental import pallas as pl
from jax.experimental.pallas import tpu as pltpu

F32 = jnp.float32
BF16 = jnp.bfloat16

D_MODEL = 1024
GRID_W = 64
EPS = 1e-6
H_A = 4
DH_A = 128
W_A = H_A * DH_A
CHUNK_A = 128
HQ_B = 8
HKV_B = 2
DH_B = 64
W_B = HQ_B * DH_B
ROPE_THETA = 10000.0
H_C = 16
DH_C = 64
W_C = H_C * DH_C
WIN_R = 8
WIN_C = 16

LANES = 128
NA_ROWS = 4
NA_TILES = 2
NA_RBS = 8
NA_BLKS = -(-(NA_ROWS + WIN_R - 1) // NA_ROWS)
NA_WIN = NA_BLKS * NA_ROWS
assert WIN_R // 2 == NA_ROWS
BF16_ROWS = 16
PROJ_SUBTILE = 256
OUTPROJ_TM = 1024
KEY_CHUNK = 256
QK_AHEAD = 3
LOG2E = 1.4426950408889634
QSCALE = DH_B ** -0.5 * LOG2E
assert DH_B == DH_C
MASKED = 1e30
VMEM_LIMIT = 56 * 1024 * 1024


def _cparams(sem):
    return pltpu.CompilerParams(dimension_semantics=sem, vmem_limit_bytes=VMEM_LIMIT)


def _silu(x):
    return x / (1.0 + jnp.exp(-x))


def _sigmoid(x):
    return 1.0 / (1.0 + jnp.exp(-x))


def _log_sigmoid(x):
    return jnp.minimum(x, 0.0) - jnp.log1p(jnp.exp(-jnp.abs(x)))


def _dot_nt(a, b):
    return lax.dot_general(a, b, (((1,), (1,)), ((), ())), preferred_element_type=F32)


MOD_ROWS = 3


def _mod_kernel(c_ref, w_ref, b_ref, o_ref, sb_sc):
    d, tn = w_ref.shape[1], w_ref.shape[2]
    sub = 8

    @pl.when(jnp.logical_and(pl.program_id(0) == 0, pl.program_id(1) == 0))
    def _():
        s = _silu(c_ref[...])
        for r in range(MOD_ROWS):
            sb_sc[r] = jnp.broadcast_to(s[r:r + 1, :], (LANES, d)).T

    def body(g, acc):
        rows = pl.ds(pl.multiple_of(g * sub, sub), sub)
        w = w_ref[0, rows, :]
        out = []
        for r in range(MOD_ROWS):
            sb = sb_sc[r, rows, :]
            out.append([acc[r][j] + sb * w[:, j * LANES:(j + 1) * LANES] for j in range(tn // LANES)])
        return out

    zero = jnp.zeros((sub, LANES), F32)
    acc = lax.fori_loop(0, d // sub, body, [[zero] * (tn // LANES) for _ in range(MOD_ROWS)], unroll=8)
    rows = [jnp.sum(jnp.concatenate(a, axis=1), axis=0, keepdims=True) + b_ref[0] for a in acc]
    o_ref[0] = jnp.concatenate(rows + [jnp.zeros((8 - MOD_ROWS, tn), F32)], axis=0)


def _modulation(cvec, w_mod, b_mod):
    depth, d, n = w_mod.shape
    tn = n // 4
    return pl.pallas_call(
        _mod_kernel,
        grid=(depth, n // tn),
        in_specs=[pl.BlockSpec((8, d), lambda l, j: (0, 0)),
                  pl.BlockSpec((1, d, tn), lambda l, j: (l, 0, j)),
                  pl.BlockSpec((1, 1, tn), lambda l, j: (l, 0, j))],
        out_specs=pl.BlockSpec((1, 8, tn), lambda l, j: (l, 0, j)),
        out_shape=jax.ShapeDtypeStruct((depth, 8, n), F32),
        scratch_shapes=[pltpu.VMEM((MOD_ROWS, d, LANES), F32)],
        compiler_params=_cparams(("arbitrary", "arbitrary")),
        name="modulation",
    )(cvec, w_mod, b_mod.reshape(depth, 1, n))


def _head_norm(x, g):
    lo = lax.broadcasted_iota(jnp.int32, x.shape, 1) < DH_B
    x2 = x * x
    s_lo = jnp.sum(jnp.where(lo, x2, 0.0), axis=-1, keepdims=True)
    s_hi = jnp.sum(jnp.where(lo, 0.0, x2), axis=-1, keepdims=True)
    ms = jnp.where(lo, s_lo, s_hi) * (1.0 / DH_B)
    return x * lax.rsqrt(ms + EPS) * g


def _rope(x, cos, sin):
    quarter = DH_B // 4
    first = (lax.broadcasted_iota(jnp.int32, x.shape, 1) % (2 * quarter)) < quarter
    partner = jnp.where(first, pltpu.roll(x, LANES - quarter, 1), pltpu.roll(x, quarter, 1))
    return x * cos + partner * sin


def _store_t(ref, val, t_block, tok0):
    n_tok = val.shape[1]
    if t_block is None:
        ref[0, :, tok0:tok0 + n_tok] = val.astype(ref.dtype)
    else:
        for i in range(n_tok // t_block):
            ref[0, tok0 // t_block + i] = val[:, i * t_block:(i + 1) * t_block].astype(ref.dtype)


def _proj_kernel(*refs, n_seg, t_sizes, t_blocks, t_plain, gate_rows, gqa, row0, per_batch):
    x_ref, g_ref, mod_ref = refs[:3]
    pos = 3
    w_refs = refs[pos:pos + n_seg]
    pos += n_seg
    if t_sizes:
        wt_ref = refs[pos]
        pos += 1
    if gate_rows:
        bgt_ref = refs[pos]
        pos += 1
    if gqa is not None:
        rope, knt_block = gqa
        gq_ref, gk_ref = refs[pos:pos + 2]
        pos += 2
        if rope:
            cos_ref, sin_ref = refs[pos:pos + 2]
            pos += 2
    n_plain = n_seg - (1 if gqa is not None else 0)
    o_refs = refs[pos:pos + n_plain]
    pos += n_plain
    ot_refs = refs[pos:pos + len(t_blocks)]
    pos += len(t_blocks)
    otp_refs = {}
    for i, flag in enumerate(t_plain):
        if flag:
            otp_refs[i] = refs[pos]
            pos += 1
    if gate_rows:
        gto_ref = refs[pos]
        pos += 1
    if gqa is not None:
        q_out, k_out = refs[pos:pos + 2]
        pos += 2
        if knt_block is not None:
            knt_out = refs[pos]
            pos += 1
    t_offs = [0]
    for n in t_sizes:
        t_offs.append(t_offs[-1] + n)

    d = x_ref.shape[-1]
    tm = x_ref.shape[1]
    row = row0 + (pl.program_id(0) if per_batch else 0)
    shift = mod_ref[pl.ds(row, 1), 0:d]
    scale = mod_ref[pl.ds(row, 1), d:2 * d]
    sub = PROJ_SUBTILE
    for s in range(tm // sub):
        rows = slice(s * sub, (s + 1) * sub)
        x = x_ref[0, rows, :]
        r = lax.rsqrt(jnp.mean(x * x, axis=-1, keepdims=True) + EPS)
        h = (x * r * g_ref[...]) * (1.0 + scale) + shift
        hb = h.astype(BF16)
        if gqa is not None:
            res = jnp.dot(hb, w_refs[-1][...], preferred_element_type=F32)
        for w_ref, o_ref in zip(w_refs[:n_plain], o_refs):
            o_ref[0, rows, :] = jnp.dot(hb, w_ref[...], preferred_element_type=F32).astype(o_ref.dtype)
        if t_sizes:
            res_t = _dot_nt(wt_ref[...], hb)
            for i, (ot_ref, tb, off, end) in enumerate(zip(ot_refs, t_blocks, t_offs, t_offs[1:])):
                _store_t(ot_ref, res_t[off:end], tb, s * sub)
                if i in otp_refs:
                    otp_refs[i][0, rows, :] = res_t[off:end].T.astype(otp_refs[i].dtype)
            if gate_rows:
                off = t_offs[len(t_blocks)]
                gto_ref[0, :, rows] = res_t[off:off + gate_rows] + bgt_ref[...]
        if gqa is not None:
            nq = W_B // LANES
            for j in range(nq + 1):
                xn = _head_norm(res[:, j * LANES:(j + 1) * LANES], gq_ref[...] if j < nq else gk_ref[...])
                if j == nq and knt_block is not None:
                    _store_t(knt_out, xn.T, knt_block, s * sub)
                if rope:
                    xn = _rope(xn, cos_ref[rows, :], sin_ref[rows, :])
                if j < nq:
                    q_out[0, rows, j * LANES:(j + 1) * LANES] = (xn * QSCALE).astype(q_out.dtype)
                else:
                    k_out[0, rows, :] = xn.astype(k_out.dtype)


def _proj(x, g_pre, mod, segs, wt_all, tsegs, b_gates, gqa, *, row0, per_batch, tm):
    bsz, t, d = x.shape
    grid = (bsz, t // tm)
    const = lambda b, i: (0, 0)
    in_specs = [pl.BlockSpec((1, tm, d), lambda b, i: (b, i, 0)),
                pl.BlockSpec((1, d), const),
                pl.BlockSpec(mod.shape, const)]
    args = [x, g_pre.reshape(1, d), mod]
    plain_ws = [w for w, _ in segs] + ([gqa["w"]] if gqa is not None else [])
    for w in plain_ws + ([wt_all] if wt_all is not None else []):
        in_specs.append(pl.BlockSpec(w.shape, const))
        args.append(w)
    gate_rows = 0
    t_plain = tuple(len(ts) > 3 and ts[3] for ts in tsegs)
    tsegs = [ts[:3] for ts in tsegs]
    t_sizes = [n for n, _, _ in tsegs]
    if b_gates is not None:
        gate_rows = b_gates.shape[0]
        t_sizes.append(LANES)
        in_specs.append(pl.BlockSpec((gate_rows, 1), const))
        args.append(b_gates.reshape(gate_rows, 1))
    assert sum(t_sizes) == (0 if wt_all is None else wt_all.shape[0])
    if gqa is not None:
        in_specs += [pl.BlockSpec((1, LANES), const)] * 2
        args += [jnp.tile(gqa["g_q"], 2).reshape(1, LANES), jnp.tile(gqa["g_k"], 2).reshape(1, LANES)]
        if gqa["rope"] is not None:
            in_specs += [pl.BlockSpec((tm, LANES), lambda b, i: (i, 0))] * 2
            args += list(gqa["rope"])
    out_specs, out_shape = [], []
    for w, dt in segs:
        n = w.shape[1]
        out_specs.append(pl.BlockSpec((1, tm, n), lambda b, i: (b, i, 0)))
        out_shape.append(jax.ShapeDtypeStruct((bsz, t, n), dt))

    def add_t_out(n, dt, tb):
        if tb is None:
            out_specs.append(pl.BlockSpec((1, n, tm), lambda b, i: (b, 0, i)))
            out_shape.append(jax.ShapeDtypeStruct((bsz, n, t), dt))
        else:
            out_specs.append(pl.BlockSpec((1, tm // tb, n, tb), lambda b, i: (b, i, 0, 0)))
            out_shape.append(jax.ShapeDtypeStruct((bsz, t // tb, n, tb), dt))

    for n, dt, tb in tsegs:
        add_t_out(n, dt, tb)
    for (n, _, _), flag in zip(tsegs, t_plain):
        if flag:
            out_specs.append(pl.BlockSpec((1, tm, n), lambda b, i: (b, i, 0)))
            out_shape.append(jax.ShapeDtypeStruct((bsz, t, n), BF16))
    if b_gates is not None:
        add_t_out(gate_rows, F32, None)
    gqa_static = None
    if gqa is not None:
        out_specs += [pl.BlockSpec((1, tm, W_B), lambda b, i: (b, i, 0)),
                      pl.BlockSpec((1, tm, LANES), lambda b, i: (b, i, 0))]
        out_shape += [jax.ShapeDtypeStruct((bsz, t, W_B), BF16), jax.ShapeDtypeStruct((bsz, t, LANES), BF16)]
        if gqa["knt_block"] is not None:
            add_t_out(LANES, F32, gqa["knt_block"])
        gqa_static = (gqa["rope"] is not None, gqa["knt_block"])
    kern = functools.partial(_proj_kernel, n_seg=len(plain_ws), t_sizes=tuple(t_sizes),
                             t_blocks=tuple(tb for _, _, tb in tsegs), t_plain=t_plain,
                             gate_rows=gate_rows, gqa=gqa_static,
                             row0=row0, per_batch=per_batch)
    return pl.pallas_call(
        kern, grid=grid, in_specs=in_specs, out_specs=out_specs, out_shape=out_shape,
        compiler_params=_cparams(("arbitrary", "arbitrary")), name="in_proj",
    )(*args)


def _outproj_kernel(*refs, n_in, row0, per_batch):
    a_refs = refs[:n_in]
    z_ref, w_ref, x_ref, gp_ref, mod_ref, o_ref = refs[n_in:n_in + 6]
    d = x_ref.shape[-1]
    row = row0 + (pl.program_id(0) if per_batch else 0)
    gate = mod_ref[pl.ds(row, 1), 2 * d:3 * d]
    z = z_ref[0].astype(F32)
    sz = _silu(z)
    acc = None
    off = 0
    for a_ref in a_refs:
        kk = a_ref.shape[-1]
        y = (a_ref[0].astype(F32) * sz[:, off:off + kk]).astype(BF16)
        part = jnp.dot(y, w_ref[off:off + kk, :], preferred_element_type=F32)
        acc = part if acc is None else acc + part
        off += kk
    r = lax.rsqrt(jnp.mean(acc * acc, axis=-1, keepdims=True) + EPS)
    o_ref[0] = x_ref[0] + gate * (acc * r * gp_ref[...])


def _outproj(parts, z, w_out, x, g_post, mod, *, row0, per_batch, tm):
    bsz, t, d = x.shape
    grid = (bsz, t // tm)
    in_specs, args = [], []
    for a in parts:
        in_specs.append(pl.BlockSpec((1, tm, a.shape[-1]), lambda b, i: (b, i, 0)))
        args.append(a)
    in_specs += [pl.BlockSpec((1, tm, z.shape[-1]), lambda b, i: (b, i, 0)),
                 pl.BlockSpec(w_out.shape, lambda b, i: (0, 0)),
                 pl.BlockSpec((1, tm, d), lambda b, i: (b, i, 0)),
                 pl.BlockSpec((1, d), lambda b, i: (0, 0)),
                 pl.BlockSpec(mod.shape, lambda b, i: (0, 0))]
    args += [z, w_out, x, g_post.reshape(1, d), mod]
    kern = functools.partial(_outproj_kernel, n_in=len(parts), row0=row0, per_batch=per_batch)
    return pl.pallas_call(
        kern, grid=grid, in_specs=in_specs,
        out_specs=pl.BlockSpec((1, tm, d), lambda b, i: (b, i, 0)),
        out_shape=jax.ShapeDtypeStruct((bsz, t, d), F32),
        compiler_params=_cparams(("arbitrary", "arbitrary")), name="out_proj",
    )(*args)


def _split3_bf16(x):
    hi = x.astype(BF16)
    r1 = x - hi.astype(F32)
    mid = r1.astype(BF16)
    lo = (r1 - mid.astype(F32)).astype(BF16)
    return hi, mid, lo


def _mlstm_gate_rows(ig_f, fg_f, ig_b, fg_b):
    rr, ll = fg_f.shape
    lf = _log_sigmoid(jnp.concatenate([fg_f, fg_b], axis=0))
    pieces = jnp.concatenate(_split3_bf16(lf), axis=0)
    u = lax.broadcasted_iota(jnp.int32, (ll, ll), 0)
    t = lax.broadcasted_iota(jnp.int32, (ll, ll), 1)
    out = []
    for d, ig in enumerate((ig_f, ig_b)):
        tri = jnp.where((u >= t) if d else (u <= t), 1.0, 0.0).astype(BF16)
        y = jnp.dot(pieces, tri, preferred_element_type=F32)
        rows = slice(d * rr, (d + 1) * rr)
        b = y[0:2 * rr][rows] + y[2 * rr:4 * rr][rows] + y[4 * rr:6 * rr][rows]
        c = ig - b
        cmax = jnp.broadcast_to(jnp.max(c, axis=-1, keepdims=True), c.shape)
        tot = jnp.broadcast_to(jnp.sum(lf[rows], axis=-1, keepdims=True), c.shape)
        out.append((c, b, cmax, tot))
    return out


def _mlstm_kernel(*refs, nc, hps, unroll, has_init, emit_state):
    q_ref, k_ref, vt_ref, ot_ref, gtt_ref, ghn_ref = refs[:6]
    pos = 6
    if has_init:
        c0_ref, n0_ref, m0_ref = refs[pos:pos + 3]
        pos += 3
    ha_ref = refs[pos]
    pos += 1
    if emit_state:
        cout_ref, nout_ref, mout_ref = refs[pos:pos + 3]
        pos += 3
    hs_sc, gate_sc, c_sc, n_sc, m_sc = refs[pos:pos + 5]
    ll, dh = CHUNK_A, DH_A
    kscale = dh ** -0.5
    hd0 = pl.program_id(1) * hps
    chains = [(i, d) for i in range(hps) for d in range(2)]

    for i, d in chains:
        if has_init:
            c_sc[i, d] = c0_ref[0, d, i]
            n_sc[i, d] = n0_ref[0, i, d:d + 1, :]
            m_sc[i, d] = m0_ref[0, i, d:d + 1, :]
        else:
            c_sc[i, d] = jnp.zeros((dh, dh), F32)
            n_sc[i, d] = jnp.zeros((1, dh), F32)
            m_sc[i, d] = jnp.zeros((1, dh), F32)

    def gate_rows(col0):
        return gtt_ref[0, pl.ds(pl.multiple_of((col0 + hd0) * nc, 8), hps * nc), :]

    gates = _mlstm_gate_rows(gate_rows(0), gate_rows(H_A), gate_rows(2 * H_A), gate_rows(3 * H_A))
    for d in range(2):
        for kind in range(4):
            gate_sc[d, kind] = gates[d][kind]

    ghn_t = [jnp.broadcast_to(ghn_ref[:, i * dh:(i + 1) * dh], (ll, dh)).T for i in range(hps)]
    si = lax.broadcasted_iota(jnp.int32, (ll, ll), 0)
    ti = lax.broadcasted_iota(jnp.int32, (ll, ll), 1)

    def run_trip(j, first_touch):
        jobs = []
        for u in range(unroll):
            step = j * unroll + u
            for i, d in chains:
                jobs.append((i, d, step if d == 0 else nc - 1 - step))
        state = {ch: [c_sc[ch], n_sc[ch], m_sc[ch]] for ch in chains}
        hcols = [slice(i * dh, (i + 1) * dh) for i in range(hps)]

        def tok_rows(cidx):
            return pl.ds(pl.multiple_of(cidx * ll, ll), ll)

        rows = []
        for i, d, cidx in jobs:
            c_r, b_r, cmax_r, tot_r = (gate_sc[d, kind, pl.ds(i * nc + cidx, 1), :] for kind in range(4))
            m_st = state[(i, d)][2]
            m_c = jnp.maximum(m_st, cmax_r)
            state[(i, d)][2] = tot_r + m_c
            rows.append((c_r, b_r, m_st, jnp.exp(m_st - m_c), jnp.exp(c_r - m_c) * kscale))
        start = []
        for (i, d, cidx), (_, _, _, a_st, wk) in zip(jobs, rows):
            k = k_ref[0, tok_rows(cidx), hcols[i]]
            vt = vt_ref[0, cidx, hcols[i], :]
            vw = jnp.concatenate([vt.astype(F32) * wk, jnp.broadcast_to(wk, (BF16_ROWS, ll))], axis=0)
            upd = jnp.dot(vw.astype(BF16), k, preferred_element_type=F32)
            c_st, n_st, _ = state[(i, d)]
            start.append((c_st, n_st))
            state[(i, d)][0] = a_st * c_st + upd[:dh]
            state[(i, d)][1] = a_st * n_st + upd[dh:dh + 1]
        prods = []
        for (i, d, cidx), (c_st, n_st) in zip(jobs, start):
            q = q_ref[0, tok_rows(cidx), hcols[i]]
            k = k_ref[0, tok_rows(cidx), hcols[i]]
            n16 = jnp.broadcast_to(n_st.astype(BF16), (BF16_ROWS, dh))
            prods.append(_dot_nt(jnp.concatenate([k, c_st.astype(BF16), n16], axis=0), q))
        for (i, d, cidx), (c_r, b_r, m_st, _, _), r in zip(jobs, rows, prods):
            allowed = (si >= ti) if d else (si <= ti)
            cb = jnp.where(allowed, jnp.broadcast_to(c_r, (ll, ll)).T, -jnp.inf)
            big_m = jnp.maximum(m_st, jnp.max(cb, axis=0, keepdims=True))
            p = jnp.exp(cb - (big_m - float(np.log(kscale)))) * r[:ll]
            w_inter = jnp.exp(m_st - big_m)
            vt = vt_ref[0, cidx, hcols[i], :]
            num = jnp.dot(vt, p.astype(BF16), preferred_element_type=F32) - r[ll:ll + dh] * (-w_inter)
            den = w_inter * r[ll + dh:ll + dh + 1] + jnp.sum(p, axis=0, keepdims=True)
            h_t = num / jnp.maximum(jnp.abs(den), jnp.exp(-(b_r + big_m)))
            if first_touch:
                hs_sc[i, cidx] = h_t
            else:
                hsum = h_t + hs_sc[i, cidx]
                rn = lax.rsqrt(jnp.mean(hsum * hsum, axis=0, keepdims=True) + EPS)
                out_t = (hsum * rn * ghn_t[i]) * _sigmoid(ot_ref[0, cidx, hcols[i], :].astype(F32))
                ha_ref[0, tok_rows(cidx), hcols[i]] = out_t.T.astype(ha_ref.dtype)
        for ch in chains:
            c_sc[ch], n_sc[ch], m_sc[ch] = state[ch]

    def make_body(first_touch):
        def body(j, carry):
            run_trip(j, first_touch)
            return carry
        return body

    trips = nc // unroll
    lax.fori_loop(0, trips // 2, make_body(True), 0)
    lax.fori_loop(trips // 2, trips, make_body(False), 0)

    if emit_state:
        for i, d in chains:
            cout_ref[0, 0, d, i] = c_sc[i, d]
            nout_ref[0, i, d:d + 1, :] = n_sc[i, d]
            mout_ref[0, i, d:d + 1, :] = m_sc[i, d]


def _mlstm(qk, vt, ot, gates_t, g_hn, init, *, emit_state):
    bsz, t, _ = qk.shape
    nc = t // CHUNK_A
    hps, unroll = (H_A, 1) if nc < 8 else (1, 8)
    assert (nc // 2) % unroll == 0 and nc % 2 == 0
    gtt = gates_t.reshape(bsz, 4 * H_A * nc, CHUNK_A)
    wh = hps * DH_A

    def tblk():
        return pl.BlockSpec((1, nc, wh, CHUNK_A), lambda b, g: (b, 0, g, 0))

    in_specs = [pl.BlockSpec((1, t, wh), lambda b, g: (b, 0, g)),
                pl.BlockSpec((1, t, wh), lambda b, g: (b, 0, H_A // hps + g)),
                tblk(), tblk(),
                pl.BlockSpec((1, 4 * H_A * nc, CHUNK_A), lambda b, g: (b, 0, 0)),
                pl.BlockSpec((1, wh), lambda b, g: (0, g))]
    args = [qk, qk, vt, ot, gtt, g_hn.reshape(1, W_A)]
    if init is not None:
        c0, n0, m0 = init
        in_specs += [pl.BlockSpec((1, 2, hps, DH_A, DH_A), lambda b, g: (b, 0, g, 0, 0)),
                     pl.BlockSpec((1, hps, 2, DH_A), lambda b, g: (b, g, 0, 0)),
                     pl.BlockSpec((1, hps, 2, DH_A), lambda b, g: (b, g, 0, 0))]
        args += [c0, n0, m0]
    out_specs = [pl.BlockSpec((1, t, wh), lambda b, g: (b, 0, g))]
    out_shape = [jax.ShapeDtypeStruct((bsz, t, W_A), BF16)]
    if emit_state:
        out_specs += [pl.BlockSpec((1, 1, 2, hps, DH_A, DH_A), lambda b, g: (b, 0, 0, g, 0, 0)),
                      pl.BlockSpec((1, hps, 2, DH_A), lambda b, g: (b, g, 0, 0)),
                      pl.BlockSpec((1, hps, 2, DH_A), lambda b, g: (b, g, 0, 0))]
        out_shape += [jax.ShapeDtypeStruct((bsz, 1, 2, H_A, DH_A, DH_A), F32),
                      jax.ShapeDtypeStruct((bsz, H_A, 2, DH_A), F32),
                      jax.ShapeDtypeStruct((bsz, H_A, 2, DH_A), F32)]
    kern = functools.partial(_mlstm_kernel, nc=nc, hps=hps, unroll=unroll, has_init=init is not None,
                             emit_state=emit_state)
    return pl.pallas_call(
        kern, grid=(bsz, H_A // hps), in_specs=in_specs, out_specs=out_specs, out_shape=out_shape,
        scratch_shapes=[pltpu.VMEM((hps, nc, DH_A, CHUNK_A), F32), pltpu.VMEM((2, 4, hps * nc, CHUNK_A), F32),
                        pltpu.VMEM((hps, 2, DH_A, DH_A), F32), pltpu.VMEM((hps, 2, 1, DH_A), F32),
                        pltpu.VMEM((hps, 2, 1, DH_A), F32)],
        compiler_params=_cparams(("arbitrary", "arbitrary")), name="mlstm",
    )(*args)


def _rope_tables(n_tok):
    t = np.arange(n_tok)
    row = (t // GRID_W).astype(np.float32)
    colp = (t % GRID_W).astype(np.float32)
    quarter = DH_B // 4
    freqs = (np.float32(ROPE_THETA) ** (-np.arange(quarter, dtype=np.float32) / np.float32(quarter))).astype(np.float32)
    ar = row[:, None] * freqs
    ac = colp[:, None] * freqs
    cos = np.concatenate([np.cos(ar), np.cos(ar), np.cos(ac), np.cos(ac)], axis=-1)
    sin = np.concatenate([-np.sin(ar), np.sin(ar), -np.sin(ac), np.sin(ac)], axis=-1)
    return jnp.asarray(np.tile(cos, (1, 2)), F32), jnp.asarray(np.tile(sin, (1, 2)), F32)


def _split_heads_q(q):
    lo = lax.broadcasted_iota(jnp.int32, q.shape, 1) < (LANES // 2)
    zero = jnp.zeros_like(q)
    return jnp.concatenate([jnp.where(lo, q, zero), jnp.where(lo, zero, q)], axis=0)


def _merge_heads_o(o_t, tq):
    half = LANES // 2
    return jnp.concatenate([o_t[:half, :tq], o_t[half:, tq:]], axis=0).T


def _with_ones_rows(vt):
    return jnp.concatenate([vt, jnp.ones((BF16_ROWS, vt.shape[1]), BF16)], axis=0)


def _online_softmax_pv_t(problems):
    seq = [(pi, ci) for pi, (_, chunks, _) in enumerate(problems) for ci in range(len(chunks))]
    scores = {}

    def issue(t):
        pi, ci = seq[t]
        qm, chunks, _ = problems[pi]
        scores[(pi, ci)] = _dot_nt(chunks[ci][0], qm)

    for t in range(min(QK_AHEAD, len(seq))):
        issue(t)
    m = acc = None
    for t, (pi, ci) in enumerate(seq):
        if t + QK_AHEAD < len(seq):
            issue(t + QK_AHEAD)
        _, chunks, emit = problems[pi]
        _, vt, penalty = chunks[ci]
        s = scores.pop((pi, ci))
        if penalty is not None:
            s = s - penalty
        mc = jnp.max(s, axis=0, keepdims=True)
        m_new = mc if ci == 0 else jnp.maximum(m, mc)
        p = jnp.exp2(s - m_new).astype(BF16)
        part = jnp.dot(_with_ones_rows(vt), p, preferred_element_type=F32)
        acc = part if ci == 0 else jnp.exp2(m - m_new) * acc + part
        m = m_new
        if ci == len(chunks) - 1:
            dv = vt.shape[0]
            emit(acc[:dv] / acc[dv:dv + 1])


def _attn_kernel(*refs, n_tiles, kv_shared, has_cache):
    q_ref, k_ref, vt_ref = refs[:3]
    pos = 3
    if has_cache:
        kc_ref, vtc_ref = refs[pos:pos + 2]
        pos += 2
    o_ref = refs[pos]
    tq = q_ref.shape[1]
    s_len = k_ref.shape[1]
    sc = min(s_len, KEY_CHUNK)
    problems = []
    for j in range(n_tiles):
        kj = 0 if kv_shared else j
        kcols = slice(kj * LANES, (kj + 1) * LANES)
        qm = _split_heads_q(q_ref[0, :, j * LANES:(j + 1) * LANES])
        chunks = [(k_ref[0, c * sc:(c + 1) * sc, kcols].astype(BF16),
                   vt_ref[0, kcols, c * sc:(c + 1) * sc].astype(BF16), None) for c in range(s_len // sc)]
        if has_cache:
            chunks.append((kc_ref[0], vtc_ref[0], None))

        def emit(o_t, j=j):
            o_ref[0, :, j * LANES:(j + 1) * LANES] = _merge_heads_o(o_t, tq).astype(o_ref.dtype)

        problems.append((qm, chunks, emit))
    _online_softmax_pv_t(problems)


def _attention(q, k, vt, cache, *, kv_shared, tq, n_tiles):
    bsz, t, w = q.shape
    s = k.shape[1]
    wt = n_tiles * LANES
    if kv_shared:
        k_spec = pl.BlockSpec((1, s, LANES), lambda b, i, g: (b, 0, 0))
        vt_spec = pl.BlockSpec((1, LANES, s), lambda b, i, g: (b, 0, 0))
    else:
        k_spec = pl.BlockSpec((1, s, wt), lambda b, i, g: (b, 0, g))
        vt_spec = pl.BlockSpec((1, wt, s), lambda b, i, g: (b, g, 0))
    in_specs = [pl.BlockSpec((1, tq, wt), lambda b, i, g: (b, i, g)), k_spec, vt_spec]
    args = [q, k, vt]
    if cache is not None:
        p = cache[0].shape[1]
        in_specs += [pl.BlockSpec((1, p, LANES), lambda b, i, g: (b, 0, 0)),
                     pl.BlockSpec((1, LANES, p), lambda b, i, g: (b, 0, 0))]
        args += list(cache)
    kern = functools.partial(_attn_kernel, n_tiles=n_tiles, kv_shared=kv_shared,
                             has_cache=cache is not None)
    return pl.pallas_call(
        kern, grid=(bsz, t // tq, w // wt), in_specs=in_specs,
        out_specs=pl.BlockSpec((1, tq, wt), lambda b, i, g: (b, i, g)),
        out_shape=jax.ShapeDtypeStruct((bsz, t, w), BF16),
        compiler_params=_cparams(("arbitrary", "arbitrary", "arbitrary")), name="attention",
    )(*args)


def _na_bias_blocks(variant):
    out = {}
    for qr in range(NA_ROWS):
        for kr in range(NA_WIN):
            if variant == 0:
                dr = kr - qr if kr < WIN_R else None
            elif variant == 1:
                dr = kr - qr - WIN_R // 2 if qr <= kr < qr + WIN_R else None
            else:
                dr = kr - qr - (NA_WIN - NA_ROWS) if kr >= NA_WIN - WIN_R else None
            out[(qr, kr)] = None if dr is None else dr + WIN_R - 1
    return out


def _na_kernel(q_ref, k_ref, vt_ref, kc_ref, vtc_ref, bc_ref, o_ref, bias_sc, *, n_rb):
    w = GRID_W
    tq = NA_ROWS * w

    @pl.when(jnp.logical_and(pl.program_id(1) == 0, pl.program_id(2) == 0))
    def _():
        for variant in range(3):
            for (qr, kr), di in _na_bias_blocks(variant).items():
                for hh in range(2 * NA_TILES):
                    val = jnp.full((w, w), MASKED, F32) if di is None else bc_ref[hh, di]
                    bias_sc[variant, kr * w:(kr + 1) * w, hh * tq + qr * w:hh * tq + (qr + 1) * w] = val

    n_rows = n_rb * NA_ROWS
    problems = []
    for rr in range(NA_RBS):
        rb = pl.program_id(2) * NA_RBS + rr
        variant = jnp.where(rb == 0, 0, jnp.where(rb == n_rb - 1, 2, 1))
        ws = jnp.clip(rb * NA_ROWS - WIN_R // 2, 0, n_rows - NA_WIN)
        blk0 = ws // NA_ROWS
        start = ws * w
        qrows = slice(rr * tq, (rr + 1) * tq)
        for j in range(NA_TILES):
            cols = slice(j * LANES, (j + 1) * LANES)
            qcols = slice(j * 2 * tq, (j + 1) * 2 * tq)
            qm = _split_heads_q(q_ref[0, qrows, cols])
            chunks = [(k_ref[0, pl.ds(pl.multiple_of(start + i * tq, tq), tq), cols],
                       vt_ref[0, blk0 + i, cols, :],
                       bias_sc[variant, i * tq:(i + 1) * tq, qcols]) for i in range(NA_BLKS)]
            chunks.append((kc_ref[0, :, cols], vtc_ref[0, cols, :], None))

            def emit(o_t, qrows=qrows, cols=cols):
                o_ref[0, qrows, cols] = _merge_heads_o(o_t, tq).astype(o_ref.dtype)

            problems.append((qm, chunks, emit))
    _online_softmax_pv_t(problems)


def _na_bias_table(rpb):
    c = np.arange(GRID_W)
    cs = np.clip(c - WIN_C // 2, 0, GRID_W - WIN_C)
    ck = np.arange(GRID_W)
    valid = (ck[:, None] >= cs[None, :]) & (ck[:, None] < cs[None, :] + WIN_C)
    idx = np.clip(ck[:, None] - c[None, :] + WIN_C - 1, 0, 2 * WIN_C - 2)
    onehot = (idx[..., None] == np.arange(2 * WIN_C - 1)).astype(np.float32)
    tab = jnp.einsum('hrx,kcx->hrkc', rpb, jnp.asarray(onehot), precision=lax.Precision.HIGHEST)
    return jnp.where(jnp.asarray(valid), tab * -LOG2E, MASKED)


def _na_attention(q, k, vt, kc, vtc, rpb):
    bsz, t, w = q.shape
    p = kc.shape[1]
    tq = NA_ROWS * GRID_W
    n_rb = t // tq
    bc = _na_bias_table(rpb)
    wt = NA_TILES * LANES
    tqs = NA_RBS * tq
    return pl.pallas_call(
        functools.partial(_na_kernel, n_rb=n_rb),
        grid=(w // wt, bsz, n_rb // NA_RBS),
        in_specs=[pl.BlockSpec((1, tqs, wt), lambda j, b, r: (b, r, j)),
                  pl.BlockSpec((1, t, wt), lambda j, b, r: (b, 0, j)),
                  pl.BlockSpec((1, n_rb, wt, tq), lambda j, b, r: (b, 0, j, 0)),
                  pl.BlockSpec((1, p, wt), lambda j, b, r: (b, 0, j)),
                  pl.BlockSpec((1, wt, p), lambda j, b, r: (b, j, 0)),
                  pl.BlockSpec((2 * NA_TILES, 2 * WIN_R - 1, GRID_W, GRID_W), lambda j, b, r: (j, 0, 0, 0))],
        out_specs=pl.BlockSpec((1, tqs, wt), lambda j, b, r: (b, r, j)),
        out_shape=jax.ShapeDtypeStruct((bsz, t, w), BF16),
        scratch_shapes=[pltpu.VMEM((3, NA_WIN * GRID_W, NA_TILES * 2 * tq), F32)],
        compiler_params=_cparams(("arbitrary", "arbitrary", "arbitrary")), name="na_attention",
    )(q, k, vt, kc, vtc, bc)


def _relayout_kernel(w_ref, *o_refs, plans, transposed, axis):
    w = w_ref[...]
    tile = w.shape[1 - axis]
    for o_ref, pieces, tr in zip(o_refs, plans, transposed):
        vals = []
        for src, size, scale in pieces:
            if src is None:
                val = jnp.zeros((size, tile) if axis == 0 else (tile, size), F32)
            else:
                val = lax.slice_in_dim(w, src, src + size, axis=axis)
                if scale != 1.0:
                    val = val * scale
            vals.append(val)
        val = vals[0] if len(vals) == 1 else jnp.concatenate(vals, axis=axis)
        o_ref[...] = (val.T if tr else val).astype(o_ref.dtype)


def _relayout(w, plans, transposed, *, axis, tile):
    grid = (w.shape[1 - axis] // tile,)
    full = w.shape[1 - axis]

    def spec(extent, tr):
        along_rows = (axis == 0) != tr
        shape = (extent, tile) if along_rows else (tile, extent)
        return pl.BlockSpec(shape, (lambda i: (0, i)) if along_rows else (lambda i: (i, 0)))

    def shape(extent, tr):
        return (extent, full) if (axis == 0) != tr else (full, extent)

    sizes = [sum(size for _, size, _ in pieces) for pieces in plans]
    return pl.pallas_call(
        functools.partial(_relayout_kernel, plans=plans, transposed=tuple(transposed), axis=axis),
        grid=grid, in_specs=[spec(w.shape[axis], False)],
        out_specs=[spec(n, tr) for n, tr in zip(sizes, transposed)],
        out_shape=[jax.ShapeDtypeStruct(shape(n, tr), BF16) for n, tr in zip(sizes, transposed)],
        compiler_params=_cparams(("arbitrary",)), name="w_relayout",
    )(w)


_GQA_PERM = np.array([0, 4, 1, 5, 2, 6, 3, 7])


def _tok_major(cache):
    b, h, p, dh = cache.shape
    return cache.transpose(0, 2, 1, 3).reshape(b, p, h * dh)


def _feat_major(cache):
    b, h, p, dh = cache.shape
    return cache.transpose(0, 1, 3, 2).reshape(b, h * dh, p)


def _head_major_from_t(x_t, n_heads):
    b, w, t = x_t.shape
    return jnp.swapaxes(x_t.reshape(b, 1, n_heads, w // n_heads, t), -1, -2)


def _even_layer(xp, xs, mod, g_pre, g_post, w_in, b_gates, g_hn, g_q, g_k, w_out, st_c, st_n, st_m,
                ck, cv, rope_tabs):
    o_v, o_o = 2 * W_A, 3 * W_A
    o_g = 4 * W_A
    o_qb = o_g + 4 * H_A
    o_kb = o_qb + W_B
    o_vb = o_kb + HKV_B * DH_B
    o_z = o_vb + HKV_B * DH_B
    def heads(src0):
        return [(src0 + int(h) * DH_B, DH_B, 1.0) for h in _GQA_PERM]

    n_g = 4 * H_A
    w_qk, w_gqa, w_z, wt_all = _relayout(
        jnp.swapaxes(w_in, 0, 1),
        [[(0, o_v, 1.0)],
         heads(o_qb) + [(o_kb, HKV_B * DH_B, 1.0)],
         [(o_z, W_A, 1.0)] + heads(o_z + W_A),
         [(o_v, 2 * W_A, 1.0), (o_vb, HKV_B * DH_B, 1.0), (o_g, n_g, 1.0), (None, LANES - n_g, 1.0)]],
        [True, True, True, False], axis=0, tile=256)
    (w_o,) = _relayout(w_out, [[(0, W_A, 1.0)] + heads(W_A)], [False], axis=0, tile=256)

    def stream(x, row0, per_batch, init, cache, rope, emit):
        bsz, t, _ = x.shape
        tm = 1024
        xf = x if per_batch else x.reshape(1, bsz * t, D_MODEL)
        n_vb = HKV_B * DH_B
        vb_seg = (n_vb, BF16, None) if per_batch else (n_vb, F32, t)
        gqa = dict(w=w_gqa, g_q=g_q, g_k=g_k, rope=rope, knt_block=t if emit else None)
        outs = _proj(xf, g_pre, mod, [(w_qk, BF16), (w_z, BF16)], wt_all,
                     [(W_A, BF16, CHUNK_A), (W_A, BF16, CHUNK_A), vb_seg], b_gates, gqa,
                     row0=row0, per_batch=per_batch, tm=tm)
        qk, z, vta, ota, vbt, gtt, qn, kn = outs[:8]
        if not per_batch:
            qk, z, qn, kn = (a.reshape(bsz, t, a.shape[-1]) for a in (qk, z, qn, kn))
            gtt = gtt.reshape(4 * H_A, bsz, t).transpose(1, 0, 2)
            vbt = vbt.reshape(bsz, HKV_B * DH_B, t)
            vta, ota = (a.reshape(bsz, t // CHUNK_A, W_A, CHUNK_A) for a in (vta, ota))
        res = _mlstm(qk, vta, ota, gtt, g_hn, init, emit_state=emit)
        ha = res[0]
        if per_batch:
            hb = _attention(qn, kn, vbt, cache, kv_shared=True, tq=512, n_tiles=4)
        else:
            hb = _attention(qn, kn, vbt, cache, kv_shared=True, tq=t, n_tiles=W_B // LANES)
        y = _outproj([ha.reshape(xf.shape[0], -1, W_A), hb.reshape(xf.shape[0], -1, W_B)],
                     z.reshape(xf.shape[0], -1, W_A + W_B), w_o, xf, g_post, mod,
                     row0=row0, per_batch=per_batch, tm=OUTPROJ_TM)
        knt = outs[8].reshape(bsz, HKV_B * DH_B, t) if emit else None
        return y.reshape(bsz, t, D_MODEL), res[1:], knt, vbt

    yp, st, knt_p, vbt_p = stream(xp, 0, False, None, None, None, True)
    n0 = st_n.transpose(0, 2, 1, 3)
    m0 = jnp.broadcast_to(st_m.transpose(0, 2, 1)[..., None], n0.shape)
    cache = (_tok_major(ck).astype(BF16), _feat_major(cv).astype(BF16))
    ys, _, _, _ = stream(xs, 1, True, (st_c, n0, m0), cache, rope_tabs, False)
    c_out, n_out, m_out = st
    new_n = n_out.transpose(0, 2, 1, 3)[:, None]
    new_m = m_out[..., 0].transpose(0, 2, 1)[:, None]
    return (yp, ys, c_out, new_n, new_m, _head_major_from_t(knt_p, HKV_B), _head_major_from_t(vbt_p, HKV_B))


def _odd_layer(xp, xs, mod, g_pre, g_post, w_in, rpb, w_out, ck, cv):
    w_q, w_k, w_z, wt_kv = _relayout(
        w_in, [[(0, W_C, QSCALE)], [(W_C, W_C, 1.0)], [(3 * W_C, W_C, 1.0)], [(W_C, 2 * W_C, 1.0)]],
        [False, False, False, True], axis=1, tile=256)
    (w_o,) = _relayout(w_out, [[(0, W_C, 1.0)]], [False], axis=0, tile=256)
    tm = 512
    bsz, t, _ = xp.shape
    xf = xp.reshape(1, bsz * t, D_MODEL)
    q, z, kt, vt, k = _proj(xf, g_pre, mod, [(w_q, BF16), (w_z, BF16)], wt_kv,
                            [(W_C, F32, t, True), (W_C, F32, t)], None, None, row0=0, per_batch=False, tm=tm)
    kt = kt.reshape(bsz, W_C, t)
    vt = vt.reshape(bsz, W_C, t)
    o = _attention(q.reshape(bsz, t, W_C), k.reshape(bsz, t, W_C), vt, None,
                   kv_shared=False, tq=t, n_tiles=W_C // LANES)
    yp = _outproj([o.reshape(1, bsz * t, W_C)], z, w_o, xf, g_post, mod, row0=0, per_batch=False,
                  tm=OUTPROJ_TM)
    yp = yp.reshape(bsz, t, D_MODEL)
    q, k, z, vts = _proj(xs, g_pre, mod, [(w_q, BF16), (w_k, BF16), (w_z, BF16)], wt_kv[W_C:],
                         [(W_C, BF16, NA_ROWS * GRID_W)], None, None, row0=1, per_batch=True, tm=2 * tm)
    o = _na_attention(q, k, vts, _tok_major(ck).astype(BF16), _feat_major(cv).astype(BF16), rpb)
    ys = _outproj([o], z, w_o, xs, g_post, mod, row0=1, per_batch=True, tm=OUTPROJ_TM)
    return yp, ys, _head_major_from_t(kt, H_C), _head_major_from_t(vt, H_C)


def kernel(x_prompt, x_sample, state_mlstm_C, state_mlstm_n, state_mlstm_m, cache_gqa_k, cache_gqa_v,
           cache_na_k, cache_na_v, c, c_ctx, w_mod, b_mod, g_pre, g_post, w_in_ab, b_gates_ab, g_hnorm_a,
           g_qnorm_b, g_knorm_b, w_out_ab, w_in_c, rpb_c, w_out_c):
    depth = w_mod.shape[0]
    assert depth == 2 and c.shape[0] == 2
    cvec = jnp.concatenate([c_ctx[None], c, jnp.zeros((8 - 1 - c.shape[0], D_MODEL), F32)], axis=0)
    mod = _modulation(cvec, w_mod, b_mod)
    rope_tabs = _rope_tables(x_sample.shape[1])
    xp, xs, c_out, n_out, m_out, gk, gv = _even_layer(
        x_prompt, x_sample, mod[0], g_pre[0], g_post[0], w_in_ab[0], b_gates_ab[0], g_hnorm_a[0],
        g_qnorm_b[0], g_knorm_b[0], w_out_ab[0], state_mlstm_C[:, 0], state_mlstm_n[:, 0],
        state_mlstm_m[:, 0], cache_gqa_k[:, 0], cache_gqa_v[:, 0], rope_tabs)
    xp, xs, nk, nv = _odd_layer(xp, xs, mod[1], g_pre[1], g_post[1], w_in_c[0], rpb_c[0], w_out_c[0],
                                cache_na_k[:, 0], cache_na_v[:, 0])
    return (xp, xs, c_out, n_out, m_out, gk, gv, nk, nv)
```

```python
import functools

import jax
import jax.numpy as jnp
import numpy as np
from jax import lax
from jax.experimental import pallas as pl
from jax.experimental.pallas import tpu as pltpu

F32 = jnp.float32
BF16 = jnp.bfloat16

D_MODEL = 1024
GRID_W = 64
EPS = 1e-6
H_A = 4
DH_A = 128
W_A = H_A * DH_A
CHUNK_A = 128
HQ_B = 8
HKV_B = 2
DH_B = 64
W_B = HQ_B * DH_B
ROPE_THETA = 10000.0
H_C = 16
DH_C = 64
W_C = H_C * DH_C
WIN_R = 8
WIN_C = 16

LANES = 128
NA_ROWS = 4
NA_TILES = 2
NA_RBS = 8
NA_BLKS = -(-(NA_ROWS + WIN_R - 1) // NA_ROWS)
NA_WIN = NA_BLKS * NA_ROWS
assert WIN_R // 2 == NA_ROWS
BF16_ROWS = 16
PROJ_SUBTILE = 256
OUTPROJ_TM = 1024
KEY_CHUNK = 256
QK_AHEAD = 3
LOG2E = 1.4426950408889634
QSCALE = DH_B ** -0.5 * LOG2E
assert DH_B == DH_C
MASKED = 1e30
VMEM_LIMIT = 56 * 1024 * 1024


def _cparams(sem):
    return pltpu.CompilerParams(dimension_semantics=sem, vmem_limit_bytes=VMEM_LIMIT)


def _silu(x):
    return x / (1.0 + jnp.exp(-x))


def _sigmoid(x):
    return 1.0 / (1.0 + jnp.exp(-x))


def _log_sigmoid(x):
    return jnp.minimum(x, 0.0) - jnp.log1p(jnp.exp(-jnp.abs(x)))


def _dot_nt(a, b):
    return lax.dot_general(a, b, (((1,), (1,)), ((), ())), preferred_element_type=F32)


MOD_ROWS = 3


def _mod_kernel(c_ref, w_ref, b_ref, o_ref, sb_sc):
    d, tn = w_ref.shape[1], w_ref.shape[2]
    sub = 8

    @pl.when(jnp.logical_and(pl.program_id(0) == 0, pl.program_id(1) == 0))
    def _():
        s = _silu(c_ref[...])
        for r in range(MOD_ROWS):
            sb_sc[r] = jnp.broadcast_to(s[r:r + 1, :], (LANES, d)).T

    def body(g, acc):
        rows = pl.ds(pl.multiple_of(g * sub, sub), sub)
        w = w_ref[0, rows, :]
        out = []
        for r in range(MOD_ROWS):
            sb = sb_sc[r, rows, :]
            out.append([acc[r][j] + sb * w[:, j * LANES:(j + 1) * LANES] for j in range(tn // LANES)])
        return out

    zero = jnp.zeros((sub, LANES), F32)
    acc = lax.fori_loop(0, d // sub, body, [[zero] * (tn // LANES) for _ in range(MOD_ROWS)], unroll=8)
    rows = [jnp.sum(jnp.concatenate(a, axis=1), axis=0, keepdims=True) + b_ref[0] for a in acc]
    o_ref[0] = jnp.concatenate(rows + [jnp.zeros((8 - MOD_ROWS, tn), F32)], axis=0)


def _modulation(cvec, w_mod, b_mod):
    depth, d, n = w_mod.shape
    tn = n // 4
    return pl.pallas_call(
        _mod_kernel,
        grid=(depth, n // tn),
        in_specs=[pl.BlockSpec((8, d), lambda l, j: (0, 0)),
                  pl.BlockSpec((1, d, tn), lambda l, j: (l, 0, j)),
                  pl.BlockSpec((1, 1, tn), lambda l, j: (l, 0, j))],
        out_specs=pl.BlockSpec((1, 8, tn), lambda l, j: (l, 0, j)),
        out_shape=jax.ShapeDtypeStruct((depth, 8, n), F32),
        scratch_shapes=[pltpu.VMEM((MOD_ROWS, d, LANES), F32)],
        compiler_params=_cparams(("arbitrary", "arbitrary")),
        name="modulation",
    )(cvec, w_mod, b_mod.reshape(depth, 1, n))


def _head_norm(x, g):
    lo = lax.broadcasted_iota(jnp.int32, x.shape, 1) < DH_B
    x2 = x * x
    s_lo = jnp.sum(jnp.where(lo, x2, 0.0), axis=-1, keepdims=True)
    s_hi = jnp.sum(jnp.where(lo, 0.0, x2), axis=-1, keepdims=True)
    ms = jnp.where(lo, s_lo, s_hi) * (1.0 / DH_B)
    return x * lax.rsqrt(ms + EPS) * g


def _rope(x, cos, sin):
    quarter = DH_B // 4
    first = (lax.broadcasted_iota(jnp.int32, x.shape, 1) % (2 * quarter)) < quarter
    partner = jnp.where(first, pltpu.roll(x, LANES - quarter, 1), pltpu.roll(x, quarter, 1))
    return x * cos + partner * sin


def _store_t(ref, val, t_block, tok0):
    n_tok = val.shape[1]
    if t_block is None:
        ref[0, :, tok0:tok0 + n_tok] = val.astype(ref.dtype)
    else:
        for i in range(n_tok // t_block):
            ref[0, tok0 // t_block + i] = val[:, i * t_block:(i + 1) * t_block].astype(ref.dtype)


def _proj_kernel(*refs, n_seg, t_sizes, t_blocks, t_plain, gate_rows, gqa, row0, per_batch):
    x_ref, g_ref, mod_ref = refs[:3]
    pos = 3
    w_refs = refs[pos:pos + n_seg]
    pos += n_seg
    if t_sizes:
        wt_ref = refs[pos]
        pos += 1
    if gate_rows:
        bgt_ref = refs[pos]
        pos += 1
    if gqa is not None:
        rope, knt_block = gqa
        gq_ref, gk_ref = refs[pos:pos + 2]
        pos += 2
        if rope:
            cos_ref, sin_ref = refs[pos:pos + 2]
            pos += 2
    n_plain = n_seg - (1 if gqa is not None else 0)
    o_refs = refs[pos:pos + n_plain]
    pos += n_plain
    ot_refs = refs[pos:pos + len(t_blocks)]
    pos += len(t_blocks)
    otp_refs = {}
    for i, flag in enumerate(t_plain):
        if flag:
            otp_refs[i] = refs[pos]
            pos += 1
    if gate_rows:
        gto_ref = refs[pos]
        pos += 1
    if gqa is not None:
        q_out, k_out = refs[pos:pos + 2]
        pos += 2
        if knt_block is not None:
            knt_out = refs[pos]
            pos += 1
    t_offs = [0]
    for n in t_sizes:
        t_offs.append(t_offs[-1] + n)

    d = x_ref.shape[-1]
    tm = x_ref.shape[1]
    row = row0 + (pl.program_id(0) if per_batch else 0)
    shift = mod_ref[pl.ds(row, 1), 0:d]
    scale = mod_ref[pl.ds(row, 1), d:2 * d]
    sub = PROJ_SUBTILE
    for s in range(tm // sub):
        rows = slice(s * sub, (s + 1) * sub)
        x = x_ref[0, rows, :]
        r = lax.rsqrt(jnp.mean(x * x, axis=-1, keepdims=True) + EPS)
        h = (x * r * g_ref[...]) * (1.0 + scale) + shift
        hb = h.astype(BF16)
        if gqa is not None:
            res = jnp.dot(hb, w_refs[-1][...], preferred_element_type=F32)
        for w_ref, o_ref in zip(w_refs[:n_plain], o_refs):
            o_ref[0, rows, :] = jnp.dot(hb, w_ref[...], preferred_element_type=F32).astype(o_ref.dtype)
        if t_sizes:
            res_t = _dot_nt(wt_ref[...], hb)
            for i, (ot_ref, tb, off, end) in enumerate(zip(ot_refs, t_blocks, t_offs, t_offs[1:])):
                _store_t(ot_ref, res_t[off:end], tb, s * sub)
                if i in otp_refs:
                    otp_refs[i][0, rows, :] = res_t[off:end].T.astype(otp_refs[i].dtype)
            if gate_rows:
                off = t_offs[len(t_blocks)]
                gto_ref[0, :, rows] = res_t[off:off + gate_rows] + bgt_ref[...]
        if gqa is not None:
            nq = W_B // LANES
            for j in range(nq + 1):
                xn = _head_norm(res[:, j * LANES:(j + 1) * LANES], gq_ref[...] if j < nq else gk_ref[...])
                if j == nq and knt_block is not None:
                    _store_t(knt_out, xn.T, knt_block, s * sub)
                if rope:
                    xn = _rope(xn, cos_ref[rows, :], sin_ref[rows, :])
                if j < nq:
                    q_out[0, rows, j * LANES:(j + 1) * LANES] = (xn * QSCALE).astype(q_out.dtype)
                else:
                    k_out[0, rows, :] = xn.astype(k_out.dtype)


def _proj(x, g_pre, mod, segs, wt_all, tsegs, b_gates, gqa, *, row0, per_batch, tm):
    bsz, t, d = x.shape
    grid = (bsz, t // tm)
    const = lambda b, i: (0, 0)
    in_specs = [pl.BlockSpec((1, tm, d), lambda b, i: (b, i, 0)),
                pl.BlockSpec((1, d), const),
                pl.BlockSpec(mod.shape, const)]
    args = [x, g_pre.reshape(1, d), mod]
    plain_ws = [w for w, _ in segs] + ([gqa["w"]] if gqa is not None else [])
    for w in plain_ws + ([wt_all] if wt_all is not None else []):
        in_specs.append(pl.BlockSpec(w.shape, const))
        args.append(w)
    gate_rows = 0
    t_plain = tuple(len(ts) > 3 and ts[3] for ts in tsegs)
    tsegs = [ts[:3] for ts in tsegs]
    t_sizes = [n for n, _, _ in tsegs]
    if b_gates is not None:
        gate_rows = b_gates.shape[0]
        t_sizes.append(LANES)
        in_specs.append(pl.BlockSpec((gate_rows, 1), const))
        args.append(b_gates.reshape(gate_rows, 1))
    assert sum(t_sizes) == (0 if wt_all is None else wt_all.shape[0])
    if gqa is not None:
        in_specs += [pl.BlockSpec((1, LANES), const)] * 2
        args += [jnp.tile(gqa["g_q"], 2).reshape(1, LANES), jnp.tile(gqa["g_k"], 2).reshape(1, LANES)]
        if gqa["rope"] is not None:
            in_specs += [pl.BlockSpec((tm, LANES), lambda b, i: (i, 0))] * 2
            args += list(gqa["rope"])
    out_specs, out_shape = [], []
    for w, dt in segs:
        n = w.shape[1]
        out_specs.append(pl.BlockSpec((1, tm, n), lambda b, i: (b, i, 0)))
        out_shape.append(jax.ShapeDtypeStruct((bsz, t, n), dt))

    def add_t_out(n, dt, tb):
        if tb is None:
            out_specs.append(pl.BlockSpec((1, n, tm), lambda b, i: (b, 0, i)))
            out_shape.append(jax.ShapeDtypeStruct((bsz, n, t), dt))
        else:
            out_specs.append(pl.BlockSpec((1, tm // tb, n, tb), lambda b, i: (b, i, 0, 0)))
            out_shape.append(jax.ShapeDtypeStruct((bsz, t // tb, n, tb), dt))

    for n, dt, tb in tsegs:
        add_t_out(n, dt, tb)
    for (n, _, _), flag in zip(tsegs, t_plain):
        if flag:
            out_specs.append(pl.BlockSpec((1, tm, n), lambda b, i: (b, i, 0)))
            out_shape.append(jax.ShapeDtypeStruct((bsz, t, n), BF16))
    if b_gates is not None:
        add_t_out(gate_rows, F32, None)
    gqa_static = None
    if gqa is not None:
        out_specs += [pl.BlockSpec((1, tm, W_B), lambda b, i: (b, i, 0)),
                      pl.BlockSpec((1, tm, LANES), lambda b, i: (b, i, 0))]
        out_shape += [jax.ShapeDtypeStruct((bsz, t, W_B), BF16), jax.ShapeDtypeStruct((bsz, t, LANES), BF16)]
        if gqa["knt_block"] is not None:
            add_t_out(LANES, F32, gqa["knt_block"])
        gqa_static = (gqa["rope"] is not None, gqa["knt_block"])
    kern = functools.partial(_proj_kernel, n_seg=len(plain_ws), t_sizes=tuple(t_sizes),
                             t_blocks=tuple(tb for _, _, tb in tsegs), t_plain=t_plain,
                             gate_rows=gate_rows, gqa=gqa_static,
                             row0=row0, per_batch=per_batch)
    return pl.pallas_call(
        kern, grid=grid, in_specs=in_specs, out_specs=out_specs, out_shape=out_shape,
        compiler_params=_cparams(("arbitrary", "arbitrary")), name="in_proj",
    )(*args)


def _outproj_kernel(*refs, n_in, row0, per_batch):
    a_refs = refs[:n_in]
    z_ref, w_ref, x_ref, gp_ref, mod_ref, o_ref = refs[n_in:n_in + 6]
    d = x_ref.shape[-1]
    row = row0 + (pl.program_id(0) if per_batch else 0)
    gate = mod_ref[pl.ds(row, 1), 2 * d:3 * d]
    z = z_ref[0].astype(F32)
    sz = _silu(z)
    acc = None
    off = 0
    for a_ref in a_refs:
        kk = a_ref.shape[-1]
        y = (a_ref[0].astype(F32) * sz[:, off:off + kk]).astype(BF16)
        part = jnp.dot(y, w_ref[off:off + kk, :], preferred_element_type=F32)
        acc = part if acc is None else acc + part
        off += kk
    r = lax.rsqrt(jnp.mean(acc * acc, axis=-1, keepdims=True) + EPS)
    o_ref[0] = x_ref[0] + gate * (acc * r * gp_ref[...])


def _outproj(parts, z, w_out, x, g_post, mod, *, row0, per_batch, tm):
    bsz, t, d = x.shape
    grid = (bsz, t // tm)
    in_specs, args = [], []
    for a in parts:
        in_specs.append(pl.BlockSpec((1, tm, a.shape[-1]), lambda b, i: (b, i, 0)))
        args.append(a)
    in_specs += [pl.BlockSpec((1, tm, z.shape[-1]), lambda b, i: (b, i, 0)),
                 pl.BlockSpec(w_out.shape, lambda b, i: (0, 0)),
                 pl.BlockSpec((1, tm, d), lambda b, i: (b, i, 0)),
                 pl.BlockSpec((1, d), lambda b, i: (0, 0)),
                 pl.BlockSpec(mod.shape, lambda b, i: (0, 0))]
    args += [z, w_out, x, g_post.reshape(1, d), mod]
    kern = functools.partial(_outproj_kernel, n_in=len(parts), row0=row0, per_batch=per_batch)
    return pl.pallas_call(
        kern, grid=grid, in_specs=in_specs,
        out_specs=pl.BlockSpec((1, tm, d), lambda b, i: (b, i, 0)),
        out_shape=jax.ShapeDtypeStruct((bsz, t, d), F32),
        compiler_params=_cparams(("arbitrary", "arbitrary")), name="out_proj",
    )(*args)


def _split3_bf16(x):
    hi = x.astype(BF16)
    r1 = x - hi.astype(F32)
    mid = r1.astype(BF16)
    lo = (r1 - mid.astype(F32)).astype(BF16)
    return hi, mid, lo


def _mlstm_gate_rows(ig_f, fg_f, ig_b, fg_b):
    rr, ll = fg_f.shape
    lf = _log_sigmoid(jnp.concatenate([fg_f, fg_b], axis=0))
    pieces = jnp.concatenate(_split3_bf16(lf), axis=0)
    u = lax.broadcasted_iota(jnp.int32, (ll, ll), 0)
    t = lax.broadcasted_iota(jnp.int32, (ll, ll), 1)
    out = []
    for d, ig in enumerate((ig_f, ig_b)):
        tri = jnp.where((u >= t) if d else (u <= t), 1.0, 0.0).astype(BF16)
        y = jnp.dot(pieces, tri, preferred_element_type=F32)
        rows = slice(d * rr, (d + 1) * rr)
        b = y[0:2 * rr][rows] + y[2 * rr:4 * rr][rows] + y[4 * rr:6 * rr][rows]
        c = ig - b
        cmax = jnp.broadcast_to(jnp.max(c, axis=-1, keepdims=True), c.shape)
        tot = jnp.broadcast_to(jnp.sum(lf[rows], axis=-1, keepdims=True), c.shape)
        out.append((c, b, cmax, tot))
    return out


def _mlstm_kernel(*refs, nc, hps, bps, unroll, has_init, emit_state):
    q_ref, k_ref, vt_ref, ot_ref, gtt_ref, ghn_ref = refs[:6]
    pos = 6
    if has_init:
        c0_ref, n0_ref, m0_ref = refs[pos:pos + 3]
        pos += 3
    ha_ref = refs[pos]
    pos += 1
    if emit_state:
        cout_ref, nout_ref, mout_ref = refs[pos:pos + 3]
        pos += 3
    hs_sc, gate_sc, c_sc, n_sc, m_sc = refs[pos:pos + 5]
    ll, dh = CHUNK_A, DH_A
    kscale = dh ** -0.5
    hd0 = pl.program_id(1) * hps
    chains = [(bb, i, d) for bb in range(bps) for i in range(hps) for d in range(2)]

    for ch in chains:
        bb, i, d = ch
        if has_init:
            c_sc[ch] = c0_ref[bb, d, i]
            n_sc[ch] = n0_ref[bb, i, d:d + 1, :]
            m_sc[ch] = m0_ref[bb, i, d:d + 1, :]
        else:
            c_sc[ch] = jnp.zeros((dh, dh), F32)
            n_sc[ch] = jnp.zeros((1, dh), F32)
            m_sc[ch] = jnp.zeros((1, dh), F32)

    for bb in range(bps):
        def gate_rows(col0, bb=bb):
            return gtt_ref[bb, pl.ds(pl.multiple_of((col0 + hd0) * nc, 8), hps * nc), :]

        gates = _mlstm_gate_rows(gate_rows(0), gate_rows(H_A), gate_rows(2 * H_A), gate_rows(3 * H_A))
        for d in range(2):
            for kind in range(4):
                gate_sc[bb, d, kind] = gates[d][kind]

    ghn_t = [jnp.broadcast_to(ghn_ref[:, i * dh:(i + 1) * dh], (ll, dh)).T for i in range(hps)]
    si = lax.broadcasted_iota(jnp.int32, (ll, ll), 0)
    ti = lax.broadcasted_iota(jnp.int32, (ll, ll), 1)

    def run_trip(j, first_touch):
        jobs = []
        for u in range(unroll):
            step = j * unroll + u
            for ch in chains:
                jobs.append((ch, step if ch[2] == 0 else nc - 1 - step))
        state = {ch: [c_sc[ch], n_sc[ch], m_sc[ch]] for ch in chains}
        hcols = [slice(i * dh, (i + 1) * dh) for i in range(hps)]

        def tok_rows(cidx):
            return pl.ds(pl.multiple_of(cidx * ll, ll), ll)

        rows = []
        for ch, cidx in jobs:
            bb, i, d = ch
            c_r, b_r, cmax_r, tot_r = (gate_sc[bb, d, kind, pl.ds(i * nc + cidx, 1), :] for kind in range(4))
            m_st = state[ch][2]
            m_c = jnp.maximum(m_st, cmax_r)
            state[ch][2] = tot_r + m_c
            rows.append((c_r, b_r, m_st, jnp.exp(m_st - m_c), jnp.exp(c_r - m_c) * kscale))
        start = []
        for (ch, cidx), (_, _, _, a_st, wk) in zip(jobs, rows):
            bb, i, _ = ch
            k = k_ref[bb, tok_rows(cidx), hcols[i]]
            vt = vt_ref[bb, cidx, hcols[i], :]
            vw = jnp.concatenate([vt.astype(F32) * wk, jnp.broadcast_to(wk, (BF16_ROWS, ll))], axis=0)
            upd = jnp.dot(vw.astype(BF16), k, preferred_element_type=F32)
            c_st, n_st, _ = state[ch]
            start.append((c_st, n_st))
            state[ch][0] = a_st * c_st + upd[:dh]
            state[ch][1] = a_st * n_st + upd[dh:dh + 1]
        prods = []
        for (ch, cidx), (c_st, n_st) in zip(jobs, start):
            bb, i, _ = ch
            q = q_ref[bb, tok_rows(cidx), hcols[i]]
            k = k_ref[bb, tok_rows(cidx), hcols[i]]
            n16 = jnp.broadcast_to(n_st.astype(BF16), (BF16_ROWS, dh))
            prods.append(_dot_nt(jnp.concatenate([k, c_st.astype(BF16), n16], axis=0), q))
        for (ch, cidx), (c_r, b_r, m_st, _, _), r in zip(jobs, rows, prods):
            bb, i, d = ch
            allowed = (si >= ti) if d else (si <= ti)
            cb = jnp.where(allowed, jnp.broadcast_to(c_r, (ll, ll)).T, -jnp.inf)
            big_m = jnp.maximum(m_st, jnp.max(cb, axis=0, keepdims=True))
            p = jnp.exp(cb - (big_m - float(np.log(kscale)))) * r[:ll]
            w_inter = jnp.exp(m_st - big_m)
            vt = vt_ref[bb, cidx, hcols[i], :]
            num = jnp.dot(vt, p.astype(BF16), preferred_element_type=F32) - r[ll:ll + dh] * (-w_inter)
            den = w_inter * r[ll + dh:ll + dh + 1] + jnp.sum(p, axis=0, keepdims=True)
            h_t = num / jnp.maximum(jnp.abs(den), jnp.exp(-(b_r + big_m)))
            if first_touch:
                hs_sc[bb, i, cidx] = h_t
            else:
                hsum = h_t + hs_sc[bb, i, cidx]
                rn = lax.rsqrt(jnp.mean(hsum * hsum, axis=0, keepdims=True) + EPS)
                out_t = (hsum * rn * ghn_t[i]) * _sigmoid(ot_ref[bb, cidx, hcols[i], :].astype(F32))
                ha_ref[bb, tok_rows(cidx), hcols[i]] = out_t.T.astype(ha_ref.dtype)
        for ch in chains:
            c_sc[ch], n_sc[ch], m_sc[ch] = state[ch]

    def make_body(first_touch):
        def body(j, carry):
            run_trip(j, first_touch)
            return carry
        return body

    trips = nc // unroll
    lax.fori_loop(0, trips // 2, make_body(True), 0)
    lax.fori_loop(trips // 2, trips, make_body(False), 0)

    if emit_state:
        for ch in chains:
            bb, i, d = ch
            cout_ref[bb, 0, d, i] = c_sc[ch]
            nout_ref[bb, i, d:d + 1, :] = n_sc[ch]
            mout_ref[bb, i, d:d + 1, :] = m_sc[ch]


def _mlstm(qk, vt, ot, gates_t, g_hn, init, *, emit_state):
    bsz, t, _ = qk.shape
    nc = t // CHUNK_A
    hps, bps, unroll = (H_A, 2, 1) if nc < 8 else (1, 1, 8)
    assert (nc // 2) % unroll == 0 and nc % 2 == 0 and bsz % bps == 0
    gtt = gates_t.reshape(bsz, 4 * H_A * nc, CHUNK_A)
    wh = hps * DH_A

    def tblk():
        return pl.BlockSpec((bps, nc, wh, CHUNK_A), lambda b, g: (b, 0, g, 0))

    in_specs = [pl.BlockSpec((bps, t, wh), lambda b, g: (b, 0, g)),
                pl.BlockSpec((bps, t, wh), lambda b, g: (b, 0, H_A // hps + g)),
                tblk(), tblk(),
                pl.BlockSpec((bps, 4 * H_A * nc, CHUNK_A), lambda b, g: (b, 0, 0)),
                pl.BlockSpec((1, wh), lambda b, g: (0, g))]
    args = [qk, qk, vt, ot, gtt, g_hn.reshape(1, W_A)]
    if init is not None:
        c0, n0, m0 = init
        in_specs += [pl.BlockSpec((bps, 2, hps, DH_A, DH_A), lambda b, g: (b, 0, g, 0, 0)),
                     pl.BlockSpec((bps, hps, 2, DH_A), lambda b, g: (b, g, 0, 0)),
                     pl.BlockSpec((bps, hps, 2, DH_A), lambda b, g: (b, g, 0, 0))]
        args += [c0, n0, m0]
    out_specs = [pl.BlockSpec((bps, t, wh), lambda b, g: (b, 0, g))]
    out_shape = [jax.ShapeDtypeStruct((bsz, t, W_A), BF16)]
    if emit_state:
        out_specs += [pl.BlockSpec((bps, 1, 2, hps, DH_A, DH_A), lambda b, g: (b, 0, 0, g, 0, 0)),
                      pl.BlockSpec((bps, hps, 2, DH_A), lambda b, g: (b, g, 0, 0)),
                      pl.BlockSpec((bps, hps, 2, DH_A), lambda b, g: (b, g, 0, 0))]
        out_shape += [jax.ShapeDtypeStruct((bsz, 1, 2, H_A, DH_A, DH_A), F32),
                      jax.ShapeDtypeStruct((bsz, H_A, 2, DH_A), F32),
                      jax.ShapeDtypeStruct((bsz, H_A, 2, DH_A), F32)]
    kern = functools.partial(_mlstm_kernel, nc=nc, hps=hps, bps=bps, unroll=unroll,
                             has_init=init is not None, emit_state=emit_state)
    return pl.pallas_call(
        kern, grid=(bsz // bps, H_A // hps), in_specs=in_specs, out_specs=out_specs, out_shape=out_shape,
        scratch_shapes=[pltpu.VMEM((bps, hps, nc, DH_A, CHUNK_A), F32),
                        pltpu.VMEM((bps, 2, 4, hps * nc, CHUNK_A), F32),
                        pltpu.VMEM((bps, hps, 2, DH_A, DH_A), F32), pltpu.VMEM((bps, hps, 2, 1, DH_A), F32),
                        pltpu.VMEM((bps, hps, 2, 1, DH_A), F32)],
        compiler_params=_cparams(("arbitrary", "arbitrary")), name="mlstm",
    )(*args)


def _rope_tables(n_tok):
    t = np.arange(n_tok)
    row = (t // GRID_W).astype(np.float32)
    colp = (t % GRID_W).astype(np.float32)
    quarter = DH_B // 4
    freqs = (np.float32(ROPE_THETA) ** (-np.arange(quarter, dtype=np.float32) / np.float32(quarter))).astype(np.float32)
    ar = row[:, None] * freqs
    ac = colp[:, None] * freqs
    cos = np.concatenate([np.cos(ar), np.cos(ar), np.cos(ac), np.cos(ac)], axis=-1)
    sin = np.concatenate([-np.sin(ar), np.sin(ar), -np.sin(ac), np.sin(ac)], axis=-1)
    return jnp.asarray(np.tile(cos, (1, 2)), F32), jnp.asarray(np.tile(sin, (1, 2)), F32)


def _split_heads_q(q):
    lo = lax.broadcasted_iota(jnp.int32, q.shape, 1) < (LANES // 2)
    zero = jnp.zeros_like(q)
    return jnp.concatenate([jnp.where(lo, q, zero), jnp.where(lo, zero, q)], axis=0)


def _merge_heads_o(o_t, tq):
    half = LANES // 2
    return jnp.concatenate([o_t[:half, :tq], o_t[half:, tq:]], axis=0).T


def _with_ones_rows(vt):
    return jnp.concatenate([vt, jnp.ones((BF16_ROWS, vt.shape[1]), BF16)], axis=0)


def _online_softmax_pv_t(problems):
    seq = [(pi, ci) for pi, (_, chunks, _) in enumerate(problems) for ci in range(len(chunks))]
    scores = {}

    def issue(t):
        pi, ci = seq[t]
        qm, chunks, _ = problems[pi]
        scores[(pi, ci)] = _dot_nt(chunks[ci][0], qm)

    for t in range(min(QK_AHEAD, len(seq))):
        issue(t)
    m = acc = None
    for t, (pi, ci) in enumerate(seq):
        if t + QK_AHEAD < len(seq):
            issue(t + QK_AHEAD)
        _, chunks, emit = problems[pi]
        _, vt, penalty = chunks[ci]
        s = scores.pop((pi, ci))
        if penalty is not None:
            s = s - penalty
        mc = jnp.max(s, axis=0, keepdims=True)
        m_new = mc if ci == 0 else jnp.maximum(m, mc)
        p = jnp.exp2(s - m_new).astype(BF16)
        part = jnp.dot(_with_ones_rows(vt), p, preferred_element_type=F32)
        acc = part if ci == 0 else jnp.exp2(m - m_new) * acc + part
        m = m_new
        if ci == len(chunks) - 1:
            dv = vt.shape[0]
            emit(acc[:dv] / acc[dv:dv + 1])


def _attn_kernel(*refs, n_tiles, kv_shared, has_cache):
    q_ref, k_ref, vt_ref = refs[:3]
    pos = 3
    if has_cache:
        kc_ref, vtc_ref = refs[pos:pos + 2]
        pos += 2
    o_ref = refs[pos]
    tq = q_ref.shape[1]
    s_len = k_ref.shape[1]
    sc = min(s_len, KEY_CHUNK)
    problems = []
    for j in range(n_tiles):
        kj = 0 if kv_shared else j
        kcols = slice(kj * LANES, (kj + 1) * LANES)
        qm = _split_heads_q(q_ref[0, :, j * LANES:(j + 1) * LANES])
        chunks = [(k_ref[0, c * sc:(c + 1) * sc, kcols].astype(BF16),
                   vt_ref[0, kcols, c * sc:(c + 1) * sc].astype(BF16), None) for c in range(s_len // sc)]
        if has_cache:
            chunks.append((kc_ref[0], vtc_ref[0], None))

        def emit(o_t, j=j):
            o_ref[0, :, j * LANES:(j + 1) * LANES] = _merge_heads_o(o_t, tq).astype(o_ref.dtype)

        problems.append((qm, chunks, emit))
    _online_softmax_pv_t(problems)


def _attention(q, k, vt, cache, *, kv_shared, tq, n_tiles):
    bsz, t, w = q.shape
    s = k.shape[1]
    wt = n_tiles * LANES
    if kv_shared:
        k_spec = pl.BlockSpec((1, s, LANES), lambda b, i, g: (b, 0, 0))
        vt_spec = pl.BlockSpec((1, LANES, s), lambda b, i, g: (b, 0, 0))
    else:
        k_spec = pl.BlockSpec((1, s, wt), lambda b, i, g: (b, 0, g))
        vt_spec = pl.BlockSpec((1, wt, s), lambda b, i, g: (b, g, 0))
    in_specs = [pl.BlockSpec((1, tq, wt), lambda b, i, g: (b, i, g)), k_spec, vt_spec]
    args = [q, k, vt]
    if cache is not None:
        p = cache[0].shape[1]
        in_specs += [pl.BlockSpec((1, p, LANES), lambda b, i, g: (b, 0, 0)),
                     pl.BlockSpec((1, LANES, p), lambda b, i, g: (b, 0, 0))]
        args += list(cache)
    kern = functools.partial(_attn_kernel, n_tiles=n_tiles, kv_shared=kv_shared,
                             has_cache=cache is not None)
    return pl.pallas_call(
        kern, grid=(bsz, t // tq, w // wt), in_specs=in_specs,
        out_specs=pl.BlockSpec((1, tq, wt), lambda b, i, g: (b, i, g)),
        out_shape=jax.ShapeDtypeStruct((bsz, t, w), BF16),
        compiler_params=_cparams(("arbitrary", "arbitrary", "arbitrary")), name="attention",
    )(*args)


def _na_bias_blocks(variant):
    out = {}
    for qr in range(NA_ROWS):
        for kr in range(NA_WIN):
            if variant == 0:
                dr = kr - qr if kr < WIN_R else None
            elif variant == 1:
                dr = kr - qr - WIN_R // 2 if qr <= kr < qr + WIN_R else None
            else:
                dr = kr - qr - (NA_WIN - NA_ROWS) if kr >= NA_WIN - WIN_R else None
            out[(qr, kr)] = None if dr is None else dr + WIN_R - 1
    return out


def _na_kernel(q_ref, k_ref, vt_ref, kc_ref, vtc_ref, bc_ref, o_ref, bias_sc, *, n_rb):
    w = GRID_W
    tq = NA_ROWS * w

    @pl.when(jnp.logical_and(pl.program_id(1) == 0, pl.program_id(2) == 0))
    def _():
        ck = lax.broadcasted_iota(jnp.int32, (w, w), 0)
        cs = jnp.clip(lax.broadcasted_iota(jnp.int32, (w, w), 1) - WIN_C // 2, 0, w - WIN_C)
        valid = jnp.logical_and(ck >= cs, ck < cs + WIN_C)
        masked = jnp.full((w, w), MASKED, F32)
        tiles = {}

        def tile(hh, di):
            if (hh, di) not in tiles:
                rows = jnp.broadcast_to(bc_ref[hh, di:di + 1, :], (w, LANES))
                rolled = pltpu.roll(rows, 0, 1, stride=1, stride_axis=0)
                tiles[(hh, di)] = jnp.where(valid, rolled[:, :w], masked)
            return tiles[(hh, di)]

        for variant in range(3):
            for (qr, kr), di in _na_bias_blocks(variant).items():
                for hh in range(2 * NA_TILES):
                    val = masked if di is None else tile(hh, di)
                    bias_sc[variant, kr * w:(kr + 1) * w, hh * tq + qr * w:hh * tq + (qr + 1) * w] = val

    n_rows = n_rb * NA_ROWS
    problems = []
    for rr in range(NA_RBS):
        rb = pl.program_id(2) * NA_RBS + rr
        variant = jnp.where(rb == 0, 0, jnp.where(rb == n_rb - 1, 2, 1))
        ws = jnp.clip(rb * NA_ROWS - WIN_R // 2, 0, n_rows - NA_WIN)
        blk0 = ws // NA_ROWS
        start = ws * w
        qrows = slice(rr * tq, (rr + 1) * tq)
        for j in range(NA_TILES):
            cols = slice(j * LANES, (j + 1) * LANES)
            qcols = slice(j * 2 * tq, (j + 1) * 2 * tq)
            qm = _split_heads_q(q_ref[0, qrows, cols])
            chunks = [(k_ref[0, pl.ds(pl.multiple_of(start + i * tq, tq), tq), cols],
                       vt_ref[0, blk0 + i, cols, :],
                       bias_sc[variant, i * tq:(i + 1) * tq, qcols]) for i in range(NA_BLKS)]
            chunks.append((kc_ref[0, :, cols], vtc_ref[0, cols, :], None))

            def emit(o_t, qrows=qrows, cols=cols):
                o_ref[0, qrows, cols] = _merge_heads_o(o_t, tq).astype(o_ref.dtype)

            problems.append((qm, chunks, emit))
    _online_softmax_pv_t(problems)


def _na_bias_table(rpb):
    pad = jnp.zeros(rpb.shape[:2] + (LANES - (2 * WIN_C - 1),), F32)
    return jnp.concatenate([rpb[..., WIN_C - 1::-1], pad, rpb[..., :WIN_C - 1:-1]], axis=-1) * -LOG2E


def _na_attention(q, k, vt, kc, vtc, rpb):
    bsz, t, w = q.shape
    p = kc.shape[1]
    tq = NA_ROWS * GRID_W
    n_rb = t // tq
    bc = _na_bias_table(rpb)
    wt = NA_TILES * LANES
    tqs = NA_RBS * tq
    return pl.pallas_call(
        functools.partial(_na_kernel, n_rb=n_rb),
        grid=(w // wt, bsz, n_rb // NA_RBS),
        in_specs=[pl.BlockSpec((1, tqs, wt), lambda j, b, r: (b, r, j)),
                  pl.BlockSpec((1, t, wt), lambda j, b, r: (b, 0, j)),
                  pl.BlockSpec((1, n_rb, wt, tq), lambda j, b, r: (b, 0, j, 0)),
                  pl.BlockSpec((1, p, wt), lambda j, b, r: (b, 0, j)),
                  pl.BlockSpec((1, wt, p), lambda j, b, r: (b, j, 0)),
                  pl.BlockSpec((2 * NA_TILES, 2 * WIN_R - 1, LANES), lambda j, b, r: (j, 0, 0))],
        out_specs=pl.BlockSpec((1, tqs, wt), lambda j, b, r: (b, r, j)),
        out_shape=jax.ShapeDtypeStruct((bsz, t, w), BF16),
        scratch_shapes=[pltpu.VMEM((3, NA_WIN * GRID_W, NA_TILES * 2 * tq), F32)],
        compiler_params=_cparams(("arbitrary", "arbitrary", "arbitrary")), name="na_attention",
    )(q, k, vt, kc, vtc, bc)


def _relayout_kernel(w_ref, *o_refs, plans, transposed, axis):
    w = w_ref[...]
    tile = w.shape[1 - axis]
    for o_ref, pieces, tr in zip(o_refs, plans, transposed):
        vals = []
        for src, size, scale in pieces:
            if src is None:
                val = jnp.zeros((size, tile) if axis == 0 else (tile, size), F32)
            else:
                val = lax.slice_in_dim(w, src, src + size, axis=axis)
                if scale != 1.0:
                    val = val * scale
            vals.append(val)
        val = vals[0] if len(vals) == 1 else jnp.concatenate(vals, axis=axis)
        o_ref[...] = (val.T if tr else val).astype(o_ref.dtype)


def _relayout(w, plans, transposed, *, axis, tile):
    grid = (w.shape[1 - axis] // tile,)
    full = w.shape[1 - axis]

    def spec(extent, tr):
        along_rows = (axis == 0) != tr
        shape = (extent, tile) if along_rows else (tile, extent)
        return pl.BlockSpec(shape, (lambda i: (0, i)) if along_rows else (lambda i: (i, 0)))

    def shape(extent, tr):
        return (extent, full) if (axis == 0) != tr else (full, extent)

    sizes = [sum(size for _, size, _ in pieces) for pieces in plans]
    return pl.pallas_call(
        functools.partial(_relayout_kernel, plans=plans, transposed=tuple(transposed), axis=axis),
        grid=grid, in_specs=[spec(w.shape[axis], False)],
        out_specs=[spec(n, tr) for n, tr in zip(sizes, transposed)],
        out_shape=[jax.ShapeDtypeStruct(shape(n, tr), BF16) for n, tr in zip(sizes, transposed)],
        compiler_params=_cparams(("arbitrary",)), name="w_relayout",
    )(w)


_GQA_PERM = np.array([0, 4, 1, 5, 2, 6, 3, 7])


def _tok_major(cache):
    b, h, p, dh = cache.shape
    return cache.transpose(0, 2, 1, 3).reshape(b, p, h * dh)


def _feat_major(cache):
    b, h, p, dh = cache.shape
    return cache.transpose(0, 1, 3, 2).reshape(b, h * dh, p)


def _head_major_from_t(x_t, n_heads):
    b, w, t = x_t.shape
    return jnp.swapaxes(x_t.reshape(b, 1, n_heads, w // n_heads, t), -1, -2)


def _even_layer(xp, xs, mod, g_pre, g_post, w_in, b_gates, g_hn, g_q, g_k, w_out, st_c, st_n, st_m,
                ck, cv, rope_tabs):
    o_v, o_o = 2 * W_A, 3 * W_A
    o_g = 4 * W_A
    o_qb = o_g + 4 * H_A
    o_kb = o_qb + W_B
    o_vb = o_kb + HKV_B * DH_B
    o_z = o_vb + HKV_B * DH_B
    def heads(src0):
        return [(src0 + int(h) * DH_B, DH_B, 1.0) for h in _GQA_PERM]

    n_g = 4 * H_A
    w_qk, w_gqa, w_z, wt_all = _relayout(
        jnp.swapaxes(w_in, 0, 1),
        [[(0, o_v, 1.0)],
         heads(o_qb) + [(o_kb, HKV_B * DH_B, 1.0)],
         [(o_z, W_A, 1.0)] + heads(o_z + W_A),
         [(o_v, 2 * W_A, 1.0), (o_vb, HKV_B * DH_B, 1.0), (o_g, n_g, 1.0), (None, LANES - n_g, 1.0)]],
        [True, True, True, False], axis=0, tile=256)
    (w_o,) = _relayout(w_out, [[(0, W_A, 1.0)] + heads(W_A)], [False], axis=0, tile=256)

    def stream(x, row0, per_batch, init, cache, rope, emit):
        bsz, t, _ = x.shape
        tm = 1024
        xf = x if per_batch else x.reshape(1, bsz * t, D_MODEL)
        n_vb = HKV_B * DH_B
        vb_seg = (n_vb, BF16, None) if per_batch else (n_vb, F32, t)
        gqa = dict(w=w_gqa, g_q=g_q, g_k=g_k, rope=rope, knt_block=t if emit else None)
        outs = _proj(xf, g_pre, mod, [(w_qk, BF16), (w_z, BF16)], wt_all,
                     [(W_A, BF16, CHUNK_A), (W_A, BF16, CHUNK_A), vb_seg], b_gates, gqa,
                     row0=row0, per_batch=per_batch, tm=tm)
        qk, z, vta, ota, vbt, gtt, qn, kn = outs[:8]
        if not per_batch:
            qk, z, qn, kn = (a.reshape(bsz, t, a.shape[-1]) for a in (qk, z, qn, kn))
            gtt = gtt.reshape(4 * H_A, bsz, t).transpose(1, 0, 2)
            vbt = vbt.reshape(bsz, HKV_B * DH_B, t)
            vta, ota = (a.reshape(bsz, t // CHUNK_A, W_A, CHUNK_A) for a in (vta, ota))
        res = _mlstm(qk, vta, ota, gtt, g_hn, init, emit_state=emit)
        ha = res[0]
        if per_batch:
            hb = _attention(qn, kn, vbt, cache, kv_shared=True, tq=512, n_tiles=4)
        else:
            hb = _attention(qn, kn, vbt, cache, kv_shared=True, tq=t, n_tiles=W_B // LANES)
        y = _outproj([ha.reshape(xf.shape[0], -1, W_A), hb.reshape(xf.shape[0], -1, W_B)],
                     z.reshape(xf.shape[0], -1, W_A + W_B), w_o, xf, g_post, mod,
                     row0=row0, per_batch=per_batch, tm=OUTPROJ_TM)
        knt = outs[8].reshape(bsz, HKV_B * DH_B, t) if emit else None
        return y.reshape(bsz, t, D_MODEL), res[1:], knt, vbt

    yp, st, knt_p, vbt_p = stream(xp, 0, False, None, None, None, True)
    n0 = st_n.transpose(0, 2, 1, 3)
    m0 = jnp.broadcast_to(st_m.transpose(0, 2, 1)[..., None], n0.shape)
    cache = (_tok_major(ck).astype(BF16), _feat_major(cv).astype(BF16))
    ys, _, _, _ = stream(xs, 1, True, (st_c, n0, m0), cache, rope_tabs, False)
    c_out, n_out, m_out = st
    new_n = n_out.transpose(0, 2, 1, 3)[:, None]
    new_m = m_out[..., 0].transpose(0, 2, 1)[:, None]
    return (yp, ys, c_out, new_n, new_m, _head_major_from_t(knt_p, HKV_B), _head_major_from_t(vbt_p, HKV_B))


def _odd_layer(xp, xs, mod, g_pre, g_post, w_in, rpb, w_out, ck, cv):
    w_q, w_k, w_z, wt_kv = _relayout(
        w_in, [[(0, W_C, QSCALE)], [(W_C, W_C, 1.0)], [(3 * W_C, W_C, 1.0)], [(W_C, 2 * W_C, 1.0)]],
        [False, False, False, True], axis=1, tile=256)
    (w_o,) = _relayout(w_out, [[(0, W_C, 1.0)]], [False], axis=0, tile=256)
    tm = 512
    bsz, t, _ = xp.shape
    xf = xp.reshape(1, bsz * t, D_MODEL)
    q, z, kt, vt, k = _proj(xf, g_pre, mod, [(w_q, BF16), (w_z, BF16)], wt_kv,
                            [(W_C, F32, t, True), (W_C, F32, t)], None, None, row0=0, per_batch=False, tm=tm)
    kt = kt.reshape(bsz, W_C, t)
    vt = vt.reshape(bsz, W_C, t)
    o = _attention(q.reshape(bsz, t, W_C), k.reshape(bsz, t, W_C), vt, None,
                   kv_shared=False, tq=t, n_tiles=W_C // LANES)
    yp = _outproj([o.reshape(1, bsz * t, W_C)], z, w_o, xf, g_post, mod, row0=0, per_batch=False,
                  tm=OUTPROJ_TM)
    yp = yp.reshape(bsz, t, D_MODEL)
    q, k, z, vts = _proj(xs, g_pre, mod, [(w_q, BF16), (w_k, BF16), (w_z, BF16)], wt_kv[W_C:],
                         [(W_C, BF16, NA_ROWS * GRID_W)], None, None, row0=1, per_batch=True, tm=2 * tm)
    o = _na_attention(q, k, vts, _tok_major(ck).astype(BF16), _feat_major(cv).astype(BF16), rpb)
    ys = _outproj([o], z, w_o, xs, g_post, mod, row0=1, per_batch=True, tm=OUTPROJ_TM)
    return yp, ys, _head_major_from_t(kt, H_C), _head_major_from_t(vt, H_C)


def kernel(x_prompt, x_sample, state_mlstm_C, state_mlstm_n, state_mlstm_m, cache_gqa_k, cache_gqa_v,
           cache_na_k, cache_na_v, c, c_ctx, w_mod, b_mod, g_pre, g_post, w_in_ab, b_gates_ab, g_hnorm_a,
           g_qnorm_b, g_knorm_b, w_out_ab, w_in_c, rpb_c, w_out_c):
    depth = w_mod.shape[0]
    assert depth == 2 and c.shape[0] == 2
    cvec = jnp.concatenate([c_ctx[None], c, jnp.zeros((8 - 1 - c.shape[0], D_MODEL), F32)], axis=0)
    mod = _modulation(cvec, w_mod, b_mod)
    rope_tabs = _rope_tables(x_sample.shape[1])
    xp, xs, c_out, n_out, m_out, gk, gv = _even_layer(
        x_prompt, x_sample, mod[0], g_pre[0], g_post[0], w_in_ab[0], b_gates_ab[0], g_hnorm_a[0],
        g_qnorm_b[0], g_knorm_b[0], w_out_ab[0], state_mlstm_C[:, 0], state_mlstm_n[:, 0],
        state_mlstm_m[:, 0], cache_gqa_k[:, 0], cache_gqa_v[:, 0], rope_tabs)
    xp, xs, nk, nv = _odd_layer(xp, xs, mod[1], g_pre[1], g_post[1], w_in_c[0], rpb_c[0], w_out_c[0],
                                cache_na_k[:, 0], cache_na_v[:, 0])
    return (xp, xs, c_out, n_out, m_out, gk, gv, nk, nv)
```

```python
import functools

import jax
import jax.numpy as jnp
import numpy as np
from jax import lax
from jax.experimental import pallas as pl
from jax.experimental.pallas import tpu as pltpu

F32 = jnp.float32
BF16 = jnp.bfloat16

D_MODEL = 1024
GRID_W = 64
EPS = 1e-6
H_A = 4
DH_A = 128
W_A = H_A * DH_A
CHUNK_A = 128
HQ_B = 8
HKV_B = 2
DH_B = 64
W_B = HQ_B * DH_B
ROPE_THETA = 10000.0
H_C = 16
DH_C = 64
W_C = H_C * DH_C
WIN_R = 8
WIN_C = 16

LANES = 128
NA_ROWS = 4
NA_TILES = 2
NA_RBS = 8
NA_BLKS = -(-(NA_ROWS + WIN_R - 1) // NA_ROWS)
NA_WIN = NA_BLKS * NA_ROWS
assert WIN_R // 2 == NA_ROWS
BF16_ROWS = 16
PROJ_SUBTILE = 256
OUTPROJ_TM = 1024
KEY_CHUNK = 256
QK_AHEAD = 3
LOG2E = 1.4426950408889634
QSCALE = DH_B ** -0.5 * LOG2E
assert DH_B == DH_C
MASKED = 1e30
VMEM_LIMIT = 56 * 1024 * 1024


def _cparams(sem):
    return pltpu.CompilerParams(dimension_semantics=sem, vmem_limit_bytes=VMEM_LIMIT)


def _silu(x):
    return x / (1.0 + jnp.exp(-x))


def _sigmoid(x):
    return 1.0 / (1.0 + jnp.exp(-x))


def _log_sigmoid(x):
    return jnp.minimum(x, 0.0) - jnp.log1p(jnp.exp(-jnp.abs(x)))


def _dot_nt(a, b):
    return lax.dot_general(a, b, (((1,), (1,)), ((), ())), preferred_element_type=F32)


MOD_ROWS = 3
MOD_COLS = 1536


def _mod_kernel(c_ref, w_ref, b_ref, o_ref, sb_sc):
    tk, n = w_ref.shape[1], w_ref.shape[2]
    d = sb_sc.shape[1]
    sub = 8
    kb = pl.program_id(1)

    @pl.when(jnp.logical_and(pl.program_id(0) == 0, kb == 0))
    def _():
        s = _silu(c_ref[...])
        for r in range(MOD_ROWS):
            sb_sc[r] = jnp.broadcast_to(s[r:r + 1, :], (LANES, d)).T

    @pl.when(kb == 0)
    def _():
        o_ref[0] = jnp.concatenate([jnp.broadcast_to(b_ref[0], (MOD_ROWS, n)),
                                    jnp.zeros((8 - MOD_ROWS, n), F32)], axis=0)

    nt = MOD_COLS // LANES
    for c0 in range(0, n, MOD_COLS):
        def body(g, acc, c0=c0):
            rows = pl.multiple_of(g * sub, sub)
            w = w_ref[0, pl.ds(rows, sub), c0:c0 + MOD_COLS]
            out = []
            for r in range(MOD_ROWS):
                sb = sb_sc[r, pl.ds(kb * tk + rows, sub), :]
                out.append([acc[r][j] + sb * w[:, j * LANES:(j + 1) * LANES] for j in range(nt)])
            return out

        zero = jnp.zeros((sub, LANES), F32)
        acc = lax.fori_loop(0, tk // sub, body, [[zero] * nt for _ in range(MOD_ROWS)], unroll=8)
        for r, a in enumerate(acc):
            o_ref[0, r:r + 1, c0:c0 + MOD_COLS] += jnp.sum(jnp.concatenate(a, axis=1), axis=0, keepdims=True)


def _modulation(cvec, w_mod, b_mod):
    depth, d, n = w_mod.shape
    tk = d // 4
    return pl.pallas_call(
        _mod_kernel,
        grid=(depth, d // tk),
        in_specs=[pl.BlockSpec((8, d), lambda l, k: (0, 0)),
                  pl.BlockSpec((1, tk, n), lambda l, k: (l, k, 0)),
                  pl.BlockSpec((1, 1, n), lambda l, k: (l, 0, 0))],
        out_specs=pl.BlockSpec((1, 8, n), lambda l, k: (l, 0, 0)),
        out_shape=jax.ShapeDtypeStruct((depth, 8, n), F32),
        scratch_shapes=[pltpu.VMEM((MOD_ROWS, d, LANES), F32)],
        compiler_params=_cparams(("arbitrary", "arbitrary")),
        name="modulation",
    )(cvec, w_mod, b_mod.reshape(depth, 1, n))


def _head_norm(x, g):
    lo = lax.broadcasted_iota(jnp.int32, x.shape, 1) < DH_B
    x2 = x * x
    s_lo = jnp.sum(jnp.where(lo, x2, 0.0), axis=-1, keepdims=True)
    s_hi = jnp.sum(jnp.where(lo, 0.0, x2), axis=-1, keepdims=True)
    ms = jnp.where(lo, s_lo, s_hi) * (1.0 / DH_B)
    return x * lax.rsqrt(ms + EPS) * g


def _rope(x, cos, sin):
    quarter = DH_B // 4
    first = (lax.broadcasted_iota(jnp.int32, x.shape, 1) % (2 * quarter)) < quarter
    partner = jnp.where(first, pltpu.roll(x, LANES - quarter, 1), pltpu.roll(x, quarter, 1))
    return x * cos + partner * sin


def _store_t(ref, val, t_block, tok0):
    n_tok = val.shape[1]
    if t_block is None:
        ref[0, :, tok0:tok0 + n_tok] = val.astype(ref.dtype)
    else:
        for i in range(n_tok // t_block):
            ref[0, tok0 // t_block + i] = val[:, i * t_block:(i + 1) * t_block].astype(ref.dtype)


def _proj_kernel(*refs, n_seg, t_sizes, t_blocks, t_plain, gate_rows, gqa, row0, per_batch):
    x_ref, g_ref, mod_ref = refs[:3]
    pos = 3
    w_refs = refs[pos:pos + n_seg]
    pos += n_seg
    if t_sizes:
        wt_ref = refs[pos]
        pos += 1
    if gate_rows:
        bgt_ref = refs[pos]
        pos += 1
    if gqa is not None:
        rope, knt_block = gqa
        gq_ref, gk_ref = refs[pos:pos + 2]
        pos += 2
        if rope:
            cos_ref, sin_ref = refs[pos:pos + 2]
            pos += 2
    n_plain = n_seg - (1 if gqa is not None else 0)
    o_refs = refs[pos:pos + n_plain]
    pos += n_plain
    ot_refs = refs[pos:pos + len(t_blocks)]
    pos += len(t_blocks)
    otp_refs = {}
    for i, flag in enumerate(t_plain):
        if flag:
            otp_refs[i] = refs[pos]
            pos += 1
    if gate_rows:
        gto_ref = refs[pos]
        pos += 1
    if gqa is not None:
        q_out, k_out = refs[pos:pos + 2]
        pos += 2
        if knt_block is not None:
            knt_out = refs[pos]
            pos += 1
    t_offs = [0]
    for n in t_sizes:
        t_offs.append(t_offs[-1] + n)

    d = x_ref.shape[-1]
    tm = x_ref.shape[1]
    row = row0 + (pl.program_id(0) if per_batch else 0)
    shift = mod_ref[pl.ds(row, 1), 0:d]
    scale = mod_ref[pl.ds(row, 1), d:2 * d]
    sub = PROJ_SUBTILE
    for s in range(tm // sub):
        rows = slice(s * sub, (s + 1) * sub)
        x = x_ref[0, rows, :]
        r = lax.rsqrt(jnp.mean(x * x, axis=-1, keepdims=True) + EPS)
        h = (x * r * g_ref[...]) * (1.0 + scale) + shift
        hb = h.astype(BF16)
        if gqa is not None:
            res = jnp.dot(hb, w_refs[-1][...], preferred_element_type=F32)
        for w_ref, o_ref in zip(w_refs[:n_plain], o_refs):
            o_ref[0, rows, :] = jnp.dot(hb, w_ref[...], preferred_element_type=F32).astype(o_ref.dtype)
        if t_sizes:
            res_t = _dot_nt(wt_ref[...], hb)
            for i, (ot_ref, tb, off, end) in enumerate(zip(ot_refs, t_blocks, t_offs, t_offs[1:])):
                _store_t(ot_ref, res_t[off:end], tb, s * sub)
                if i in otp_refs:
                    otp_refs[i][0, rows, :] = res_t[off:end].T.astype(otp_refs[i].dtype)
            if gate_rows:
                off = t_offs[len(t_blocks)]
                gto_ref[0, :, rows] = res_t[off:off + gate_rows] + bgt_ref[...]
        if gqa is not None:
            nq = W_B // LANES
            for j in range(nq + 1):
                xn = _head_norm(res[:, j * LANES:(j + 1) * LANES], gq_ref[...] if j < nq else gk_ref[...])
                if j == nq and knt_block is not None:
                    _store_t(knt_out, xn.T, knt_block, s * sub)
                if rope:
                    xn = _rope(xn, cos_ref[rows, :], sin_ref[rows, :])
                if j < nq:
                    q_out[0, rows, j * LANES:(j + 1) * LANES] = (xn * QSCALE).astype(q_out.dtype)
                else:
                    k_out[0, rows, :] = xn.astype(k_out.dtype)


def _proj(x, g_pre, mod, segs, wt_all, tsegs, b_gates, gqa, *, row0, per_batch, tm):
    bsz, t, d = x.shape
    grid = (bsz, t // tm)
    const = lambda b, i: (0, 0)
    in_specs = [pl.BlockSpec((1, tm, d), lambda b, i: (b, i, 0)),
                pl.BlockSpec((1, d), const),
                pl.BlockSpec(mod.shape, const)]
    args = [x, g_pre.reshape(1, d), mod]
    plain_ws = [w for w, _ in segs] + ([gqa["w"]] if gqa is not None else [])
    for w in plain_ws + ([wt_all] if wt_all is not None else []):
        in_specs.append(pl.BlockSpec(w.shape, const))
        args.append(w)
    gate_rows = 0
    t_plain = tuple(len(ts) > 3 and ts[3] for ts in tsegs)
    tsegs = [ts[:3] for ts in tsegs]
    t_sizes = [n for n, _, _ in tsegs]
    if b_gates is not None:
        gate_rows = b_gates.shape[0]
        t_sizes.append(LANES)
        in_specs.append(pl.BlockSpec((gate_rows, 1), const))
        args.append(b_gates.reshape(gate_rows, 1))
    assert sum(t_sizes) == (0 if wt_all is None else wt_all.shape[0])
    if gqa is not None:
        in_specs += [pl.BlockSpec((1, LANES), const)] * 2
        args += [jnp.tile(gqa["g_q"], 2).reshape(1, LANES), jnp.tile(gqa["g_k"], 2).reshape(1, LANES)]
        if gqa["rope"] is not None:
            in_specs += [pl.BlockSpec((tm, LANES), lambda b, i: (i, 0))] * 2
            args += list(gqa["rope"])
    out_specs, out_shape = [], []
    for w, dt in segs:
        n = w.shape[1]
        out_specs.append(pl.BlockSpec((1, tm, n), lambda b, i: (b, i, 0)))
        out_shape.append(jax.ShapeDtypeStruct((bsz, t, n), dt))

    def add_t_out(n, dt, tb):
        if tb is None:
            out_specs.append(pl.BlockSpec((1, n, tm), lambda b, i: (b, 0, i)))
            out_shape.append(jax.ShapeDtypeStruct((bsz, n, t), dt))
        else:
            out_specs.append(pl.BlockSpec((1, tm // tb, n, tb), lambda b, i: (b, i, 0, 0)))
            out_shape.append(jax.ShapeDtypeStruct((bsz, t // tb, n, tb), dt))

    for n, dt, tb in tsegs:
        add_t_out(n, dt, tb)
    for (n, _, _), flag in zip(tsegs, t_plain):
        if flag:
            out_specs.append(pl.BlockSpec((1, tm, n), lambda b, i: (b, i, 0)))
            out_shape.append(jax.ShapeDtypeStruct((bsz, t, n), BF16))
    if b_gates is not None:
        add_t_out(gate_rows, F32, None)
    gqa_static = None
    if gqa is not None:
        out_specs += [pl.BlockSpec((1, tm, W_B), lambda b, i: (b, i, 0)),
                      pl.BlockSpec((1, tm, LANES), lambda b, i: (b, i, 0))]
        out_shape += [jax.ShapeDtypeStruct((bsz, t, W_B), BF16), jax.ShapeDtypeStruct((bsz, t, LANES), BF16)]
        if gqa["knt_block"] is not None:
            add_t_out(LANES, F32, gqa["knt_block"])
        gqa_static = (gqa["rope"] is not None, gqa["knt_block"])
    kern = functools.partial(_proj_kernel, n_seg=len(plain_ws), t_sizes=tuple(t_sizes),
                             t_blocks=tuple(tb for _, _, tb in tsegs), t_plain=t_plain,
                             gate_rows=gate_rows, gqa=gqa_static,
                             row0=row0, per_batch=per_batch)
    return pl.pallas_call(
        kern, grid=grid, in_specs=in_specs, out_specs=out_specs, out_shape=out_shape,
        compiler_params=_cparams(("arbitrary", "arbitrary")), name="in_proj",
    )(*args)


def _outproj_kernel(*refs, n_in, row0, per_batch):
    a_refs = refs[:n_in]
    z_ref, w_ref, x_ref, gp_ref, mod_ref, o_ref = refs[n_in:n_in + 6]
    d = x_ref.shape[-1]
    row = row0 + (pl.program_id(0) if per_batch else 0)
    gate = mod_ref[pl.ds(row, 1), 2 * d:3 * d]
    z = z_ref[0].astype(F32)
    sz = _silu(z)
    acc = None
    off = 0
    for a_ref in a_refs:
        kk = a_ref.shape[-1]
        y = (a_ref[0].astype(F32) * sz[:, off:off + kk]).astype(BF16)
        part = jnp.dot(y, w_ref[off:off + kk, :], preferred_element_type=F32)
        acc = part if acc is None else acc + part
        off += kk
    r = lax.rsqrt(jnp.mean(acc * acc, axis=-1, keepdims=True) + EPS)
    o_ref[0] = x_ref[0] + gate * (acc * r * gp_ref[...])


def _outproj(parts, z, w_out, x, g_post, mod, *, row0, per_batch, tm):
    bsz, t, d = x.shape
    grid = (bsz, t // tm)
    in_specs, args = [], []
    for a in parts:
        in_specs.append(pl.BlockSpec((1, tm, a.shape[-1]), lambda b, i: (b, i, 0)))
        args.append(a)
    in_specs += [pl.BlockSpec((1, tm, z.shape[-1]), lambda b, i: (b, i, 0)),
                 pl.BlockSpec(w_out.shape, lambda b, i: (0, 0)),
                 pl.BlockSpec((1, tm, d), lambda b, i: (b, i, 0)),
                 pl.BlockSpec((1, d), lambda b, i: (0, 0)),
                 pl.BlockSpec(mod.shape, lambda b, i: (0, 0))]
    args += [z, w_out, x, g_post.reshape(1, d), mod]
    kern = functools.partial(_outproj_kernel, n_in=len(parts), row0=row0, per_batch=per_batch)
    return pl.pallas_call(
        kern, grid=grid, in_specs=in_specs,
        out_specs=pl.BlockSpec((1, tm, d), lambda b, i: (b, i, 0)),
        out_shape=jax.ShapeDtypeStruct((bsz, t, d), F32),
        compiler_params=_cparams(("arbitrary", "arbitrary")), name="out_proj",
    )(*args)


def _split3_bf16(x):
    hi = x.astype(BF16)
    r1 = x - hi.astype(F32)
    mid = r1.astype(BF16)
    lo = (r1 - mid.astype(F32)).astype(BF16)
    return hi, mid, lo


def _mlstm_gate_rows(ig_f, fg_f, ig_b, fg_b):
    rr, ll = fg_f.shape
    lf = _log_sigmoid(jnp.concatenate([fg_f, fg_b], axis=0))
    pieces = jnp.concatenate(_split3_bf16(lf), axis=0)
    u = lax.broadcasted_iota(jnp.int32, (ll, ll), 0)
    t = lax.broadcasted_iota(jnp.int32, (ll, ll), 1)
    out = []
    for d, ig in enumerate((ig_f, ig_b)):
        tri = jnp.where((u >= t) if d else (u <= t), 1.0, 0.0).astype(BF16)
        y = jnp.dot(pieces, tri, preferred_element_type=F32)
        rows = slice(d * rr, (d + 1) * rr)
        b = y[0:2 * rr][rows] + y[2 * rr:4 * rr][rows] + y[4 * rr:6 * rr][rows]
        c = ig - b
        cmax = jnp.broadcast_to(jnp.max(c, axis=-1, keepdims=True), c.shape)
        tot = jnp.broadcast_to(jnp.sum(lf[rows], axis=-1, keepdims=True), c.shape)
        out.append((c, b, cmax, tot))
    return out


def _mlstm_kernel(*refs, nc, hps, bps, unroll, has_init, emit_state):
    q_ref, k_ref, vt_ref, ot_ref, gtt_ref, ghn_ref = refs[:6]
    pos = 6
    if has_init:
        c0_ref, n0_ref, m0_ref = refs[pos:pos + 3]
        pos += 3
    ha_ref = refs[pos]
    pos += 1
    if emit_state:
        cout_ref, nout_ref, mout_ref = refs[pos:pos + 3]
        pos += 3
    hs_sc, gate_sc, c_sc, n_sc, m_sc = refs[pos:pos + 5]
    ll, dh = CHUNK_A, DH_A
    kscale = dh ** -0.5
    hd0 = pl.program_id(1) * hps
    chains = [(bb, i, d) for bb in range(bps) for i in range(hps) for d in range(2)]

    for ch in chains:
        bb, i, d = ch
        if has_init:
            c_sc[ch] = c0_ref[bb, d, i]
            n_sc[ch] = n0_ref[bb, i, d:d + 1, :]
            m_sc[ch] = m0_ref[bb, i, d:d + 1, :]
        else:
            c_sc[ch] = jnp.zeros((dh, dh), F32)
            n_sc[ch] = jnp.zeros((1, dh), F32)
            m_sc[ch] = jnp.zeros((1, dh), F32)

    for bb in range(bps):
        def gate_rows(col0, bb=bb):
            return gtt_ref[bb, pl.ds(pl.multiple_of((col0 + hd0) * nc, 8), hps * nc), :]

        gates = _mlstm_gate_rows(gate_rows(0), gate_rows(H_A), gate_rows(2 * H_A), gate_rows(3 * H_A))
        for d in range(2):
            for kind in range(4):
                gate_sc[bb, d, kind] = gates[d][kind]

    ghn_t = [jnp.broadcast_to(ghn_ref[:, i * dh:(i + 1) * dh], (ll, dh)).T for i in range(hps)]
    si = lax.broadcasted_iota(jnp.int32, (ll, ll), 0)
    ti = lax.broadcasted_iota(jnp.int32, (ll, ll), 1)

    def run_trip(j, first_touch):
        jobs = []
        for u in range(unroll):
            step = j * unroll + u
            for ch in chains:
                jobs.append((ch, step if ch[2] == 0 else nc - 1 - step))
        state = {ch: [c_sc[ch], n_sc[ch], m_sc[ch]] for ch in chains}
        hcols = [slice(i * dh, (i + 1) * dh) for i in range(hps)]

        def tok_rows(cidx):
            return pl.ds(pl.multiple_of(cidx * ll, ll), ll)

        rows = []
        for ch, cidx in jobs:
            bb, i, d = ch
            c_r, b_r, cmax_r, tot_r = (gate_sc[bb, d, kind, pl.ds(i * nc + cidx, 1), :] for kind in range(4))
            m_st = state[ch][2]
            m_c = jnp.maximum(m_st, cmax_r)
            state[ch][2] = tot_r + m_c
            rows.append((c_r, b_r, m_st, jnp.exp(m_st - m_c), jnp.exp(c_r - m_c) * kscale))
        start = []
        for (ch, cidx), (_, _, _, a_st, wk) in zip(jobs, rows):
            bb, i, _ = ch
            k = k_ref[bb, tok_rows(cidx), hcols[i]]
            vt = vt_ref[bb, cidx, hcols[i], :]
            vw = jnp.concatenate([vt.astype(F32) * wk, jnp.broadcast_to(wk, (BF16_ROWS, ll))], axis=0)
            upd = jnp.dot(vw.astype(BF16), k, preferred_element_type=F32)
            c_st, n_st, _ = state[ch]
            start.append((c_st, n_st))
            state[ch][0] = a_st * c_st + upd[:dh]
            state[ch][1] = a_st * n_st + upd[dh:dh + 1]
        prods = []
        for (ch, cidx), (c_st, n_st) in zip(jobs, start):
            bb, i, _ = ch
            q = q_ref[bb, tok_rows(cidx), hcols[i]]
            k = k_ref[bb, tok_rows(cidx), hcols[i]]
            n16 = jnp.broadcast_to(n_st.astype(BF16), (BF16_ROWS, dh))
            prods.append(_dot_nt(jnp.concatenate([k, c_st.astype(BF16), n16], axis=0), q))
        for (ch, cidx), (c_r, b_r, m_st, _, _), r in zip(jobs, rows, prods):
            bb, i, d = ch
            allowed = (si >= ti) if d else (si <= ti)
            cb = jnp.where(allowed, jnp.broadcast_to(c_r, (ll, ll)).T, -jnp.inf)
            big_m = jnp.maximum(m_st, jnp.max(cb, axis=0, keepdims=True))
            p = jnp.exp(cb - (big_m - float(np.log(kscale)))) * r[:ll]
            w_inter = jnp.exp(m_st - big_m)
            vt = vt_ref[bb, cidx, hcols[i], :]
            num = jnp.dot(vt, p.astype(BF16), preferred_element_type=F32) - r[ll:ll + dh] * (-w_inter)
            den = w_inter * r[ll + dh:ll + dh + 1] + jnp.sum(p, axis=0, keepdims=True)
            h_t = num / jnp.maximum(jnp.abs(den), jnp.exp(-(b_r + big_m)))
            if first_touch:
                hs_sc[bb, i, cidx] = h_t
            else:
                hsum = h_t + hs_sc[bb, i, cidx]
                rn = lax.rsqrt(jnp.mean(hsum * hsum, axis=0, keepdims=True) + EPS)
                out_t = (hsum * rn * ghn_t[i]) * _sigmoid(ot_ref[bb, cidx, hcols[i], :].astype(F32))
                ha_ref[bb, tok_rows(cidx), hcols[i]] = out_t.T.astype(ha_ref.dtype)
        for ch in chains:
            c_sc[ch], n_sc[ch], m_sc[ch] = state[ch]

    def make_body(first_touch):
        def body(j, carry):
            run_trip(j, first_touch)
            return carry
        return body

    trips = nc // unroll
    lax.fori_loop(0, trips // 2, make_body(True), 0)
    lax.fori_loop(trips // 2, trips, make_body(False), 0)

    if emit_state:
        for ch in chains:
            bb, i, d = ch
            cout_ref[bb, 0, d, i] = c_sc[ch]
            nout_ref[bb, i, d:d + 1, :] = n_sc[ch]
            mout_ref[bb, i, d:d + 1, :] = m_sc[ch]


def _mlstm(qk, vt, ot, gates_t, g_hn, init, *, emit_state):
    bsz, t, _ = qk.shape
    nc = t // CHUNK_A
    hps, bps, unroll = (H_A, 2, 1) if nc < 8 else (1, 1, 8)
    assert (nc // 2) % unroll == 0 and nc % 2 == 0 and bsz % bps == 0
    gtt = gates_t.reshape(bsz, 4 * H_A * nc, CHUNK_A)
    wh = hps * DH_A

    def tblk():
        return pl.BlockSpec((bps, nc, wh, CHUNK_A), lambda b, g: (b, 0, g, 0))

    in_specs = [pl.BlockSpec((bps, t, wh), lambda b, g: (b, 0, g)),
                pl.BlockSpec((bps, t, wh), lambda b, g: (b, 0, H_A // hps + g)),
                tblk(), tblk(),
                pl.BlockSpec((bps, 4 * H_A * nc, CHUNK_A), lambda b, g: (b, 0, 0)),
                pl.BlockSpec((1, wh), lambda b, g: (0, g))]
    args = [qk, qk, vt, ot, gtt, g_hn.reshape(1, W_A)]
    if init is not None:
        c0, n0, m0 = init
        in_specs += [pl.BlockSpec((bps, 2, hps, DH_A, DH_A), lambda b, g: (b, 0, g, 0, 0)),
                     pl.BlockSpec((bps, hps, 2, DH_A), lambda b, g: (b, g, 0, 0)),
                     pl.BlockSpec((bps, hps, 2, DH_A), lambda b, g: (b, g, 0, 0))]
        args += [c0, n0, m0]
    out_specs = [pl.BlockSpec((bps, t, wh), lambda b, g: (b, 0, g))]
    out_shape = [jax.ShapeDtypeStruct((bsz, t, W_A), BF16)]
    if emit_state:
        out_specs += [pl.BlockSpec((bps, 1, 2, hps, DH_A, DH_A), lambda b, g: (b, 0, 0, g, 0, 0)),
                      pl.BlockSpec((bps, hps, 2, DH_A), lambda b, g: (b, g, 0, 0)),
                      pl.BlockSpec((bps, hps, 2, DH_A), lambda b, g: (b, g, 0, 0))]
        out_shape += [jax.ShapeDtypeStruct((bsz, 1, 2, H_A, DH_A, DH_A), F32),
                      jax.ShapeDtypeStruct((bsz, H_A, 2, DH_A), F32),
                      jax.ShapeDtypeStruct((bsz, H_A, 2, DH_A), F32)]
    kern = functools.partial(_mlstm_kernel, nc=nc, hps=hps, bps=bps, unroll=unroll,
                             has_init=init is not None, emit_state=emit_state)
    return pl.pallas_call(
        kern, grid=(bsz // bps, H_A // hps), in_specs=in_specs, out_specs=out_specs, out_shape=out_shape,
        scratch_shapes=[pltpu.VMEM((bps, hps, nc, DH_A, CHUNK_A), F32),
                        pltpu.VMEM((bps, 2, 4, hps * nc, CHUNK_A), F32),
                        pltpu.VMEM((bps, hps, 2, DH_A, DH_A), F32), pltpu.VMEM((bps, hps, 2, 1, DH_A), F32),
                        pltpu.VMEM((bps, hps, 2, 1, DH_A), F32)],
        compiler_params=_cparams(("arbitrary", "arbitrary")), name="mlstm",
    )(*args)


def _rope_tables(n_tok):
    t = np.arange(n_tok)
    row = (t // GRID_W).astype(np.float32)
    colp = (t % GRID_W).astype(np.float32)
    quarter = DH_B // 4
    freqs = (np.float32(ROPE_THETA) ** (-np.arange(quarter, dtype=np.float32) / np.float32(quarter))).astype(np.float32)
    ar = row[:, None] * freqs
    ac = colp[:, None] * freqs
    cos = np.concatenate([np.cos(ar), np.cos(ar), np.cos(ac), np.cos(ac)], axis=-1)
    sin = np.concatenate([-np.sin(ar), np.sin(ar), -np.sin(ac), np.sin(ac)], axis=-1)
    return jnp.asarray(np.tile(cos, (1, 2)), F32), jnp.asarray(np.tile(sin, (1, 2)), F32)


def _split_heads_q(q):
    lo = lax.broadcasted_iota(jnp.int32, q.shape, 1) < (LANES // 2)
    zero = jnp.zeros_like(q)
    return jnp.concatenate([jnp.where(lo, q, zero), jnp.where(lo, zero, q)], axis=0)


def _merge_heads_o(o_t, tq):
    half = LANES // 2
    return jnp.concatenate([o_t[:half, :tq], o_t[half:, tq:]], axis=0).T


def _with_ones_rows(vt):
    return jnp.concatenate([vt, jnp.ones((BF16_ROWS, vt.shape[1]), BF16)], axis=0)


def _online_softmax_pv_t(problems):
    seq = [(pi, ci) for pi, (_, chunks, _) in enumerate(problems) for ci in range(len(chunks))]
    scores = {}

    def issue(t):
        pi, ci = seq[t]
        qm, chunks, _ = problems[pi]
        scores[(pi, ci)] = _dot_nt(chunks[ci][0], qm)

    for t in range(min(QK_AHEAD, len(seq))):
        issue(t)
    m = acc = None
    for t, (pi, ci) in enumerate(seq):
        if t + QK_AHEAD < len(seq):
            issue(t + QK_AHEAD)
        _, chunks, emit = problems[pi]
        _, vt, penalty = chunks[ci]
        s = scores.pop((pi, ci))
        if penalty is not None:
            s = s - penalty
        mc = jnp.max(s, axis=0, keepdims=True)
        m_new = mc if ci == 0 else jnp.maximum(m, mc)
        p = jnp.exp2(s - m_new).astype(BF16)
        part = jnp.dot(_with_ones_rows(vt), p, preferred_element_type=F32)
        acc = part if ci == 0 else jnp.exp2(m - m_new) * acc + part
        m = m_new
        if ci == len(chunks) - 1:
            dv = vt.shape[0]
            emit(acc[:dv] / acc[dv:dv + 1])


def _attn_kernel(*refs, n_tiles, kv_shared, has_cache):
    q_ref, k_ref, vt_ref = refs[:3]
    pos = 3
    if has_cache:
        kc_ref, vtc_ref = refs[pos:pos + 2]
        pos += 2
    o_ref = refs[pos]
    tq = q_ref.shape[1]
    s_len = k_ref.shape[1]
    sc = min(s_len, KEY_CHUNK)
    problems = []
    for j in range(n_tiles):
        kj = 0 if kv_shared else j
        kcols = slice(kj * LANES, (kj + 1) * LANES)
        qm = _split_heads_q(q_ref[0, :, j * LANES:(j + 1) * LANES])
        chunks = [(k_ref[0, c * sc:(c + 1) * sc, kcols].astype(BF16),
                   vt_ref[0, kcols, c * sc:(c + 1) * sc].astype(BF16), None) for c in range(s_len // sc)]
        if has_cache:
            chunks.append((kc_ref[0], vtc_ref[0], None))

        def emit(o_t, j=j):
            o_ref[0, :, j * LANES:(j + 1) * LANES] = _merge_heads_o(o_t, tq).astype(o_ref.dtype)

        problems.append((qm, chunks, emit))
    _online_softmax_pv_t(problems)


def _attention(q, k, vt, cache, *, kv_shared, tq, n_tiles):
    bsz, t, w = q.shape
    s = k.shape[1]
    wt = n_tiles * LANES
    if kv_shared:
        k_spec = pl.BlockSpec((1, s, LANES), lambda b, i, g: (b, 0, 0))
        vt_spec = pl.BlockSpec((1, LANES, s), lambda b, i, g: (b, 0, 0))
    else:
        k_spec = pl.BlockSpec((1, s, wt), lambda b, i, g: (b, 0, g))
        vt_spec = pl.BlockSpec((1, wt, s), lambda b, i, g: (b, g, 0))
    in_specs = [pl.BlockSpec((1, tq, wt), lambda b, i, g: (b, i, g)), k_spec, vt_spec]
    args = [q, k, vt]
    if cache is not None:
        p = cache[0].shape[1]
        in_specs += [pl.BlockSpec((1, p, LANES), lambda b, i, g: (b, 0, 0)),
                     pl.BlockSpec((1, LANES, p), lambda b, i, g: (b, 0, 0))]
        args += list(cache)
    kern = functools.partial(_attn_kernel, n_tiles=n_tiles, kv_shared=kv_shared,
                             has_cache=cache is not None)
    return pl.pallas_call(
        kern, grid=(bsz, t // tq, w // wt), in_specs=in_specs,
        out_specs=pl.BlockSpec((1, tq, wt), lambda b, i, g: (b, i, g)),
        out_shape=jax.ShapeDtypeStruct((bsz, t, w), BF16),
        compiler_params=_cparams(("arbitrary", "arbitrary", "arbitrary")), name="attention",
    )(*args)


def _na_bias_blocks(variant):
    out = {}
    for qr in range(NA_ROWS):
        for kr in range(NA_WIN):
            if variant == 0:
                dr = kr - qr if kr < WIN_R else None
            elif variant == 1:
                dr = kr - qr - WIN_R // 2 if qr <= kr < qr + WIN_R else None
            else:
                dr = kr - qr - (NA_WIN - NA_ROWS) if kr >= NA_WIN - WIN_R else None
            out[(qr, kr)] = None if dr is None else dr + WIN_R - 1
    return out


def _na_kernel(q_ref, k_ref, vt_ref, kc_ref, vtc_ref, bc_ref, o_ref, bias_sc, *, n_rb):
    w = GRID_W
    tq = NA_ROWS * w

    @pl.when(jnp.logical_and(pl.program_id(1) == 0, pl.program_id(2) == 0))
    def _():
        ck = lax.broadcasted_iota(jnp.int32, (w, w), 0)
        cs = jnp.clip(lax.broadcasted_iota(jnp.int32, (w, w), 1) - WIN_C // 2, 0, w - WIN_C)
        valid = jnp.logical_and(ck >= cs, ck < cs + WIN_C)
        masked = jnp.full((w, w), MASKED, F32)
        tiles = {}

        def tile(hh, di):
            if (hh, di) not in tiles:
                rows = jnp.broadcast_to(bc_ref[hh, di:di + 1, :], (w, LANES))
                rolled = pltpu.roll(rows, 0, 1, stride=1, stride_axis=0)
                tiles[(hh, di)] = jnp.where(valid, rolled[:, :w], masked)
            return tiles[(hh, di)]

        for variant in range(3):
            for (qr, kr), di in _na_bias_blocks(variant).items():
                for hh in range(2 * NA_TILES):
                    val = masked if di is None else tile(hh, di)
                    bias_sc[variant, kr * w:(kr + 1) * w, hh * tq + qr * w:hh * tq + (qr + 1) * w] = val

    n_rows = n_rb * NA_ROWS
    problems = []
    for rr in range(NA_RBS):
        rb = pl.program_id(2) * NA_RBS + rr
        variant = jnp.where(rb == 0, 0, jnp.where(rb == n_rb - 1, 2, 1))
        ws = jnp.clip(rb * NA_ROWS - WIN_R // 2, 0, n_rows - NA_WIN)
        blk0 = ws // NA_ROWS
        start = ws * w
        qrows = slice(rr * tq, (rr + 1) * tq)
        for j in range(NA_TILES):
            cols = slice(j * LANES, (j + 1) * LANES)
            qcols = slice(j * 2 * tq, (j + 1) * 2 * tq)
            qm = _split_heads_q(q_ref[0, qrows, cols])
            chunks = [(k_ref[0, pl.ds(pl.multiple_of(start + i * tq, tq), tq), cols],
                       vt_ref[0, blk0 + i, cols, :],
                       bias_sc[variant, i * tq:(i + 1) * tq, qcols]) for i in range(NA_BLKS)]
            chunks.append((kc_ref[0, :, cols], vtc_ref[0, cols, :], None))

            def emit(o_t, qrows=qrows, cols=cols):
                o_ref[0, qrows, cols] = _merge_heads_o(o_t, tq).astype(o_ref.dtype)

            problems.append((qm, chunks, emit))
    _online_softmax_pv_t(problems)


def _na_bias_table(rpb):
    pad = jnp.zeros(rpb.shape[:2] + (LANES - (2 * WIN_C - 1),), F32)
    return jnp.concatenate([rpb[..., WIN_C - 1::-1], pad, rpb[..., :WIN_C - 1:-1]], axis=-1) * -LOG2E


def _na_attention(q, k, vt, kc, vtc, rpb):
    bsz, t, w = q.shape
    p = kc.shape[1]
    tq = NA_ROWS * GRID_W
    n_rb = t // tq
    bc = _na_bias_table(rpb)
    wt = NA_TILES * LANES
    tqs = NA_RBS * tq
    return pl.pallas_call(
        functools.partial(_na_kernel, n_rb=n_rb),
        grid=(w // wt, bsz, n_rb // NA_RBS),
        in_specs=[pl.BlockSpec((1, tqs, wt), lambda j, b, r: (b, r, j)),
                  pl.BlockSpec((1, t, wt), lambda j, b, r: (b, 0, j)),
                  pl.BlockSpec((1, n_rb, wt, tq), lambda j, b, r: (b, 0, j, 0)),
                  pl.BlockSpec((1, p, wt), lambda j, b, r: (b, 0, j)),
                  pl.BlockSpec((1, wt, p), lambda j, b, r: (b, j, 0)),
                  pl.BlockSpec((2 * NA_TILES, 2 * WIN_R - 1, LANES), lambda j, b, r: (j, 0, 0))],
        out_specs=pl.BlockSpec((1, tqs, wt), lambda j, b, r: (b, r, j)),
        out_shape=jax.ShapeDtypeStruct((bsz, t, w), BF16),
        scratch_shapes=[pltpu.VMEM((3, NA_WIN * GRID_W, NA_TILES * 2 * tq), F32)],
        compiler_params=_cparams(("arbitrary", "arbitrary", "arbitrary")), name="na_attention",
    )(q, k, vt, kc, vtc, bc)


def _relayout_kernel(w_ref, *o_refs, plans, transposed, axis):
    w = w_ref[...]
    tile = w.shape[1 - axis]
    for o_ref, pieces, tr in zip(o_refs, plans, transposed):
        vals = []
        for src, size, scale in pieces:
            if src is None:
                val = jnp.zeros((size, tile) if axis == 0 else (tile, size), F32)
            else:
                val = lax.slice_in_dim(w, src, src + size, axis=axis)
                if scale != 1.0:
                    val = val * scale
            vals.append(val)
        val = vals[0] if len(vals) == 1 else jnp.concatenate(vals, axis=axis)
        o_ref[...] = (val.T if tr else val).astype(o_ref.dtype)


def _relayout(w, plans, transposed, *, axis, tile):
    grid = (w.shape[1 - axis] // tile,)
    full = w.shape[1 - axis]

    def spec(extent, tr):
        along_rows = (axis == 0) != tr
        shape = (extent, tile) if along_rows else (tile, extent)
        return pl.BlockSpec(shape, (lambda i: (0, i)) if along_rows else (lambda i: (i, 0)))

    def shape(extent, tr):
        return (extent, full) if (axis == 0) != tr else (full, extent)

    sizes = [sum(size for _, size, _ in pieces) for pieces in plans]
    return pl.pallas_call(
        functools.partial(_relayout_kernel, plans=plans, transposed=tuple(transposed), axis=axis),
        grid=grid, in_specs=[spec(w.shape[axis], False)],
        out_specs=[spec(n, tr) for n, tr in zip(sizes, transposed)],
        out_shape=[jax.ShapeDtypeStruct(shape(n, tr), BF16) for n, tr in zip(sizes, transposed)],
        compiler_params=_cparams(("arbitrary",)), name="w_relayout",
    )(w)


_GQA_PERM = np.array([0, 4, 1, 5, 2, 6, 3, 7])


def _tok_major(cache):
    b, h, p, dh = cache.shape
    return cache.transpose(0, 2, 1, 3).reshape(b, p, h * dh)


def _feat_major(cache):
    b, h, p, dh = cache.shape
    return cache.transpose(0, 1, 3, 2).reshape(b, h * dh, p)


def _head_major_from_t(x_t, n_heads):
    b, w, t = x_t.shape
    return jnp.swapaxes(x_t.reshape(b, 1, n_heads, w // n_heads, t), -1, -2)


def _even_layer(xp, xs, mod, g_pre, g_post, w_in, b_gates, g_hn, g_q, g_k, w_out, st_c, st_n, st_m,
                ck, cv, rope_tabs):
    o_v, o_o = 2 * W_A, 3 * W_A
    o_g = 4 * W_A
    o_qb = o_g + 4 * H_A
    o_kb = o_qb + W_B
    o_vb = o_kb + HKV_B * DH_B
    o_z = o_vb + HKV_B * DH_B
    def heads(src0):
        return [(src0 + int(h) * DH_B, DH_B, 1.0) for h in _GQA_PERM]

    n_g = 4 * H_A
    w_qk, w_gqa, w_z, wt_all = _relayout(
        jnp.swapaxes(w_in, 0, 1),
        [[(0, o_v, 1.0)],
         heads(o_qb) + [(o_kb, HKV_B * DH_B, 1.0)],
         [(o_z, W_A, 1.0)] + heads(o_z + W_A),
         [(o_v, 2 * W_A, 1.0), (o_vb, HKV_B * DH_B, 1.0), (o_g, n_g, 1.0), (None, LANES - n_g, 1.0)]],
        [True, True, True, False], axis=0, tile=256)
    (w_o,) = _relayout(w_out, [[(0, W_A, 1.0)] + heads(W_A)], [False], axis=0, tile=256)

    def stream(x, row0, per_batch, init, cache, rope, emit):
        bsz, t, _ = x.shape
        tm = 1024
        xf = x if per_batch else x.reshape(1, bsz * t, D_MODEL)
        n_vb = HKV_B * DH_B
        vb_seg = (n_vb, BF16, None) if per_batch else (n_vb, F32, t)
        gqa = dict(w=w_gqa, g_q=g_q, g_k=g_k, rope=rope, knt_block=t if emit else None)
        outs = _proj(xf, g_pre, mod, [(w_qk, BF16), (w_z, BF16)], wt_all,
                     [(W_A, BF16, CHUNK_A), (W_A, BF16, CHUNK_A), vb_seg], b_gates, gqa,
                     row0=row0, per_batch=per_batch, tm=tm)
        qk, z, vta, ota, vbt, gtt, qn, kn = outs[:8]
        if not per_batch:
            qk, z, qn, kn = (a.reshape(bsz, t, a.shape[-1]) for a in (qk, z, qn, kn))
            gtt = gtt.reshape(4 * H_A, bsz, t).transpose(1, 0, 2)
            vbt = vbt.reshape(bsz, HKV_B * DH_B, t)
            vta, ota = (a.reshape(bsz, t // CHUNK_A, W_A, CHUNK_A) for a in (vta, ota))
        res = _mlstm(qk, vta, ota, gtt, g_hn, init, emit_state=emit)
        ha = res[0]
        if per_batch:
            hb = _attention(qn, kn, vbt, cache, kv_shared=True, tq=512, n_tiles=4)
        else:
            hb = _attention(qn, kn, vbt, cache, kv_shared=True, tq=t, n_tiles=W_B // LANES)
        y = _outproj([ha.reshape(xf.shape[0], -1, W_A), hb.reshape(xf.shape[0], -1, W_B)],
                     z.reshape(xf.shape[0], -1, W_A + W_B), w_o, xf, g_post, mod,
                     row0=row0, per_batch=per_batch, tm=OUTPROJ_TM)
        knt = outs[8].reshape(bsz, HKV_B * DH_B, t) if emit else None
        return y.reshape(bsz, t, D_MODEL), res[1:], knt, vbt

    yp, st, knt_p, vbt_p = stream(xp, 0, False, None, None, None, True)
    n0 = st_n.transpose(0, 2, 1, 3)
    m0 = jnp.broadcast_to(st_m.transpose(0, 2, 1)[..., None], n0.shape)
    cache = (_tok_major(ck).astype(BF16), _feat_major(cv).astype(BF16))
    ys, _, _, _ = stream(xs, 1, True, (st_c, n0, m0), cache, rope_tabs, False)
    c_out, n_out, m_out = st
    new_n = n_out.transpose(0, 2, 1, 3)[:, None]
    new_m = m_out[..., 0].transpose(0, 2, 1)[:, None]
    return (yp, ys, c_out, new_n, new_m, _head_major_from_t(knt_p, HKV_B), _head_major_from_t(vbt_p, HKV_B))


def _odd_layer(xp, xs, mod, g_pre, g_post, w_in, rpb, w_out, ck, cv):
    w_q, w_k, w_z, wt_kv = _relayout(
        w_in, [[(0, W_C, QSCALE)], [(W_C, W_C, 1.0)], [(3 * W_C, W_C, 1.0)], [(W_C, 2 * W_C, 1.0)]],
        [False, False, False, True], axis=1, tile=256)
    (w_o,) = _relayout(w_out, [[(0, W_C, 1.0)]], [False], axis=0, tile=256)
    tm = 512
    bsz, t, _ = xp.shape
    xf = xp.reshape(1, bsz * t, D_MODEL)
    q, z, kt, vt, k = _proj(xf, g_pre, mod, [(w_q, BF16), (w_z, BF16)], wt_kv,
                            [(W_C, F32, t, True), (W_C, F32, t)], None, None, row0=0, per_batch=False, tm=tm)
    kt = kt.reshape(bsz, W_C, t)
    vt = vt.reshape(bsz, W_C, t)
    o = _attention(q.reshape(bsz, t, W_C), k.reshape(bsz, t, W_C), vt, None,
                   kv_shared=False, tq=t, n_tiles=W_C // LANES)
    yp = _outproj([o.reshape(1, bsz * t, W_C)], z, w_o, xf, g_post, mod, row0=0, per_batch=False,
                  tm=OUTPROJ_TM)
    yp = yp.reshape(bsz, t, D_MODEL)
    q, k, z, vts = _proj(xs, g_pre, mod, [(w_q, BF16), (w_k, BF16), (w_z, BF16)], wt_kv[W_C:],
                         [(W_C, BF16, NA_ROWS * GRID_W)], None, None, row0=1, per_batch=True, tm=2 * tm)
    o = _na_attention(q, k, vts, _tok_major(ck).astype(BF16), _feat_major(cv).astype(BF16), rpb)
    ys = _outproj([o], z, w_o, xs, g_post, mod, row0=1, per_batch=True, tm=OUTPROJ_TM)
    return yp, ys, _head_major_from_t(kt, H_C), _head_major_from_t(vt, H_C)


def kernel(x_prompt, x_sample, state_mlstm_C, state_mlstm_n, state_mlstm_m, cache_gqa_k, cache_gqa_v,
           cache_na_k, cache_na_v, c, c_ctx, w_mod, b_mod, g_pre, g_post, w_in_ab, b_gates_ab, g_hnorm_a,
           g_qnorm_b, g_knorm_b, w_out_ab, w_in_c, rpb_c, w_out_c):
    depth = w_mod.shape[0]
    assert depth == 2 and c.shape[0] == 2
    cvec = jnp.concatenate([c_ctx[None], c, jnp.zeros((8 - 1 - c.shape[0], D_MODEL), F32)], axis=0)
    mod = _modulation(cvec, w_mod, b_mod)
    rope_tabs = _rope_tables(x_sample.shape[1])
    xp, xs, c_out, n_out, m_out, gk, gv = _even_layer(
        x_prompt, x_sample, mod[0], g_pre[0], g_post[0], w_in_ab[0], b_gates_ab[0], g_hnorm_a[0],
        g_qnorm_b[0], g_knorm_b[0], w_out_ab[0], state_mlstm_C[:, 0], state_mlstm_n[:, 0],
        state_mlstm_m[:, 0], cache_gqa_k[:, 0], cache_gqa_v[:, 0], rope_tabs)
    xp, xs, nk, nv = _odd_layer(xp, xs, mod[1], g_pre[1], g_post[1], w_in_c[0], rpb_c[0], w_out_c[0],
                                cache_na_k[:, 0], cache_na_v[:, 0])
    return (xp, xs, c_out, n_out, m_out, gk, gv, nk, nv)
```

```python
import functools

import jax
import jax.numpy as jnp
import numpy as np
from jax import lax
from jax.experimental import pallas as pl
from jax.experimental.pallas import tpu as pltpu

F32 = jnp.float32
BF16 = jnp.bfloat16

D_MODEL = 1024
GRID_W = 64
EPS = 1e-6
H_A = 4
DH_A = 128
W_A = H_A * DH_A
CHUNK_A = 128
HQ_B = 8
HKV_B = 2
DH_B = 64
W_B = HQ_B * DH_B
ROPE_THETA = 10000.0
H_C = 16
DH_C = 64
W_C = H_C * DH_C
WIN_R = 8
WIN_C = 16

LANES = 128
NA_ROWS = 4
NA_TILES = 2
NA_RBS = 8
NA_BLKS = -(-(NA_ROWS + WIN_R - 1) // NA_ROWS)
NA_WIN = NA_BLKS * NA_ROWS
assert WIN_R // 2 == NA_ROWS
BF16_ROWS = 16
PROJ_SUBTILE = 256
OUTPROJ_TM = 1024
KEY_CHUNK = 256
QK_AHEAD = 3
LOG2E = 1.4426950408889634
QSCALE = DH_B ** -0.5 * LOG2E
assert DH_B == DH_C
MASKED = 1e30
VMEM_LIMIT = 56 * 1024 * 1024


def _cparams(sem):
    return pltpu.CompilerParams(dimension_semantics=sem, vmem_limit_bytes=VMEM_LIMIT)


def _silu(x):
    return x / (1.0 + jnp.exp(-x))


def _sigmoid(x):
    return 1.0 / (1.0 + jnp.exp(-x))


def _log_sigmoid(x):
    return jnp.minimum(x, 0.0) - jnp.log1p(jnp.exp(-jnp.abs(x)))


def _dot_nt(a, b):
    return lax.dot_general(a, b, (((1,), (1,)), ((), ())), preferred_element_type=F32)


MOD_ROWS = 3


def _mod_kernel(c_ref, w_ref, b_ref, o_ref, sb_sc):
    d, tn = w_ref.shape[1], w_ref.shape[2]
    sub = 8

    @pl.when(jnp.logical_and(pl.program_id(0) == 0, pl.program_id(1) == 0))
    def _():
        s = _silu(c_ref[...])
        for r in range(MOD_ROWS):
            sb_sc[r] = jnp.broadcast_to(s[r:r + 1, :], (LANES, d)).T

    def body(g, acc):
        rows = pl.ds(pl.multiple_of(g * sub, sub), sub)
        w = w_ref[0, rows, :]
        out = []
        for r in range(MOD_ROWS):
            sb = sb_sc[r, rows, :]
            out.append([acc[r][j] + sb * w[:, j * LANES:(j + 1) * LANES] for j in range(tn // LANES)])
        return out

    zero = jnp.zeros((sub, LANES), F32)
    acc = lax.fori_loop(0, d // sub, body, [[zero] * (tn // LANES) for _ in range(MOD_ROWS)], unroll=8)
    rows = [jnp.sum(jnp.concatenate(a, axis=1), axis=0, keepdims=True) + b_ref[0] for a in acc]
    o_ref[0] = jnp.concatenate(rows + [jnp.zeros((8 - MOD_ROWS, tn), F32)], axis=0)


def _modulation(cvec, w_mod, b_mod):
    depth, d, n = w_mod.shape
    tn = n // 4
    return pl.pallas_call(
        _mod_kernel,
        grid=(depth, n // tn),
        in_specs=[pl.BlockSpec((8, d), lambda l, j: (0, 0)),
                  pl.BlockSpec((1, d, tn), lambda l, j: (l, 0, j)),
                  pl.BlockSpec((1, 1, tn), lambda l, j: (l, 0, j))],
        out_specs=pl.BlockSpec((1, 8, tn), lambda l, j: (l, 0, j)),
        out_shape=jax.ShapeDtypeStruct((depth, 8, n), F32),
        scratch_shapes=[pltpu.VMEM((MOD_ROWS, d, LANES), F32)],
        compiler_params=_cparams(("arbitrary", "arbitrary")),
        name="modulation",
    )(cvec, w_mod, b_mod.reshape(depth, 1, n))


def _head_norm(x, g):
    lo = lax.broadcasted_iota(jnp.int32, x.shape, 1) < DH_B
    x2 = x * x
    s_lo = jnp.sum(jnp.where(lo, x2, 0.0), axis=-1, keepdims=True)
    s_hi = jnp.sum(jnp.where(lo, 0.0, x2), axis=-1, keepdims=True)
    ms = jnp.where(lo, s_lo, s_hi) * (1.0 / DH_B)
    return x * lax.rsqrt(ms + EPS) * g


def _rope(x, cos, sin):
    quarter = DH_B // 4
    first = (lax.broadcasted_iota(jnp.int32, x.shape, 1) % (2 * quarter)) < quarter
    partner = jnp.where(first, pltpu.roll(x, LANES - quarter, 1), pltpu.roll(x, quarter, 1))
    return x * cos + partner * sin


def _store_t(ref, val, t_block, tok0):
    n_tok = val.shape[1]
    if t_block is None:
        ref[0, :, tok0:tok0 + n_tok] = val.astype(ref.dtype)
    else:
        for i in range(n_tok // t_block):
            ref[0, tok0 // t_block + i] = val[:, i * t_block:(i + 1) * t_block].astype(ref.dtype)


def _proj_kernel(*refs, n_seg, t_sizes, t_blocks, t_plain, gate_rows, gqa, row0, per_batch):
    x_ref, g_ref, mod_ref = refs[:3]
    pos = 3
    w_refs = refs[pos:pos + n_seg]
    pos += n_seg
    if t_sizes:
        wt_ref = refs[pos]
        pos += 1
    if gate_rows:
        bgt_ref = refs[pos]
        pos += 1
    if gqa is not None:
        rope, knt_block = gqa
        gq_ref, gk_ref = refs[pos:pos + 2]
        pos += 2
        if rope:
            cos_ref, sin_ref = refs[pos:pos + 2]
            pos += 2
    n_plain = n_seg - (1 if gqa is not None else 0)
    o_refs = refs[pos:pos + n_plain]
    pos += n_plain
    ot_refs = refs[pos:pos + len(t_blocks)]
    pos += len(t_blocks)
    otp_refs = {}
    for i, flag in enumerate(t_plain):
        if flag:
            otp_refs[i] = refs[pos]
            pos += 1
    if gate_rows:
        gto_ref = refs[pos]
        pos += 1
    if gqa is not None:
        q_out, k_out = refs[pos:pos + 2]
        pos += 2
        if knt_block is not None:
            knt_out = refs[pos]
            pos += 1
    t_offs = [0]
    for n in t_sizes:
        t_offs.append(t_offs[-1] + n)

    d = x_ref.shape[-1]
    tm = x_ref.shape[1]
    row = row0 + (pl.program_id(0) if per_batch else 0)
    shift = mod_ref[pl.ds(row, 1), 0:d]
    scale = mod_ref[pl.ds(row, 1), d:2 * d]
    sub = PROJ_SUBTILE
    for s in range(tm // sub):
        rows = slice(s * sub, (s + 1) * sub)
        x = x_ref[0, rows, :]
        r = lax.rsqrt(jnp.mean(x * x, axis=-1, keepdims=True) + EPS)
        h = (x * r * g_ref[...]) * (1.0 + scale) + shift
        hb = h.astype(BF16)
        if gqa is not None:
            res = jnp.dot(hb, w_refs[-1][...], preferred_element_type=F32)
        for w_ref, o_ref in zip(w_refs[:n_plain], o_refs):
            o_ref[0, rows, :] = jnp.dot(hb, w_ref[...], preferred_element_type=F32).astype(o_ref.dtype)
        if t_sizes:
            res_t = _dot_nt(wt_ref[...], hb)
            for i, (ot_ref, tb, off, end) in enumerate(zip(ot_refs, t_blocks, t_offs, t_offs[1:])):
                _store_t(ot_ref, res_t[off:end], tb, s * sub)
                if i in otp_refs:
                    otp_refs[i][0, rows, :] = res_t[off:end].T.astype(otp_refs[i].dtype)
            if gate_rows:
                off = t_offs[len(t_blocks)]
                gto_ref[0, :, rows] = res_t[off:off + gate_rows] + bgt_ref[...]
        if gqa is not None:
            nq = W_B // LANES
            for j in range(nq + 1):
                xn = _head_norm(res[:, j * LANES:(j + 1) * LANES], gq_ref[...] if j < nq else gk_ref[...])
                if j == nq and knt_block is not None:
                    _store_t(knt_out, xn.T, knt_block, s * sub)
                if rope:
                    xn = _rope(xn, cos_ref[rows, :], sin_ref[rows, :])
                if j < nq:
                    q_out[0, rows, j * LANES:(j + 1) * LANES] = (xn * QSCALE).astype(q_out.dtype)
                else:
                    k_out[0, rows, :] = xn.astype(k_out.dtype)


def _proj(x, g_pre, mod, segs, wt_all, tsegs, b_gates, gqa, *, row0, per_batch, tm):
    bsz, t, d = x.shape
    grid = (bsz, t // tm)
    const = lambda b, i: (0, 0)
    in_specs = [pl.BlockSpec((1, tm, d), lambda b, i: (b, i, 0)),
                pl.BlockSpec((1, d), const),
                pl.BlockSpec(mod.shape, const)]
    args = [x, g_pre.reshape(1, d), mod]
    plain_ws = [w for w, _ in segs] + ([gqa["w"]] if gqa is not None else [])
    for w in plain_ws + ([wt_all] if wt_all is not None else []):
        in_specs.append(pl.BlockSpec(w.shape, const, pipeline_mode=pl.Buffered(1)))
        args.append(w)
    gate_rows = 0
    t_plain = tuple(len(ts) > 3 and ts[3] for ts in tsegs)
    tsegs = [ts[:3] for ts in tsegs]
    t_sizes = [n for n, _, _ in tsegs]
    if b_gates is not None:
        gate_rows = b_gates.shape[0]
        t_sizes.append(LANES)
        in_specs.append(pl.BlockSpec((gate_rows, 1), const))
        args.append(b_gates.reshape(gate_rows, 1))
    assert sum(t_sizes) == (0 if wt_all is None else wt_all.shape[0])
    if gqa is not None:
        in_specs += [pl.BlockSpec((1, LANES), const)] * 2
        args += [jnp.tile(gqa["g_q"], 2).reshape(1, LANES), jnp.tile(gqa["g_k"], 2).reshape(1, LANES)]
        if gqa["rope"] is not None:
            in_specs += [pl.BlockSpec((tm, LANES), lambda b, i: (i, 0))] * 2
            args += list(gqa["rope"])
    out_specs, out_shape = [], []
    for w, dt in segs:
        n = w.shape[1]
        out_specs.append(pl.BlockSpec((1, tm, n), lambda b, i: (b, i, 0)))
        out_shape.append(jax.ShapeDtypeStruct((bsz, t, n), dt))

    def add_t_out(n, dt, tb):
        if tb is None:
            out_specs.append(pl.BlockSpec((1, n, tm), lambda b, i: (b, 0, i)))
            out_shape.append(jax.ShapeDtypeStruct((bsz, n, t), dt))
        else:
            out_specs.append(pl.BlockSpec((1, tm // tb, n, tb), lambda b, i: (b, i, 0, 0)))
            out_shape.append(jax.ShapeDtypeStruct((bsz, t // tb, n, tb), dt))

    for n, dt, tb in tsegs:
        add_t_out(n, dt, tb)
    for (n, _, _), flag in zip(tsegs, t_plain):
        if flag:
            out_specs.append(pl.BlockSpec((1, tm, n), lambda b, i: (b, i, 0)))
            out_shape.append(jax.ShapeDtypeStruct((bsz, t, n), BF16))
    if b_gates is not None:
        add_t_out(gate_rows, F32, None)
    gqa_static = None
    if gqa is not None:
        out_specs += [pl.BlockSpec((1, tm, W_B), lambda b, i: (b, i, 0)),
                      pl.BlockSpec((1, tm, LANES), lambda b, i: (b, i, 0))]
        out_shape += [jax.ShapeDtypeStruct((bsz, t, W_B), BF16), jax.ShapeDtypeStruct((bsz, t, LANES), BF16)]
        if gqa["knt_block"] is not None:
            add_t_out(LANES, F32, gqa["knt_block"])
        gqa_static = (gqa["rope"] is not None, gqa["knt_block"])
    kern = functools.partial(_proj_kernel, n_seg=len(plain_ws), t_sizes=tuple(t_sizes),
                             t_blocks=tuple(tb for _, _, tb in tsegs), t_plain=t_plain,
                             gate_rows=gate_rows, gqa=gqa_static,
                             row0=row0, per_batch=per_batch)
    return pl.pallas_call(
        kern, grid=grid, in_specs=in_specs, out_specs=out_specs, out_shape=out_shape,
        compiler_params=_cparams(("arbitrary", "arbitrary")), name="in_proj",
    )(*args)


def _outproj_kernel(*refs, n_in, row0, per_batch):
    a_refs = refs[:n_in]
    z_ref, w_ref, x_ref, gp_ref, mod_ref, o_ref = refs[n_in:n_in + 6]
    d = x_ref.shape[-1]
    row = row0 + (pl.program_id(0) if per_batch else 0)
    gate = mod_ref[pl.ds(row, 1), 2 * d:3 * d]
    z = z_ref[0].astype(F32)
    sz = _silu(z)
    acc = None
    off = 0
    for a_ref in a_refs:
        kk = a_ref.shape[-1]
        y = (a_ref[0].astype(F32) * sz[:, off:off + kk]).astype(BF16)
        part = jnp.dot(y, w_ref[off:off + kk, :], preferred_element_type=F32)
        acc = part if acc is None else acc + part
        off += kk
    r = lax.rsqrt(jnp.mean(acc * acc, axis=-1, keepdims=True) + EPS)
    o_ref[0] = x_ref[0] + gate * (acc * r * gp_ref[...])


def _outproj(parts, z, w_out, x, g_post, mod, *, row0, per_batch, tm):
    bsz, t, d = x.shape
    grid = (bsz, t // tm)
    in_specs, args = [], []
    for a in parts:
        in_specs.append(pl.BlockSpec((1, tm, a.shape[-1]), lambda b, i: (b, i, 0)))
        args.append(a)
    in_specs += [pl.BlockSpec((1, tm, z.shape[-1]), lambda b, i: (b, i, 0)),
                 pl.BlockSpec(w_out.shape, lambda b, i: (0, 0)),
                 pl.BlockSpec((1, tm, d), lambda b, i: (b, i, 0)),
                 pl.BlockSpec((1, d), lambda b, i: (0, 0)),
                 pl.BlockSpec(mod.shape, lambda b, i: (0, 0))]
    args += [z, w_out, x, g_post.reshape(1, d), mod]
    kern = functools.partial(_outproj_kernel, n_in=len(parts), row0=row0, per_batch=per_batch)
    return pl.pallas_call(
        kern, grid=grid, in_specs=in_specs,
        out_specs=pl.BlockSpec((1, tm, d), lambda b, i: (b, i, 0)),
        out_shape=jax.ShapeDtypeStruct((bsz, t, d), F32),
        compiler_params=_cparams(("arbitrary", "arbitrary")), name="out_proj",
    )(*args)


def _split3_bf16(x):
    hi = x.astype(BF16)
    r1 = x - hi.astype(F32)
    mid = r1.astype(BF16)
    lo = (r1 - mid.astype(F32)).astype(BF16)
    return hi, mid, lo


def _mlstm_gate_rows(ig_f, fg_f, ig_b, fg_b):
    rr, ll = fg_f.shape
    lf = _log_sigmoid(jnp.concatenate([fg_f, fg_b], axis=0))
    pieces = jnp.concatenate(_split3_bf16(lf), axis=0)
    u = lax.broadcasted_iota(jnp.int32, (ll, ll), 0)
    t = lax.broadcasted_iota(jnp.int32, (ll, ll), 1)
    out = []
    for d, ig in enumerate((ig_f, ig_b)):
        tri = jnp.where((u >= t) if d else (u <= t), 1.0, 0.0).astype(BF16)
        y = jnp.dot(pieces, tri, preferred_element_type=F32)
        rows = slice(d * rr, (d + 1) * rr)
        b = y[0:2 * rr][rows] + y[2 * rr:4 * rr][rows] + y[4 * rr:6 * rr][rows]
        c = ig - b
        cmax = jnp.broadcast_to(jnp.max(c, axis=-1, keepdims=True), c.shape)
        tot = jnp.broadcast_to(jnp.sum(lf[rows], axis=-1, keepdims=True), c.shape)
        out.append((c, b, cmax, tot))
    return out


def _mlstm_kernel(*refs, nc, hps, bps, unroll, has_init, emit_state):
    q_ref, k_ref, vt_ref, ot_ref, gtt_ref, ghn_ref = refs[:6]
    pos = 6
    if has_init:
        c0_ref, n0_ref, m0_ref = refs[pos:pos + 3]
        pos += 3
    ha_ref = refs[pos]
    pos += 1
    if emit_state:
        cout_ref, nout_ref, mout_ref = refs[pos:pos + 3]
        pos += 3
    hs_sc, gate_sc, c_sc, n_sc, m_sc = refs[pos:pos + 5]
    ll, dh = CHUNK_A, DH_A
    kscale = dh ** -0.5
    hd0 = pl.program_id(1) * hps
    chains = [(bb, i, d) for bb in range(bps) for i in range(hps) for d in range(2)]

    for ch in chains:
        bb, i, d = ch
        if has_init:
            c_sc[ch] = c0_ref[bb, d, i]
            n_sc[ch] = n0_ref[bb, i, d:d + 1, :]
            m_sc[ch] = m0_ref[bb, i, d:d + 1, :]
        else:
            c_sc[ch] = jnp.zeros((dh, dh), F32)
            n_sc[ch] = jnp.zeros((1, dh), F32)
            m_sc[ch] = jnp.zeros((1, dh), F32)

    for bb in range(bps):
        def gate_rows(col0, bb=bb):
            return gtt_ref[bb, pl.ds(pl.multiple_of((col0 + hd0) * nc, 8), hps * nc), :]

        gates = _mlstm_gate_rows(gate_rows(0), gate_rows(H_A), gate_rows(2 * H_A), gate_rows(3 * H_A))
        for d in range(2):
            for kind in range(4):
                gate_sc[bb, d, kind] = gates[d][kind]

    ghn_t = [jnp.broadcast_to(ghn_ref[:, i * dh:(i + 1) * dh], (ll, dh)).T for i in range(hps)]
    si = lax.broadcasted_iota(jnp.int32, (ll, ll), 0)
    ti = lax.broadcasted_iota(jnp.int32, (ll, ll), 1)

    def run_trip(j, first_touch):
        jobs = []
        for u in range(unroll):
            step = j * unroll + u
            for ch in chains:
                jobs.append((ch, step if ch[2] == 0 else nc - 1 - step))
        state = {ch: [c_sc[ch], n_sc[ch], m_sc[ch]] for ch in chains}
        hcols = [slice(i * dh, (i + 1) * dh) for i in range(hps)]

        def tok_rows(cidx):
            return pl.ds(pl.multiple_of(cidx * ll, ll), ll)

        rows = []
        for ch, cidx in jobs:
            bb, i, d = ch
            c_r, b_r, cmax_r, tot_r = (gate_sc[bb, d, kind, pl.ds(i * nc + cidx, 1), :] for kind in range(4))
            m_st = state[ch][2]
            m_c = jnp.maximum(m_st, cmax_r)
            state[ch][2] = tot_r + m_c
            rows.append((c_r, b_r, m_st, jnp.exp(m_st - m_c), jnp.exp(c_r - m_c) * kscale))
        start = []
        for (ch, cidx), (_, _, _, a_st, wk) in zip(jobs, rows):
            bb, i, _ = ch
            k = k_ref[bb, tok_rows(cidx), hcols[i]]
            vt = vt_ref[bb, cidx, hcols[i], :]
            vw = jnp.concatenate([vt.astype(F32) * wk, jnp.broadcast_to(wk, (BF16_ROWS, ll))], axis=0)
            upd = jnp.dot(vw.astype(BF16), k, preferred_element_type=F32)
            c_st, n_st, _ = state[ch]
            start.append((c_st, n_st))
            state[ch][0] = a_st * c_st + upd[:dh]
            state[ch][1] = a_st * n_st + upd[dh:dh + 1]
        prods = []
        for (ch, cidx), (c_st, n_st) in zip(jobs, start):
            bb, i, _ = ch
            q = q_ref[bb, tok_rows(cidx), hcols[i]]
            k = k_ref[bb, tok_rows(cidx), hcols[i]]
            n16 = jnp.broadcast_to(n_st.astype(BF16), (BF16_ROWS, dh))
            prods.append(_dot_nt(jnp.concatenate([k, c_st.astype(BF16), n16], axis=0), q))
        for (ch, cidx), (c_r, b_r, m_st, _, _), r in zip(jobs, rows, prods):
            bb, i, d = ch
            allowed = (si >= ti) if d else (si <= ti)
            cb = jnp.where(allowed, jnp.broadcast_to(c_r, (ll, ll)).T, -jnp.inf)
            big_m = jnp.maximum(m_st, jnp.max(cb, axis=0, keepdims=True))
            p = jnp.exp(cb - (big_m - float(np.log(kscale)))) * r[:ll]
            w_inter = jnp.exp(m_st - big_m)
            vt = vt_ref[bb, cidx, hcols[i], :]
            num = jnp.dot(vt, p.astype(BF16), preferred_element_type=F32) - r[ll:ll + dh] * (-w_inter)
            den = w_inter * r[ll + dh:ll + dh + 1] + jnp.sum(p, axis=0, keepdims=True)
            h_t = num / jnp.maximum(jnp.abs(den), jnp.exp(-(b_r + big_m)))
            if first_touch:
                hs_sc[bb, i, cidx] = h_t
            else:
                hsum = h_t + hs_sc[bb, i, cidx]
                rn = lax.rsqrt(jnp.mean(hsum * hsum, axis=0, keepdims=True) + EPS)
                out_t = (hsum * rn * ghn_t[i]) * _sigmoid(ot_ref[bb, cidx, hcols[i], :].astype(F32))
                ha_ref[bb, tok_rows(cidx), hcols[i]] = out_t.T.astype(ha_ref.dtype)
        for ch in chains:
            c_sc[ch], n_sc[ch], m_sc[ch] = state[ch]

    def make_body(first_touch):
        def body(j, carry):
            run_trip(j, first_touch)
            return carry
        return body

    trips = nc // unroll
    lax.fori_loop(0, trips // 2, make_body(True), 0)
    lax.fori_loop(trips // 2, trips, make_body(False), 0)

    if emit_state:
        for ch in chains:
            bb, i, d = ch
            cout_ref[bb, 0, d, i] = c_sc[ch]
            nout_ref[bb, i, d:d + 1, :] = n_sc[ch]
            mout_ref[bb, i, d:d + 1, :] = m_sc[ch]


def _mlstm(qk, vt, ot, gates_t, g_hn, init, *, emit_state):
    bsz, t, _ = qk.shape
    nc = t // CHUNK_A
    hps, bps, unroll = (H_A, 2, 1) if nc < 8 else (1, 1, 8)
    assert (nc // 2) % unroll == 0 and nc % 2 == 0 and bsz % bps == 0
    gtt = gates_t.reshape(bsz, 4 * H_A * nc, CHUNK_A)
    wh = hps * DH_A

    def tblk():
        return pl.BlockSpec((bps, nc, wh, CHUNK_A), lambda b, g: (b, 0, g, 0))

    in_specs = [pl.BlockSpec((bps, t, wh), lambda b, g: (b, 0, g)),
                pl.BlockSpec((bps, t, wh), lambda b, g: (b, 0, H_A // hps + g)),
                tblk(), tblk(),
                pl.BlockSpec((bps, 4 * H_A * nc, CHUNK_A), lambda b, g: (b, 0, 0)),
                pl.BlockSpec((1, wh), lambda b, g: (0, g))]
    args = [qk, qk, vt, ot, gtt, g_hn.reshape(1, W_A)]
    if init is not None:
        c0, n0, m0 = init
        in_specs += [pl.BlockSpec((bps, 2, hps, DH_A, DH_A), lambda b, g: (b, 0, g, 0, 0)),
                     pl.BlockSpec((bps, hps, 2, DH_A), lambda b, g: (b, g, 0, 0)),
                     pl.BlockSpec((bps, hps, 2, DH_A), lambda b, g: (b, g, 0, 0))]
        args += [c0, n0, m0]
    out_specs = [pl.BlockSpec((bps, t, wh), lambda b, g: (b, 0, g))]
    out_shape = [jax.ShapeDtypeStruct((bsz, t, W_A), BF16)]
    if emit_state:
        out_specs += [pl.BlockSpec((bps, 1, 2, hps, DH_A, DH_A), lambda b, g: (b, 0, 0, g, 0, 0)),
                      pl.BlockSpec((bps, hps, 2, DH_A), lambda b, g: (b, g, 0, 0)),
                      pl.BlockSpec((bps, hps, 2, DH_A), lambda b, g: (b, g, 0, 0))]
        out_shape += [jax.ShapeDtypeStruct((bsz, 1, 2, H_A, DH_A, DH_A), F32),
                      jax.ShapeDtypeStruct((bsz, H_A, 2, DH_A), F32),
                      jax.ShapeDtypeStruct((bsz, H_A, 2, DH_A), F32)]
    kern = functools.partial(_mlstm_kernel, nc=nc, hps=hps, bps=bps, unroll=unroll,
                             has_init=init is not None, emit_state=emit_state)
    return pl.pallas_call(
        kern, grid=(bsz // bps, H_A // hps), in_specs=in_specs, out_specs=out_specs, out_shape=out_shape,
        scratch_shapes=[pltpu.VMEM((bps, hps, nc, DH_A, CHUNK_A), F32),
                        pltpu.VMEM((bps, 2, 4, hps * nc, CHUNK_A), F32),
                        pltpu.VMEM((bps, hps, 2, DH_A, DH_A), F32), pltpu.VMEM((bps, hps, 2, 1, DH_A), F32),
                        pltpu.VMEM((bps, hps, 2, 1, DH_A), F32)],
        compiler_params=_cparams(("arbitrary", "arbitrary")), name="mlstm",
    )(*args)


def _rope_tables(n_tok):
    t = np.arange(n_tok)
    row = (t // GRID_W).astype(np.float32)
    colp = (t % GRID_W).astype(np.float32)
    quarter = DH_B // 4
    freqs = (np.float32(ROPE_THETA) ** (-np.arange(quarter, dtype=np.float32) / np.float32(quarter))).astype(np.float32)
    ar = row[:, None] * freqs
    ac = colp[:, None] * freqs
    cos = np.concatenate([np.cos(ar), np.cos(ar), np.cos(ac), np.cos(ac)], axis=-1)
    sin = np.concatenate([-np.sin(ar), np.sin(ar), -np.sin(ac), np.sin(ac)], axis=-1)
    return jnp.asarray(np.tile(cos, (1, 2)), F32), jnp.asarray(np.tile(sin, (1, 2)), F32)


def _split_heads_q(q):
    lo = lax.broadcasted_iota(jnp.int32, q.shape, 1) < (LANES // 2)
    zero = jnp.zeros_like(q)
    return jnp.concatenate([jnp.where(lo, q, zero), jnp.where(lo, zero, q)], axis=0)


def _merge_heads_o(o_t, tq):
    half = LANES // 2
    return jnp.concatenate([o_t[:half, :tq], o_t[half:, tq:]], axis=0).T


def _with_ones_rows(vt):
    return jnp.concatenate([vt, jnp.ones((BF16_ROWS, vt.shape[1]), BF16)], axis=0)


def _online_softmax_pv_t(problems):
    seq = [(pi, ci) for pi, (_, chunks, _) in enumerate(problems) for ci in range(len(chunks))]
    scores = {}

    def issue(t):
        pi, ci = seq[t]
        qm, chunks, _ = problems[pi]
        scores[(pi, ci)] = _dot_nt(chunks[ci][0], qm)

    for t in range(min(QK_AHEAD, len(seq))):
        issue(t)
    m = acc = None
    for t, (pi, ci) in enumerate(seq):
        if t + QK_AHEAD < len(seq):
            issue(t + QK_AHEAD)
        _, chunks, emit = problems[pi]
        _, vt, penalty = chunks[ci]
        s = scores.pop((pi, ci))
        if penalty is not None:
            s = s - penalty
        mc = jnp.max(s, axis=0, keepdims=True)
        m_new = mc if ci == 0 else jnp.maximum(m, mc)
        p = jnp.exp2(s - m_new).astype(BF16)
        part = jnp.dot(_with_ones_rows(vt), p, preferred_element_type=F32)
        acc = part if ci == 0 else jnp.exp2(m - m_new) * acc + part
        m = m_new
        if ci == len(chunks) - 1:
            dv = vt.shape[0]
            emit(acc[:dv] / acc[dv:dv + 1])


def _attn_kernel(*refs, n_tiles, kv_shared, has_cache):
    q_ref, k_ref, vt_ref = refs[:3]
    pos = 3
    if has_cache:
        kc_ref, vtc_ref = refs[pos:pos + 2]
        pos += 2
    o_ref = refs[pos]
    tq = q_ref.shape[1]
    s_len = k_ref.shape[1]
    sc = min(s_len, KEY_CHUNK)
    problems = []
    for j in range(n_tiles):
        kj = 0 if kv_shared else j
        kcols = slice(kj * LANES, (kj + 1) * LANES)
        qm = _split_heads_q(q_ref[0, :, j * LANES:(j + 1) * LANES])
        chunks = [(k_ref[0, c * sc:(c + 1) * sc, kcols].astype(BF16),
                   vt_ref[0, kcols, c * sc:(c + 1) * sc].astype(BF16), None) for c in range(s_len // sc)]
        if has_cache:
            chunks.append((kc_ref[0], vtc_ref[0], None))

        def emit(o_t, j=j):
            o_ref[0, :, j * LANES:(j + 1) * LANES] = _merge_heads_o(o_t, tq).astype(o_ref.dtype)

        problems.append((qm, chunks, emit))
    _online_softmax_pv_t(problems)


def _attention(q, k, vt, cache, *, kv_shared, tq, n_tiles):
    bsz, t, w = q.shape
    s = k.shape[1]
    wt = n_tiles * LANES
    if kv_shared:
        k_spec = pl.BlockSpec((1, s, LANES), lambda b, i, g: (b, 0, 0))
        vt_spec = pl.BlockSpec((1, LANES, s), lambda b, i, g: (b, 0, 0))
    else:
        k_spec = pl.BlockSpec((1, s, wt), lambda b, i, g: (b, 0, g))
        vt_spec = pl.BlockSpec((1, wt, s), lambda b, i, g: (b, g, 0))
    in_specs = [pl.BlockSpec((1, tq, wt), lambda b, i, g: (b, i, g)), k_spec, vt_spec]
    args = [q, k, vt]
    if cache is not None:
        p = cache[0].shape[1]
        in_specs += [pl.BlockSpec((1, p, LANES), lambda b, i, g: (b, 0, 0)),
                     pl.BlockSpec((1, LANES, p), lambda b, i, g: (b, 0, 0))]
        args += list(cache)
    kern = functools.partial(_attn_kernel, n_tiles=n_tiles, kv_shared=kv_shared,
                             has_cache=cache is not None)
    return pl.pallas_call(
        kern, grid=(bsz, t // tq, w // wt), in_specs=in_specs,
        out_specs=pl.BlockSpec((1, tq, wt), lambda b, i, g: (b, i, g)),
        out_shape=jax.ShapeDtypeStruct((bsz, t, w), BF16),
        compiler_params=_cparams(("arbitrary", "arbitrary", "arbitrary")), name="attention",
    )(*args)


def _na_bias_blocks(variant):
    out = {}
    for qr in range(NA_ROWS):
        for kr in range(NA_WIN):
            if variant == 0:
                dr = kr - qr if kr < WIN_R else None
            elif variant == 1:
                dr = kr - qr - WIN_R // 2 if qr <= kr < qr + WIN_R else None
            else:
                dr = kr - qr - (NA_WIN - NA_ROWS) if kr >= NA_WIN - WIN_R else None
            out[(qr, kr)] = None if dr is None else dr + WIN_R - 1
    return out


def _na_kernel(q_ref, k_ref, vt_ref, kc_ref, vtc_ref, bc_ref, o_ref, bias_sc, *, n_rb):
    w = GRID_W
    tq = NA_ROWS * w

    @pl.when(jnp.logical_and(pl.program_id(1) == 0, pl.program_id(2) == 0))
    def _():
        ck = lax.broadcasted_iota(jnp.int32, (w, w), 0)
        cs = jnp.clip(lax.broadcasted_iota(jnp.int32, (w, w), 1) - WIN_C // 2, 0, w - WIN_C)
        valid = jnp.logical_and(ck >= cs, ck < cs + WIN_C)
        masked = jnp.full((w, w), MASKED, F32)
        tiles = {}

        def tile(hh, di):
            if (hh, di) not in tiles:
                rows = jnp.broadcast_to(bc_ref[hh, di:di + 1, :], (w, LANES))
                rolled = pltpu.roll(rows, 0, 1, stride=1, stride_axis=0)
                tiles[(hh, di)] = jnp.where(valid, rolled[:, :w], masked)
            return tiles[(hh, di)]

        for variant in range(3):
            for (qr, kr), di in _na_bias_blocks(variant).items():
                for hh in range(2 * NA_TILES):
                    val = masked if di is None else tile(hh, di)
                    bias_sc[variant, kr * w:(kr + 1) * w, hh * tq + qr * w:hh * tq + (qr + 1) * w] = val

    n_rows = n_rb * NA_ROWS
    problems = []
    for rr in range(NA_RBS):
        rb = pl.program_id(2) * NA_RBS + rr
        variant = jnp.where(rb == 0, 0, jnp.where(rb == n_rb - 1, 2, 1))
        ws = jnp.clip(rb * NA_ROWS - WIN_R // 2, 0, n_rows - NA_WIN)
        blk0 = ws // NA_ROWS
        start = ws * w
        qrows = slice(rr * tq, (rr + 1) * tq)
        for j in range(NA_TILES):
            cols = slice(j * LANES, (j + 1) * LANES)
            qcols = slice(j * 2 * tq, (j + 1) * 2 * tq)
            qm = _split_heads_q(q_ref[0, qrows, cols])
            chunks = [(k_ref[0, pl.ds(pl.multiple_of(start + i * tq, tq), tq), cols],
                       vt_ref[0, blk0 + i, cols, :],
                       bias_sc[variant, i * tq:(i + 1) * tq, qcols]) for i in range(NA_BLKS)]
            chunks.append((kc_ref[0, :, cols], vtc_ref[0, cols, :], None))

            def emit(o_t, qrows=qrows, cols=cols):
                o_ref[0, qrows, cols] = _merge_heads_o(o_t, tq).astype(o_ref.dtype)

            problems.append((qm, chunks, emit))
    _online_softmax_pv_t(problems)


def _na_bias_table(rpb):
    pad = jnp.zeros(rpb.shape[:2] + (LANES - (2 * WIN_C - 1),), F32)
    return jnp.concatenate([rpb[..., WIN_C - 1::-1], pad, rpb[..., :WIN_C - 1:-1]], axis=-1) * -LOG2E


def _na_attention(q, k, vt, kc, vtc, rpb):
    bsz, t, w = q.shape
    p = kc.shape[1]
    tq = NA_ROWS * GRID_W
    n_rb = t // tq
    bc = _na_bias_table(rpb)
    wt = NA_TILES * LANES
    tqs = NA_RBS * tq
    return pl.pallas_call(
        functools.partial(_na_kernel, n_rb=n_rb),
        grid=(w // wt, bsz, n_rb // NA_RBS),
        in_specs=[pl.BlockSpec((1, tqs, wt), lambda j, b, r: (b, r, j)),
                  pl.BlockSpec((1, t, wt), lambda j, b, r: (b, 0, j)),
                  pl.BlockSpec((1, n_rb, wt, tq), lambda j, b, r: (b, 0, j, 0)),
                  pl.BlockSpec((1, p, wt), lambda j, b, r: (b, 0, j)),
                  pl.BlockSpec((1, wt, p), lambda j, b, r: (b, j, 0)),
                  pl.BlockSpec((2 * NA_TILES, 2 * WIN_R - 1, LANES), lambda j, b, r: (j, 0, 0))],
        out_specs=pl.BlockSpec((1, tqs, wt), lambda j, b, r: (b, r, j)),
        out_shape=jax.ShapeDtypeStruct((bsz, t, w), BF16),
        scratch_shapes=[pltpu.VMEM((3, NA_WIN * GRID_W, NA_TILES * 2 * tq), F32)],
        compiler_params=_cparams(("arbitrary", "arbitrary", "arbitrary")), name="na_attention",
    )(q, k, vt, kc, vtc, bc)


def _relayout_kernel(w_ref, *o_refs, plans, transposed, axis):
    w = w_ref[...]
    tile = w.shape[1 - axis]
    for o_ref, pieces, tr in zip(o_refs, plans, transposed):
        vals = []
        for src, size, scale in pieces:
            if src is None:
                val = jnp.zeros((size, tile) if axis == 0 else (tile, size), F32)
            else:
                val = lax.slice_in_dim(w, src, src + size, axis=axis)
                if scale != 1.0:
                    val = val * scale
            vals.append(val)
        val = vals[0] if len(vals) == 1 else jnp.concatenate(vals, axis=axis)
        o_ref[...] = (val.T if tr else val).astype(o_ref.dtype)


def _relayout(w, plans, transposed, *, axis, tile):
    grid = (w.shape[1 - axis] // tile,)
    full = w.shape[1 - axis]

    def spec(extent, tr):
        along_rows = (axis == 0) != tr
        shape = (extent, tile) if along_rows else (tile, extent)
        return pl.BlockSpec(shape, (lambda i: (0, i)) if along_rows else (lambda i: (i, 0)))

    def shape(extent, tr):
        return (extent, full) if (axis == 0) != tr else (full, extent)

    sizes = [sum(size for _, size, _ in pieces) for pieces in plans]
    return pl.pallas_call(
        functools.partial(_relayout_kernel, plans=plans, transposed=tuple(transposed), axis=axis),
        grid=grid, in_specs=[spec(w.shape[axis], False)],
        out_specs=[spec(n, tr) for n, tr in zip(sizes, transposed)],
        out_shape=[jax.ShapeDtypeStruct(shape(n, tr), BF16) for n, tr in zip(sizes, transposed)],
        compiler_params=_cparams(("arbitrary",)), name="w_relayout",
    )(w)


_GQA_PERM = np.array([0, 4, 1, 5, 2, 6, 3, 7])


def _tok_major(cache):
    b, h, p, dh = cache.shape
    return cache.transpose(0, 2, 1, 3).reshape(b, p, h * dh)


def _feat_major(cache):
    b, h, p, dh = cache.shape
    return cache.transpose(0, 1, 3, 2).reshape(b, h * dh, p)


def _head_major_from_t(x_t, n_heads):
    b, w, t = x_t.shape
    return jnp.swapaxes(x_t.reshape(b, 1, n_heads, w // n_heads, t), -1, -2)


def _even_layer(xp, xs, mod, g_pre, g_post, w_in, b_gates, g_hn, g_q, g_k, w_out, st_c, st_n, st_m,
                ck, cv, rope_tabs):
    o_v, o_o = 2 * W_A, 3 * W_A
    o_g = 4 * W_A
    o_qb = o_g + 4 * H_A
    o_kb = o_qb + W_B
    o_vb = o_kb + HKV_B * DH_B
    o_z = o_vb + HKV_B * DH_B
    def heads(src0):
        return [(src0 + int(h) * DH_B, DH_B, 1.0) for h in _GQA_PERM]

    n_g = 4 * H_A
    w_qk, w_gqa, w_z, wt_all = _relayout(
        jnp.swapaxes(w_in, 0, 1),
        [[(0, o_v, 1.0)],
         heads(o_qb) + [(o_kb, HKV_B * DH_B, 1.0)],
         [(o_z, W_A, 1.0)] + heads(o_z + W_A),
         [(o_v, 2 * W_A, 1.0), (o_vb, HKV_B * DH_B, 1.0), (o_g, n_g, 1.0), (None, LANES - n_g, 1.0)]],
        [True, True, True, False], axis=0, tile=256)
    (w_o,) = _relayout(w_out, [[(0, W_A, 1.0)] + heads(W_A)], [False], axis=0, tile=256)

    def stream(x, row0, per_batch, init, cache, rope, emit):
        bsz, t, _ = x.shape
        tm = 1024
        xf = x if per_batch else x.reshape(1, bsz * t, D_MODEL)
        n_vb = HKV_B * DH_B
        vb_seg = (n_vb, BF16, None) if per_batch else (n_vb, F32, t)
        gqa = dict(w=w_gqa, g_q=g_q, g_k=g_k, rope=rope, knt_block=t if emit else None)
        outs = _proj(xf, g_pre, mod, [(w_qk, BF16), (w_z, BF16)], wt_all,
                     [(W_A, BF16, CHUNK_A), (W_A, BF16, CHUNK_A), vb_seg], b_gates, gqa,
                     row0=row0, per_batch=per_batch, tm=tm)
        qk, z, vta, ota, vbt, gtt, qn, kn = outs[:8]
        if not per_batch:
            qk, z, qn, kn = (a.reshape(bsz, t, a.shape[-1]) for a in (qk, z, qn, kn))
            gtt = gtt.reshape(4 * H_A, bsz, t).transpose(1, 0, 2)
            vbt = vbt.reshape(bsz, HKV_B * DH_B, t)
            vta, ota = (a.reshape(bsz, t // CHUNK_A, W_A, CHUNK_A) for a in (vta, ota))
        res = _mlstm(qk, vta, ota, gtt, g_hn, init, emit_state=emit)
        ha = res[0]
        if per_batch:
            hb = _attention(qn, kn, vbt, cache, kv_shared=True, tq=512, n_tiles=4)
        else:
            hb = _attention(qn, kn, vbt, cache, kv_shared=True, tq=t, n_tiles=W_B // LANES)
        y = _outproj([ha.reshape(xf.shape[0], -1, W_A), hb.reshape(xf.shape[0], -1, W_B)],
                     z.reshape(xf.shape[0], -1, W_A + W_B), w_o, xf, g_post, mod,
                     row0=row0, per_batch=per_batch, tm=OUTPROJ_TM)
        knt = outs[8].reshape(bsz, HKV_B * DH_B, t) if emit else None
        return y.reshape(bsz, t, D_MODEL), res[1:], knt, vbt

    yp, st, knt_p, vbt_p = stream(xp, 0, False, None, None, None, True)
    n0 = st_n.transpose(0, 2, 1, 3)
    m0 = jnp.broadcast_to(st_m.transpose(0, 2, 1)[..., None], n0.shape)
    cache = (_tok_major(ck).astype(BF16), _feat_major(cv).astype(BF16))
    ys, _, _, _ = stream(xs, 1, True, (st_c, n0, m0), cache, rope_tabs, False)
    c_out, n_out, m_out = st
    new_n = n_out.transpose(0, 2, 1, 3)[:, None]
    new_m = m_out[..., 0].transpose(0, 2, 1)[:, None]
    return (yp, ys, c_out, new_n, new_m, _head_major_from_t(knt_p, HKV_B), _head_major_from_t(vbt_p, HKV_B))


def _odd_layer(xp, xs, mod, g_pre, g_post, w_in, rpb, w_out, ck, cv):
    w_q, w_k, w_z, wt_kv = _relayout(
        w_in, [[(0, W_C, QSCALE)], [(W_C, W_C, 1.0)], [(3 * W_C, W_C, 1.0)], [(W_C, 2 * W_C, 1.0)]],
        [False, False, False, True], axis=1, tile=256)
    (w_o,) = _relayout(w_out, [[(0, W_C, 1.0)]], [False], axis=0, tile=256)
    tm = 512
    bsz, t, _ = xp.shape
    xf = xp.reshape(1, bsz * t, D_MODEL)
    q, z, kt, vt, k = _proj(xf, g_pre, mod, [(w_q, BF16), (w_z, BF16)], wt_kv,
                            [(W_C, F32, t, True), (W_C, F32, t)], None, None, row0=0, per_batch=False,
                            tm=2 * tm)
    kt = kt.reshape(bsz, W_C, t)
    vt = vt.reshape(bsz, W_C, t)
    o = _attention(q.reshape(bsz, t, W_C), k.reshape(bsz, t, W_C), vt, None,
                   kv_shared=False, tq=t, n_tiles=W_C // LANES)
    yp = _outproj([o.reshape(1, bsz * t, W_C)], z, w_o, xf, g_post, mod, row0=0, per_batch=False,
                  tm=OUTPROJ_TM)
    yp = yp.reshape(bsz, t, D_MODEL)
    q, k, z, vts = _proj(xs, g_pre, mod, [(w_q, BF16), (w_k, BF16), (w_z, BF16)], wt_kv[W_C:],
                         [(W_C, BF16, NA_ROWS * GRID_W)], None, None, row0=1, per_batch=True, tm=2 * tm)
    o = _na_attention(q, k, vts, _tok_major(ck).astype(BF16), _feat_major(cv).astype(BF16), rpb)
    ys = _outproj([o], z, w_o, xs, g_post, mod, row0=1, per_batch=True, tm=OUTPROJ_TM)
    return yp, ys, _head_major_from_t(kt, H_C), _head_major_from_t(vt, H_C)


def kernel(x_prompt, x_sample, state_mlstm_C, state_mlstm_n, state_mlstm_m, cache_gqa_k, cache_gqa_v,
           cache_na_k, cache_na_v, c, c_ctx, w_mod, b_mod, g_pre, g_post, w_in_ab, b_gates_ab, g_hnorm_a,
           g_qnorm_b, g_knorm_b, w_out_ab, w_in_c, rpb_c, w_out_c):
    depth = w_mod.shape[0]
    assert depth == 2 and c.shape[0] == 2
    cvec = jnp.concatenate([c_ctx[None], c, jnp.zeros((8 - 1 - c.shape[0], D_MODEL), F32)], axis=0)
    mod = _modulation(cvec, w_mod, b_mod)
    rope_tabs = _rope_tables(x_sample.shape[1])
    xp, xs, c_out, n_out, m_out, gk, gv = _even_layer(
        x_prompt, x_sample, mod[0], g_pre[0], g_post[0], w_in_ab[0], b_gates_ab[0], g_hnorm_a[0],
        g_qnorm_b[0], g_knorm_b[0], w_out_ab[0], state_mlstm_C[:, 0], state_mlstm_n[:, 0],
        state_mlstm_m[:, 0], cache_gqa_k[:, 0], cache_gqa_v[:, 0], rope_tabs)
    xp, xs, nk, nv = _odd_layer(xp, xs, mod[1], g_pre[1], g_post[1], w_in_c[0], rpb_c[0], w_out_c[0],
                                cache_na_k[:, 0], cache_na_v[:, 0])
    return (xp, xs, c_out, n_out, m_out, gk, gv, nk, nv)
```
